```python
import math, functools
import jax, jax.numpy as jnp
from jax import lax
import numpy as np

D_MODEL = 1024
BATCH = 2
SEQ = 8192
DEPTH = 1
DEC_BATCH = 128
DEC_SEQ = 8
PAST_LEN = 8192
PAGE_SIZE = 128

V_HEAD = 128
MLA_HEADS = D_MODEL // V_HEAD
QK_NOPE = 128
QK_ROPE = 64
Q_LORA = 3 * D_MODEL // 8
KV_LORA = D_MODEL // 4
ROPE_THETA = 10000.0
MLA_SCALE = (QK_NOPE + QK_ROPE) ** -0.5
Q_BLOCK = 128
GDN_DK = 128
GDN_DV = 128
GDN_HEADS = D_MODEL // GDN_DV
GDN_QK = GDN_HEADS * GDN_DK
GDN_V = GDN_HEADS * GDN_DV
GDN_CONV_CH = 2 * GDN_QK + GDN_V
CONV_W = 4
GDN_CHUNK = 64
N_EXPERTS = 32
TOP_K = 4
D_EXPERT = D_MODEL
SWIGLU_LIMIT = 7.0
SWIGLU_ALPHA = 1.702
MOE_BLOCK = 128
NORM_EPS = 1e-6
DEEP_ALPHA = (2 * DEPTH) ** 0.25
DEEP_BETA = (8 * DEPTH) ** -0.25
IN_SPLITS = (Q_LORA, KV_LORA, QK_ROPE, GDN_CONV_CH, GDN_V, GDN_HEADS, GDN_HEADS, D_MODEL, D_MODEL)
D_IN = sum(IN_SPLITS)

kernel_name = 'hybrid_mla_gdn_moe_deepnorm_step'


def rms_norm(x, w):
    xf = x.astype(jnp.float32)
    y = xf * lax.rsqrt(jnp.mean(xf * xf, -1, keepdims=True) + NORM_EPS)
    return (y * w.astype(jnp.float32)).astype(x.dtype)


def layer_norm(x, g, b):
    xf = x.astype(jnp.float32)
    mu = jnp.mean(xf, -1, keepdims=True)
    xc = xf - mu
    var = jnp.mean(xc * xc, -1, keepdims=True)
    y = xc * lax.rsqrt(var + NORM_EPS) * g.astype(jnp.float32) + b.astype(jnp.float32)
    return y.astype(x.dtype)


def l2_normalize(x):
    xf = x.astype(jnp.float32)
    return xf * lax.rsqrt(jnp.sum(xf * xf, -1, keepdims=True) + NORM_EPS)


def rope(x, pos):
    half = QK_ROPE // 2
    inv = ROPE_THETA ** (-jnp.arange(half, dtype=jnp.float32) / half)
    ang = pos.astype(jnp.float32)[:, None] * inv[None, :]
    cos = jnp.cos(ang)[None, :, None, :]
    sin = jnp.sin(ang)[None, :, None, :]
    xf = x.astype(jnp.float32)
    x1, x2 = xf[..., :half], xf[..., half:]
    return jnp.concatenate([x1 * cos - x2 * sin, x2 * cos + x1 * sin], -1).astype(x.dtype)


def split_in_proj(x, w_in):
    h = jnp.einsum('btd,de->bte', x, w_in)
    parts, off = [], 0
    for n in IN_SPLITS:
        parts.append(h[..., off:off + n])
        off += n
    return parts


def mla_project(q_lat, kv_lat, k_r, pos, q_norm_w, kv_norm_w, w_uq, w_uk):
    q = jnp.einsum('btr,rhe->bthe', rms_norm(q_lat, q_norm_w), w_uq)
    q_nope, q_rope = q[..., :QK_NOPE], rope(q[..., QK_NOPE:], pos)
    q_abs = jnp.einsum('bthn,chn->bthc', q_nope, w_uk)
    c = rms_norm(kv_lat, kv_norm_w)
    kr = rope(k_r[:, :, None, :], pos)[:, :, 0, :]
    return q_abs, q_rope, c, kr


def mla_attend_prompt(q_abs, q_rope, c, kr):
    B, T = c.shape[:2]
    qb = min(Q_BLOCK, T)
    nb = T // qb
    qa = q_abs.reshape(B, nb, qb, MLA_HEADS, KV_LORA).swapaxes(0, 1)
    qr = q_rope.reshape(B, nb, qb, MLA_HEADS, QK_ROPE).swapaxes(0, 1)
    kpos = jnp.arange(T)

    def block(args):
        qa_b, qr_b, i = args
        s = (jnp.einsum('bqhc,bkc->bhqk', qa_b, c) + jnp.einsum('bqhr,bkr->bhqk', qr_b, kr)).astype(jnp.float32) * MLA_SCALE
        qpos = i * qb + jnp.arange(qb)
        s = jnp.where(kpos[None, :] <= qpos[:, None], s, -jnp.inf)
        p = jax.nn.softmax(s, axis=-1).astype(c.dtype)
        return jnp.einsum('bhqk,bkc->bqhc', p, c)

    o = lax.map(block, (qa, qr, jnp.arange(nb)))
    return o.swapaxes(0, 1).reshape(B, T, MLA_HEADS, KV_LORA)


def mla_attend_sample(q_abs, q_rope, c, kr, c_past, kr_past):
    T = c.shape[1]
    L = c_past.shape[1]
    s_past = (jnp.einsum('bqhc,bkc->bhqk', q_abs, c_past) + jnp.einsum('bqhr,bkr->bhqk', q_rope, kr_past)).astype(jnp.float32) * MLA_SCALE
    s_new = (jnp.einsum('bqhc,bkc->bhqk', q_abs, c) + jnp.einsum('bqhr,bkr->bhqk', q_rope, kr)).astype(jnp.float32) * MLA_SCALE
    s_new = jnp.where(jnp.tril(jnp.ones((T, T), bool)), s_new, -jnp.inf)
    p = jax.nn.softmax(jnp.concatenate([s_past, s_new], -1), axis=-1).astype(c.dtype)
    return jnp.einsum('bhqk,bkc->bqhc', p[..., :L], c_past) + jnp.einsum('bhqk,bkc->bqhc', p[..., L:], c)


def short_conv(x, conv_state, conv_w):
    T = x.shape[1]
    xp = jnp.concatenate([conv_state.astype(x.dtype), x], axis=1)
    out = xp[:, 0:T] * conv_w[0]
    for j in range(1, CONV_W):
        out = out + xp[:, j:j + T] * conv_w[j]
    return out, xp[:, T:]


def gated_delta_chunked(q, k, v, g, beta, S0):
    B, T, H, DK = q.shape
    C = min(GDN_CHUNK, T)
    N = T // C

    def chunks(a):
        return jnp.moveaxis(a.reshape((B, N, C) + a.shape[2:]), 3, 2)

    q, k, v, g, beta = (chunks(a) for a in (q, k, v, g, beta))
    gc = jnp.cumsum(g, -1)
    causal = jnp.tril(jnp.ones((C, C), bool))
    strict = jnp.tril(jnp.ones((C, C), bool), -1)
    decay = jnp.where(causal, jnp.exp(jnp.where(causal, gc[..., :, None] - gc[..., None, :], 0.0)), 0.0)
    kb = k * beta[..., None]
    lower = jnp.where(strict, jnp.einsum('bnhid,bnhjd->bnhij', kb, k) * decay, 0.0)
    eye = jnp.eye(C, dtype=jnp.float32)
    tinv = lax.linalg.triangular_solve(lower + eye, jnp.broadcast_to(eye, lower.shape),
                                       left_side=True, lower=True, unit_diagonal=True)
    u = tinv @ (v * beta[..., None])
    w = tinv @ (kb * jnp.exp(gc)[..., None])
    attn = jnp.einsum('bnhid,bnhjd->bnhij', q, k) * decay
    q_dec = q * jnp.exp(gc)[..., None]
    k_dec = k * jnp.exp(gc[..., -1:] - gc)[..., None]
    g_last = jnp.exp(gc[..., -1])

    def step(S, xs):
        u_n, w_n, attn_n, qd_n, kd_n, gl_n = xs
        v_new = u_n - jnp.einsum('bhcd,bhde->bhce', w_n, S)
        o_n = jnp.einsum('bhcd,bhde->bhce', qd_n, S) + jnp.einsum('bhij,bhje->bhie', attn_n, v_new)
        S = S * gl_n[..., None, None] + jnp.einsum('bhcd,bhce->bhde', kd_n, v_new)
        return S, o_n

    xs = tuple(jnp.moveaxis(a, 1, 0) for a in (u, w, attn, q_dec, k_dec, g_last))
    S, o = lax.scan(step, S0, xs)
    o = jnp.moveaxis(jnp.moveaxis(o, 0, 1), 2, 3).reshape(B, T, H, v.shape[-1])
    return o, S


def gdn_branch(qkv, z, a, b, conv_state, ssm_state, conv_w, a_log, dt_bias, gdn_norm_w):
    B, T, _ = qkv.shape
    conv, new_conv = short_conv(qkv, conv_state, conv_w)
    act = jax.nn.silu(conv)
    q = l2_normalize(act[..., :GDN_QK].reshape(B, T, GDN_HEADS, GDN_DK)) * (GDN_DK ** -0.5)
    k = l2_normalize(act[..., GDN_QK:2 * GDN_QK].reshape(B, T, GDN_HEADS, GDN_DK))
    v = act[..., 2 * GDN_QK:].reshape(B, T, GDN_HEADS, GDN_DV).astype(jnp.float32)
    g = -jnp.exp(a_log.astype(jnp.float32)) * jax.nn.softplus(a.astype(jnp.float32) + dt_bias.astype(jnp.float32))
    beta = jax.nn.sigmoid(b.astype(jnp.float32))
    o, S = gated_delta_chunked(q, k, v, g, beta, ssm_state.astype(jnp.float32))
    o = rms_norm(o, gdn_norm_w) * jax.nn.silu(z.reshape(B, T, GDN_HEADS, GDN_DV).astype(jnp.float32))
    return o.reshape(B, T, GDN_V).astype(qkv.dtype), new_conv, S.astype(ssm_state.dtype)


def moe_ffn(x, w_router, b_router, w_gu, b_gu, w_down, b_down):
    B, T, D = x.shape
    M = B * T
    xf = x.reshape(M, D)
    logits = (xf @ w_router + b_router).astype(jnp.float32)
    top_val, top_idx = lax.top_k(logits, TOP_K)
    gates = jax.nn.softmax(top_val, axis=-1)
    A = M * TOP_K
    flat_e = top_idx.reshape(A)
    flat_tok = jnp.repeat(jnp.arange(M, dtype=jnp.int32), TOP_K)
    order = jnp.argsort(flat_e, stable=True)
    e_sorted = flat_e[order]
    tok_sorted = flat_tok[order]
    gate_sorted = gates.reshape(A)[order]
    counts = jnp.bincount(flat_e, length=N_EXPERTS)
    padded = (counts + MOE_BLOCK - 1) // MOE_BLOCK * MOE_BLOCK
    pad_end = jnp.cumsum(padded)
    pad_start = pad_end - padded
    start = jnp.cumsum(counts) - counts
    dest = pad_start[e_sorted] + jnp.arange(A) - start[e_sorted]
    n_blocks = -(-A // MOE_BLOCK) + N_EXPERTS
    rows = jnp.full((n_blocks * MOE_BLOCK,), M, jnp.int32).at[dest].set(tok_sorted)
    block_e = jnp.minimum(jnp.searchsorted(pad_end, jnp.arange(n_blocks) * MOE_BLOCK, side='right'), N_EXPERTS - 1)
    x_pad = jnp.concatenate([xf, jnp.zeros((1, D), xf.dtype)], 0)

    def expert_block(args):
        r, e = args
        h = x_pad[r] @ w_gu[e] + b_gu[e]
        gate = jnp.minimum(h[:, :D_EXPERT], SWIGLU_LIMIT)
        up = jnp.clip(h[:, D_EXPERT:], -SWIGLU_LIMIT, SWIGLU_LIMIT)
        act = (up + 1.0) * gate * jax.nn.sigmoid(SWIGLU_ALPHA * gate)
        return act @ w_down[e] + b_down[e]

    out = lax.map(expert_block, (rows.reshape(n_blocks, MOE_BLOCK), block_e)).reshape(-1, D)
    y = jnp.zeros((M, D), x.dtype).at[tok_sorted].add(out[dest] * gate_sorted[:, None].astype(x.dtype))
    return y.reshape(B, T, D)


def decoder_layer(x, pos, attend, conv_state, ssm_state, w_in, q_norm_w, kv_norm_w, w_uq, w_uk, w_uv,
                  conv_w, a_log, dt_bias, gdn_norm_w, w_o, ln1_g, ln1_b, w_router, b_router,
                  w_gu, b_gu, w_down, b_down, ln2_g, ln2_b):
    B, T, _ = x.shape
    q_lat, kv_lat, k_r, qkv, z, a, b, g_a, g_b = split_in_proj(x, w_in)
    q_abs, q_rope, c, kr = mla_project(q_lat, kv_lat, k_r, pos, q_norm_w, kv_norm_w, w_uq, w_uk)
    o_mla = jnp.einsum('bthc,chv->bthv', attend(q_abs, q_rope, c, kr), w_uv).reshape(B, T, D_MODEL)
    o_gdn, new_conv, new_ssm = gdn_branch(qkv, z, a, b, conv_state, ssm_state, conv_w, a_log, dt_bias, gdn_norm_w)
    mix = jax.nn.sigmoid(g_a) * o_mla + jax.nn.sigmoid(g_b) * o_gdn
    x = layer_norm(DEEP_ALPHA * x + jnp.einsum('btd,de->bte', mix, w_o), ln1_g, ln1_b)
    x = layer_norm(DEEP_ALPHA * x + moe_ffn(x, w_router, b_router, w_gu, b_gu, w_down, b_down), ln2_g, ln2_b)
    return x, c, kr, new_conv, new_ssm


def setup_inputs(seed: int = 0) -> dict:
    key = jax.random.key(seed)
    ks = iter(jax.random.split(key, 40))
    f32 = jnp.float32

    def nrm(shape, scale):
        return jax.random.normal(next(ks), shape, f32) * scale

    def gain(shape):
        return 1.0 + nrm(shape, 0.01)

    L = DEPTH
    n_pages = PAST_LEN // PAGE_SIZE
    n_used = DEC_BATCH * n_pages
    n_pool = n_used + n_used // 4
    page_table = jax.random.permutation(next(ks), n_pool)[:n_used].reshape(DEC_BATCH, n_pages).astype(jnp.int32)
    dt = jnp.exp(jax.random.uniform(next(ks), (L, GDN_HEADS), f32, math.log(1e-3), math.log(1e-1)))
    a_log = jnp.log(jax.random.uniform(next(ks), (L, GDN_HEADS), f32, 1.0, 16.0))
    return {
        'x_prompt': nrm((BATCH, SEQ, D_MODEL), 1.0),
        'x_sample': nrm((DEC_BATCH, DEC_SEQ, D_MODEL), 1.0),
        'cache_ckv': nrm((L, n_pool, PAGE_SIZE, KV_LORA), 1.0),
        'cache_krope': nrm((L, n_pool, PAGE_SIZE, QK_ROPE), 1.0),
        'page_table': page_table,
        'state_conv': nrm((L, DEC_BATCH, CONV_W - 1, GDN_CONV_CH), 1.0),
        'state_ssm': nrm((L, DEC_BATCH, GDN_HEADS, GDN_DK, GDN_DV), 0.1),
        'w_in': nrm((L, D_MODEL, D_IN), D_MODEL ** -0.5),
        'q_norm_w': gain((L, Q_LORA)),
        'kv_norm_w': gain((L, KV_LORA)),
        'w_uq': nrm((L, Q_LORA, MLA_HEADS, QK_NOPE + QK_ROPE), Q_LORA ** -0.5),
        'w_uk': nrm((L, KV_LORA, MLA_HEADS, QK_NOPE), KV_LORA ** -0.5),
        'w_uv': nrm((L, KV_LORA, MLA_HEADS, V_HEAD), KV_LORA ** -0.5 * DEEP_BETA),
        'conv_w': nrm((L, CONV_W, GDN_CONV_CH), CONV_W ** -0.5),
        'a_log': a_log,
        'dt_bias': dt + jnp.log(-jnp.expm1(-dt)),
        'gdn_norm_w': gain((L, GDN_DV)),
        'w_o': nrm((L, D_MODEL, D_MODEL), D_MODEL ** -0.5 * DEEP_BETA),
        'ln1_g': gain((L, D_MODEL)),
        'ln1_b': nrm((L, D_MODEL), 0.01),
        'w_router': nrm((L, D_MODEL, N_EXPERTS), D_MODEL ** -0.5),
        'b_router': nrm((L, N_EXPERTS), 0.01),
        'w_gu': nrm((L, N_EXPERTS, D_MODEL, 2 * D_EXPERT), D_MODEL ** -0.5),
        'b_gu': nrm((L, N_EXPERTS, 2 * D_EXPERT), 0.01),
        'w_down': nrm((L, N_EXPERTS, D_EXPERT, D_MODEL), D_EXPERT ** -0.5 * DEEP_BETA),
        'b_down': nrm((L, N_EXPERTS, D_MODEL), 0.01),
        'ln2_g': gain((L, D_MODEL)),
        'ln2_b': nrm((L, D_MODEL), 0.01),
    }


def reference(x_prompt, x_sample, cache_ckv, cache_krope, page_table, state_conv, state_ssm,
              w_in, q_norm_w, kv_norm_w, w_uq, w_uk, w_uv, conv_w, a_log, dt_bias, gdn_norm_w,
              w_o, ln1_g, ln1_b, w_router, b_router, w_gu, b_gu, w_down, b_down, ln2_g, ln2_b):
    Bp, Tp, _ = x_prompt.shape
    Bs, Ts, _ = x_sample.shape
    pos_p = jnp.arange(Tp, dtype=jnp.int32)
    pos_s = PAST_LEN + jnp.arange(Ts, dtype=jnp.int32)
    past = page_table.shape[1] * PAGE_SIZE
    hp, hs = x_prompt, x_sample
    ckv_p, kr_p, conv_p, ssm_p = [], [], [], []
    ckv_s, kr_s, conv_s, ssm_s = [], [], [], []
    for l in range(DEPTH):
        lw = (w_in[l], q_norm_w[l], kv_norm_w[l], w_uq[l], w_uk[l], w_uv[l], conv_w[l], a_log[l],
              dt_bias[l], gdn_norm_w[l], w_o[l], ln1_g[l], ln1_b[l], w_router[l], b_router[l],
              w_gu[l], b_gu[l], w_down[l], b_down[l], ln2_g[l], ln2_b[l])
        hp, c, kr, cv, st = decoder_layer(
            hp, pos_p, mla_attend_prompt,
            jnp.zeros((Bp, CONV_W - 1, GDN_CONV_CH), x_prompt.dtype),
            jnp.zeros((Bp, GDN_HEADS, GDN_DK, GDN_DV), x_prompt.dtype), *lw)
        ckv_p.append(c)
        kr_p.append(kr)
        conv_p.append(cv)
        ssm_p.append(st)
        c_past = cache_ckv[l][page_table].reshape(Bs, past, KV_LORA)
        kr_past = cache_krope[l][page_table].reshape(Bs, past, QK_ROPE)
        attend_s = functools.partial(mla_attend_sample, c_past=c_past, kr_past=kr_past)
        hs, c, kr, cv, st = decoder_layer(hs, pos_s, attend_s, state_conv[l], state_ssm[l], *lw)
        ckv_s.append(c)
        kr_s.append(kr)
        conv_s.append(cv)
        ssm_s.append(st)
    return (hp, hs, jnp.stack(ckv_p), jnp.stack(kr_p), jnp.stack(conv_p), jnp.stack(ssm_p),
            jnp.stack(ckv_s), jnp.stack(kr_s), jnp.stack(conv_s), jnp.stack(ssm_s))
```

```python
import functools
import math

import jax
import jax.numpy as jnp
from jax import lax
from jax.experimental import pallas as pl
from jax.experimental.pallas import tpu as pltpu

F32 = jnp.float32
BF16 = jnp.bfloat16

ROPE_THETA = 10000.0
NORM_EPS = 1e-6
TOP_K = 4
SWIGLU_LIMIT = 7.0
SWIGLU_ALPHA = 1.702
GDN_CHUNK = 64
GDN_DIAG = 16
MOE_ROWS = 256
LANE = 128
SUBLANE = 8
VMEM_LIMIT = 56 * 1024 * 1024
NEG = -1e30
LOG2E = 1.4426950408889634


def _cparams(sem, vmem=None):
    return pltpu.CompilerParams(dimension_semantics=sem, vmem_limit_bytes=vmem)


def _dot(a, b, **kw):
    return jnp.dot(a, b, preferred_element_type=F32, **kw)


def _dot_nt(a, b, **kw):
    return lax.dot_general(a, b, (((1,), (1,)), ((), ())), preferred_element_type=F32, **kw)


def _sigmoid(x):
    return 1.0 / (1.0 + jnp.exp(-x))


def _layer_norm(v, g, b):
    mu = jnp.mean(v, -1, keepdims=True)
    vc = v - mu
    var = jnp.mean(vc * vc, -1, keepdims=True)
    return vc * lax.rsqrt(var + NORM_EPS) * g + b


def _inproj_kernel(x_ref, w_ref, o_ref, xb_ref):
    @pl.when(pl.program_id(1) == 0)
    def _():
        xb_ref[...] = x_ref[...].astype(BF16)

    o_ref[...] = _dot(xb_ref[...], w_ref[...])


def _in_proj(x, w, tm, tn):
    m, k = x.shape
    n = w.shape[1]
    return pl.pallas_call(
        _inproj_kernel,
        grid=(m // tm, n // tn),
        in_specs=[pl.BlockSpec((tm, k), lambda i, j: (i, 0)),
                  pl.BlockSpec((k, tn), lambda i, j: (0, j))],
        out_specs=pl.BlockSpec((tm, tn), lambda i, j: (i, j)),
        out_shape=jax.ShapeDtypeStruct((m, n), F32),
        scratch_shapes=[pltpu.VMEM((tm, k), BF16)],
        compiler_params=_cparams(("parallel", "arbitrary"), VMEM_LIMIT),
        name="in_proj",
    )(x, w)


def _mla_proj_kernel(h_ref, cos_ref, sin_ref, qnw_ref, kvnw_ref, wq_ref, wuk_ref,
                     qf_ref, c_ref, kr_ref, kf_ref, *, n_heads, q_lora, kv_lora, nope, rope, scale):
    hs = h_ref[...]
    cos = cos_ref[...]
    sin = sin_ref[...]
    q_lat = hs[:, :q_lora]
    qn = q_lat * lax.rsqrt(jnp.mean(q_lat * q_lat, -1, keepdims=True) + NORM_EPS) * qnw_ref[...]
    qn = qn.astype(BF16)
    kv = hs[:, q_lora:q_lora + kv_lora]
    c = kv * lax.rsqrt(jnp.mean(kv * kv, -1, keepdims=True) + NORM_EPS) * kvnw_ref[...]
    o = q_lora + kv_lora
    kr = hs[:, o:o + LANE] * cos + hs[:, o + LANE:o + 2 * LANE] * sin
    c_ref[...] = c
    kr_ref[...] = kr[:, :rope]
    kf_ref[:, :kv_lora] = c.astype(BF16)
    kf_ref[:, kv_lora:] = kr.astype(BF16)
    for h in range(n_heads):
        qh = _dot(qn, wq_ref[h])
        qa = _dot(qh[:, :nope].astype(BF16), wuk_ref[h])
        qr = qh[:, nope:nope + LANE] * cos + qh[:, nope + LANE:nope + 2 * LANE] * sin
        qf_ref[h, :, :kv_lora] = (qa * scale).astype(BF16)
        qf_ref[h, :, kv_lora:] = (qr * scale).astype(BF16)


def _mla_proj(h, col_blk, cos_t, sin_t, qnw, kvnw, wq, wuk, *, tm, q_lora, kv_lora, nope, rope, scale):
    m = h.shape[0]
    n_heads = wq.shape[0]
    wcol = q_lora + kv_lora + 3 * LANE
    kw = kv_lora + LANE
    kern = functools.partial(_mla_proj_kernel, n_heads=n_heads, q_lora=q_lora, kv_lora=kv_lora,
                             nope=nope, rope=rope, scale=scale)
    return pl.pallas_call(
        kern,
        grid=(m // tm,),
        in_specs=[pl.BlockSpec((tm, wcol), lambda i: (i, col_blk)),
                  pl.BlockSpec((tm, LANE), lambda i: (i, 0)),
                  pl.BlockSpec((tm, LANE), lambda i: (i, 0)),
                  pl.BlockSpec((1, q_lora), lambda i: (0, 0)),
                  pl.BlockSpec((1, kv_lora), lambda i: (0, 0)),
                  pl.BlockSpec(wq.shape, lambda i: (0, 0, 0)),
                  pl.BlockSpec(wuk.shape, lambda i: (0, 0, 0))],
        out_specs=[pl.BlockSpec((n_heads, tm, kw), lambda i: (0, i, 0)),
                   pl.BlockSpec((tm, kv_lora), lambda i: (i, 0)),
                   pl.BlockSpec((tm, rope), lambda i: (i, 0)),
                   pl.BlockSpec((tm, kw), lambda i: (i, 0))],
        out_shape=[jax.ShapeDtypeStruct((n_heads, m, kw), BF16),
                   jax.ShapeDtypeStruct((m, kv_lora), F32),
                   jax.ShapeDtypeStruct((m, rope), F32),
                   jax.ShapeDtypeStruct((m, kw), BF16)],
        compiler_params=_cparams(("parallel",), VMEM_LIMIT),
        name="mla_proj",
    )(h, cos_t, sin_t, qnw, kvnw, wq, wuk)


def _softmax_step(s, v, m_ref, l_ref, acc_ref):
    m_prev = m_ref[...]
    m_new = jnp.maximum(m_prev, jnp.max(s, -1, keepdims=True))
    alpha = jnp.exp2(m_prev - m_new)
    p = jnp.exp2(s - m_new)
    l_ref[...] = alpha * l_ref[...] + jnp.sum(p, -1, keepdims=True)
    acc_ref[...] = alpha * acc_ref[...] + _dot(p.astype(BF16), v)
    m_ref[...] = m_new


def _attn_prompt_kernel(qf_ref, kf_ref, wuv_ref, o_ref, m_ref, l_ref, acc_ref, *, n_heads, tq, dv, vh):
    i = pl.program_id(1)
    rows = n_heads * tq
    q = qf_ref[...].reshape(rows, qf_ref.shape[-1])
    m_ref[...] = jnp.full(m_ref.shape, NEG, F32)
    l_ref[...] = jnp.zeros(l_ref.shape, F32)
    acc_ref[...] = jnp.zeros(acc_ref.shape, F32)

    def step(j, masked):
        k = kf_ref[pl.ds(pl.multiple_of(j * tq, tq), tq), :]
        s = _dot_nt(q, k)
        if masked:
            row = lax.broadcasted_iota(jnp.int32, (tq, tq), 0)
            col = lax.broadcasted_iota(jnp.int32, (tq, tq), 1)
            s = jnp.where((col <= row)[None], s.reshape(n_heads, tq, tq), NEG).reshape(rows, tq)
        _softmax_step(s, k[:, :dv], m_ref, l_ref, acc_ref)

    def body(j, carry):
        step(j, False)
        return carry

    lax.fori_loop(0, i, body, 0)
    step(i, True)
    o = acc_ref[...] / l_ref[...]
    for h in range(n_heads):
        oh = o[h * tq:(h + 1) * tq].astype(BF16)
        o_ref[:, h * vh:(h + 1) * vh] = _dot(oh, wuv_ref[h])


def _attn_prompt(qf, kf, wuv, *, batch, seq, tq):
    n_heads, _, kw = qf.shape
    dv, vh = wuv.shape[1], wuv.shape[2]
    nq = seq // tq
    rows = n_heads * tq
    kern = functools.partial(_attn_prompt_kernel, n_heads=n_heads, tq=tq, dv=dv, vh=vh)
    return pl.pallas_call(
        kern,
        grid=(batch, nq),
        in_specs=[pl.BlockSpec((n_heads, tq, kw), lambda b, i: (0, b * nq + i, 0)),
                  pl.BlockSpec((seq, kw), lambda b, i: (b, 0)),
                  pl.BlockSpec(wuv.shape, lambda b, i: (0, 0, 0))],
        out_specs=pl.BlockSpec((tq, n_heads * vh), lambda b, i: (b * nq + i, 0)),
        out_shape=jax.ShapeDtypeStruct((batch * seq, n_heads * vh), F32),
        scratch_shapes=[pltpu.VMEM((rows, 1), F32), pltpu.VMEM((rows, 1), F32),
                        pltpu.VMEM((rows, dv), F32)],
        compiler_params=_cparams(("parallel", "arbitrary"), VMEM_LIMIT),
        name="attn_prompt",
    )(qf, kf, wuv)


def _attn_sample_kernel(pt_ref, q_ref, cn_ref, krn_ref, *rest, pp, page, n_heads, ts, dv, dr, vh):
    c_pages = rest[:pp]
    r_pages = rest[pp:2 * pp]
    wuv_ref = rest[2 * pp]
    o_ref = rest[2 * pp + 1]
    kbuf, knew, m_ref, l_ref, acc_ref = rest[2 * pp + 2:]
    g = pl.program_id(1)
    rows = n_heads * ts
    kw = kbuf.shape[1]

    @pl.when(g == 0)
    def _():
        m_ref[...] = jnp.full(m_ref.shape, NEG, F32)
        l_ref[...] = jnp.zeros(l_ref.shape, F32)
        acc_ref[...] = jnp.zeros(acc_ref.shape, F32)
        kbuf[:, dv:] = jnp.zeros((kbuf.shape[0], kw - dv), BF16)

    for u in range(pp):
        kbuf[u * page:(u + 1) * page, :dv] = c_pages[u][0].astype(BF16)
        kbuf[u * page:(u + 1) * page, dv:dv + dr] = r_pages[u][0].astype(BF16)
    q = q_ref[0]
    k = kbuf[...]
    _softmax_step(_dot_nt(q, k), k[:, :dv], m_ref, l_ref, acc_ref)

    @pl.when(g == pl.num_programs(1) - 1)
    def _():
        knew[...] = jnp.zeros(knew.shape, F32)
        knew[:ts, :dv] = cn_ref[...]
        knew[:ts, dv:dv + dr] = krn_ref[...]
        kn = knew[...].astype(BF16)
        nk = kn.shape[0]
        row = lax.broadcasted_iota(jnp.int32, (ts, nk), 0)
        col = lax.broadcasted_iota(jnp.int32, (ts, nk), 1)
        s = _dot_nt(q, kn)
        s = jnp.where((col <= row)[None], s.reshape(n_heads, ts, nk), NEG).reshape(rows, nk)
        _softmax_step(s, kn[:, :dv], m_ref, l_ref, acc_ref)
        o = acc_ref[...] / l_ref[...]
        for h in range(n_heads):
            oh = o[h * ts:(h + 1) * ts].astype(BF16)
            o_ref[:, h * vh:(h + 1) * vh] = _dot(oh, wuv_ref[h])


def _attn_sample(page_table, qs, c_all, kr_all, row_blk_off, cache_c, cache_r, wuv, *, pp):
    bs, rows, kw = qs.shape
    n_heads, dv, vh = wuv.shape
    ts = rows // n_heads
    n_pages = page_table.shape[1]
    page = cache_c.shape[1]
    dr = cache_r.shape[2]
    kern = functools.partial(_attn_sample_kernel, pp=pp, page=page, n_heads=n_heads, ts=ts,
                             dv=dv, dr=dr, vh=vh)

    def page_map(u):
        return lambda b, g, pt: (pt[b, g * pp + u], 0, 0)

    in_specs = [pl.BlockSpec((1, rows, kw), lambda b, g, pt: (b, 0, 0)),
                pl.BlockSpec((ts, dv), lambda b, g, pt: (row_blk_off + b, 0)),
                pl.BlockSpec((ts, dr), lambda b, g, pt: (row_blk_off + b, 0))]
    in_specs += [pl.BlockSpec((1, page, dv), page_map(u)) for u in range(pp)]
    in_specs += [pl.BlockSpec((1, page, dr), page_map(u)) for u in range(pp)]
    in_specs += [pl.BlockSpec(wuv.shape, lambda b, g, pt: (0, 0, 0))]
    grid_spec = pltpu.PrefetchScalarGridSpec(
        num_scalar_prefetch=1,
        grid=(bs, n_pages // pp),
        in_specs=in_specs,
        out_specs=pl.BlockSpec((ts, n_heads * vh), lambda b, g, pt: (b, 0)),
        scratch_shapes=[pltpu.VMEM((pp * page, kw), BF16), pltpu.VMEM((LANE, kw), F32),
                        pltpu.VMEM((rows, 1), F32), pltpu.VMEM((rows, 1), F32),
                        pltpu.VMEM((rows, dv), F32)])
    return pl.pallas_call(
        kern,
        grid_spec=grid_spec,
        out_shape=jax.ShapeDtypeStruct((bs * ts, n_heads * vh), F32),
        compiler_params=_cparams(("parallel", "arbitrary"), VMEM_LIMIT),
        name="attn_sample",
    )(page_table, qs, c_all, kr_all, *([cache_c] * pp), *([cache_r] * pp), wuv)


def _neumann_inv(low, eye, steps):
    p = eye - low
    x = low
    for _ in range(steps):
        x = _dot(x, x)
        p = p + _dot(p, x)
    return p


def _gdn_kernel(qkv_ref, ab_ref, z_ref, cs_ref, s0_ref, cw_ref, alog_ref, dtb_ref, nw_ref,
                o_ref, sfin_ref, ncv_ref, xp_ref, s_ref, *, chunk, n_heads, dk, dv, conv_w):
    n = pl.program_id(1)
    c = chunk
    hi = lax.Precision.HIGHEST

    @pl.when(n == 0)
    def _():
        s_ref[...] = s0_ref[0]
        xp_ref[0:SUBLANE, :] = cs_ref[0]

    xp_ref[SUBLANE:SUBLANE + c, :] = qkv_ref[...]
    cw = cw_ref[...]
    conv = xp_ref[pl.ds(SUBLANE - (conv_w - 1), c), :] * cw[0:1, :]
    for j in range(1, conv_w):
        conv = conv + xp_ref[pl.ds(SUBLANE - (conv_w - 1) + j, c), :] * cw[j:j + 1, :]
    tail = xp_ref[c:c + SUBLANE, :]
    xp_ref[0:SUBLANE, :] = tail
    act = conv * _sigmoid(conv)

    ab = ab_ref[...]
    apb = ab + dtb_ref[...]
    softplus = jnp.maximum(apb, 0.0) + jnp.log(1.0 + jnp.exp(-jnp.abs(apb)))
    gfull = -jnp.exp(alog_ref[...]) * softplus
    betaf = _sigmoid(ab)

    r_i = lax.broadcasted_iota(jnp.int32, (c, c), 0)
    c_i = lax.broadcasted_iota(jnp.int32, (c, c), 1)
    causal = c_i <= r_i
    strict = c_i < r_i
    eye_c = (r_i == c_i).astype(F32)
    db = min(GDN_DIAG, c)
    n_blk = c // db
    sh = int(math.log2(db))
    same_blk = jnp.right_shift(r_i, sh) == jnp.right_shift(c_i, sh)
    gc = _dot(causal.astype(F32), gfull, precision=hi)
    r_l = lax.broadcasted_iota(jnp.int32, (LANE, LANE), 0)
    c_l = lax.broadcasted_iota(jnp.int32, (LANE, LANE), 1)
    gc_t = _dot_nt((r_l == c_l).astype(F32), gc, precision=hi)

    hk = n_heads * dk
    for h in range(n_heads):
        qh = act[:, h * dk:(h + 1) * dk]
        kh = act[:, hk + h * dk:hk + (h + 1) * dk]
        vh = act[:, 2 * hk + h * dv:2 * hk + (h + 1) * dv]
        qh = qh * lax.rsqrt(jnp.sum(qh * qh, -1, keepdims=True) + NORM_EPS) * (dk ** -0.5)
        kh = kh * lax.rsqrt(jnp.sum(kh * kh, -1, keepdims=True) + NORM_EPS)
        beta = betaf[:, n_heads + h:n_heads + h + 1]
        gcol = gc[:, h:h + 1]
        grow = gc_t[h:h + 1, :]
        decay = jnp.where(causal, jnp.exp(jnp.where(causal, gcol - grow, 0.0)), 0.0)
        kb = kh * beta
        lower = jnp.where(strict, _dot_nt(kb, kh) * decay, 0.0)
        attn = _dot_nt(qh, kh) * decay
        l_d = jnp.where(same_blk, lower, 0.0)
        t_inv = _neumann_inv(l_d, eye_c, int(math.log2(db)) - 1)
        if n_blk > 1:
            m_inv = _neumann_inv(_dot(t_inv, lower - l_d), eye_c, int(math.log2(n_blk)) - 1)
            t_inv = _dot(m_inv, t_inv)
        egc = jnp.exp(gcol)
        uw = _dot(t_inv, jnp.concatenate([vh * beta, kb * egc], axis=1))
        s = s_ref[h]
        v_new = uw[:, :dv] - _dot(uw[:, dv:], s)
        o = _dot(qh * egc, s) + _dot(attn, v_new)
        glast = gc[c - 1:c, h:h + 1]
        kdec = kh * jnp.exp(glast - gcol)
        s_ref[h] = s * jnp.exp(glast) + lax.dot_general(
            kdec, v_new, (((0,), (0,)), ((), ())), preferred_element_type=F32)
        zh = z_ref[:, h * dv:(h + 1) * dv]
        on = o * lax.rsqrt(jnp.mean(o * o, -1, keepdims=True) + NORM_EPS) * nw_ref[...]
        o_ref[:, h * dv:(h + 1) * dv] = on * (zh * _sigmoid(zh))

    @pl.when(n == pl.num_programs(1) - 1)
    def _():
        sfin_ref[0] = s_ref[...]
        ncv_ref[0] = tail


def _gdn(h, row_blk_off, conv_state8, ssm_state, cw, alog_p, dtb_p, nw, *, batch, seq, chunk,
         qkv_blk, ab_blk, z_blk, conv_w):
    _, n_heads, dk, dv = ssm_state.shape
    nc = seq // chunk
    ch = cw.shape[1]
    kern = functools.partial(_gdn_kernel, chunk=chunk, n_heads=n_heads, dk=dk, dv=dv, conv_w=conv_w)
    row = lambda b, n: row_blk_off + b * nc + n
    return pl.pallas_call(
        kern,
        grid=(batch, nc),
        in_specs=[pl.BlockSpec((chunk, ch), lambda b, n: (row(b, n), qkv_blk)),
                  pl.BlockSpec((chunk, LANE), lambda b, n: (row(b, n), ab_blk)),
                  pl.BlockSpec((chunk, n_heads * dv), lambda b, n: (row(b, n), z_blk)),
                  pl.BlockSpec((1, SUBLANE, ch), lambda b, n: (b, 0, 0)),
                  pl.BlockSpec((1, n_heads, dk, dv), lambda b, n: (b, 0, 0, 0)),
                  pl.BlockSpec(cw.shape, lambda b, n: (0, 0)),
                  pl.BlockSpec((1, LANE), lambda b, n: (0, 0)),
                  pl.BlockSpec((1, LANE), lambda b, n: (0, 0)),
                  pl.BlockSpec((1, dv), lambda b, n: (0, 0))],
        out_specs=[pl.BlockSpec((chunk, n_heads * dv), lambda b, n: (b * nc + n, 0)),
                   pl.BlockSpec((1, n_heads, dk, dv), lambda b, n: (b, 0, 0, 0)),
                   pl.BlockSpec((1, SUBLANE, ch), lambda b, n: (b, 0, 0))],
        out_shape=[jax.ShapeDtypeStruct((batch * seq, n_heads * dv), F32),
                   jax.ShapeDtypeStruct(ssm_state.shape, F32),
                   jax.ShapeDtypeStruct((batch, SUBLANE, ch), F32)],
        scratch_shapes=[pltpu.VMEM((chunk + SUBLANE, ch), F32),
                        pltpu.VMEM((n_heads, dk, dv), F32)],
        compiler_params=_cparams(("parallel", "arbitrary"), VMEM_LIMIT),
        name="gdn",
    )(h, h, h, conv_state8, ssm_state, cw, alog_p, dtb_p, nw)


def _post_kernel(x_ref, om_ref, og_ref, ga_ref, gb_ref, wo_ref, g1_ref, b1_ref, wr_ref, br_ref,
                 x1_ref, idx_ref, gate_ref, *, alpha, top_k):
    mix = _sigmoid(ga_ref[...]) * om_ref[...] + _sigmoid(gb_ref[...]) * og_ref[...]
    y = _dot(mix.astype(BF16), wo_ref[...])
    x1 = _layer_norm(alpha * x_ref[...] + y, g1_ref[...], b1_ref[...])
    x1_ref[...] = x1
    logits = _dot_nt(wr_ref[...], x1, precision=lax.Precision.HIGHEST) + br_ref[...]
    n_exp, tm = logits.shape
    e_i = lax.broadcasted_iota(jnp.int32, (n_exp, tm), 0)
    vals, idxs = [], []
    for _ in range(top_k):
        mx = jnp.max(logits, axis=0, keepdims=True)
        ix = jnp.min(jnp.where(logits == mx, e_i, n_exp), axis=0, keepdims=True)
        vals.append(mx)
        idxs.append(ix)
        logits = jnp.where(e_i == ix, -jnp.inf, logits)
    es = [jnp.exp(v - vals[0]) for v in vals]
    tot = es[0]
    for e in es[1:]:
        tot = tot + e
    r_i = lax.broadcasted_iota(jnp.int32, (SUBLANE, tm), 0)
    idx_o = jnp.zeros((SUBLANE, tm), jnp.int32)
    gate_o = jnp.zeros((SUBLANE, tm), F32)
    for k in range(top_k):
        idx_o = jnp.where(r_i == k, idxs[k], idx_o)
        gate_o = jnp.where(r_i == k, es[k] / tot, gate_o)
    idx_ref[...] = idx_o
    gate_ref[...] = gate_o


def _post(x, om, og, h, ga_blk, gb_blk, wo, g1, b1, wr_t, br, *, tm, alpha):
    m, d = x.shape
    n_exp = wr_t.shape[0]
    kern = functools.partial(_post_kernel, alpha=alpha, top_k=TOP_K)
    row = lambda i: (i, 0)
    const = lambda i: (0, 0)
    return pl.pallas_call(
        kern,
        grid=(m // tm,),
        in_specs=[pl.BlockSpec((tm, d), row), pl.BlockSpec((tm, d), row), pl.BlockSpec((tm, d), row),
                  pl.BlockSpec((tm, d), lambda i: (i, ga_blk)),
                  pl.BlockSpec((tm, d), lambda i: (i, gb_blk)),
                  pl.BlockSpec(wo.shape, const), pl.BlockSpec((1, d), const), pl.BlockSpec((1, d), const),
                  pl.BlockSpec(wr_t.shape, const), pl.BlockSpec((n_exp, 1), const)],
        out_specs=[pl.BlockSpec((tm, d), row),
                   pl.BlockSpec((SUBLANE, tm), lambda i: (0, i)),
                   pl.BlockSpec((SUBLANE, tm), lambda i: (0, i))],
        out_shape=[jax.ShapeDtypeStruct((m, d), F32),
                   jax.ShapeDtypeStruct((SUBLANE, m), jnp.int32),
                   jax.ShapeDtypeStruct((SUBLANE, m), F32)],
        compiler_params=_cparams(("parallel",), VMEM_LIMIT),
        name="post_mix",
    )(x, om, og, h, h, wo, g1, b1, wr_t, br)


def _moe_kernel(be_ref, rows_ref, nused_ref, x_hbm, wgu_ref, bgu_ref, wd_ref, bd_ref, o_ref,
                xbuf, sem, wgu_bf, wd_bf, *, blk, d_exp):
    i = pl.program_id(0)
    n_used = nused_ref[0]
    slot = i % 2

    def row_copy(tok, s, r):
        return pltpu.make_async_copy(x_hbm.at[pl.ds(tok, 1)], xbuf.at[s, pl.ds(r, 1)], sem.at[s])

    def issue(b, s):
        def body(r, carry):
            row_copy(rows_ref[b * blk + r], s, r).start()
            return carry
        lax.fori_loop(0, blk, body, 0)

    @pl.when(jnp.logical_and(i == 0, n_used > 0))
    def _():
        issue(0, 0)

    @pl.when(i + 1 < n_used)
    def _():
        issue(i + 1, 1 - slot)

    @pl.when(i < n_used)
    def _():
        e = be_ref[i]
        e_prev = be_ref[jnp.maximum(i - 1, 0)]

        @pl.when(jnp.logical_or(i == 0, e != e_prev))
        def _():
            wgu_bf[...] = wgu_ref[0].astype(BF16)
            wd_bf[...] = wd_ref[0].astype(BF16)

        pltpu.make_async_copy(x_hbm.at[pl.ds(0, blk)], xbuf.at[slot], sem.at[slot]).wait()
        xb = xbuf[slot].astype(BF16)
        hh = _dot(xb, wgu_bf[...]) + bgu_ref[0]
        gate = jnp.minimum(hh[:, :d_exp], SWIGLU_LIMIT)
        up = jnp.clip(hh[:, d_exp:], -SWIGLU_LIMIT, SWIGLU_LIMIT)
        act = (up + 1.0) * gate * _sigmoid(SWIGLU_ALPHA * gate)
        o_ref[...] = _dot(act.astype(BF16), wd_bf[...]) + bd_ref[0]

    @pl.when(i >= n_used)
    def _():
        o_ref[...] = jnp.zeros(o_ref.shape, F32)


def _moe_experts(block_e, rows, n_used, x1, w_gu, b_gu, w_down, b_down, *, blk):
    n_exp, d, d2 = w_gu.shape
    d_exp = d2 // 2
    nb = block_e.shape[0]
    kern = functools.partial(_moe_kernel, blk=blk, d_exp=d_exp)
    grid_spec = pltpu.PrefetchScalarGridSpec(
        num_scalar_prefetch=3,
        grid=(nb,),
        in_specs=[pl.BlockSpec(memory_space=pl.ANY),
                  pl.BlockSpec((1, d, d2), lambda i, be, rw, nu: (be[i], 0, 0)),
                  pl.BlockSpec((1, 1, d2), lambda i, be, rw, nu: (be[i], 0, 0)),
                  pl.BlockSpec((1, d_exp, d), lambda i, be, rw, nu: (be[i], 0, 0)),
                  pl.BlockSpec((1, 1, d), lambda i, be, rw, nu: (be[i], 0, 0))],
        out_specs=pl.BlockSpec((blk, d), lambda i, be, rw, nu: (i, 0)),
        scratch_shapes=[pltpu.VMEM((2, blk, d), F32), pltpu.SemaphoreType.DMA((2,)),
                        pltpu.VMEM((d, d2), BF16), pltpu.VMEM((d_exp, d), BF16)])
    return pl.pallas_call(
        kern,
        grid_spec=grid_spec,
        out_shape=jax.ShapeDtypeStruct((nb * blk, d), F32),
        compiler_params=_cparams(("arbitrary",), VMEM_LIMIT),
        name="moe_experts",
    )(block_e, rows, n_used, x1, w_gu, b_gu.reshape(n_exp, 1, d2), w_down, b_down.reshape(n_exp, 1, d))


def _combine_kernel(dest_ref, x1_ref, gate_ref, ys_hbm, g2_ref, b2_ref, o_ref, ybuf, sem,
                    *, tm, top_k, alpha, m_total):
    i = pl.program_id(0)
    nsteps = pl.num_programs(0)
    slot = i % 2
    n_rows = top_k * tm

    def issue(b, s):
        def body(r, carry):
            k = r // tm
            t = r - k * tm
            d = dest_ref[k * m_total + b * tm + t]
            pltpu.make_async_copy(ys_hbm.at[pl.ds(d, 1)], ybuf.at[s, pl.ds(r, 1)], sem.at[s]).start()
            return carry
        lax.fori_loop(0, n_rows, body, 0)

    @pl.when(i == 0)
    def _():
        issue(0, 0)

    @pl.when(i + 1 < nsteps)
    def _():
        issue(i + 1, 1 - slot)

    pltpu.make_async_copy(ys_hbm.at[pl.ds(0, n_rows)], ybuf.at[slot], sem.at[slot]).wait()
    g = gate_ref[...]
    y = g[:, 0:1] * ybuf[slot, 0:tm, :]
    for k in range(1, top_k):
        y = y + g[:, k:k + 1] * ybuf[slot, k * tm:(k + 1) * tm, :]
    o_ref[...] = _layer_norm(alpha * x1_ref[...] + y, g2_ref[...], b2_ref[...])


def _combine(dest_km, x1, gates_mk, ys, g2, b2, *, tm, alpha):
    m, d = x1.shape
    kern = functools.partial(_combine_kernel, tm=tm, top_k=TOP_K, alpha=alpha, m_total=m)
    grid_spec = pltpu.PrefetchScalarGridSpec(
        num_scalar_prefetch=1,
        grid=(m // tm,),
        in_specs=[pl.BlockSpec((tm, d), lambda i, ds: (i, 0)),
                  pl.BlockSpec((tm, SUBLANE), lambda i, ds: (i, 0)),
                  pl.BlockSpec(memory_space=pl.ANY),
                  pl.BlockSpec((1, d), lambda i, ds: (0, 0)),
                  pl.BlockSpec((1, d), lambda i, ds: (0, 0))],
        out_specs=pl.BlockSpec((tm, d), lambda i, ds: (i, 0)),
        scratch_shapes=[pltpu.VMEM((2, TOP_K * tm, d), F32), pltpu.SemaphoreType.DMA((2,))])
    return pl.pallas_call(
        kern,
        grid_spec=grid_spec,
        out_shape=jax.ShapeDtypeStruct((m, d), F32),
        compiler_params=_cparams(("arbitrary",), VMEM_LIMIT),
        name="moe_combine",
    )(dest_km, x1, gates_mk, ys, g2, b2)


def _pack_w_in(w_in, splits):
    q_lora, kv_lora, rope, conv_ch, gv, nh, _, d, _ = splits
    offs = [0]
    for s in splits:
        offs.append(offs[-1] + s)
    part = [w_in[:, offs[i]:offs[i + 1]] for i in range(len(splits))]
    q_lat, kv_lat, k_r, qkv, z, a, b, g_a, g_b = part
    dm = w_in.shape[0]
    half = rope // 2
    zpad = lambda n: jnp.zeros((dm, n), w_in.dtype)
    k_sw = jnp.concatenate([k_r[:, half:], k_r[:, :half]], axis=1)
    small = jnp.concatenate([q_lat, kv_lat, k_r, zpad(LANE - rope), k_sw, zpad(LANE - rope),
                             a, b, zpad(LANE - 2 * nh)], axis=1)
    return jnp.concatenate([qkv, small, z, g_a, g_b], axis=1).astype(BF16)


def _pack_w_uq(w_uq, nope, rope):
    w = jnp.transpose(w_uq, (1, 0, 2))
    half = rope // 2
    r = w[..., nope:]
    zp = jnp.zeros(r.shape[:-1] + (LANE - rope,), w.dtype)
    r_sw = jnp.concatenate([r[..., half:], r[..., :half]], axis=-1)
    return jnp.concatenate([w[..., :nope], r, zp, r_sw, zp], axis=-1).astype(BF16)


def _rope_tables(pos, rope):
    half = rope // 2
    inv = ROPE_THETA ** (-jnp.arange(half, dtype=F32) / half)
    ang = pos.astype(F32)[:, None] * inv[None, :]
    cos, sin = jnp.cos(ang), jnp.sin(ang)
    zp = jnp.zeros((pos.shape[0], LANE - rope), F32)
    return (jnp.concatenate([cos, cos, zp], axis=1), jnp.concatenate([-sin, sin, zp], axis=1))


def _route_meta(idx_t, m, n_exp, blk):
    a = m * TOP_K
    e_flat = idx_t[:TOP_K].T.reshape(a)
    onehot = (e_flat[:, None] == jnp.arange(n_exp, dtype=jnp.int32)[None, :]).astype(jnp.int32)
    csum = jnp.cumsum(onehot, axis=0)
    rank = jnp.take_along_axis(csum, e_flat[:, None], axis=1)[:, 0] - 1
    counts = csum[-1]
    padded = (counts + blk - 1) // blk * blk
    pad_end = jnp.cumsum(padded)
    pad_start = pad_end - padded
    dest = (pad_start[e_flat] + rank).astype(jnp.int32)
    nb = a // blk + n_exp
    tok = jnp.arange(a, dtype=jnp.int32) // TOP_K
    rows = jnp.zeros((nb * blk,), jnp.int32).at[dest].set(tok)
    block_e = jnp.minimum(jnp.searchsorted(pad_end, jnp.arange(nb, dtype=jnp.int32) * blk, side='right'),
                          n_exp - 1).astype(jnp.int32)
    n_used = (pad_end[-1] // blk).astype(jnp.int32).reshape(1)
    dest_km = dest.reshape(m, TOP_K).T.reshape(a)
    return block_e, rows, n_used, dest_km


def kernel(x_prompt, x_sample, cache_ckv, cache_krope, page_table, state_conv, state_ssm, w_in, q_norm_w, kv_norm_w, w_uq, w_uk, w_uv, conv_w, a_log, dt_bias, gdn_norm_w, w_o, ln1_g, ln1_b, w_router, b_router, w_gu, b_gu, w_down, b_down, ln2_g, ln2_b):
    bp, tp, d = x_prompt.shape
    bs, ts, _ = x_sample.shape
    depth = w_in.shape[0]
    q_lora, n_heads, qk = w_uq.shape[1:]
    kv_lora, _, nope = w_uk.shape[1:]
    rope = qk - nope
    vh = w_uv.shape[3]
    cw_taps, conv_ch = conv_w.shape[1:]
    g_heads = a_log.shape[1]
    dk, dv = state_ssm.shape[3:]
    n_exp = w_router.shape[2]
    page = cache_ckv.shape[2]
    past = page_table.shape[1] * page
    splits = (q_lora, kv_lora, rope, conv_ch, g_heads * dv, g_heads, g_heads, d, d)
    assert sum(splits) == w_in.shape[2]
    alpha = (2 * depth) ** 0.25
    scale = (nope + rope) ** -0.5 * LOG2E
    mp, ms = bp * tp, bs * ts
    m = mp + ms
    small_w = q_lora + kv_lora + 3 * LANE
    assert small_w == d and conv_ch % d == 0
    qkv_blk, small_blk = 0, conv_ch // d
    z_blk, ga_blk, gb_blk = small_blk + 1, small_blk + 2, small_blk + 3
    ab_blk = (conv_ch + q_lora + kv_lora + 2 * LANE) // LANE

    cos_p, sin_p = _rope_tables(jnp.arange(tp, dtype=jnp.int32), rope)
    cos_s, sin_s = _rope_tables(past + jnp.arange(ts, dtype=jnp.int32), rope)
    cos_t = jnp.concatenate([jnp.tile(cos_p, (bp, 1)), jnp.tile(cos_s, (bs, 1))], axis=0)
    sin_t = jnp.concatenate([jnp.tile(sin_p, (bp, 1)), jnp.tile(sin_s, (bs, 1))], axis=0)

    x = jnp.concatenate([x_prompt.reshape(mp, d), x_sample.reshape(ms, d)], axis=0)
    outs = {k: [] for k in ("ckv_p", "kr_p", "conv_p", "ssm_p", "ckv_s", "kr_s", "conv_s", "ssm_s")}
    pad_lanes = lambda v: jnp.pad(v, (0, LANE - v.shape[0])).reshape(1, LANE)
    for l in range(depth):
        w_pack = _pack_w_in(w_in[l], splits)
        wq = _pack_w_uq(w_uq[l], nope, rope)
        wuk = jnp.transpose(w_uk[l], (1, 2, 0)).astype(BF16)
        wuv = jnp.transpose(w_uv[l], (1, 0, 2)).astype(BF16)
        h = _in_proj(x, w_pack, tm=1024, tn=1024)
        qf, c_all, kr_all, kf = _mla_proj(
            h, small_blk, cos_t, sin_t, q_norm_w[l].reshape(1, q_lora), kv_norm_w[l].reshape(1, kv_lora),
            wq, wuk, tm=256, q_lora=q_lora, kv_lora=kv_lora, nope=nope, rope=rope, scale=scale)
        om_p = _attn_prompt(qf, kf, wuv, batch=bp, seq=tp, tq=256)
        qs = qf[:, mp:, :].reshape(n_heads, bs, ts, qf.shape[-1])
        qs = jnp.transpose(qs, (1, 0, 2, 3)).reshape(bs, n_heads * ts, qf.shape[-1])
        om_s = _attn_sample(page_table, qs, c_all, kr_all, mp // ts, cache_ckv[l], cache_krope[l], wuv, pp=8)

        alog_p = pad_lanes(a_log[l])
        dtb_p = pad_lanes(dt_bias[l])
        nw = gdn_norm_w[l].reshape(1, dv)
        gdn_kw = dict(qkv_blk=qkv_blk, ab_blk=ab_blk, z_blk=z_blk, conv_w=cw_taps)
        zeros_conv = jnp.zeros((bp, SUBLANE, conv_ch), F32)
        zeros_ssm = jnp.zeros((bp, g_heads, dk, dv), F32)
        og_p, ssm_p, ncv_p = _gdn(h, 0, zeros_conv, zeros_ssm, conv_w[l], alog_p, dtb_p, nw,
                                  batch=bp, seq=tp, chunk=min(GDN_CHUNK, tp), **gdn_kw)
        conv8_s = jnp.pad(state_conv[l], ((0, 0), (SUBLANE - (cw_taps - 1), 0), (0, 0)))
        og_s, ssm_s, ncv_s = _gdn(h, mp // ts, conv8_s, state_ssm[l], conv_w[l], alog_p, dtb_p, nw,
                                  batch=bs, seq=ts, chunk=ts, **gdn_kw)

        om = jnp.concatenate([om_p, om_s], axis=0)
        og = jnp.concatenate([og_p, og_s], axis=0)
        x1, idx_t, gate_t = _post(
            x, om, og, h, ga_blk, gb_blk, w_o[l].astype(BF16), ln1_g[l].reshape(1, d), ln1_b[l].reshape(1, d),
            w_router[l].T, b_router[l].reshape(n_exp, 1), tm=256, alpha=alpha)
        block_e, rows, n_used, dest_km = _route_meta(idx_t, m, n_exp, MOE_ROWS)
        ys = _moe_experts(block_e, rows, n_used, x1, w_gu[l], b_gu[l], w_down[l], b_down[l], blk=MOE_ROWS)
        x = _combine(dest_km, x1, gate_t.T, ys, ln2_g[l].reshape(1, d), ln2_b[l].reshape(1, d),
                     tm=128, alpha=alpha)

        outs["ckv_p"].append(c_all[:mp].reshape(bp, tp, kv_lora))
        outs["kr_p"].append(kr_all[:mp].reshape(bp, tp, rope))
        outs["conv_p"].append(ncv_p[:, SUBLANE - (cw_taps - 1):, :])
        outs["ssm_p"].append(ssm_p)
        outs["ckv_s"].append(c_all[mp:].reshape(bs, ts, kv_lora))
        outs["kr_s"].append(kr_all[mp:].reshape(bs, ts, rope))
        outs["conv_s"].append(ncv_s[:, SUBLANE - (cw_taps - 1):, :])
        outs["ssm_s"].append(ssm_s)

    return (x[:mp].reshape(bp, tp, d), x[mp:].reshape(bs, ts, d),
            jnp.stack(outs["ckv_p"]), jnp.stack(outs["kr_p"]), jnp.stack(outs["conv_p"]), jnp.stack(outs["ssm_p"]),
            jnp.stack(outs["ckv_s"]), jnp.stack(outs["kr_s"]), jnp.stack(outs["conv_s"]), jnp.stack(outs["ssm_s"]))
```

```python
import functools
import math

import jax
import jax.numpy as jnp
from jax import lax
from jax.experimental import pallas as pl
from jax.experimental.pallas import tpu as pltpu

F32 = jnp.float32
BF16 = jnp.bfloat16

ROPE_THETA = 10000.0
NORM_EPS = 1e-6
TOP_K = 4
SWIGLU_LIMIT = 7.0
SWIGLU_ALPHA = 1.702
GDN_CHUNK = 64
GDN_DIAG = 16
GDN_SAMPLE_GROUP = 8
MOE_ROWS = 256
PAGE_CHUNK = 8
LANE = 128
SUBLANE = 8
VMEM_LIMIT = 56 * 1024 * 1024
NEG = -1e30
LOG2E = 1.4426950408889634


def _cparams(sem, vmem=None):
    return pltpu.CompilerParams(dimension_semantics=sem, vmem_limit_bytes=vmem)


def _dot(a, b, **kw):
    return jnp.dot(a, b, preferred_element_type=F32, **kw)


def _dot_nt(a, b, **kw):
    return lax.dot_general(a, b, (((1,), (1,)), ((), ())), preferred_element_type=F32, **kw)


def _bdot(a, b):
    return lax.dot_general(a, b, (((2,), (1,)), ((0,), (0,))), preferred_element_type=F32)


def _bdot_nt(a, b):
    return lax.dot_general(a, b, (((2,), (2,)), ((0,), (0,))), preferred_element_type=F32)


def _bdot_tn(a, b):
    return lax.dot_general(a, b, (((1,), (1,)), ((0,), (0,))), preferred_element_type=F32)


def _sigmoid(x):
    return 1.0 / (1.0 + jnp.exp(-x))


def _layer_norm(v, g, b):
    mu = jnp.mean(v, -1, keepdims=True)
    vc = v - mu
    var = jnp.mean(vc * vc, -1, keepdims=True)
    return vc * lax.rsqrt(var + NORM_EPS) * g + b


def _inproj_kernel(x_ref, w_ref, o_ref, xb_ref):
    @pl.when(pl.program_id(1) == 0)
    def _():
        xb_ref[...] = x_ref[...].astype(BF16)

    o_ref[...] = _dot(xb_ref[...], w_ref[...])


def _in_proj(x, w, tm, tn):
    m, k = x.shape
    n = w.shape[1]
    return pl.pallas_call(
        _inproj_kernel,
        grid=(m // tm, n // tn),
        in_specs=[pl.BlockSpec((tm, k), lambda i, j: (i, 0)),
                  pl.BlockSpec((k, tn), lambda i, j: (0, j))],
        out_specs=pl.BlockSpec((tm, tn), lambda i, j: (i, j)),
        out_shape=jax.ShapeDtypeStruct((m, n), F32),
        scratch_shapes=[pltpu.VMEM((tm, k), BF16)],
        compiler_params=_cparams(("parallel", "arbitrary"), VMEM_LIMIT),
        name="in_proj",
    )(x, w)


def _mla_proj_kernel(h_ref, cos_ref, sin_ref, qnw_ref, kvnw_ref, wq_ref, wuk_ref,
                     qf_ref, c_ref, kr_ref, kf_ref, *, n_heads, q_lora, kv_lora, nope, rope, scale):
    hs = h_ref[...]
    cos = cos_ref[...]
    sin = sin_ref[...]
    q_lat = hs[:, :q_lora]
    qn = q_lat * lax.rsqrt(jnp.mean(q_lat * q_lat, -1, keepdims=True) + NORM_EPS) * qnw_ref[...]
    qn = qn.astype(BF16)
    kv = hs[:, q_lora:q_lora + kv_lora]
    c = kv * lax.rsqrt(jnp.mean(kv * kv, -1, keepdims=True) + NORM_EPS) * kvnw_ref[...]
    o = q_lora + kv_lora
    kr = hs[:, o:o + LANE] * cos + hs[:, o + LANE:o + 2 * LANE] * sin
    c_ref[...] = c
    kr_ref[...] = kr[:, :rope]
    kf_ref[:, :kv_lora] = c.astype(BF16)
    kf_ref[:, kv_lora:] = kr.astype(BF16)
    for h in range(n_heads):
        qh = _dot(qn, wq_ref[h])
        qa = _dot(qh[:, :nope].astype(BF16), wuk_ref[h])
        qr = qh[:, nope:nope + LANE] * cos + qh[:, nope + LANE:nope + 2 * LANE] * sin
        qf_ref[h, :, :kv_lora] = (qa * scale).astype(BF16)
        qf_ref[h, :, kv_lora:] = (qr * scale).astype(BF16)


def _mla_proj(h, col_blk, cos_t, sin_t, qnw, kvnw, wq, wuk, *, tm, q_lora, kv_lora, nope, rope, scale):
    m = h.shape[0]
    n_heads = wq.shape[0]
    wcol = q_lora + kv_lora + 3 * LANE
    kw = kv_lora + LANE
    kern = functools.partial(_mla_proj_kernel, n_heads=n_heads, q_lora=q_lora, kv_lora=kv_lora,
                             nope=nope, rope=rope, scale=scale)
    return pl.pallas_call(
        kern,
        grid=(m // tm,),
        in_specs=[pl.BlockSpec((tm, wcol), lambda i: (i, col_blk)),
                  pl.BlockSpec((tm, LANE), lambda i: (i, 0)),
                  pl.BlockSpec((tm, LANE), lambda i: (i, 0)),
                  pl.BlockSpec((1, q_lora), lambda i: (0, 0)),
                  pl.BlockSpec((1, kv_lora), lambda i: (0, 0)),
                  pl.BlockSpec(wq.shape, lambda i: (0, 0, 0)),
                  pl.BlockSpec(wuk.shape, lambda i: (0, 0, 0))],
        out_specs=[pl.BlockSpec((n_heads, tm, kw), lambda i: (0, i, 0)),
                   pl.BlockSpec((tm, kv_lora), lambda i: (i, 0)),
                   pl.BlockSpec((tm, rope), lambda i: (i, 0)),
                   pl.BlockSpec((tm, kw), lambda i: (i, 0))],
        out_shape=[jax.ShapeDtypeStruct((n_heads, m, kw), BF16),
                   jax.ShapeDtypeStruct((m, kv_lora), F32),
                   jax.ShapeDtypeStruct((m, rope), F32),
                   jax.ShapeDtypeStruct((m, kw), BF16)],
        compiler_params=_cparams(("parallel",), VMEM_LIMIT),
        name="mla_proj",
    )(h, cos_t, sin_t, qnw, kvnw, wq, wuk)


def _softmax_step(s, v, m_ref, l_ref, acc_ref):
    m_prev = m_ref[...]
    m_new = jnp.maximum(m_prev, jnp.max(s, -1, keepdims=True))
    alpha = jnp.exp2(m_prev - m_new)
    p = jnp.exp2(s - m_new)
    l_ref[...] = alpha * l_ref[...] + jnp.sum(p, -1, keepdims=True)
    acc_ref[...] = alpha * acc_ref[...] + _dot(p.astype(BF16), v)
    m_ref[...] = m_new


def _attn_prompt_kernel(qf_ref, kf_ref, wuv_ref, o_ref, m_ref, l_ref, acc_ref, *, n_heads, tq, tk, dv, vh, n_split):
    i = pl.program_id(1)
    m_ref[...] = jnp.full(m_ref.shape, NEG, F32)
    l_ref[...] = jnp.zeros(l_ref.shape, F32)
    acc_ref[...] = jnp.zeros(acc_ref.shape, F32)
    n_full = (i * tq) // tk
    off = i * tq - n_full * tk
    hs = n_heads // n_split
    rs = hs * tq

    def step(j, masked):
        k = kf_ref[pl.ds(pl.multiple_of(j * tk, tk), tk), :]
        for g in range(n_split):
            q = qf_ref[g * hs:(g + 1) * hs].reshape(rs, qf_ref.shape[-1])
            s = _dot_nt(q, k)
            if masked:
                row = lax.broadcasted_iota(jnp.int32, (tq, tk), 0)
                col = lax.broadcasted_iota(jnp.int32, (tq, tk), 1)
                s = jnp.where((col <= row + off)[None], s.reshape(hs, tq, tk), NEG).reshape(rs, tk)
            sl = slice(g * rs, (g + 1) * rs)
            _softmax_step(s, k[:, :dv], m_ref.at[sl], l_ref.at[sl], acc_ref.at[sl])

    def body(j, carry):
        step(j, False)
        return carry

    lax.fori_loop(0, n_full, body, 0)
    step(n_full, True)
    o = acc_ref[...] / l_ref[...]
    for h in range(n_heads):
        oh = o[h * tq:(h + 1) * tq].astype(BF16)
        o_ref[:, h * vh:(h + 1) * vh] = _dot(oh, wuv_ref[h])


def _attn_prompt(qf, kf, wuv, *, batch, seq, tq, tk):
    n_heads, _, kw = qf.shape
    dv, vh = wuv.shape[1], wuv.shape[2]
    nq = seq // tq
    rows = n_heads * tq
    assert seq % tk == 0 and tk % tq == 0
    kern = functools.partial(_attn_prompt_kernel, n_heads=n_heads, tq=tq, tk=tk, dv=dv, vh=vh, n_split=2)
    return pl.pallas_call(
        kern,
        grid=(batch, nq),
        in_specs=[pl.BlockSpec((n_heads, tq, kw), lambda b, i: (0, b * nq + i, 0)),
                  pl.BlockSpec((seq, kw), lambda b, i: (b, 0)),
                  pl.BlockSpec(wuv.shape, lambda b, i: (0, 0, 0))],
        out_specs=pl.BlockSpec((tq, n_heads * vh), lambda b, i: (b * nq + i, 0)),
        out_shape=jax.ShapeDtypeStruct((batch * seq, n_heads * vh), F32),
        scratch_shapes=[pltpu.VMEM((rows, 1), F32), pltpu.VMEM((rows, 1), F32),
                        pltpu.VMEM((rows, dv), F32)],
        compiler_params=_cparams(("parallel", "arbitrary"), VMEM_LIMIT),
        name="attn_prompt",
    )(qf, kf, wuv)


def _attn_sample_kernel(pt_ref, q_ref, cn_ref, krn_ref, cc_hbm, cr_hbm, wuv_ref, o_ref,
                        cbuf, rbuf, sem, *, n_pages, page, n_heads, ts, dv, dr, vh):
    b = pl.program_id(0)
    nb = pl.num_programs(0)
    slot = b % 2
    rows = n_heads * ts

    def fetch(bb, s):
        for p in range(n_pages):
            pg = pt_ref[bb, p]
            pltpu.make_async_copy(cc_hbm.at[pg], cbuf.at[s, p], sem.at[0, s]).start()
            pltpu.make_async_copy(cr_hbm.at[pg], rbuf.at[s, p], sem.at[1, s]).start()

    def wait(s):
        pltpu.make_async_copy(cc_hbm.at[pl.ds(0, n_pages)], cbuf.at[s], sem.at[0, s]).wait()
        pltpu.make_async_copy(cr_hbm.at[pl.ds(0, n_pages)], rbuf.at[s], sem.at[1, s]).wait()

    @pl.when(b == 0)
    def _():
        fetch(0, 0)

    fetch(jnp.minimum(b + 1, nb - 1), 1 - slot)
    wait(slot)

    q = q_ref[0].astype(F32)
    qc = q[:, :dv]
    qr = q[:, dv:dv + dr]
    n_chunks = n_pages // PAGE_CHUNK
    ck = PAGE_CHUNK * page
    parts = []
    for ch in range(n_chunks):
        c_ch = cbuf[slot, ch * PAGE_CHUNK:(ch + 1) * PAGE_CHUNK].reshape(ck, dv)
        r_ch = jnp.concatenate([rbuf[slot, ch * PAGE_CHUNK + u] for u in range(PAGE_CHUNK)], axis=1)
        parts.append(_dot_nt(qc, c_ch) + _dot(qr, r_ch))
    cn = jnp.concatenate([cn_ref[...], jnp.zeros((LANE - ts, dv), F32)], axis=0)
    krn = jnp.concatenate([krn_ref[...], jnp.zeros((LANE - ts, dr), F32)], axis=0)
    row = lax.broadcasted_iota(jnp.int32, (ts, LANE), 0)
    col = lax.broadcasted_iota(jnp.int32, (ts, LANE), 1)
    s_new = (_dot_nt(qc, cn) + _dot_nt(qr, krn)).reshape(n_heads, ts, LANE)
    s_new = jnp.where((col <= row)[None], s_new, NEG).reshape(rows, LANE)

    m = jnp.max(s_new, -1, keepdims=True)
    for s in parts:
        m = jnp.maximum(m, jnp.max(s, -1, keepdims=True))
    p_new = jnp.exp2(s_new - m)
    l = jnp.sum(p_new, -1, keepdims=True)
    acc = _dot(p_new, cn)
    for ch in range(n_chunks):
        p = jnp.exp2(parts[ch] - m)
        l = l + jnp.sum(p, -1, keepdims=True)
        acc = acc + _dot(p, cbuf[slot, ch * PAGE_CHUNK:(ch + 1) * PAGE_CHUNK].reshape(ck, dv))
    o = acc / l
    for h in range(n_heads):
        oh = o[h * ts:(h + 1) * ts].astype(BF16)
        o_ref[:, h * vh:(h + 1) * vh] = _dot(oh, wuv_ref[h])

    @pl.when(b == nb - 1)
    def _():
        wait(1 - slot)


def _attn_sample(page_table, qs, c_all, kr_all, row_blk_off, cache_c, cache_rt, wuv):
    bs, rows, kw = qs.shape
    n_heads, dv, vh = wuv.shape
    ts = rows // n_heads
    n_pages = page_table.shape[1]
    page = cache_c.shape[1]
    dr = cache_rt.shape[1]
    assert n_pages % PAGE_CHUNK == 0
    kern = functools.partial(_attn_sample_kernel, n_pages=n_pages, page=page, n_heads=n_heads, ts=ts,
                             dv=dv, dr=dr, vh=vh)
    grid_spec = pltpu.PrefetchScalarGridSpec(
        num_scalar_prefetch=1,
        grid=(bs,),
        in_specs=[pl.BlockSpec((1, rows, kw), lambda b, pt: (b, 0, 0)),
                  pl.BlockSpec((ts, dv), lambda b, pt: (row_blk_off + b, 0)),
                  pl.BlockSpec((ts, dr), lambda b, pt: (row_blk_off + b, 0)),
                  pl.BlockSpec(memory_space=pl.ANY),
                  pl.BlockSpec(memory_space=pl.ANY),
                  pl.BlockSpec(wuv.shape, lambda b, pt: (0, 0, 0))],
        out_specs=pl.BlockSpec((ts, n_heads * vh), lambda b, pt: (b, 0)),
        scratch_shapes=[pltpu.VMEM((2, n_pages, page, dv), F32), pltpu.VMEM((2, n_pages, dr, page), F32),
                        pltpu.SemaphoreType.DMA((2, 2))])
    return pl.pallas_call(
        kern,
        grid_spec=grid_spec,
        out_shape=jax.ShapeDtypeStruct((bs * ts, n_heads * vh), F32),
        compiler_params=_cparams(("arbitrary",), VMEM_LIMIT),
        name="attn_sample",
    )(page_table, qs, c_all, kr_all, cache_c, cache_rt, wuv)


def _neumann_inv(low, eye, steps):
    p = eye - low
    x = low
    for _ in range(steps):
        x = _bdot(x, x)
        p = p + _bdot(p, x)
    return p


def _gdn_kernel(*refs, n_in, n_seq, chunk, n_heads, dk, dv, conv_w):
    qkv_refs, ab_refs, z_refs = refs[:n_in], refs[n_in:2 * n_in], refs[2 * n_in:3 * n_in]
    (cs_ref, s0_ref, cw_ref, alog_ref, dtb_ref, nw_ref,
     o_ref, sfin_ref, ncv_ref, xp_ref, s_ref) = refs[3 * n_in:]
    n = pl.program_id(1)
    c = chunk
    per = n_seq // n_in
    n_prob = n_seq * n_heads
    hi = lax.Precision.HIGHEST

    def seq_rows(group, g):
        k = g % per
        return group[g // per][k * c:(k + 1) * c, :]

    @pl.when(n == 0)
    def _():
        s_ref[...] = s0_ref[...].reshape(n_prob, dk, dv)
        xp_ref[:, 0:SUBLANE, :] = cs_ref[...]

    r_i = lax.broadcasted_iota(jnp.int32, (c, c), 0)
    c_i = lax.broadcasted_iota(jnp.int32, (c, c), 1)
    causal = (c_i <= r_i)[None]
    strict = (c_i < r_i)[None]
    eye_c = (r_i == c_i).astype(F32)[None]
    db = min(GDN_DIAG, c)
    n_blk = c // db
    sh = int(math.log2(db))
    same_blk = (jnp.right_shift(r_i, sh) == jnp.right_shift(c_i, sh))[None]
    r_l = lax.broadcasted_iota(jnp.int32, (LANE, LANE), 0)
    c_l = lax.broadcasted_iota(jnp.int32, (LANE, LANE), 1)
    eye_l = (r_l == c_l).astype(F32)
    tril = (c_i <= r_i).astype(F32)
    cw = cw_ref[...]
    hk = n_heads * dk

    q_l, k_l, v_l, z_l, beta_l, gcol_l, grow_l, tails = [], [], [], [], [], [], [], []
    for g in range(n_seq):
        xp_ref[g, SUBLANE:SUBLANE + c, :] = seq_rows(qkv_refs, g)
        conv = xp_ref[g, pl.ds(SUBLANE - (conv_w - 1), c), :] * cw[0:1, :]
        for j in range(1, conv_w):
            conv = conv + xp_ref[g, pl.ds(SUBLANE - (conv_w - 1) + j, c), :] * cw[j:j + 1, :]
        tail = xp_ref[g, c:c + SUBLANE, :]
        xp_ref[g, 0:SUBLANE, :] = tail
        tails.append(tail)
        act = conv * _sigmoid(conv)
        ab = seq_rows(ab_refs, g)
        apb = ab + dtb_ref[...]
        softplus = jnp.maximum(apb, 0.0) + jnp.log(1.0 + jnp.exp(-jnp.abs(apb)))
        gfull = -jnp.exp(alog_ref[...]) * softplus
        betaf = _sigmoid(ab)
        gc = _dot(tril, gfull, precision=hi)
        gc_t = _dot_nt(eye_l, gc, precision=hi)
        zg = seq_rows(z_refs, g)
        for h in range(n_heads):
            q_l.append(act[:, h * dk:(h + 1) * dk])
            k_l.append(act[:, hk + h * dk:hk + (h + 1) * dk])
            v_l.append(act[:, 2 * hk + h * dv:2 * hk + (h + 1) * dv])
            z_l.append(zg[:, h * dv:(h + 1) * dv])
            beta_l.append(betaf[:, n_heads + h:n_heads + h + 1])
            gcol_l.append(gc[:, h:h + 1])
            grow_l.append(gc_t[h:h + 1, :])

    q = jnp.stack(q_l)
    k = jnp.stack(k_l)
    v = jnp.stack(v_l)
    z = jnp.stack(z_l)
    beta = jnp.stack(beta_l)
    gcol = jnp.stack(gcol_l)
    grow = jnp.stack(grow_l)
    q = q * lax.rsqrt(jnp.sum(q * q, -1, keepdims=True) + NORM_EPS) * (dk ** -0.5)
    k = k * lax.rsqrt(jnp.sum(k * k, -1, keepdims=True) + NORM_EPS)
    decay = jnp.where(causal, jnp.exp(jnp.where(causal, gcol - grow, 0.0)), 0.0)
    kb = k * beta
    lower = jnp.where(strict, _bdot_nt(kb, k) * decay, 0.0)
    attn = _bdot_nt(q, k) * decay
    l_d = jnp.where(same_blk, lower, 0.0)
    t_inv = _neumann_inv(l_d, eye_c, int(math.log2(db)) - 1)
    if n_blk > 1:
        m_inv = _neumann_inv(_bdot(t_inv, lower - l_d), eye_c, int(math.log2(n_blk)) - 1)
        t_inv = _bdot(m_inv, t_inv)
    egc = jnp.exp(gcol)
    uw = _bdot(t_inv, jnp.concatenate([v * beta, kb * egc], axis=2))
    s = s_ref[...]
    v_new = uw[:, :, :dv] - _bdot(uw[:, :, dv:], s)
    o = _bdot(q * egc, s) + _bdot(attn, v_new)
    glast = gcol[:, c - 1:c, :]
    kdec = k * jnp.exp(glast - gcol)
    s_new = s * jnp.exp(glast) + _bdot_tn(kdec, v_new)
    s_ref[...] = s_new
    on = o * lax.rsqrt(jnp.mean(o * o, -1, keepdims=True) + NORM_EPS) * nw_ref[...] * (z * _sigmoid(z))
    for g in range(n_seq):
        for h in range(n_heads):
            o_ref[g, :, h * dv:(h + 1) * dv] = on[g * n_heads + h]

    @pl.when(n == pl.num_programs(1) - 1)
    def _():
        sfin_ref[...] = s_new.reshape(n_seq, n_heads, dk, dv)
        for g in range(n_seq):
            ncv_ref[g] = tails[g]


def _gdn(h, row_off, conv_state8, ssm_state, cw, alog_p, dtb_p, nw, *, batch, seq, chunk, group,
         qkv_blk, ab_blk, z_blk, conv_w):
    _, n_heads, dk, dv = ssm_state.shape
    nc = seq // chunk
    ch = cw.shape[1]
    hd = n_heads * dv
    contiguous = nc == 1
    n_in = 1 if contiguous else group
    assert batch % group == 0 and row_off % (group * chunk) == 0
    kern = functools.partial(_gdn_kernel, n_in=n_in, n_seq=group, chunk=chunk, n_heads=n_heads,
                             dk=dk, dv=dv, conv_w=conv_w)
    if contiguous:
        rb = group * chunk
        row_maps = [lambda i, n: row_off // rb + i]
    else:
        rb = chunk
        row_maps = [(lambda i, n, g=g: row_off // rb + (i * group + g) * nc + n) for g in range(group)]

    def specs(width, col_blk):
        return [pl.BlockSpec((rb, width), lambda i, n, r=r: (r(i, n), col_blk)) for r in row_maps]

    in_specs = specs(ch, qkv_blk) + specs(LANE, ab_blk) + specs(hd, z_blk)
    in_specs += [pl.BlockSpec((group, SUBLANE, ch), lambda i, n: (i, 0, 0)),
                 pl.BlockSpec((group, n_heads, dk, dv), lambda i, n: (i, 0, 0, 0)),
                 pl.BlockSpec(cw.shape, lambda i, n: (0, 0)),
                 pl.BlockSpec((1, LANE), lambda i, n: (0, 0)),
                 pl.BlockSpec((1, LANE), lambda i, n: (0, 0)),
                 pl.BlockSpec((1, dv), lambda i, n: (0, 0))]
    o3, sfin, ncv = pl.pallas_call(
        kern,
        grid=(batch // group, nc),
        in_specs=in_specs,
        out_specs=[pl.BlockSpec((group, chunk, hd), lambda i, n: (i, n, 0)),
                   pl.BlockSpec((group, n_heads, dk, dv), lambda i, n: (i, 0, 0, 0)),
                   pl.BlockSpec((group, SUBLANE, ch), lambda i, n: (i, 0, 0))],
        out_shape=[jax.ShapeDtypeStruct((batch, seq, hd), F32),
                   jax.ShapeDtypeStruct(ssm_state.shape, F32),
                   jax.ShapeDtypeStruct((batch, SUBLANE, ch), F32)],
        scratch_shapes=[pltpu.VMEM((group, chunk + SUBLANE, ch), F32),
                        pltpu.VMEM((group * n_heads, dk, dv), F32)],
        compiler_params=_cparams(("parallel", "arbitrary"), VMEM_LIMIT),
        name="gdn",
    )(*([h] * (3 * n_in)), conv_state8, ssm_state, cw, alog_p, dtb_p, nw)
    return o3.reshape(batch * seq, hd), sfin, ncv


def _post_kernel(x_ref, om_ref, og_ref, ga_ref, gb_ref, wo_ref, g1_ref, b1_ref, wr_ref, br_ref,
                 x1_ref, idx_ref, gate_ref, *, alpha, top_k):
    mix = _sigmoid(ga_ref[...]) * om_ref[...] + _sigmoid(gb_ref[...]) * og_ref[...]
    y = _dot(mix.astype(BF16), wo_ref[...])
    x1 = _layer_norm(alpha * x_ref[...] + y, g1_ref[...], b1_ref[...])
    x1_ref[...] = x1
    logits = _dot_nt(wr_ref[...], x1, precision=lax.Precision.HIGHEST) + br_ref[...]
    n_exp, tm = logits.shape
    e_i = lax.broadcasted_iota(jnp.int32, (n_exp, tm), 0)
    vals, idxs = [], []
    for _ in range(top_k):
        mx = jnp.max(logits, axis=0, keepdims=True)
        ix = jnp.min(jnp.where(logits == mx, e_i, n_exp), axis=0, keepdims=True)
        vals.append(mx)
        idxs.append(ix)
        logits = jnp.where(e_i == ix, -jnp.inf, logits)
    es = [jnp.exp(v - vals[0]) for v in vals]
    tot = es[0]
    for e in es[1:]:
        tot = tot + e
    r_i = lax.broadcasted_iota(jnp.int32, (SUBLANE, tm), 0)
    idx_o = jnp.zeros((SUBLANE, tm), jnp.int32)
    gate_o = jnp.zeros((SUBLANE, tm), F32)
    for k in range(top_k):
        idx_o = jnp.where(r_i == k, idxs[k], idx_o)
        gate_o = jnp.where(r_i == k, es[k] / tot, gate_o)
    idx_ref[...] = idx_o
    gate_ref[...] = gate_o


def _post(x, om, og, h, ga_blk, gb_blk, wo, g1, b1, wr_t, br, *, tm, alpha):
    m, d = x.shape
    n_exp = wr_t.shape[0]
    kern = functools.partial(_post_kernel, alpha=alpha, top_k=TOP_K)
    row = lambda i: (i, 0)
    const = lambda i: (0, 0)
    return pl.pallas_call(
        kern,
        grid=(m // tm,),
        in_specs=[pl.BlockSpec((tm, d), row), pl.BlockSpec((tm, d), row), pl.BlockSpec((tm, d), row),
                  pl.BlockSpec((tm, d), lambda i: (i, ga_blk)),
                  pl.BlockSpec((tm, d), lambda i: (i, gb_blk)),
                  pl.BlockSpec(wo.shape, const), pl.BlockSpec((1, d), const), pl.BlockSpec((1, d), const),
                  pl.BlockSpec(wr_t.shape, const), pl.BlockSpec((n_exp, 1), const)],
        out_specs=[pl.BlockSpec((tm, d), row),
                   pl.BlockSpec((SUBLANE, tm), lambda i: (0, i)),
                   pl.BlockSpec((SUBLANE, tm), lambda i: (0, i))],
        out_shape=[jax.ShapeDtypeStruct((m, d), F32),
                   jax.ShapeDtypeStruct((SUBLANE, m), jnp.int32),
                   jax.ShapeDtypeStruct((SUBLANE, m), F32)],
        compiler_params=_cparams(("parallel",), VMEM_LIMIT),
        name="post_mix",
    )(x, om, og, h, h, wo, g1, b1, wr_t, br)


def _moe_kernel(be_ref, rows_ref, nused_ref, x_hbm, wgu_ref, bgu_ref, wd_ref, bd_ref, o_ref,
                xbuf, sem, wgu_bf, wd_bf, *, blk, d_exp):
    i = pl.program_id(0)
    n_used = nused_ref[0]
    slot = i % 2

    def row_copy(tok, s, r):
        return pltpu.make_async_copy(x_hbm.at[pl.ds(tok, 1)], xbuf.at[s, pl.ds(r, 1)], sem.at[s])

    def wait(s):
        pltpu.make_async_copy(x_hbm.at[pl.ds(0, blk)], xbuf.at[s], sem.at[s]).wait()

    @pl.when(jnp.logical_and(i == 0, n_used > 0))
    def _():
        def body(r, carry):
            row_copy(rows_ref[r], 0, r).start()
            return carry
        lax.fori_loop(0, blk, body, 0)

    @pl.when(i < n_used)
    def _():
        e = be_ref[i]
        e_prev = be_ref[jnp.maximum(i - 1, 0)]

        @pl.when(jnp.logical_or(i == 0, e != e_prev))
        def _():
            wgu_bf[...] = wgu_ref[0].astype(BF16)
            wd_bf[...] = wd_ref[0].astype(BF16)

        nxt = jnp.minimum(i + 1, n_used - 1) * blk
        for r in range(blk):
            row_copy(rows_ref[nxt + r], 1 - slot, r).start()
        wait(slot)
        xb = xbuf[slot].astype(BF16)
        hh = _dot(xb, wgu_bf[...]) + bgu_ref[0]
        gate = jnp.minimum(hh[:, :d_exp], SWIGLU_LIMIT)
        up = jnp.clip(hh[:, d_exp:], -SWIGLU_LIMIT, SWIGLU_LIMIT)
        act = (up + 1.0) * gate * _sigmoid(SWIGLU_ALPHA * gate)
        o_ref[...] = _dot(act.astype(BF16), wd_bf[...]) + bd_ref[0]

        @pl.when(i == n_used - 1)
        def _():
            wait(1 - slot)

    @pl.when(i >= n_used)
    def _():
        o_ref[...] = jnp.zeros(o_ref.shape, F32)


def _moe_experts(block_e, rows, n_used, x1, w_gu, b_gu, w_down, b_down, *, blk):
    n_exp, d, d2 = w_gu.shape
    d_exp = d2 // 2
    nb = block_e.shape[0]
    kern = functools.partial(_moe_kernel, blk=blk, d_exp=d_exp)
    grid_spec = pltpu.PrefetchScalarGridSpec(
        num_scalar_prefetch=3,
        grid=(nb,),
        in_specs=[pl.BlockSpec(memory_space=pl.ANY),
                  pl.BlockSpec((1, d, d2), lambda i, be, rw, nu: (be[i], 0, 0)),
                  pl.BlockSpec((1, 1, d2), lambda i, be, rw, nu: (be[i], 0, 0)),
                  pl.BlockSpec((1, d_exp, d), lambda i, be, rw, nu: (be[i], 0, 0)),
                  pl.BlockSpec((1, 1, d), lambda i, be, rw, nu: (be[i], 0, 0))],
        out_specs=pl.BlockSpec((blk, d), lambda i, be, rw, nu: (i, 0)),
        scratch_shapes=[pltpu.VMEM((2, blk, d), F32), pltpu.SemaphoreType.DMA((2,)),
                        pltpu.VMEM((d, d2), BF16), pltpu.VMEM((d_exp, d), BF16)])
    return pl.pallas_call(
        kern,
        grid_spec=grid_spec,
        out_shape=jax.ShapeDtypeStruct((nb * blk, d), F32),
        compiler_params=_cparams(("arbitrary",), VMEM_LIMIT),
        name="moe_experts",
    )(block_e, rows, n_used, x1, w_gu, b_gu.reshape(n_exp, 1, d2), w_down, b_down.reshape(n_exp, 1, d))


def _combine_kernel(dest_ref, x1_ref, gate_ref, ys_hbm, g2_ref, b2_ref, o_ref, ybuf, sem,
                    *, tm, top_k, alpha, m_total):
    i = pl.program_id(0)
    nsteps = pl.num_programs(0)
    slot = i % 2
    n_rows = top_k * tm

    def row_copy(d, s, r):
        return pltpu.make_async_copy(ys_hbm.at[pl.ds(d, 1)], ybuf.at[s, pl.ds(r, 1)], sem.at[s])

    def wait(s):
        pltpu.make_async_copy(ys_hbm.at[pl.ds(0, n_rows)], ybuf.at[s], sem.at[s]).wait()

    @pl.when(i == 0)
    def _():
        def body(r, carry):
            k = r // tm
            row_copy(dest_ref[k * m_total + r - k * tm], 0, r).start()
            return carry
        lax.fori_loop(0, n_rows, body, 0)

    nxt = jnp.minimum(i + 1, nsteps - 1) * tm
    for r in range(n_rows):
        k, t = divmod(r, tm)
        row_copy(dest_ref[k * m_total + nxt + t], 1 - slot, r).start()
    wait(slot)
    g = gate_ref[...]
    y = g[:, 0:1] * ybuf[slot, 0:tm, :]
    for k in range(1, top_k):
        y = y + g[:, k:k + 1] * ybuf[slot, k * tm:(k + 1) * tm, :]
    o_ref[...] = _layer_norm(alpha * x1_ref[...] + y, g2_ref[...], b2_ref[...])

    @pl.when(i == nsteps - 1)
    def _():
        wait(1 - slot)


def _combine(dest_km, x1, gates_mk, ys, g2, b2, *, tm, alpha):
    m, d = x1.shape
    kern = functools.partial(_combine_kernel, tm=tm, top_k=TOP_K, alpha=alpha, m_total=m)
    grid_spec = pltpu.PrefetchScalarGridSpec(
        num_scalar_prefetch=1,
        grid=(m // tm,),
        in_specs=[pl.BlockSpec((tm, d), lambda i, ds: (i, 0)),
                  pl.BlockSpec((tm, SUBLANE), lambda i, ds: (i, 0)),
                  pl.BlockSpec(memory_space=pl.ANY),
                  pl.BlockSpec((1, d), lambda i, ds: (0, 0)),
                  pl.BlockSpec((1, d), lambda i, ds: (0, 0))],
        out_specs=pl.BlockSpec((tm, d), lambda i, ds: (i, 0)),
        scratch_shapes=[pltpu.VMEM((2, TOP_K * tm, d), F32), pltpu.SemaphoreType.DMA((2,))])
    return pl.pallas_call(
        kern,
        grid_spec=grid_spec,
        out_shape=jax.ShapeDtypeStruct((m, d), F32),
        compiler_params=_cparams(("arbitrary",), VMEM_LIMIT),
        name="moe_combine",
    )(dest_km, x1, gates_mk, ys, g2, b2)


def _pack_w_in(w_in, splits):
    q_lora, kv_lora, rope, conv_ch, gv, nh, _, d, _ = splits
    offs = [0]
    for s in splits:
        offs.append(offs[-1] + s)
    part = [w_in[:, offs[i]:offs[i + 1]] for i in range(len(splits))]
    q_lat, kv_lat, k_r, qkv, z, a, b, g_a, g_b = part
    dm = w_in.shape[0]
    half = rope // 2
    zpad = lambda n: jnp.zeros((dm, n), w_in.dtype)
    k_sw = jnp.concatenate([k_r[:, half:], k_r[:, :half]], axis=1)
    small = jnp.concatenate([q_lat, kv_lat, k_r, zpad(LANE - rope), k_sw, zpad(LANE - rope),
                             a, b, zpad(LANE - 2 * nh)], axis=1)
    return jnp.concatenate([qkv, small, z, g_a, g_b], axis=1).astype(BF16)


def _pack_w_uq(w_uq, nope, rope):
    w = jnp.transpose(w_uq, (1, 0, 2))
    half = rope // 2
    r = w[..., nope:]
    zp = jnp.zeros(r.shape[:-1] + (LANE - rope,), w.dtype)
    r_sw = jnp.concatenate([r[..., half:], r[..., :half]], axis=-1)
    return jnp.concatenate([w[..., :nope], r, zp, r_sw, zp], axis=-1).astype(BF16)


def _rope_tables(pos, rope):
    half = rope // 2
    inv = ROPE_THETA ** (-jnp.arange(half, dtype=F32) / half)
    ang = pos.astype(F32)[:, None] * inv[None, :]
    cos, sin = jnp.cos(ang), jnp.sin(ang)
    zp = jnp.zeros((pos.shape[0], LANE - rope), F32)
    return (jnp.concatenate([cos, cos, zp], axis=1), jnp.concatenate([-sin, sin, zp], axis=1))


def _route_meta(idx_t, m, n_exp, blk):
    a = m * TOP_K
    e_flat = idx_t[:TOP_K].T.reshape(a)
    onehot = (e_flat[:, None] == jnp.arange(n_exp, dtype=jnp.int32)[None, :]).astype(jnp.int32)
    csum = jnp.cumsum(onehot, axis=0)
    rank = jnp.take_along_axis(csum, e_flat[:, None], axis=1)[:, 0] - 1
    counts = csum[-1]
    padded = (counts + blk - 1) // blk * blk
    pad_end = jnp.cumsum(padded)
    pad_start = pad_end - padded
    dest = (pad_start[e_flat] + rank).astype(jnp.int32)
    nb = a // blk + n_exp
    tok = jnp.arange(a, dtype=jnp.int32) // TOP_K
    rows = jnp.zeros((nb * blk,), jnp.int32).at[dest].set(tok)
    block_e = jnp.minimum(jnp.searchsorted(pad_end, jnp.arange(nb, dtype=jnp.int32) * blk, side='right'),
                          n_exp - 1).astype(jnp.int32)
    n_used = (pad_end[-1] // blk).astype(jnp.int32).reshape(1)
    dest_km = dest.reshape(m, TOP_K).T.reshape(a)
    return block_e, rows, n_used, dest_km


def kernel(x_prompt, x_sample, cache_ckv, cache_krope, page_table, state_conv, state_ssm, w_in, q_norm_w, kv_norm_w, w_uq, w_uk, w_uv, conv_w, a_log, dt_bias, gdn_norm_w, w_o, ln1_g, ln1_b, w_router, b_router, w_gu, b_gu, w_down, b_down, ln2_g, ln2_b):
    bp, tp, d = x_prompt.shape
    bs, ts, _ = x_sample.shape
    depth = w_in.shape[0]
    q_lora, n_heads, qk = w_uq.shape[1:]
    kv_lora, _, nope = w_uk.shape[1:]
    rope = qk - nope
    vh = w_uv.shape[3]
    cw_taps, conv_ch = conv_w.shape[1:]
    g_heads = a_log.shape[1]
    dk, dv = state_ssm.shape[3:]
    n_exp = w_router.shape[2]
    page = cache_ckv.shape[2]
    past = page_table.shape[1] * page
    splits = (q_lora, kv_lora, rope, conv_ch, g_heads * dv, g_heads, g_heads, d, d)
    assert sum(splits) == w_in.shape[2]
    alpha = (2 * depth) ** 0.25
    scale = (nope + rope) ** -0.5 * LOG2E
    mp, ms = bp * tp, bs * ts
    m = mp + ms
    small_w = q_lora + kv_lora + 3 * LANE
    assert small_w == d and conv_ch % d == 0
    qkv_blk, small_blk = 0, conv_ch // d
    z_blk, ga_blk, gb_blk = small_blk + 1, small_blk + 2, small_blk + 3
    ab_blk = (conv_ch + q_lora + kv_lora + 2 * LANE) // LANE

    cos_p, sin_p = _rope_tables(jnp.arange(tp, dtype=jnp.int32), rope)
    cos_s, sin_s = _rope_tables(past + jnp.arange(ts, dtype=jnp.int32), rope)
    cos_t = jnp.concatenate([jnp.tile(cos_p, (bp, 1)), jnp.tile(cos_s, (bs, 1))], axis=0)
    sin_t = jnp.concatenate([jnp.tile(sin_p, (bp, 1)), jnp.tile(sin_s, (bs, 1))], axis=0)

    x = jnp.concatenate([x_prompt.reshape(mp, d), x_sample.reshape(ms, d)], axis=0)
    outs = {k: [] for k in ("ckv_p", "kr_p", "conv_p", "ssm_p", "ckv_s", "kr_s", "conv_s", "ssm_s")}
    pad_lanes = lambda v: jnp.pad(v, (0, LANE - v.shape[0])).reshape(1, LANE)
    for l in range(depth):
        w_pack = _pack_w_in(w_in[l], splits)
        wq = _pack_w_uq(w_uq[l], nope, rope)
        wuk = jnp.transpose(w_uk[l], (1, 2, 0)).astype(BF16)
        wuv = jnp.transpose(w_uv[l], (1, 0, 2)).astype(BF16)
        h = _in_proj(x, w_pack, tm=1024, tn=1024)
        qf, c_all, kr_all, kf = _mla_proj(
            h, small_blk, cos_t, sin_t, q_norm_w[l].reshape(1, q_lora), kv_norm_w[l].reshape(1, kv_lora),
            wq, wuk, tm=256, q_lora=q_lora, kv_lora=kv_lora, nope=nope, rope=rope, scale=scale)
        om_p = _attn_prompt(qf, kf, wuv, batch=bp, seq=tp, tq=128, tk=min(1024, tp))
        qs = qf[:, mp:, :].reshape(n_heads, bs, ts, qf.shape[-1])
        qs = jnp.transpose(qs, (1, 0, 2, 3)).reshape(bs, n_heads * ts, qf.shape[-1])
        om_s = _attn_sample(page_table, qs, c_all, kr_all, mp // ts, cache_ckv[l],
                            jnp.swapaxes(cache_krope[l], 1, 2), wuv)

        alog_p = pad_lanes(a_log[l])
        dtb_p = pad_lanes(dt_bias[l])
        nw = gdn_norm_w[l].reshape(1, dv)
        gdn_kw = dict(qkv_blk=qkv_blk, ab_blk=ab_blk, z_blk=z_blk, conv_w=cw_taps)
        zeros_conv = jnp.zeros((bp, SUBLANE, conv_ch), F32)
        zeros_ssm = jnp.zeros((bp, g_heads, dk, dv), F32)
        og_p, ssm_p, ncv_p = _gdn(h, 0, zeros_conv, zeros_ssm, conv_w[l], alog_p, dtb_p, nw,
                                  batch=bp, seq=tp, chunk=min(GDN_CHUNK, tp), group=bp, **gdn_kw)
        conv8_s = jnp.pad(state_conv[l], ((0, 0), (SUBLANE - (cw_taps - 1), 0), (0, 0)))
        og_s, ssm_s, ncv_s = _gdn(h, mp, conv8_s, state_ssm[l], conv_w[l], alog_p, dtb_p, nw,
                                  batch=bs, seq=ts, chunk=ts, group=math.gcd(bs, GDN_SAMPLE_GROUP), **gdn_kw)

        om = jnp.concatenate([om_p, om_s], axis=0)
        og = jnp.concatenate([og_p, og_s], axis=0)
        x1, idx_t, gate_t = _post(
            x, om, og, h, ga_blk, gb_blk, w_o[l].astype(BF16), ln1_g[l].reshape(1, d), ln1_b[l].reshape(1, d),
            w_router[l].T, b_router[l].reshape(n_exp, 1), tm=256, alpha=alpha)
        block_e, rows, n_used, dest_km = _route_meta(idx_t, m, n_exp, MOE_ROWS)
        ys = _moe_experts(block_e, rows, n_used, x1, w_gu[l], b_gu[l], w_down[l], b_down[l], blk=MOE_ROWS)
        x = _combine(dest_km, x1, gate_t.T, ys, ln2_g[l].reshape(1, d), ln2_b[l].reshape(1, d),
                     tm=128, alpha=alpha)

        outs["ckv_p"].append(c_all[:mp].reshape(bp, tp, kv_lora))
        outs["kr_p"].append(kr_all[:mp].reshape(bp, tp, rope))
        outs["conv_p"].append(ncv_p[:, SUBLANE - (cw_taps - 1):, :])
        outs["ssm_p"].append(ssm_p)
        outs["ckv_s"].append(c_all[mp:].reshape(bs, ts, kv_lora))
        outs["kr_s"].append(kr_all[mp:].reshape(bs, ts, rope))
        outs["conv_s"].append(ncv_s[:, SUBLANE - (cw_taps - 1):, :])
        outs["ssm_s"].append(ssm_s)

    return (x[:mp].reshape(bp, tp, d), x[mp:].reshape(bs, ts, d),
            jnp.stack(outs["ckv_p"]), jnp.stack(outs["kr_p"]), jnp.stack(outs["conv_p"]), jnp.stack(outs["ssm_p"]),
            jnp.stack(outs["ckv_s"]), jnp.stack(outs["kr_s"]), jnp.stack(outs["conv_s"]), jnp.stack(outs["ssm_s"]))
```

```python
import functools
import math

import jax
import jax.numpy as jnp
from jax import lax
from jax.experimental import pallas as pl
from jax.experimental.pallas import tpu as pltpu

F32 = jnp.float32
BF16 = jnp.bfloat16

ROPE_THETA = 10000.0
NORM_EPS = 1e-6
TOP_K = 4
SWIGLU_LIMIT = 7.0
SWIGLU_ALPHA = 1.702
GDN_CHUNK = 64
GDN_DIAG = 16
GDN_SAMPLE_GROUP = 8
MOE_ROWS = 512
PAGE_CHUNK = 8
LANE = 128
SUBLANE = 8
VMEM_LIMIT = 56 * 1024 * 1024
NEG = -1e30
LOG2E = 1.4426950408889634


def _cparams(sem, vmem=None):
    return pltpu.CompilerParams(dimension_semantics=sem, vmem_limit_bytes=vmem)


def _dot(a, b, **kw):
    return jnp.dot(a, b, preferred_element_type=F32, **kw)


def _dot_nt(a, b, **kw):
    return lax.dot_general(a, b, (((1,), (1,)), ((), ())), preferred_element_type=F32, **kw)


def _bdot(a, b):
    return lax.dot_general(a, b, (((2,), (1,)), ((0,), (0,))), preferred_element_type=F32)


def _bdot_nt(a, b):
    return lax.dot_general(a, b, (((2,), (2,)), ((0,), (0,))), preferred_element_type=F32)


def _bdot_tn(a, b):
    return lax.dot_general(a, b, (((1,), (1,)), ((0,), (0,))), preferred_element_type=F32)


def _sigmoid(x):
    return 1.0 / (1.0 + jnp.exp(-x))


def _layer_norm(v, g, b):
    mu = jnp.mean(v, -1, keepdims=True)
    vc = v - mu
    var = jnp.mean(vc * vc, -1, keepdims=True)
    return vc * lax.rsqrt(var + NORM_EPS) * g + b


def _inproj_kernel(x_ref, w_ref, o_ref, xb_ref):
    @pl.when(pl.program_id(1) == 0)
    def _():
        xb_ref[...] = x_ref[...].astype(BF16)

    o_ref[...] = _dot(xb_ref[...], w_ref[...])


def _row_tile(m, target):
    return max(t for t in range(SUBLANE, target + 1, SUBLANE) if m % t == 0)


def _in_proj(x, w, tm, tn):
    m, k = x.shape
    n = w.shape[1]
    return pl.pallas_call(
        _inproj_kernel,
        grid=(m // tm, n // tn),
        in_specs=[pl.BlockSpec((tm, k), lambda i, j: (i, 0)),
                  pl.BlockSpec((k, tn), lambda i, j: (0, j))],
        out_specs=pl.BlockSpec((tm, tn), lambda i, j: (i, j)),
        out_shape=jax.ShapeDtypeStruct((m, n), F32),
        scratch_shapes=[pltpu.VMEM((tm, k), BF16)],
        compiler_params=_cparams(("parallel", "arbitrary"), VMEM_LIMIT),
        name="in_proj",
    )(x, w)


def _mla_proj_kernel(h_ref, cos_ref, sin_ref, qnw_ref, kvnw_ref, wq_ref, wuk_ref,
                     qf_ref, c_ref, kr_ref, kt_ref, cb_ref, *, n_heads, q_lora, kv_lora, nope, rope, scale):
    hs = h_ref[...]
    cos = cos_ref[...]
    sin = sin_ref[...]
    q_lat = hs[:, :q_lora]
    qn = q_lat * lax.rsqrt(jnp.mean(q_lat * q_lat, -1, keepdims=True) + NORM_EPS) * qnw_ref[...]
    qn = qn.astype(BF16)
    kv = hs[:, q_lora:q_lora + kv_lora]
    c = kv * lax.rsqrt(jnp.mean(kv * kv, -1, keepdims=True) + NORM_EPS) * kvnw_ref[...]
    o = q_lora + kv_lora
    kr = hs[:, o:o + LANE] * cos + hs[:, o + LANE:o + 2 * LANE] * sin
    c_ref[...] = c
    kr_ref[...] = kr[:, :rope]
    cb = c.astype(BF16)
    cb_ref[...] = cb
    kfull = jnp.concatenate([cb, kr.astype(BF16)], axis=1)
    kw = kfull.shape[1]
    eye = (lax.broadcasted_iota(jnp.int32, (kw, kw), 0) == lax.broadcasted_iota(jnp.int32, (kw, kw), 1))
    kt_ref[...] = _dot_nt(eye.astype(BF16), kfull).astype(BF16)
    for h in range(n_heads):
        qh = _dot(qn, wq_ref[h])
        qa = _dot(qh[:, :nope].astype(BF16), wuk_ref[h])
        qr = qh[:, nope:nope + LANE] * cos + qh[:, nope + LANE:nope + 2 * LANE] * sin
        qf_ref[h, :, :kv_lora] = (qa * scale).astype(BF16)
        qf_ref[h, :, kv_lora:] = (qr * scale).astype(BF16)


def _mla_proj(h, col_blk, cos_t, sin_t, qnw, kvnw, wq, wuk, *, tm, q_lora, kv_lora, nope, rope, scale):
    m = h.shape[0]
    n_heads = wq.shape[0]
    wcol = q_lora + kv_lora + 3 * LANE
    kw = kv_lora + LANE
    kern = functools.partial(_mla_proj_kernel, n_heads=n_heads, q_lora=q_lora, kv_lora=kv_lora,
                             nope=nope, rope=rope, scale=scale)
    return pl.pallas_call(
        kern,
        grid=(m // tm,),
        in_specs=[pl.BlockSpec((tm, wcol), lambda i: (i, col_blk)),
                  pl.BlockSpec((tm, LANE), lambda i: (i, 0)),
                  pl.BlockSpec((tm, LANE), lambda i: (i, 0)),
                  pl.BlockSpec((1, q_lora), lambda i: (0, 0)),
                  pl.BlockSpec((1, kv_lora), lambda i: (0, 0)),
                  pl.BlockSpec(wq.shape, lambda i: (0, 0, 0)),
                  pl.BlockSpec(wuk.shape, lambda i: (0, 0, 0))],
        out_specs=[pl.BlockSpec((n_heads, tm, kw), lambda i: (0, i, 0)),
                   pl.BlockSpec((tm, kv_lora), lambda i: (i, 0)),
                   pl.BlockSpec((tm, rope), lambda i: (i, 0)),
                   pl.BlockSpec((kw, tm), lambda i: (0, i)),
                   pl.BlockSpec((tm, kv_lora), lambda i: (i, 0))],
        out_shape=[jax.ShapeDtypeStruct((n_heads, m, kw), BF16),
                   jax.ShapeDtypeStruct((m, kv_lora), F32),
                   jax.ShapeDtypeStruct((m, rope), F32),
                   jax.ShapeDtypeStruct((kw, m), BF16),
                   jax.ShapeDtypeStruct((m, kv_lora), BF16)],
        compiler_params=_cparams(("parallel",), VMEM_LIMIT),
        name="mla_proj",
    )(h, cos_t, sin_t, qnw, kvnw, wq, wuk)


def _softmax_step(s, v, m_ref, l_ref, acc_ref):
    m_prev = m_ref[...]
    m_new = jnp.maximum(m_prev, jnp.max(s, -1, keepdims=True))
    alpha = jnp.exp2(m_prev - m_new)
    p = jnp.exp2(s - m_new)
    l_ref[...] = alpha * l_ref[...] + jnp.sum(p, -1, keepdims=True)
    acc_ref[...] = alpha * acc_ref[...] + _dot(p.astype(BF16), v)
    m_ref[...] = m_new


def _attn_prompt_kernel(qf_ref, kt_ref, v_ref, wuv_ref, o_ref, m_ref, l_ref, acc_ref, *, n_heads, tq, tk, dv, vh, n_split):
    i = pl.program_id(1)
    m_ref[...] = jnp.full(m_ref.shape, NEG, F32)
    l_ref[...] = jnp.zeros(l_ref.shape, F32)
    acc_ref[...] = jnp.zeros(acc_ref.shape, F32)
    n_full = (i * tq) // tk
    off = i * tq - n_full * tk
    hs = n_heads // n_split
    rs = hs * tq

    def step(j, masked):
        kt = kt_ref[:, pl.ds(pl.multiple_of(j * tk, tk), tk)]
        v = v_ref[pl.ds(pl.multiple_of(j * tk, tk), tk), :]
        for g in range(n_split):
            q = qf_ref[g * hs:(g + 1) * hs].reshape(rs, qf_ref.shape[-1])
            s = _dot(q, kt)
            if masked:
                row = lax.broadcasted_iota(jnp.int32, (tq, tk), 0)
                col = lax.broadcasted_iota(jnp.int32, (tq, tk), 1)
                s = jnp.where((col <= row + off)[None], s.reshape(hs, tq, tk), NEG).reshape(rs, tk)
            sl = slice(g * rs, (g + 1) * rs)
            _softmax_step(s, v, m_ref.at[sl], l_ref.at[sl], acc_ref.at[sl])

    def body(j, carry):
        step(j, False)
        return carry

    lax.fori_loop(0, n_full, body, 0)
    step(n_full, True)
    o = acc_ref[...] / l_ref[...]
    for h in range(n_heads):
        oh = o[h * tq:(h + 1) * tq].astype(BF16)
        o_ref[:, h * vh:(h + 1) * vh] = _dot(oh, wuv_ref[h])


def _attn_prompt(qf, kt, cb, wuv, *, batch, seq, tq, tk):
    n_heads, _, kw = qf.shape
    dv, vh = wuv.shape[1], wuv.shape[2]
    nq = seq // tq
    rows = n_heads * tq
    assert seq % tk == 0 and tk % tq == 0
    kern = functools.partial(_attn_prompt_kernel, n_heads=n_heads, tq=tq, tk=tk, dv=dv, vh=vh, n_split=2)
    return pl.pallas_call(
        kern,
        grid=(batch, nq),
        in_specs=[pl.BlockSpec((n_heads, tq, kw), lambda b, i: (0, b * nq + i, 0)),
                  pl.BlockSpec((kw, seq), lambda b, i: (0, b)),
                  pl.BlockSpec((seq, dv), lambda b, i: (b, 0)),
                  pl.BlockSpec(wuv.shape, lambda b, i: (0, 0, 0))],
        out_specs=pl.BlockSpec((tq, n_heads * vh), lambda b, i: (b * nq + i, 0)),
        out_shape=jax.ShapeDtypeStruct((batch * seq, n_heads * vh), F32),
        scratch_shapes=[pltpu.VMEM((rows, 1), F32), pltpu.VMEM((rows, 1), F32),
                        pltpu.VMEM((rows, dv), F32)],
        compiler_params=_cparams(("parallel", "arbitrary"), VMEM_LIMIT),
        name="attn_prompt",
    )(qf, kt, cb, wuv)


def _attn_sample_kernel(pt_ref, q_ref, cn_ref, krn_ref, cc_hbm, cr_hbm, wuv_ref, o_ref,
                        cbuf, rbuf, sem, *, n_pages, page, n_heads, ts, dv, dr, vh):
    b = pl.program_id(0)
    nb = pl.num_programs(0)
    slot = b % 2
    rows = n_heads * ts

    def fetch(bb, s):
        for p in range(n_pages):
            pg = pt_ref[bb, p]
            pltpu.make_async_copy(cc_hbm.at[pg], cbuf.at[s, p], sem.at[0, s]).start()
            pltpu.make_async_copy(cr_hbm.at[pg], rbuf.at[s, p], sem.at[1, s]).start()

    def wait(s):
        pltpu.make_async_copy(cc_hbm.at[pl.ds(0, n_pages)], cbuf.at[s], sem.at[0, s]).wait()
        pltpu.make_async_copy(cr_hbm.at[pl.ds(0, n_pages)], rbuf.at[s], sem.at[1, s]).wait()

    @pl.when(b == 0)
    def _():
        fetch(0, 0)

    wait(slot)
    fetch(jnp.minimum(b + 1, nb - 1), 1 - slot)

    q = q_ref[0].astype(F32)
    qc = q[:, :dv]
    qr = q[:, dv:dv + dr]
    n_chunks = n_pages // PAGE_CHUNK
    ck = PAGE_CHUNK * page
    parts = []
    for ch in range(n_chunks):
        c_ch = cbuf[slot, ch * PAGE_CHUNK:(ch + 1) * PAGE_CHUNK].reshape(ck, dv)
        r_ch = jnp.concatenate([rbuf[slot, ch * PAGE_CHUNK + u] for u in range(PAGE_CHUNK)], axis=1)
        parts.append(_dot_nt(qc, c_ch) + _dot(qr, r_ch))
    cn = jnp.concatenate([cn_ref[...], jnp.zeros((LANE - ts, dv), F32)], axis=0)
    krn = jnp.concatenate([krn_ref[...], jnp.zeros((LANE - ts, dr), F32)], axis=0)
    row = lax.broadcasted_iota(jnp.int32, (ts, LANE), 0)
    col = lax.broadcasted_iota(jnp.int32, (ts, LANE), 1)
    s_new = (_dot_nt(qc, cn) + _dot_nt(qr, krn)).reshape(n_heads, ts, LANE)
    s_new = jnp.where((col <= row)[None], s_new, NEG).reshape(rows, LANE)

    m = jnp.max(s_new, -1, keepdims=True)
    for s in parts:
        m = jnp.maximum(m, jnp.max(s, -1, keepdims=True))
    p_new = jnp.exp2(s_new - m)
    l = jnp.sum(p_new, -1, keepdims=True)
    acc = _dot(p_new, cn)
    for ch in range(n_chunks):
        p = jnp.exp2(parts[ch] - m)
        l = l + jnp.sum(p, -1, keepdims=True)
        acc = acc + _dot(p, cbuf[slot, ch * PAGE_CHUNK:(ch + 1) * PAGE_CHUNK].reshape(ck, dv))
    o = acc / l
    for h in range(n_heads):
        oh = o[h * ts:(h + 1) * ts].astype(BF16)
        o_ref[:, h * vh:(h + 1) * vh] = _dot(oh, wuv_ref[h])

    @pl.when(b == nb - 1)
    def _():
        wait(1 - slot)


def _attn_sample(page_table, qs, c_all, kr_all, row_blk_off, cache_c, cache_rt, wuv):
    bs, rows, kw = qs.shape
    n_heads, dv, vh = wuv.shape
    ts = rows // n_heads
    n_pages = page_table.shape[1]
    page = cache_c.shape[1]
    dr = cache_rt.shape[1]
    assert n_pages % PAGE_CHUNK == 0
    kern = functools.partial(_attn_sample_kernel, n_pages=n_pages, page=page, n_heads=n_heads, ts=ts,
                             dv=dv, dr=dr, vh=vh)
    grid_spec = pltpu.PrefetchScalarGridSpec(
        num_scalar_prefetch=1,
        grid=(bs,),
        in_specs=[pl.BlockSpec((1, rows, kw), lambda b, pt: (b, 0, 0)),
                  pl.BlockSpec((ts, dv), lambda b, pt: (row_blk_off + b, 0)),
                  pl.BlockSpec((ts, dr), lambda b, pt: (row_blk_off + b, 0)),
                  pl.BlockSpec(memory_space=pl.ANY),
                  pl.BlockSpec(memory_space=pl.ANY),
                  pl.BlockSpec(wuv.shape, lambda b, pt: (0, 0, 0))],
        out_specs=pl.BlockSpec((ts, n_heads * vh), lambda b, pt: (b, 0)),
        scratch_shapes=[pltpu.VMEM((2, n_pages, page, dv), F32), pltpu.VMEM((2, n_pages, dr, page), F32),
                        pltpu.SemaphoreType.DMA((2, 2))])
    return pl.pallas_call(
        kern,
        grid_spec=grid_spec,
        out_shape=jax.ShapeDtypeStruct((bs * ts, n_heads * vh), F32),
        compiler_params=_cparams(("arbitrary",), VMEM_LIMIT),
        name="attn_sample",
    )(page_table, qs, c_all, kr_all, cache_c, cache_rt, wuv)


def _neumann_inv(low, eye, steps):
    p = eye - low
    x = low
    for _ in range(steps):
        x = _bdot(x, x)
        p = p + _bdot(p, x)
    return p


def _gdn_kernel(*refs, n_in, n_seq, chunk, n_heads, dk, dv, conv_w):
    qkv_refs, ab_refs, z_refs = refs[:n_in], refs[n_in:2 * n_in], refs[2 * n_in:3 * n_in]
    (cs_ref, s0_ref, cw_ref, alog_ref, dtb_ref, nw_ref,
     o_ref, sfin_ref, ncv_ref, xp_ref, s_ref) = refs[3 * n_in:]
    n = pl.program_id(1)
    c = chunk
    per = n_seq // n_in
    n_prob = n_seq * n_heads
    hi = lax.Precision.HIGHEST

    def seq_rows(group, g):
        k = g % per
        return group[g // per][k * c:(k + 1) * c, :]

    @pl.when(n == 0)
    def _():
        s_ref[...] = s0_ref[...].reshape(n_prob, dk, dv)
        xp_ref[:, 0:SUBLANE, :] = cs_ref[...]

    r_i = lax.broadcasted_iota(jnp.int32, (c, c), 0)
    c_i = lax.broadcasted_iota(jnp.int32, (c, c), 1)
    causal = (c_i <= r_i)[None]
    strict = (c_i < r_i)[None]
    eye_c = (r_i == c_i).astype(F32)[None]
    db = min(GDN_DIAG, c)
    n_blk = c // db
    sh = int(math.log2(db))
    same_blk = (jnp.right_shift(r_i, sh) == jnp.right_shift(c_i, sh))[None]
    r_l = lax.broadcasted_iota(jnp.int32, (LANE, LANE), 0)
    c_l = lax.broadcasted_iota(jnp.int32, (LANE, LANE), 1)
    eye_l = (r_l == c_l).astype(F32)
    tril = (c_i <= r_i).astype(F32)
    cw = cw_ref[...]
    hk = n_heads * dk

    q_l, k_l, v_l, z_l, beta_l, gcol_l, grow_l, tails = [], [], [], [], [], [], [], []
    for g in range(n_seq):
        xp_ref[g, SUBLANE:SUBLANE + c, :] = seq_rows(qkv_refs, g)
        conv = xp_ref[g, pl.ds(SUBLANE - (conv_w - 1), c), :] * cw[0:1, :]
        for j in range(1, conv_w):
            conv = conv + xp_ref[g, pl.ds(SUBLANE - (conv_w - 1) + j, c), :] * cw[j:j + 1, :]
        tail = xp_ref[g, c:c + SUBLANE, :]
        xp_ref[g, 0:SUBLANE, :] = tail
        tails.append(tail)
        act = conv * _sigmoid(conv)
        ab = seq_rows(ab_refs, g)
        apb = ab + dtb_ref[...]
        softplus = jnp.maximum(apb, 0.0) + jnp.log(1.0 + jnp.exp(-jnp.abs(apb)))
        gfull = -jnp.exp(alog_ref[...]) * softplus
        betaf = _sigmoid(ab)
        gc = _dot(tril, gfull, precision=hi)
        gc_t = _dot_nt(eye_l, gc, precision=hi)
        zg = seq_rows(z_refs, g)
        for h in range(n_heads):
            q_l.append(act[:, h * dk:(h + 1) * dk])
            k_l.append(act[:, hk + h * dk:hk + (h + 1) * dk])
            v_l.append(act[:, 2 * hk + h * dv:2 * hk + (h + 1) * dv])
            z_l.append(zg[:, h * dv:(h + 1) * dv])
            beta_l.append(betaf[:, n_heads + h:n_heads + h + 1])
            gcol_l.append(gc[:, h:h + 1])
            grow_l.append(gc_t[h:h + 1, :])

    q = jnp.stack(q_l)
    k = jnp.stack(k_l)
    v = jnp.stack(v_l)
    z = jnp.stack(z_l)
    beta = jnp.stack(beta_l)
    gcol = jnp.stack(gcol_l)
    grow = jnp.stack(grow_l)
    q = q * lax.rsqrt(jnp.sum(q * q, -1, keepdims=True) + NORM_EPS) * (dk ** -0.5)
    k = k * lax.rsqrt(jnp.sum(k * k, -1, keepdims=True) + NORM_EPS)
    decay = jnp.where(causal, jnp.exp(jnp.where(causal, gcol - grow, 0.0)), 0.0)
    kb = k * beta
    lower = jnp.where(strict, _bdot_nt(kb, k) * decay, 0.0)
    attn = _bdot_nt(q, k) * decay
    l_d = jnp.where(same_blk, lower, 0.0)
    t_inv = _neumann_inv(l_d, eye_c, int(math.log2(db)) - 1)
    if n_blk > 1:
        m_inv = _neumann_inv(_bdot(t_inv, lower - l_d), eye_c, int(math.log2(n_blk)) - 1)
        t_inv = _bdot(m_inv, t_inv)
    egc = jnp.exp(gcol)
    uw = _bdot(t_inv, jnp.concatenate([v * beta, kb * egc], axis=2))
    s = s_ref[...]
    v_new = uw[:, :, :dv] - _bdot(uw[:, :, dv:], s)
    o = _bdot(q * egc, s) + _bdot(attn, v_new)
    glast = gcol[:, c - 1:c, :]
    kdec = k * jnp.exp(glast - gcol)
    s_new = s * jnp.exp(glast) + _bdot_tn(kdec, v_new)
    s_ref[...] = s_new
    on = o * lax.rsqrt(jnp.mean(o * o, -1, keepdims=True) + NORM_EPS) * nw_ref[...] * (z * _sigmoid(z))
    for g in range(n_seq):
        for h in range(n_heads):
            o_ref[g, :, h * dv:(h + 1) * dv] = on[g * n_heads + h]

    @pl.when(n == pl.num_programs(1) - 1)
    def _():
        sfin_ref[...] = s_new.reshape(n_seq, n_heads, dk, dv)
        for g in range(n_seq):
            ncv_ref[g] = tails[g]


def _gdn(h, row_off, conv_state8, ssm_state, cw, alog_p, dtb_p, nw, *, batch, seq, chunk, group,
         qkv_blk, ab_blk, z_blk, conv_w):
    _, n_heads, dk, dv = ssm_state.shape
    nc = seq // chunk
    ch = cw.shape[1]
    hd = n_heads * dv
    contiguous = nc == 1
    n_in = 1 if contiguous else group
    assert batch % group == 0 and row_off % (group * chunk) == 0
    kern = functools.partial(_gdn_kernel, n_in=n_in, n_seq=group, chunk=chunk, n_heads=n_heads,
                             dk=dk, dv=dv, conv_w=conv_w)
    if contiguous:
        rb = group * chunk
        row_maps = [lambda i, n: row_off // rb + i]
    else:
        rb = chunk
        row_maps = [(lambda i, n, g=g: row_off // rb + (i * group + g) * nc + n) for g in range(group)]

    def specs(width, col_blk):
        return [pl.BlockSpec((rb, width), lambda i, n, r=r: (r(i, n), col_blk)) for r in row_maps]

    in_specs = specs(ch, qkv_blk) + specs(LANE, ab_blk) + specs(hd, z_blk)
    in_specs += [pl.BlockSpec((group, SUBLANE, ch), lambda i, n: (i, 0, 0)),
                 pl.BlockSpec((group, n_heads, dk, dv), lambda i, n: (i, 0, 0, 0)),
                 pl.BlockSpec(cw.shape, lambda i, n: (0, 0)),
                 pl.BlockSpec((1, LANE), lambda i, n: (0, 0)),
                 pl.BlockSpec((1, LANE), lambda i, n: (0, 0)),
                 pl.BlockSpec((1, dv), lambda i, n: (0, 0))]
    o3, sfin, ncv = pl.pallas_call(
        kern,
        grid=(batch // group, nc),
        in_specs=in_specs,
        out_specs=[pl.BlockSpec((group, chunk, hd), lambda i, n: (i, n, 0)),
                   pl.BlockSpec((group, n_heads, dk, dv), lambda i, n: (i, 0, 0, 0)),
                   pl.BlockSpec((group, SUBLANE, ch), lambda i, n: (i, 0, 0))],
        out_shape=[jax.ShapeDtypeStruct((batch, seq, hd), F32),
                   jax.ShapeDtypeStruct(ssm_state.shape, F32),
                   jax.ShapeDtypeStruct((batch, SUBLANE, ch), F32)],
        scratch_shapes=[pltpu.VMEM((group, chunk + SUBLANE, ch), F32),
                        pltpu.VMEM((group * n_heads, dk, dv), F32)],
        compiler_params=_cparams(("parallel", "arbitrary"), VMEM_LIMIT),
        name="gdn",
    )(*([h] * (3 * n_in)), conv_state8, ssm_state, cw, alog_p, dtb_p, nw)
    return o3.reshape(batch * seq, hd), sfin, ncv


def _post_kernel(x_ref, om_ref, og_ref, ga_ref, gb_ref, wo_ref, g1_ref, b1_ref, wr_ref, br_ref,
                 x1_ref, idx_ref, gate_ref, rank_ref, cnt_ref, carry_ref, *, alpha, top_k):
    @pl.when(pl.program_id(0) == 0)
    def _():
        carry_ref[...] = jnp.zeros(carry_ref.shape, F32)

    mix = _sigmoid(ga_ref[...]) * om_ref[...] + _sigmoid(gb_ref[...]) * og_ref[...]
    y = _dot(mix.astype(BF16), wo_ref[...])
    x1 = _layer_norm(alpha * x_ref[...] + y, g1_ref[...], b1_ref[...])
    x1_ref[...] = x1
    logits = _dot_nt(wr_ref[...], x1, precision=lax.Precision.HIGHEST) + br_ref[...]
    n_exp, tm = logits.shape
    e_i = lax.broadcasted_iota(jnp.int32, (n_exp, tm), 0)
    vals, idxs = [], []
    for _ in range(top_k):
        mx = jnp.max(logits, axis=0, keepdims=True)
        ix = jnp.min(jnp.where(logits == mx, e_i, n_exp), axis=0, keepdims=True)
        vals.append(mx)
        idxs.append(ix)
        logits = jnp.where(e_i == ix, -jnp.inf, logits)
    es = [jnp.exp(v - vals[0]) for v in vals]
    tot = es[0]
    for e in es[1:]:
        tot = tot + e
    onehots = [e_i == ix for ix in idxs]
    sel = onehots[0].astype(F32)
    for oh in onehots[1:]:
        sel = sel + oh.astype(F32)
    t_r = lax.broadcasted_iota(jnp.int32, (tm, tm), 0)
    t_c = lax.broadcasted_iota(jnp.int32, (tm, tm), 1)
    before = _dot(sel.astype(BF16), (t_r < t_c).astype(BF16)) + carry_ref[...]
    carry = carry_ref[...] + jnp.sum(sel, axis=1, keepdims=True)
    carry_ref[...] = carry
    cnt_ref[...] = carry
    r_i = lax.broadcasted_iota(jnp.int32, (SUBLANE, tm), 0)
    idx_o = jnp.zeros((SUBLANE, tm), jnp.int32)
    gate_o = jnp.zeros((SUBLANE, tm), F32)
    rank_o = jnp.zeros((SUBLANE, tm), F32)
    for k in range(top_k):
        idx_o = jnp.where(r_i == k, idxs[k], idx_o)
        gate_o = jnp.where(r_i == k, es[k] / tot, gate_o)
        rank_k = jnp.sum(jnp.where(onehots[k], before, 0.0), axis=0, keepdims=True)
        rank_o = jnp.where(r_i == k, rank_k, rank_o)
    idx_ref[...] = idx_o
    gate_ref[...] = gate_o
    rank_ref[...] = rank_o.astype(jnp.int32)


def _post(x, om, og, h, ga_blk, gb_blk, wo, g1, b1, wr_t, br, *, tm, alpha):
    m, d = x.shape
    n_exp = wr_t.shape[0]
    kern = functools.partial(_post_kernel, alpha=alpha, top_k=TOP_K)
    row = lambda i: (i, 0)
    const = lambda i: (0, 0)
    return pl.pallas_call(
        kern,
        grid=(m // tm,),
        in_specs=[pl.BlockSpec((tm, d), row), pl.BlockSpec((tm, d), row), pl.BlockSpec((tm, d), row),
                  pl.BlockSpec((tm, d), lambda i: (i, ga_blk)),
                  pl.BlockSpec((tm, d), lambda i: (i, gb_blk)),
                  pl.BlockSpec(wo.shape, const), pl.BlockSpec((1, d), const), pl.BlockSpec((1, d), const),
                  pl.BlockSpec(wr_t.shape, const), pl.BlockSpec((n_exp, 1), const)],
        out_specs=[pl.BlockSpec((tm, d), row),
                   pl.BlockSpec((SUBLANE, tm), lambda i: (0, i)),
                   pl.BlockSpec((SUBLANE, tm), lambda i: (0, i)),
                   pl.BlockSpec((SUBLANE, tm), lambda i: (0, i)),
                   pl.BlockSpec((n_exp, 1), const)],
        out_shape=[jax.ShapeDtypeStruct((m, d), F32),
                   jax.ShapeDtypeStruct((SUBLANE, m), jnp.int32),
                   jax.ShapeDtypeStruct((SUBLANE, m), F32),
                   jax.ShapeDtypeStruct((SUBLANE, m), jnp.int32),
                   jax.ShapeDtypeStruct((n_exp, 1), F32)],
        scratch_shapes=[pltpu.VMEM((n_exp, 1), F32)],
        compiler_params=_cparams(("arbitrary",), VMEM_LIMIT),
        name="post_mix",
    )(x, om, og, h, h, wo, g1, b1, wr_t, br)


def _moe_kernel(be_ref, rows_ref, nused_ref, x_hbm, wgu_ref, bgu_ref, wd_ref, bd_ref, o_ref,
                xbuf, sem, wgu_bf, wd_bf, xb_ref, *, blk, d_exp):
    i = pl.program_id(0)
    n_used = nused_ref[0]
    slot = i % 2

    def row_copy(tok, s, r):
        return pltpu.make_async_copy(x_hbm.at[pl.ds(tok, 1)], xbuf.at[s, pl.ds(r, 1)], sem.at[s])

    def wait(s):
        pltpu.make_async_copy(x_hbm.at[pl.ds(0, blk)], xbuf.at[s], sem.at[s]).wait()

    @pl.when(jnp.logical_and(i == 0, n_used > 0))
    def _():
        def body(r, carry):
            row_copy(rows_ref[r], 0, r).start()
            return carry
        lax.fori_loop(0, blk, body, 0)

    @pl.when(i < n_used)
    def _():
        e = be_ref[i]
        e_prev = be_ref[jnp.maximum(i - 1, 0)]

        @pl.when(jnp.logical_or(i == 0, e != e_prev))
        def _():
            wgu_bf[...] = wgu_ref[0].astype(BF16)
            wd_bf[...] = wd_ref[0].astype(BF16)

        wait(slot)
        xb_ref[...] = xbuf[slot].astype(BF16)
        nxt = jnp.minimum(i + 1, n_used - 1) * blk
        for r in range(blk):
            row_copy(rows_ref[nxt + r], 1 - slot, r).start()
        hh = _dot(xb_ref[...], wgu_bf[...]) + bgu_ref[0]
        gate = jnp.minimum(hh[:, :d_exp], SWIGLU_LIMIT)
        up = jnp.clip(hh[:, d_exp:], -SWIGLU_LIMIT, SWIGLU_LIMIT)
        act = (up + 1.0) * gate * _sigmoid(SWIGLU_ALPHA * gate)
        o_ref[...] = _dot(act.astype(BF16), wd_bf[...]) + bd_ref[0]

        @pl.when(i == n_used - 1)
        def _():
            wait(1 - slot)

    @pl.when(i >= n_used)
    def _():
        o_ref[...] = jnp.zeros(o_ref.shape, F32)


def _moe_experts(block_e, rows, n_used, x1, w_gu, b_gu, w_down, b_down, *, blk):
    n_exp, d, d2 = w_gu.shape
    d_exp = d2 // 2
    nb = block_e.shape[0]
    kern = functools.partial(_moe_kernel, blk=blk, d_exp=d_exp)
    grid_spec = pltpu.PrefetchScalarGridSpec(
        num_scalar_prefetch=3,
        grid=(nb,),
        in_specs=[pl.BlockSpec(memory_space=pl.ANY),
                  pl.BlockSpec((1, d, d2), lambda i, be, rw, nu: (be[i], 0, 0)),
                  pl.BlockSpec((1, 1, d2), lambda i, be, rw, nu: (be[i], 0, 0)),
                  pl.BlockSpec((1, d_exp, d), lambda i, be, rw, nu: (be[i], 0, 0)),
                  pl.BlockSpec((1, 1, d), lambda i, be, rw, nu: (be[i], 0, 0))],
        out_specs=pl.BlockSpec((blk, d), lambda i, be, rw, nu: (i, 0)),
        scratch_shapes=[pltpu.VMEM((2, blk, d), F32), pltpu.SemaphoreType.DMA((2,)),
                        pltpu.VMEM((d, d2), BF16), pltpu.VMEM((d_exp, d), BF16),
                        pltpu.VMEM((blk, d), BF16)])
    return pl.pallas_call(
        kern,
        grid_spec=grid_spec,
        out_shape=jax.ShapeDtypeStruct((nb * blk, d), F32),
        compiler_params=_cparams(("arbitrary",), VMEM_LIMIT),
        name="moe_experts",
    )(block_e, rows, n_used, x1, w_gu, b_gu.reshape(n_exp, 1, d2), w_down, b_down.reshape(n_exp, 1, d))


def _combine_kernel(dest_ref, x1_ref, gate_ref, ys_hbm, g2_ref, b2_ref, op_ref, os_ref, ybuf, sem,
                    *, tm, top_k, alpha, m_total, n_first):
    i = pl.program_id(0)
    nsteps = pl.num_programs(0)
    slot = i % 2
    n_rows = top_k * tm

    def row_copy(d, s, r):
        return pltpu.make_async_copy(ys_hbm.at[pl.ds(d, 1)], ybuf.at[s, pl.ds(r, 1)], sem.at[s])

    def wait(s):
        pltpu.make_async_copy(ys_hbm.at[pl.ds(0, n_rows)], ybuf.at[s], sem.at[s]).wait()

    @pl.when(i == 0)
    def _():
        def body(r, carry):
            k = r // tm
            row_copy(dest_ref[k * m_total + r - k * tm], 0, r).start()
            return carry
        lax.fori_loop(0, n_rows, body, 0)

    wait(slot)
    nxt = jnp.minimum(i + 1, nsteps - 1) * tm
    for r in range(n_rows):
        k, t = divmod(r, tm)
        row_copy(dest_ref[k * m_total + nxt + t], 1 - slot, r).start()
    g = gate_ref[...]
    y = g[:, 0:1] * ybuf[slot, 0:tm, :]
    for k in range(1, top_k):
        y = y + g[:, k:k + 1] * ybuf[slot, k * tm:(k + 1) * tm, :]
    res = _layer_norm(alpha * x1_ref[...] + y, g2_ref[...], b2_ref[...])

    @pl.when(i < n_first)
    def _():
        op_ref[...] = res

    @pl.when(i >= n_first)
    def _():
        os_ref[...] = res

    @pl.when(i == nsteps - 1)
    def _():
        wait(1 - slot)


def _combine(dest_km, x1, gates_mk, ys, g2, b2, *, tm, alpha, m_first):
    m, d = x1.shape
    n_first = m_first // tm
    assert m_first % tm == 0 and 0 < m_first < m
    kern = functools.partial(_combine_kernel, tm=tm, top_k=TOP_K, alpha=alpha, m_total=m, n_first=n_first)
    grid_spec = pltpu.PrefetchScalarGridSpec(
        num_scalar_prefetch=1,
        grid=(m // tm,),
        in_specs=[pl.BlockSpec((tm, d), lambda i, ds: (i, 0)),
                  pl.BlockSpec((tm, SUBLANE), lambda i, ds: (i, 0)),
                  pl.BlockSpec(memory_space=pl.ANY),
                  pl.BlockSpec((1, d), lambda i, ds: (0, 0)),
                  pl.BlockSpec((1, d), lambda i, ds: (0, 0))],
        out_specs=[pl.BlockSpec((tm, d), lambda i, ds: (jnp.minimum(i, n_first - 1), 0)),
                   pl.BlockSpec((tm, d), lambda i, ds: (jnp.maximum(i - n_first, 0), 0))],
        scratch_shapes=[pltpu.VMEM((2, TOP_K * tm, d), F32), pltpu.SemaphoreType.DMA((2,))])
    return pl.pallas_call(
        kern,
        grid_spec=grid_spec,
        out_shape=[jax.ShapeDtypeStruct((m_first, d), F32), jax.ShapeDtypeStruct((m - m_first, d), F32)],
        compiler_params=_cparams(("arbitrary",), VMEM_LIMIT),
        name="moe_combine",
    )(dest_km, x1, gates_mk, ys, g2, b2)


def _pack_w_in(w_in, splits):
    q_lora, kv_lora, rope, conv_ch, gv, nh, _, d, _ = splits
    offs = [0]
    for s in splits:
        offs.append(offs[-1] + s)
    part = [w_in[:, offs[i]:offs[i + 1]] for i in range(len(splits))]
    q_lat, kv_lat, k_r, qkv, z, a, b, g_a, g_b = part
    dm = w_in.shape[0]
    half = rope // 2
    zpad = lambda n: jnp.zeros((dm, n), w_in.dtype)
    k_sw = jnp.concatenate([k_r[:, half:], k_r[:, :half]], axis=1)
    small = jnp.concatenate([q_lat, kv_lat, k_r, zpad(LANE - rope), k_sw, zpad(LANE - rope),
                             a, b, zpad(LANE - 2 * nh)], axis=1)
    return jnp.concatenate([qkv, small, z, g_a, g_b], axis=1).astype(BF16)


def _pack_w_uq(w_uq, nope, rope):
    w = jnp.transpose(w_uq, (1, 0, 2))
    half = rope // 2
    r = w[..., nope:]
    zp = jnp.zeros(r.shape[:-1] + (LANE - rope,), w.dtype)
    r_sw = jnp.concatenate([r[..., half:], r[..., :half]], axis=-1)
    return jnp.concatenate([w[..., :nope], r, zp, r_sw, zp], axis=-1).astype(BF16)


def _rope_tables(pos, rope):
    half = rope // 2
    inv = ROPE_THETA ** (-jnp.arange(half, dtype=F32) / half)
    ang = pos.astype(F32)[:, None] * inv[None, :]
    cos, sin = jnp.cos(ang), jnp.sin(ang)
    zp = jnp.zeros((pos.shape[0], LANE - rope), F32)
    return (jnp.concatenate([cos, cos, zp], axis=1), jnp.concatenate([-sin, sin, zp], axis=1))


def _route_meta(idx_t, rank_t, counts, m, n_exp, blk):
    a = m * TOP_K
    counts = counts.astype(jnp.int32)
    padded = (counts + blk - 1) // blk * blk
    pad_end = jnp.cumsum(padded)
    pad_start = pad_end - padded
    experts = jnp.arange(n_exp, dtype=jnp.int32)
    e_km = idx_t[:TOP_K]
    start_km = jnp.sum(jnp.where(e_km[:, :, None] == experts, pad_start, 0), axis=-1)
    dest_km = (start_km + rank_t[:TOP_K]).astype(jnp.int32).reshape(a)
    nb = a // blk + n_exp
    tok_km = jnp.tile(jnp.arange(m, dtype=jnp.int32), TOP_K)
    rows = jnp.zeros((nb * blk,), jnp.int32).at[dest_km].set(tok_km)
    first_row = jnp.arange(nb, dtype=jnp.int32) * blk
    block_e = jnp.minimum(jnp.sum((pad_end[None, :] <= first_row[:, None]).astype(jnp.int32), axis=1),
                          n_exp - 1).astype(jnp.int32)
    n_used = (pad_end[-1] // blk).astype(jnp.int32).reshape(1)
    return block_e, rows, n_used, dest_km


def kernel(x_prompt, x_sample, cache_ckv, cache_krope, page_table, state_conv, state_ssm, w_in, q_norm_w, kv_norm_w, w_uq, w_uk, w_uv, conv_w, a_log, dt_bias, gdn_norm_w, w_o, ln1_g, ln1_b, w_router, b_router, w_gu, b_gu, w_down, b_down, ln2_g, ln2_b):
    bp, tp, d = x_prompt.shape
    bs, ts, _ = x_sample.shape
    depth = w_in.shape[0]
    q_lora, n_heads, qk = w_uq.shape[1:]
    kv_lora, _, nope = w_uk.shape[1:]
    rope = qk - nope
    vh = w_uv.shape[3]
    cw_taps, conv_ch = conv_w.shape[1:]
    g_heads = a_log.shape[1]
    dk, dv = state_ssm.shape[3:]
    n_exp = w_router.shape[2]
    page = cache_ckv.shape[2]
    past = page_table.shape[1] * page
    splits = (q_lora, kv_lora, rope, conv_ch, g_heads * dv, g_heads, g_heads, d, d)
    assert sum(splits) == w_in.shape[2]
    alpha = (2 * depth) ** 0.25
    scale = (nope + rope) ** -0.5 * LOG2E
    mp, ms = bp * tp, bs * ts
    m = mp + ms
    small_w = q_lora + kv_lora + 3 * LANE
    assert small_w == d and conv_ch % d == 0
    qkv_blk, small_blk = 0, conv_ch // d
    z_blk, ga_blk, gb_blk = small_blk + 1, small_blk + 2, small_blk + 3
    ab_blk = (conv_ch + q_lora + kv_lora + 2 * LANE) // LANE

    cos_p, sin_p = _rope_tables(jnp.arange(tp, dtype=jnp.int32), rope)
    cos_s, sin_s = _rope_tables(past + jnp.arange(ts, dtype=jnp.int32), rope)
    cos_t = jnp.concatenate([jnp.tile(cos_p, (bp, 1)), jnp.tile(cos_s, (bs, 1))], axis=0)
    sin_t = jnp.concatenate([jnp.tile(sin_p, (bp, 1)), jnp.tile(sin_s, (bs, 1))], axis=0)

    x_p, x_s = x_prompt.reshape(mp, d), x_sample.reshape(ms, d)
    outs = {k: [] for k in ("ckv_p", "kr_p", "conv_p", "ssm_p", "ckv_s", "kr_s", "conv_s", "ssm_s")}
    pad_lanes = lambda v: jnp.pad(v, (0, LANE - v.shape[0])).reshape(1, LANE)
    for l in range(depth):
        x = jnp.concatenate([x_p, x_s], axis=0)
        w_pack = _pack_w_in(w_in[l], splits)
        wq = _pack_w_uq(w_uq[l], nope, rope)
        wuk = jnp.transpose(w_uk[l], (1, 2, 0)).astype(BF16)
        wuv = jnp.transpose(w_uv[l], (1, 0, 2)).astype(BF16)
        h = _in_proj(x, w_pack, tm=_row_tile(m, 2304), tn=512)
        qf, c_all, kr_all, kt, cb = _mla_proj(
            h, small_blk, cos_t, sin_t, q_norm_w[l].reshape(1, q_lora), kv_norm_w[l].reshape(1, kv_lora),
            wq, wuk, tm=256, q_lora=q_lora, kv_lora=kv_lora, nope=nope, rope=rope, scale=scale)
        om_p = _attn_prompt(qf, kt, cb, wuv, batch=bp, seq=tp, tq=128, tk=min(1024, tp))
        qs = qf[:, mp:, :].reshape(n_heads, bs, ts, qf.shape[-1])
        qs = jnp.transpose(qs, (1, 0, 2, 3)).reshape(bs, n_heads * ts, qf.shape[-1])
        om_s = _attn_sample(page_table, qs, c_all, kr_all, mp // ts, cache_ckv[l],
                            jnp.swapaxes(cache_krope[l], 1, 2), wuv)

        alog_p = pad_lanes(a_log[l])
        dtb_p = pad_lanes(dt_bias[l])
        nw = gdn_norm_w[l].reshape(1, dv)
        gdn_kw = dict(qkv_blk=qkv_blk, ab_blk=ab_blk, z_blk=z_blk, conv_w=cw_taps)
        zeros_conv = jnp.zeros((bp, SUBLANE, conv_ch), F32)
        zeros_ssm = jnp.zeros((bp, g_heads, dk, dv), F32)
        og_p, ssm_p, ncv_p = _gdn(h, 0, zeros_conv, zeros_ssm, conv_w[l], alog_p, dtb_p, nw,
                                  batch=bp, seq=tp, chunk=min(GDN_CHUNK, tp), group=bp, **gdn_kw)
        conv8_s = jnp.pad(state_conv[l], ((0, 0), (SUBLANE - (cw_taps - 1), 0), (0, 0)))
        og_s, ssm_s, ncv_s = _gdn(h, mp, conv8_s, state_ssm[l], conv_w[l], alog_p, dtb_p, nw,
                                  batch=bs, seq=ts, chunk=ts, group=math.gcd(bs, GDN_SAMPLE_GROUP), **gdn_kw)

        om = jnp.concatenate([om_p, om_s], axis=0)
        og = jnp.concatenate([og_p, og_s], axis=0)
        x1, idx_t, gate_t, rank_t, counts = _post(
            x, om, og, h, ga_blk, gb_blk, w_o[l].astype(BF16), ln1_g[l].reshape(1, d), ln1_b[l].reshape(1, d),
            w_router[l].T, b_router[l].reshape(n_exp, 1), tm=256, alpha=alpha)
        block_e, rows, n_used, dest_km = _route_meta(idx_t, rank_t, counts[:, 0], m, n_exp, MOE_ROWS)
        ys = _moe_experts(block_e, rows, n_used, x1, w_gu[l], b_gu[l], w_down[l], b_down[l], blk=MOE_ROWS)
        x_p, x_s = _combine(dest_km, x1, gate_t.T, ys, ln2_g[l].reshape(1, d), ln2_b[l].reshape(1, d),
                            tm=128, alpha=alpha, m_first=mp)

        outs["ckv_p"].append(c_all[:mp].reshape(bp, tp, kv_lora))
        outs["kr_p"].append(kr_all[:mp].reshape(bp, tp, rope))
        outs["conv_p"].append(ncv_p[:, SUBLANE - (cw_taps - 1):, :])
        outs["ssm_p"].append(ssm_p)
        outs["ckv_s"].append(c_all[mp:].reshape(bs, ts, kv_lora))
        outs["kr_s"].append(kr_all[mp:].reshape(bs, ts, rope))
        outs["conv_s"].append(ncv_s[:, SUBLANE - (cw_taps - 1):, :])
        outs["ssm_s"].append(ssm_s)

    return (x_p.reshape(bp, tp, d), x_s.reshape(bs, ts, d),
            jnp.stack(outs["ckv_p"]), jnp.stack(outs["kr_p"]), jnp.stack(outs["conv_p"]), jnp.stack(outs["ssm_p"]),
            jnp.stack(outs["ckv_s"]), jnp.stack(outs["kr_s"]), jnp.stack(outs["conv_s"]), jnp.stack(outs["ssm_s"]))
```

```python
import functools
import math

import jax
import jax.numpy as jnp
from jax import lax
from jax.experimental import pallas as pl
from jax.experimental.pallas import tpu as pltpu

F32 = jnp.float32
BF16 = jnp.bfloat16

ROPE_THETA = 10000.0
NORM_EPS = 1e-6
TOP_K = 4
SWIGLU_LIMIT = 7.0
SWIGLU_ALPHA = 1.702
GDN_CHUNK = 64
GDN_DIAG = 16
GDN_SAMPLE_GROUP = 8
MOE_ROWS = 512
PAGE_CHUNK = 8
LANE = 128
SUBLANE = 8
VMEM_LIMIT = 56 * 1024 * 1024
NEG = -1e30
LOG2E = 1.4426950408889634


def _cparams(sem, vmem=None):
    return pltpu.CompilerParams(dimension_semantics=sem, vmem_limit_bytes=vmem)


def _dot(a, b, **kw):
    return jnp.dot(a, b, preferred_element_type=F32, **kw)


def _dot_nt(a, b, **kw):
    return lax.dot_general(a, b, (((1,), (1,)), ((), ())), preferred_element_type=F32, **kw)


def _bdot(a, b):
    return lax.dot_general(a, b, (((2,), (1,)), ((0,), (0,))), preferred_element_type=F32)


def _bdot_nt(a, b):
    return lax.dot_general(a, b, (((2,), (2,)), ((0,), (0,))), preferred_element_type=F32)


def _bdot_tn(a, b):
    return lax.dot_general(a, b, (((1,), (1,)), ((0,), (0,))), preferred_element_type=F32)


def _sigmoid(x):
    return 1.0 / (1.0 + jnp.exp(-x))


def _layer_norm(v, g, b):
    mu = jnp.mean(v, -1, keepdims=True)
    vc = v - mu
    var = jnp.mean(vc * vc, -1, keepdims=True)
    return vc * lax.rsqrt(var + NORM_EPS) * g + b


def _inproj_kernel(x_ref, w_ref, o_ref, xb_ref):
    @pl.when(pl.program_id(1) == 0)
    def _():
        xb_ref[...] = x_ref[...].astype(BF16)

    o_ref[...] = _dot(xb_ref[...], w_ref[...])


def _row_tile(m, target):
    return max(t for t in range(SUBLANE, target + 1, SUBLANE) if m % t == 0)


def _in_proj(x, w, tm, tn):
    m, k = x.shape
    n = w.shape[1]
    return pl.pallas_call(
        _inproj_kernel,
        grid=(m // tm, n // tn),
        in_specs=[pl.BlockSpec((tm, k), lambda i, j: (i, 0)),
                  pl.BlockSpec((k, tn), lambda i, j: (0, j))],
        out_specs=pl.BlockSpec((tm, tn), lambda i, j: (i, j)),
        out_shape=jax.ShapeDtypeStruct((m, n), F32),
        scratch_shapes=[pltpu.VMEM((tm, k), BF16)],
        compiler_params=_cparams(("parallel", "arbitrary"), VMEM_LIMIT),
        name="in_proj",
    )(x, w)


def _mla_proj_kernel(h_ref, cos_ref, sin_ref, qnw_ref, kvnw_ref, wq_ref, wuk_ref,
                     qf_ref, c_ref, kr_ref, kt_ref, cb_ref, *, n_heads, q_lora, kv_lora, nope, rope, scale):
    hs = h_ref[...]
    cos = cos_ref[...]
    sin = sin_ref[...]
    q_lat = hs[:, :q_lora]
    qn = q_lat * lax.rsqrt(jnp.mean(q_lat * q_lat, -1, keepdims=True) + NORM_EPS) * qnw_ref[...]
    qn = qn.astype(BF16)
    kv = hs[:, q_lora:q_lora + kv_lora]
    c = kv * lax.rsqrt(jnp.mean(kv * kv, -1, keepdims=True) + NORM_EPS) * kvnw_ref[...]
    o = q_lora + kv_lora
    kr = hs[:, o:o + LANE] * cos + hs[:, o + LANE:o + 2 * LANE] * sin
    c_ref[...] = c
    kr_ref[...] = kr[:, :rope]
    cb = c.astype(BF16)
    cb_ref[...] = cb
    kfull = jnp.concatenate([cb, kr.astype(BF16)], axis=1)
    kw = kfull.shape[1]
    eye = (lax.broadcasted_iota(jnp.int32, (kw, kw), 0) == lax.broadcasted_iota(jnp.int32, (kw, kw), 1))
    kt_ref[...] = _dot_nt(eye.astype(BF16), kfull).astype(BF16)
    for h in range(n_heads):
        qh = _dot(qn, wq_ref[h])
        qa = _dot(qh[:, :nope].astype(BF16), wuk_ref[h])
        qr = qh[:, nope:nope + LANE] * cos + qh[:, nope + LANE:nope + 2 * LANE] * sin
        qf_ref[h, :, :kv_lora] = (qa * scale).astype(BF16)
        qf_ref[h, :, kv_lora:] = (qr * scale).astype(BF16)


def _mla_proj(h, col_blk, cos_t, sin_t, qnw, kvnw, wq, wuk, *, tm, q_lora, kv_lora, nope, rope, scale):
    m = h.shape[0]
    n_heads = wq.shape[0]
    wcol = q_lora + kv_lora + 3 * LANE
    kw = kv_lora + LANE
    kern = functools.partial(_mla_proj_kernel, n_heads=n_heads, q_lora=q_lora, kv_lora=kv_lora,
                             nope=nope, rope=rope, scale=scale)
    return pl.pallas_call(
        kern,
        grid=(m // tm,),
        in_specs=[pl.BlockSpec((tm, wcol), lambda i: (i, col_blk)),
                  pl.BlockSpec((tm, LANE), lambda i: (i, 0)),
                  pl.BlockSpec((tm, LANE), lambda i: (i, 0)),
                  pl.BlockSpec((1, q_lora), lambda i: (0, 0)),
                  pl.BlockSpec((1, kv_lora), lambda i: (0, 0)),
                  pl.BlockSpec(wq.shape, lambda i: (0, 0, 0)),
                  pl.BlockSpec(wuk.shape, lambda i: (0, 0, 0))],
        out_specs=[pl.BlockSpec((n_heads, tm, kw), lambda i: (0, i, 0)),
                   pl.BlockSpec((tm, kv_lora), lambda i: (i, 0)),
                   pl.BlockSpec((tm, rope), lambda i: (i, 0)),
                   pl.BlockSpec((kw, tm), lambda i: (0, i)),
                   pl.BlockSpec((tm, kv_lora), lambda i: (i, 0))],
        out_shape=[jax.ShapeDtypeStruct((n_heads, m, kw), BF16),
                   jax.ShapeDtypeStruct((m, kv_lora), F32),
                   jax.ShapeDtypeStruct((m, rope), F32),
                   jax.ShapeDtypeStruct((kw, m), BF16),
                   jax.ShapeDtypeStruct((m, kv_lora), BF16)],
        compiler_params=_cparams(("parallel",), VMEM_LIMIT),
        name="mla_proj",
    )(h, cos_t, sin_t, qnw, kvnw, wq, wuk)


def _softmax_step(s, v, m_ref, l_ref, acc_ref):
    m_prev = m_ref[...]
    m_new = jnp.maximum(m_prev, jnp.max(s, -1, keepdims=True))
    alpha = jnp.exp2(m_prev - m_new)
    p = jnp.exp2(s - m_new)
    l_ref[...] = alpha * l_ref[...] + jnp.sum(p, -1, keepdims=True)
    acc_ref[...] = alpha * acc_ref[...] + _dot(p.astype(BF16), v)
    m_ref[...] = m_new


def _attn_prompt_kernel(qf_ref, kt_ref, v_ref, wuv_ref, o_ref, m_ref, l_ref, acc_ref, *, n_heads, tq, tk, dv, vh, n_split):
    i = pl.program_id(1)
    m_ref[...] = jnp.full(m_ref.shape, NEG, F32)
    l_ref[...] = jnp.zeros(l_ref.shape, F32)
    acc_ref[...] = jnp.zeros(acc_ref.shape, F32)
    n_full = (i * tq) // tk
    off = i * tq - n_full * tk
    hs = n_heads // n_split
    rs = hs * tq

    def step(j, masked):
        kt = kt_ref[:, pl.ds(pl.multiple_of(j * tk, tk), tk)]
        v = v_ref[pl.ds(pl.multiple_of(j * tk, tk), tk), :]
        for g in range(n_split):
            q = qf_ref[g * hs:(g + 1) * hs].reshape(rs, qf_ref.shape[-1])
            s = _dot(q, kt)
            if masked:
                row = lax.broadcasted_iota(jnp.int32, (tq, tk), 0)
                col = lax.broadcasted_iota(jnp.int32, (tq, tk), 1)
                s = jnp.where((col <= row + off)[None], s.reshape(hs, tq, tk), NEG).reshape(rs, tk)
            sl = slice(g * rs, (g + 1) * rs)
            _softmax_step(s, v, m_ref.at[sl], l_ref.at[sl], acc_ref.at[sl])

    def body(j, carry):
        step(j, False)
        return carry

    lax.fori_loop(0, n_full, body, 0)
    step(n_full, True)
    o = acc_ref[...] / l_ref[...]
    for h in range(n_heads):
        oh = o[h * tq:(h + 1) * tq].astype(BF16)
        o_ref[:, h * vh:(h + 1) * vh] = _dot(oh, wuv_ref[h])


def _attn_prompt(qf, kt, cb, wuv, *, batch, seq, tq, tk):
    n_heads, _, kw = qf.shape
    dv, vh = wuv.shape[1], wuv.shape[2]
    nq = seq // tq
    rows = n_heads * tq
    assert seq % tk == 0 and tk % tq == 0
    kern = functools.partial(_attn_prompt_kernel, n_heads=n_heads, tq=tq, tk=tk, dv=dv, vh=vh, n_split=2)
    return pl.pallas_call(
        kern,
        grid=(batch, nq),
        in_specs=[pl.BlockSpec((n_heads, tq, kw), lambda b, i: (0, b * nq + i, 0)),
                  pl.BlockSpec((kw, seq), lambda b, i: (0, b)),
                  pl.BlockSpec((seq, dv), lambda b, i: (b, 0)),
                  pl.BlockSpec(wuv.shape, lambda b, i: (0, 0, 0))],
        out_specs=pl.BlockSpec((tq, n_heads * vh), lambda b, i: (b * nq + i, 0)),
        out_shape=jax.ShapeDtypeStruct((batch * seq, n_heads * vh), F32),
        scratch_shapes=[pltpu.VMEM((rows, 1), F32), pltpu.VMEM((rows, 1), F32),
                        pltpu.VMEM((rows, dv), F32)],
        compiler_params=_cparams(("parallel", "arbitrary"), VMEM_LIMIT),
        name="attn_prompt",
    )(qf, kt, cb, wuv)


def _attn_sample_kernel(pt_ref, q_ref, cn_ref, krn_ref, cc_hbm, cr_hbm, wuv_ref, o_ref,
                        cbuf, rbuf, sem, *, n_pages, page, n_heads, ts, dv, dr, vh):
    b = pl.program_id(0)
    nb = pl.num_programs(0)
    slot = b % 2
    rows = n_heads * ts

    def fetch(bb, s):
        for p in range(n_pages):
            pg = pt_ref[bb, p]
            pltpu.make_async_copy(cc_hbm.at[pg], cbuf.at[s, p], sem.at[0, s]).start()
            pltpu.make_async_copy(cr_hbm.at[pg], rbuf.at[s, p], sem.at[1, s]).start()

    def wait(s):
        pltpu.make_async_copy(cc_hbm.at[pl.ds(0, n_pages)], cbuf.at[s], sem.at[0, s]).wait()
        pltpu.make_async_copy(cr_hbm.at[pl.ds(0, n_pages)], rbuf.at[s], sem.at[1, s]).wait()

    @pl.when(b == 0)
    def _():
        fetch(0, 0)

    wait(slot)
    fetch(jnp.minimum(b + 1, nb - 1), 1 - slot)

    q = q_ref[0].astype(F32)
    qc = q[:, :dv]
    qr = q[:, dv:dv + dr]
    n_chunks = n_pages // PAGE_CHUNK
    ck = PAGE_CHUNK * page
    parts = []
    for ch in range(n_chunks):
        c_ch = cbuf[slot, ch * PAGE_CHUNK:(ch + 1) * PAGE_CHUNK].reshape(ck, dv)
        r_ch = jnp.concatenate([rbuf[slot, ch * PAGE_CHUNK + u] for u in range(PAGE_CHUNK)], axis=1)
        parts.append(_dot_nt(qc, c_ch) + _dot(qr, r_ch))
    cn = jnp.concatenate([cn_ref[...], jnp.zeros((LANE - ts, dv), F32)], axis=0)
    krn = jnp.concatenate([krn_ref[...], jnp.zeros((LANE - ts, dr), F32)], axis=0)
    row = lax.broadcasted_iota(jnp.int32, (ts, LANE), 0)
    col = lax.broadcasted_iota(jnp.int32, (ts, LANE), 1)
    s_new = (_dot_nt(qc, cn) + _dot_nt(qr, krn)).reshape(n_heads, ts, LANE)
    s_new = jnp.where((col <= row)[None], s_new, NEG).reshape(rows, LANE)

    m = jnp.max(s_new, -1, keepdims=True)
    for s in parts:
        m = jnp.maximum(m, jnp.max(s, -1, keepdims=True))
    p_new = jnp.exp2(s_new - m)
    l = jnp.sum(p_new, -1, keepdims=True)
    acc = _dot(p_new, cn)
    for ch in range(n_chunks):
        p = jnp.exp2(parts[ch] - m)
        l = l + jnp.sum(p, -1, keepdims=True)
        acc = acc + _dot(p, cbuf[slot, ch * PAGE_CHUNK:(ch + 1) * PAGE_CHUNK].reshape(ck, dv))
    o = acc / l
    for h in range(n_heads):
        oh = o[h * ts:(h + 1) * ts].astype(BF16)
        o_ref[:, h * vh:(h + 1) * vh] = _dot(oh, wuv_ref[h])

    @pl.when(b == nb - 1)
    def _():
        wait(1 - slot)


def _attn_sample(page_table, qs, c_all, kr_all, row_blk_off, cache_c, cache_rt, wuv):
    bs, rows, kw = qs.shape
    n_heads, dv, vh = wuv.shape
    ts = rows // n_heads
    n_pages = page_table.shape[1]
    page = cache_c.shape[1]
    dr = cache_rt.shape[1]
    assert n_pages % PAGE_CHUNK == 0
    kern = functools.partial(_attn_sample_kernel, n_pages=n_pages, page=page, n_heads=n_heads, ts=ts,
                             dv=dv, dr=dr, vh=vh)
    grid_spec = pltpu.PrefetchScalarGridSpec(
        num_scalar_prefetch=1,
        grid=(bs,),
        in_specs=[pl.BlockSpec((1, rows, kw), lambda b, pt: (b, 0, 0)),
                  pl.BlockSpec((ts, dv), lambda b, pt: (row_blk_off + b, 0)),
                  pl.BlockSpec((ts, dr), lambda b, pt: (row_blk_off + b, 0)),
                  pl.BlockSpec(memory_space=pl.ANY),
                  pl.BlockSpec(memory_space=pl.ANY),
                  pl.BlockSpec(wuv.shape, lambda b, pt: (0, 0, 0))],
        out_specs=pl.BlockSpec((ts, n_heads * vh), lambda b, pt: (b, 0)),
        scratch_shapes=[pltpu.VMEM((2, n_pages, page, dv), F32), pltpu.VMEM((2, n_pages, dr, page), F32),
                        pltpu.SemaphoreType.DMA((2, 2))])
    return pl.pallas_call(
        kern,
        grid_spec=grid_spec,
        out_shape=jax.ShapeDtypeStruct((bs * ts, n_heads * vh), F32),
        compiler_params=_cparams(("arbitrary",), VMEM_LIMIT),
        name="attn_sample",
    )(page_table, qs, c_all, kr_all, cache_c, cache_rt, wuv)


def _neumann_inv(low, eye, steps):
    p = eye - low
    x = low
    for _ in range(steps):
        x = _bdot(x, x)
        p = p + _bdot(p, x)
    return p


def _gdn_kernel(*refs, n_in, n_seq, chunk, n_heads, dk, dv, conv_w):
    qkv_refs, ab_refs, z_refs = refs[:n_in], refs[n_in:2 * n_in], refs[2 * n_in:3 * n_in]
    (cs_ref, s0_ref, cw_ref, alog_ref, dtb_ref, nw_ref,
     o_ref, sfin_ref, ncv_ref, xp_ref, s_ref) = refs[3 * n_in:]
    n = pl.program_id(1)
    c = chunk
    per = n_seq // n_in
    n_prob = n_seq * n_heads
    hi = lax.Precision.HIGHEST

    def seq_rows(group, g):
        k = g % per
        return group[g // per][k * c:(k + 1) * c, :]

    @pl.when(n == 0)
    def _():
        s_ref[...] = s0_ref[...].reshape(n_prob, dk, dv)
        xp_ref[:, 0:SUBLANE, :] = cs_ref[...]

    r_i = lax.broadcasted_iota(jnp.int32, (c, c), 0)
    c_i = lax.broadcasted_iota(jnp.int32, (c, c), 1)
    causal = (c_i <= r_i)[None]
    strict = (c_i < r_i)[None]
    eye_c = (r_i == c_i).astype(F32)[None]
    db = min(GDN_DIAG, c)
    n_blk = c // db
    sh = int(math.log2(db))
    same_blk = (jnp.right_shift(r_i, sh) == jnp.right_shift(c_i, sh))[None]
    r_l = lax.broadcasted_iota(jnp.int32, (LANE, LANE), 0)
    c_l = lax.broadcasted_iota(jnp.int32, (LANE, LANE), 1)
    eye_l = (r_l == c_l).astype(F32)
    tril = (c_i <= r_i).astype(F32)
    cw = cw_ref[...]
    hk = n_heads * dk

    q_l, k_l, v_l, z_l, beta_l, gcol_l, grow_l, tails = [], [], [], [], [], [], [], []
    for g in range(n_seq):
        xp_ref[g, SUBLANE:SUBLANE + c, :] = seq_rows(qkv_refs, g)
        conv = xp_ref[g, pl.ds(SUBLANE - (conv_w - 1), c), :] * cw[0:1, :]
        for j in range(1, conv_w):
            conv = conv + xp_ref[g, pl.ds(SUBLANE - (conv_w - 1) + j, c), :] * cw[j:j + 1, :]
        tail = xp_ref[g, c:c + SUBLANE, :]
        xp_ref[g, 0:SUBLANE, :] = tail
        tails.append(tail)
        act = conv * _sigmoid(conv)
        ab = seq_rows(ab_refs, g)
        apb = ab + dtb_ref[...]
        softplus = jnp.maximum(apb, 0.0) + jnp.log(1.0 + jnp.exp(-jnp.abs(apb)))
        gfull = -jnp.exp(alog_ref[...]) * softplus
        betaf = _sigmoid(ab)
        gc = _dot(tril, gfull, precision=hi)
        gc_t = _dot_nt(eye_l, gc, precision=hi)
        zg = seq_rows(z_refs, g)
        for h in range(n_heads):
            q_l.append(act[:, h * dk:(h + 1) * dk])
            k_l.append(act[:, hk + h * dk:hk + (h + 1) * dk])
            v_l.append(act[:, 2 * hk + h * dv:2 * hk + (h + 1) * dv])
            z_l.append(zg[:, h * dv:(h + 1) * dv])
            beta_l.append(betaf[:, n_heads + h:n_heads + h + 1])
            gcol_l.append(gc[:, h:h + 1])
            grow_l.append(gc_t[h:h + 1, :])

    q = jnp.stack(q_l)
    k = jnp.stack(k_l)
    v = jnp.stack(v_l)
    z = jnp.stack(z_l)
    beta = jnp.stack(beta_l)
    gcol = jnp.stack(gcol_l)
    grow = jnp.stack(grow_l)
    q = q * lax.rsqrt(jnp.sum(q * q, -1, keepdims=True) + NORM_EPS) * (dk ** -0.5)
    k = k * lax.rsqrt(jnp.sum(k * k, -1, keepdims=True) + NORM_EPS)
    decay = jnp.where(causal, jnp.exp(jnp.where(causal, gcol - grow, 0.0)), 0.0)
    kb = k * beta
    lower = jnp.where(strict, _bdot_nt(kb, k) * decay, 0.0)
    attn = _bdot_nt(q, k) * decay
    l_d = jnp.where(same_blk, lower, 0.0)
    t_inv = _neumann_inv(l_d, eye_c, int(math.log2(db)) - 1)
    if n_blk > 1:
        m_inv = _neumann_inv(_bdot(t_inv, lower - l_d), eye_c, int(math.log2(n_blk)) - 1)
        t_inv = _bdot(m_inv, t_inv)
    egc = jnp.exp(gcol)
    uw = _bdot(t_inv, jnp.concatenate([v * beta, kb * egc], axis=2))
    s = s_ref[...]
    v_new = uw[:, :, :dv] - _bdot(uw[:, :, dv:], s)
    o = _bdot(q * egc, s) + _bdot(attn, v_new)
    glast = gcol[:, c - 1:c, :]
    kdec = k * jnp.exp(glast - gcol)
    s_new = s * jnp.exp(glast) + _bdot_tn(kdec, v_new)
    s_ref[...] = s_new
    on = o * lax.rsqrt(jnp.mean(o * o, -1, keepdims=True) + NORM_EPS) * nw_ref[...] * (z * _sigmoid(z))
    for g in range(n_seq):
        for h in range(n_heads):
            o_ref[g, :, h * dv:(h + 1) * dv] = on[g * n_heads + h]

    @pl.when(n == pl.num_programs(1) - 1)
    def _():
        sfin_ref[...] = s_new.reshape(n_seq, n_heads, dk, dv)
        for g in range(n_seq):
            ncv_ref[g] = tails[g]


def _gdn(h, row_off, conv_state8, ssm_state, cw, alog_p, dtb_p, nw, *, batch, seq, chunk, group,
         qkv_blk, ab_blk, z_blk, conv_w):
    _, n_heads, dk, dv = ssm_state.shape
    nc = seq // chunk
    ch = cw.shape[1]
    hd = n_heads * dv
    contiguous = nc == 1
    n_in = 1 if contiguous else group
    assert batch % group == 0 and row_off % (group * chunk) == 0
    kern = functools.partial(_gdn_kernel, n_in=n_in, n_seq=group, chunk=chunk, n_heads=n_heads,
                             dk=dk, dv=dv, conv_w=conv_w)
    if contiguous:
        rb = group * chunk
        row_maps = [lambda i, n: row_off // rb + i]
    else:
        rb = chunk
        row_maps = [(lambda i, n, g=g: row_off // rb + (i * group + g) * nc + n) for g in range(group)]

    def specs(width, col_blk):
        return [pl.BlockSpec((rb, width), lambda i, n, r=r: (r(i, n), col_blk)) for r in row_maps]

    in_specs = specs(ch, qkv_blk) + specs(LANE, ab_blk) + specs(hd, z_blk)
    in_specs += [pl.BlockSpec((group, SUBLANE, ch), lambda i, n: (i, 0, 0)),
                 pl.BlockSpec((group, n_heads, dk, dv), lambda i, n: (i, 0, 0, 0)),
                 pl.BlockSpec(cw.shape, lambda i, n: (0, 0)),
                 pl.BlockSpec((1, LANE), lambda i, n: (0, 0)),
                 pl.BlockSpec((1, LANE), lambda i, n: (0, 0)),
                 pl.BlockSpec((1, dv), lambda i, n: (0, 0))]
    o3, sfin, ncv = pl.pallas_call(
        kern,
        grid=(batch // group, nc),
        in_specs=in_specs,
        out_specs=[pl.BlockSpec((group, chunk, hd), lambda i, n: (i, n, 0)),
                   pl.BlockSpec((group, n_heads, dk, dv), lambda i, n: (i, 0, 0, 0)),
                   pl.BlockSpec((group, SUBLANE, ch), lambda i, n: (i, 0, 0))],
        out_shape=[jax.ShapeDtypeStruct((batch, seq, hd), F32),
                   jax.ShapeDtypeStruct(ssm_state.shape, F32),
                   jax.ShapeDtypeStruct((batch, SUBLANE, ch), F32)],
        scratch_shapes=[pltpu.VMEM((group, chunk + SUBLANE, ch), F32),
                        pltpu.VMEM((group * n_heads, dk, dv), F32)],
        compiler_params=_cparams(("parallel", "arbitrary"), VMEM_LIMIT),
        name="gdn",
    )(*([h] * (3 * n_in)), conv_state8, ssm_state, cw, alog_p, dtb_p, nw)
    return o3.reshape(batch * seq, hd), sfin, ncv


def _post_kernel(x_ref, omp_ref, oms_ref, ogp_ref, ogs_ref, ga_ref, gb_ref, wo_ref, g1_ref, b1_ref, wr_ref, br_ref,
                 x1_ref, idx_ref, gate_ref, rank_ref, cnt_ref, carry_ref, *, alpha, top_k, n_first):
    @pl.when(pl.program_id(0) == 0)
    def _():
        carry_ref[...] = jnp.zeros(carry_ref.shape, F32)

    first = pl.program_id(0) < n_first
    om = jnp.where(first, omp_ref[...], oms_ref[...])
    og = jnp.where(first, ogp_ref[...], ogs_ref[...])
    mix = _sigmoid(ga_ref[...]) * om + _sigmoid(gb_ref[...]) * og
    y = _dot(mix.astype(BF16), wo_ref[...])
    x1 = _layer_norm(alpha * x_ref[...] + y, g1_ref[...], b1_ref[...])
    x1_ref[...] = x1
    logits = _dot_nt(wr_ref[...], x1, precision=lax.Precision.HIGHEST) + br_ref[...]
    n_exp, tm = logits.shape
    e_i = lax.broadcasted_iota(jnp.int32, (n_exp, tm), 0)
    vals, idxs = [], []
    for _ in range(top_k):
        mx = jnp.max(logits, axis=0, keepdims=True)
        ix = jnp.min(jnp.where(logits == mx, e_i, n_exp), axis=0, keepdims=True)
        vals.append(mx)
        idxs.append(ix)
        logits = jnp.where(e_i == ix, -jnp.inf, logits)
    es = [jnp.exp(v - vals[0]) for v in vals]
    tot = es[0]
    for e in es[1:]:
        tot = tot + e
    onehots = [e_i == ix for ix in idxs]
    sel = onehots[0].astype(F32)
    for oh in onehots[1:]:
        sel = sel + oh.astype(F32)
    t_r = lax.broadcasted_iota(jnp.int32, (tm, tm), 0)
    t_c = lax.broadcasted_iota(jnp.int32, (tm, tm), 1)
    before = _dot(sel.astype(BF16), (t_r < t_c).astype(BF16)) + carry_ref[...]
    carry = carry_ref[...] + jnp.sum(sel, axis=1, keepdims=True)
    carry_ref[...] = carry
    cnt_ref[...] = carry
    r_i = lax.broadcasted_iota(jnp.int32, (SUBLANE, tm), 0)
    idx_o = jnp.zeros((SUBLANE, tm), jnp.int32)
    gate_o = jnp.zeros((SUBLANE, tm), F32)
    rank_o = jnp.zeros((SUBLANE, tm), F32)
    for k in range(top_k):
        idx_o = jnp.where(r_i == k, idxs[k], idx_o)
        gate_o = jnp.where(r_i == k, es[k] / tot, gate_o)
        rank_k = jnp.sum(jnp.where(onehots[k], before, 0.0), axis=0, keepdims=True)
        rank_o = jnp.where(r_i == k, rank_k, rank_o)
    idx_ref[...] = idx_o
    gate_ref[...] = gate_o
    rank_ref[...] = rank_o.astype(jnp.int32)


def _post(x, om_p, om_s, og_p, og_s, h, ga_blk, gb_blk, wo, g1, b1, wr_t, br, *, tm, alpha):
    m, d = x.shape
    n_exp = wr_t.shape[0]
    n_first = om_p.shape[0] // tm
    assert om_p.shape[0] % tm == 0 and om_s.shape[0] % tm == 0
    kern = functools.partial(_post_kernel, alpha=alpha, top_k=TOP_K, n_first=n_first)
    row = lambda i: (i, 0)
    const = lambda i: (0, 0)
    first = lambda i: (jnp.minimum(i, n_first - 1), 0)
    second = lambda i: (jnp.maximum(i - n_first, 0), 0)
    return pl.pallas_call(
        kern,
        grid=(m // tm,),
        in_specs=[pl.BlockSpec((tm, d), row),
                  pl.BlockSpec((tm, d), first), pl.BlockSpec((tm, d), second),
                  pl.BlockSpec((tm, d), first), pl.BlockSpec((tm, d), second),
                  pl.BlockSpec((tm, d), lambda i: (i, ga_blk)),
                  pl.BlockSpec((tm, d), lambda i: (i, gb_blk)),
                  pl.BlockSpec(wo.shape, const), pl.BlockSpec((1, d), const), pl.BlockSpec((1, d), const),
                  pl.BlockSpec(wr_t.shape, const), pl.BlockSpec((n_exp, 1), const)],
        out_specs=[pl.BlockSpec((tm, d), row),
                   pl.BlockSpec((SUBLANE, tm), lambda i: (0, i)),
                   pl.BlockSpec((SUBLANE, tm), lambda i: (0, i)),
                   pl.BlockSpec((SUBLANE, tm), lambda i: (0, i)),
                   pl.BlockSpec((n_exp, 1), const)],
        out_shape=[jax.ShapeDtypeStruct((m, d), F32),
                   jax.ShapeDtypeStruct((SUBLANE, m), jnp.int32),
                   jax.ShapeDtypeStruct((SUBLANE, m), F32),
                   jax.ShapeDtypeStruct((SUBLANE, m), jnp.int32),
                   jax.ShapeDtypeStruct((n_exp, 1), F32)],
        scratch_shapes=[pltpu.VMEM((n_exp, 1), F32)],
        compiler_params=_cparams(("arbitrary",), VMEM_LIMIT),
        name="post_mix",
    )(x, om_p, om_s, og_p, og_s, h, h, wo, g1, b1, wr_t, br)


def _moe_kernel(be_ref, rows_ref, nused_ref, x_hbm, wgu_ref, bgu_ref, wd_ref, bd_ref, o_ref,
                xbuf, sem, wgu_bf, wd_bf, xb_ref, *, blk, d_exp):
    i = pl.program_id(0)
    n_used = nused_ref[0]
    slot = i % 2

    def row_copy(tok, s, r):
        return pltpu.make_async_copy(x_hbm.at[pl.ds(tok, 1)], xbuf.at[s, pl.ds(r, 1)], sem.at[s])

    def wait(s):
        pltpu.make_async_copy(x_hbm.at[pl.ds(0, blk)], xbuf.at[s], sem.at[s]).wait()

    @pl.when(jnp.logical_and(i == 0, n_used > 0))
    def _():
        second = jnp.minimum(1, n_used - 1) * blk

        def body(r, carry):
            row_copy(rows_ref[r], 0, r).start()
            row_copy(rows_ref[second + r], 1, r).start()
            return carry
        lax.fori_loop(0, blk, body, 0)

    @pl.when(i < n_used)
    def _():
        e = be_ref[i]
        e_prev = be_ref[jnp.maximum(i - 1, 0)]

        @pl.when(jnp.logical_or(i == 0, e != e_prev))
        def _():
            wgu_bf[...] = wgu_ref[0].astype(BF16)
            wd_bf[...] = wd_ref[0].astype(BF16)

        wait(slot)
        xb_ref[...] = xbuf[slot].astype(BF16)
        nxt = jnp.minimum(i + 2, n_used - 1) * blk
        for r in range(blk):
            row_copy(rows_ref[nxt + r], slot, r).start()
        hh = _dot(xb_ref[...], wgu_bf[...]) + bgu_ref[0]
        gate = jnp.minimum(hh[:, :d_exp], SWIGLU_LIMIT)
        up = jnp.clip(hh[:, d_exp:], -SWIGLU_LIMIT, SWIGLU_LIMIT)
        act = (up + 1.0) * gate * _sigmoid(SWIGLU_ALPHA * gate)
        o_ref[...] = _dot(act.astype(BF16), wd_bf[...]) + bd_ref[0]

        @pl.when(i == n_used - 1)
        def _():
            wait(0)
            wait(1)

    @pl.when(i >= n_used)
    def _():
        o_ref[...] = jnp.zeros(o_ref.shape, F32)


def _moe_experts(block_e, rows, n_used, x1, w_gu, b_gu, w_down, b_down, *, blk):
    n_exp, d, d2 = w_gu.shape
    d_exp = d2 // 2
    nb = block_e.shape[0]
    kern = functools.partial(_moe_kernel, blk=blk, d_exp=d_exp)
    grid_spec = pltpu.PrefetchScalarGridSpec(
        num_scalar_prefetch=3,
        grid=(nb,),
        in_specs=[pl.BlockSpec(memory_space=pl.ANY),
                  pl.BlockSpec((1, d, d2), lambda i, be, rw, nu: (be[i], 0, 0)),
                  pl.BlockSpec((1, 1, d2), lambda i, be, rw, nu: (be[i], 0, 0)),
                  pl.BlockSpec((1, d_exp, d), lambda i, be, rw, nu: (be[i], 0, 0)),
                  pl.BlockSpec((1, 1, d), lambda i, be, rw, nu: (be[i], 0, 0))],
        out_specs=pl.BlockSpec((blk, d), lambda i, be, rw, nu: (i, 0)),
        scratch_shapes=[pltpu.VMEM((2, blk, d), F32), pltpu.SemaphoreType.DMA((2,)),
                        pltpu.VMEM((d, d2), BF16), pltpu.VMEM((d_exp, d), BF16),
                        pltpu.VMEM((blk, d), BF16)])
    return pl.pallas_call(
        kern,
        grid_spec=grid_spec,
        out_shape=jax.ShapeDtypeStruct((nb * blk, d), F32),
        compiler_params=_cparams(("arbitrary",), VMEM_LIMIT),
        name="moe_experts",
    )(block_e, rows, n_used, x1, w_gu, b_gu.reshape(n_exp, 1, d2), w_down, b_down.reshape(n_exp, 1, d))


def _combine_kernel(dest_ref, x1_ref, gate_ref, ys_hbm, g2_ref, b2_ref, op_ref, os_ref, ybuf, sem,
                    *, tm, top_k, alpha, m_total, n_first):
    i = pl.program_id(0)
    nsteps = pl.num_programs(0)
    slot = i % 2
    n_rows = top_k * tm

    def row_copy(d, s, r):
        return pltpu.make_async_copy(ys_hbm.at[pl.ds(d, 1)], ybuf.at[s, pl.ds(r, 1)], sem.at[s])

    def wait(s):
        pltpu.make_async_copy(ys_hbm.at[pl.ds(0, n_rows)], ybuf.at[s], sem.at[s]).wait()

    @pl.when(i == 0)
    def _():
        second = jnp.minimum(1, nsteps - 1) * tm

        def body(r, carry):
            k = r // tm
            t = r - k * tm
            row_copy(dest_ref[k * m_total + t], 0, r).start()
            row_copy(dest_ref[k * m_total + second + t], 1, r).start()
            return carry
        lax.fori_loop(0, n_rows, body, 0)

    wait(slot)
    g = gate_ref[...]
    y = g[:, 0:1] * ybuf[slot, 0:tm, :]
    for k in range(1, top_k):
        y = y + g[:, k:k + 1] * ybuf[slot, k * tm:(k + 1) * tm, :]
    nxt = jnp.minimum(i + 2, nsteps - 1) * tm
    for r in range(n_rows):
        k, t = divmod(r, tm)
        row_copy(dest_ref[k * m_total + nxt + t], slot, r).start()
    res = _layer_norm(alpha * x1_ref[...] + y, g2_ref[...], b2_ref[...])

    @pl.when(i < n_first)
    def _():
        op_ref[...] = res

    @pl.when(i >= n_first)
    def _():
        os_ref[...] = res

    @pl.when(i == nsteps - 1)
    def _():
        wait(0)
        wait(1)


def _combine(dest_km, x1, gates_mk, ys, g2, b2, *, tm, alpha, m_first):
    m, d = x1.shape
    n_first = m_first // tm
    assert m_first % tm == 0 and 0 < m_first < m
    kern = functools.partial(_combine_kernel, tm=tm, top_k=TOP_K, alpha=alpha, m_total=m, n_first=n_first)
    grid_spec = pltpu.PrefetchScalarGridSpec(
        num_scalar_prefetch=1,
        grid=(m // tm,),
        in_specs=[pl.BlockSpec((tm, d), lambda i, ds: (i, 0)),
                  pl.BlockSpec((tm, SUBLANE), lambda i, ds: (i, 0)),
                  pl.BlockSpec(memory_space=pl.ANY),
                  pl.BlockSpec((1, d), lambda i, ds: (0, 0)),
                  pl.BlockSpec((1, d), lambda i, ds: (0, 0))],
        out_specs=[pl.BlockSpec((tm, d), lambda i, ds: (jnp.minimum(i, n_first - 1), 0)),
                   pl.BlockSpec((tm, d), lambda i, ds: (jnp.maximum(i - n_first, 0), 0))],
        scratch_shapes=[pltpu.VMEM((2, TOP_K * tm, d), F32), pltpu.SemaphoreType.DMA((2,))])
    return pl.pallas_call(
        kern,
        grid_spec=grid_spec,
        out_shape=[jax.ShapeDtypeStruct((m_first, d), F32), jax.ShapeDtypeStruct((m - m_first, d), F32)],
        compiler_params=_cparams(("arbitrary",), VMEM_LIMIT),
        name="moe_combine",
    )(dest_km, x1, gates_mk, ys, g2, b2)


def _pack_w_in(w_in, splits):
    q_lora, kv_lora, rope, conv_ch, gv, nh, _, d, _ = splits
    offs = [0]
    for s in splits:
        offs.append(offs[-1] + s)
    part = [w_in[:, offs[i]:offs[i + 1]] for i in range(len(splits))]
    q_lat, kv_lat, k_r, qkv, z, a, b, g_a, g_b = part
    dm = w_in.shape[0]
    half = rope // 2
    zpad = lambda n: jnp.zeros((dm, n), w_in.dtype)
    k_sw = jnp.concatenate([k_r[:, half:], k_r[:, :half]], axis=1)
    small = jnp.concatenate([q_lat, kv_lat, k_r, zpad(LANE - rope), k_sw, zpad(LANE - rope),
                             a, b, zpad(LANE - 2 * nh)], axis=1)
    return jnp.concatenate([qkv, small, z, g_a, g_b], axis=1).astype(BF16)


def _pack_w_uq(w_uq, nope, rope):
    w = jnp.transpose(w_uq, (1, 0, 2))
    half = rope // 2
    r = w[..., nope:]
    zp = jnp.zeros(r.shape[:-1] + (LANE - rope,), w.dtype)
    r_sw = jnp.concatenate([r[..., half:], r[..., :half]], axis=-1)
    return jnp.concatenate([w[..., :nope], r, zp, r_sw, zp], axis=-1).astype(BF16)


def _rope_tables(pos, rope):
    half = rope // 2
    inv = ROPE_THETA ** (-jnp.arange(half, dtype=F32) / half)
    ang = pos.astype(F32)[:, None] * inv[None, :]
    cos, sin = jnp.cos(ang), jnp.sin(ang)
    zp = jnp.zeros((pos.shape[0], LANE - rope), F32)
    return (jnp.concatenate([cos, cos, zp], axis=1), jnp.concatenate([-sin, sin, zp], axis=1))


def _route_meta(idx_t, rank_t, counts, m, n_exp, blk):
    a = m * TOP_K
    counts = counts.astype(jnp.int32)
    padded = (counts + blk - 1) // blk * blk
    pad_end = jnp.cumsum(padded)
    pad_start = pad_end - padded
    experts = jnp.arange(n_exp, dtype=jnp.int32)
    e_km = idx_t[:TOP_K]
    start_km = jnp.sum(jnp.where(e_km[:, :, None] == experts, pad_start, 0), axis=-1)
    dest_km = (start_km + rank_t[:TOP_K]).astype(jnp.int32).reshape(a)
    nb = a // blk + n_exp
    tok_km = jnp.tile(jnp.arange(m, dtype=jnp.int32), TOP_K)
    rows = jnp.zeros((nb * blk,), jnp.int32).at[dest_km].set(tok_km)
    first_row = jnp.arange(nb, dtype=jnp.int32) * blk
    block_e = jnp.minimum(jnp.sum((pad_end[None, :] <= first_row[:, None]).astype(jnp.int32), axis=1),
                          n_exp - 1).astype(jnp.int32)
    n_used = (pad_end[-1] // blk).astype(jnp.int32).reshape(1)
    return block_e, rows, n_used, dest_km


def kernel(x_prompt, x_sample, cache_ckv, cache_krope, page_table, state_conv, state_ssm, w_in, q_norm_w, kv_norm_w, w_uq, w_uk, w_uv, conv_w, a_log, dt_bias, gdn_norm_w, w_o, ln1_g, ln1_b, w_router, b_router, w_gu, b_gu, w_down, b_down, ln2_g, ln2_b):
    bp, tp, d = x_prompt.shape
    bs, ts, _ = x_sample.shape
    depth = w_in.shape[0]
    q_lora, n_heads, qk = w_uq.shape[1:]
    kv_lora, _, nope = w_uk.shape[1:]
    rope = qk - nope
    vh = w_uv.shape[3]
    cw_taps, conv_ch = conv_w.shape[1:]
    g_heads = a_log.shape[1]
    dk, dv = state_ssm.shape[3:]
    n_exp = w_router.shape[2]
    page = cache_ckv.shape[2]
    past = page_table.shape[1] * page
    splits = (q_lora, kv_lora, rope, conv_ch, g_heads * dv, g_heads, g_heads, d, d)
    assert sum(splits) == w_in.shape[2]
    alpha = (2 * depth) ** 0.25
    scale = (nope + rope) ** -0.5 * LOG2E
    mp, ms = bp * tp, bs * ts
    m = mp + ms
    small_w = q_lora + kv_lora + 3 * LANE
    assert small_w == d and conv_ch % d == 0
    qkv_blk, small_blk = 0, conv_ch // d
    z_blk, ga_blk, gb_blk = small_blk + 1, small_blk + 2, small_blk + 3
    ab_blk = (conv_ch + q_lora + kv_lora + 2 * LANE) // LANE

    cos_p, sin_p = _rope_tables(jnp.arange(tp, dtype=jnp.int32), rope)
    cos_s, sin_s = _rope_tables(past + jnp.arange(ts, dtype=jnp.int32), rope)
    cos_t = jnp.concatenate([jnp.tile(cos_p, (bp, 1)), jnp.tile(cos_s, (bs, 1))], axis=0)
    sin_t = jnp.concatenate([jnp.tile(sin_p, (bp, 1)), jnp.tile(sin_s, (bs, 1))], axis=0)

    x_p, x_s = x_prompt.reshape(mp, d), x_sample.reshape(ms, d)
    outs = {k: [] for k in ("ckv_p", "kr_p", "conv_p", "ssm_p", "ckv_s", "kr_s", "conv_s", "ssm_s")}
    pad_lanes = lambda v: jnp.pad(v, (0, LANE - v.shape[0])).reshape(1, LANE)
    for l in range(depth):
        x = jnp.concatenate([x_p, x_s], axis=0)
        w_pack = _pack_w_in(w_in[l], splits)
        wq = _pack_w_uq(w_uq[l], nope, rope)
        wuk = jnp.transpose(w_uk[l], (1, 2, 0)).astype(BF16)
        wuv = jnp.transpose(w_uv[l], (1, 0, 2)).astype(BF16)
        h = _in_proj(x, w_pack, tm=_row_tile(m, 2304), tn=512)
        qf, c_all, kr_all, kt, cb = _mla_proj(
            h, small_blk, cos_t, sin_t, q_norm_w[l].reshape(1, q_lora), kv_norm_w[l].reshape(1, kv_lora),
            wq, wuk, tm=256, q_lora=q_lora, kv_lora=kv_lora, nope=nope, rope=rope, scale=scale)
        om_p = _attn_prompt(qf, kt, cb, wuv, batch=bp, seq=tp, tq=128, tk=min(1024, tp))
        qs = qf[:, mp:, :].reshape(n_heads, bs, ts, qf.shape[-1])
        qs = jnp.transpose(qs, (1, 0, 2, 3)).reshape(bs, n_heads * ts, qf.shape[-1])
        om_s = _attn_sample(page_table, qs, c_all, kr_all, mp // ts, cache_ckv[l],
                            jnp.swapaxes(cache_krope[l], 1, 2), wuv)

        alog_p = pad_lanes(a_log[l])
        dtb_p = pad_lanes(dt_bias[l])
        nw = gdn_norm_w[l].reshape(1, dv)
        gdn_kw = dict(qkv_blk=qkv_blk, ab_blk=ab_blk, z_blk=z_blk, conv_w=cw_taps)
        zeros_conv = jnp.zeros((bp, SUBLANE, conv_ch), F32)
        zeros_ssm = jnp.zeros((bp, g_heads, dk, dv), F32)
        og_p, ssm_p, ncv_p = _gdn(h, 0, zeros_conv, zeros_ssm, conv_w[l], alog_p, dtb_p, nw,
                                  batch=bp, seq=tp, chunk=min(GDN_CHUNK, tp), group=bp, **gdn_kw)
        conv8_s = jnp.pad(state_conv[l], ((0, 0), (SUBLANE - (cw_taps - 1), 0), (0, 0)))
        og_s, ssm_s, ncv_s = _gdn(h, mp, conv8_s, state_ssm[l], conv_w[l], alog_p, dtb_p, nw,
                                  batch=bs, seq=ts, chunk=ts, group=math.gcd(bs, GDN_SAMPLE_GROUP), **gdn_kw)

        x1, idx_t, gate_t, rank_t, counts = _post(
            x, om_p, om_s, og_p, og_s, h, ga_blk, gb_blk, w_o[l].astype(BF16), ln1_g[l].reshape(1, d), ln1_b[l].reshape(1, d),
            w_router[l].T, b_router[l].reshape(n_exp, 1), tm=256, alpha=alpha)
        block_e, rows, n_used, dest_km = _route_meta(idx_t, rank_t, counts[:, 0], m, n_exp, MOE_ROWS)
        ys = _moe_experts(block_e, rows, n_used, x1, w_gu[l], b_gu[l], w_down[l], b_down[l], blk=MOE_ROWS)
        x_p, x_s = _combine(dest_km, x1, gate_t.T, ys, ln2_g[l].reshape(1, d), ln2_b[l].reshape(1, d),
                            tm=128, alpha=alpha, m_first=mp)

        outs["ckv_p"].append(c_all[:mp].reshape(bp, tp, kv_lora))
        outs["kr_p"].append(kr_all[:mp].reshape(bp, tp, rope))
        outs["conv_p"].append(ncv_p[:, SUBLANE - (cw_taps - 1):, :])
        outs["ssm_p"].append(ssm_p)
        outs["ckv_s"].append(c_all[mp:].reshape(bs, ts, kv_lora))
        outs["kr_s"].append(kr_all[mp:].reshape(bs, ts, rope))
        outs["conv_s"].append(ncv_s[:, SUBLANE - (cw_taps - 1):, :])
        outs["ssm_s"].append(ssm_s)

    return (x_p.reshape(bp, tp, d), x_s.reshape(bs, ts, d),
            jnp.stack(outs["ckv_p"]), jnp.stack(outs["kr_p"]), jnp.stack(outs["conv_p"]), jnp.stack(outs["ssm_p"]),
            jnp.stack(outs["ckv_s"]), jnp.stack(outs["kr_s"]), jnp.stack(outs["conv_s"]), jnp.stack(outs["ssm_s"]))
```

```python
import functools
import math

import jax
import jax.numpy as jnp
from jax import lax
from jax.experimental import pallas as pl
from jax.experimental.pallas import tpu as pltpu

F32 = jnp.float32
BF16 = jnp.bfloat16

ROPE_THETA = 10000.0
NORM_EPS = 1e-6
TOP_K = 4
SWIGLU_LIMIT = 7.0
SWIGLU_ALPHA = 1.702
GDN_CHUNK = 64
GDN_DIAG = 16
GDN_SAMPLE_GROUP = 8
MOE_ROWS = 512
PAGE_CHUNK = 8
LANE = 128
SUBLANE = 8
VMEM_LIMIT = 56 * 1024 * 1024
NEG = -1e30
LOG2E = 1.4426950408889634


def _cparams(sem, vmem=None):
    return pltpu.CompilerParams(dimension_semantics=sem, vmem_limit_bytes=vmem)


def _dot(a, b, **kw):
    return jnp.dot(a, b, preferred_element_type=F32, **kw)


def _dot_nt(a, b, **kw):
    return lax.dot_general(a, b, (((1,), (1,)), ((), ())), preferred_element_type=F32, **kw)


def _bdot(a, b):
    return lax.dot_general(a, b, (((2,), (1,)), ((0,), (0,))), preferred_element_type=F32)


def _bdot_nt(a, b):
    return lax.dot_general(a, b, (((2,), (2,)), ((0,), (0,))), preferred_element_type=F32)


def _bdot_tn(a, b):
    return lax.dot_general(a, b, (((1,), (1,)), ((0,), (0,))), preferred_element_type=F32)


def _sigmoid(x):
    return 1.0 / (1.0 + jnp.exp(-x))


def _layer_norm(v, g, b):
    mu = jnp.mean(v, -1, keepdims=True)
    vc = v - mu
    var = jnp.mean(vc * vc, -1, keepdims=True)
    return vc * lax.rsqrt(var + NORM_EPS) * g + b


def _inproj_kernel(x_ref, w_ref, o_ref, xb_ref):
    @pl.when(pl.program_id(1) == 0)
    def _():
        xb_ref[...] = x_ref[...].astype(BF16)

    o_ref[...] = _dot(xb_ref[...], w_ref[...])


def _row_tile(m, target):
    return max(t for t in range(SUBLANE, target + 1, SUBLANE) if m % t == 0)


def _in_proj(x, w, tm, tn):
    m, k = x.shape
    n = w.shape[1]
    return pl.pallas_call(
        _inproj_kernel,
        grid=(m // tm, n // tn),
        in_specs=[pl.BlockSpec((tm, k), lambda i, j: (i, 0)),
                  pl.BlockSpec((k, tn), lambda i, j: (0, j))],
        out_specs=pl.BlockSpec((tm, tn), lambda i, j: (i, j)),
        out_shape=jax.ShapeDtypeStruct((m, n), F32),
        scratch_shapes=[pltpu.VMEM((tm, k), BF16)],
        compiler_params=_cparams(("parallel", "arbitrary"), VMEM_LIMIT),
        name="in_proj",
    )(x, w)


def _mla_proj_kernel(h_ref, cos_ref, sin_ref, qnw_ref, kvnw_ref, wq_ref, wuk_ref,
                     qf_ref, c_ref, kr_ref, kt_ref, cb_ref, *, n_heads, q_lora, kv_lora, nope, rope, scale):
    hs = h_ref[...]
    cos = cos_ref[...]
    sin = sin_ref[...]
    q_lat = hs[:, :q_lora]
    qn = q_lat * lax.rsqrt(jnp.mean(q_lat * q_lat, -1, keepdims=True) + NORM_EPS) * qnw_ref[...]
    qn = qn.astype(BF16)
    kv = hs[:, q_lora:q_lora + kv_lora]
    c = kv * lax.rsqrt(jnp.mean(kv * kv, -1, keepdims=True) + NORM_EPS) * kvnw_ref[...]
    o = q_lora + kv_lora
    kr = hs[:, o:o + LANE] * cos + hs[:, o + LANE:o + 2 * LANE] * sin
    c_ref[...] = c
    kr_ref[...] = kr[:, :rope]
    cb = c.astype(BF16)
    cb_ref[...] = cb
    kfull = jnp.concatenate([cb, kr.astype(BF16)], axis=1)
    kw = kfull.shape[1]
    eye = (lax.broadcasted_iota(jnp.int32, (kw, kw), 0) == lax.broadcasted_iota(jnp.int32, (kw, kw), 1))
    kt_ref[...] = _dot_nt(eye.astype(BF16), kfull).astype(BF16)
    for h in range(n_heads):
        qh = _dot(qn, wq_ref[h])
        qa = _dot(qh[:, :nope].astype(BF16), wuk_ref[h])
        qr = qh[:, nope:nope + LANE] * cos + qh[:, nope + LANE:nope + 2 * LANE] * sin
        qf_ref[h, :, :kv_lora] = (qa * scale).astype(BF16)
        qf_ref[h, :, kv_lora:] = (qr * scale).astype(BF16)


def _mla_proj(h, col_blk, cos_t, sin_t, qnw, kvnw, wq, wuk, *, tm, q_lora, kv_lora, nope, rope, scale):
    m = h.shape[0]
    n_heads = wq.shape[0]
    wcol = q_lora + kv_lora + 3 * LANE
    kw = kv_lora + LANE
    kern = functools.partial(_mla_proj_kernel, n_heads=n_heads, q_lora=q_lora, kv_lora=kv_lora,
                             nope=nope, rope=rope, scale=scale)
    return pl.pallas_call(
        kern,
        grid=(m // tm,),
        in_specs=[pl.BlockSpec((tm, wcol), lambda i: (i, col_blk)),
                  pl.BlockSpec((tm, LANE), lambda i: (i, 0)),
                  pl.BlockSpec((tm, LANE), lambda i: (i, 0)),
                  pl.BlockSpec((1, q_lora), lambda i: (0, 0)),
                  pl.BlockSpec((1, kv_lora), lambda i: (0, 0)),
                  pl.BlockSpec(wq.shape, lambda i: (0, 0, 0)),
                  pl.BlockSpec(wuk.shape, lambda i: (0, 0, 0))],
        out_specs=[pl.BlockSpec((n_heads, tm, kw), lambda i: (0, i, 0)),
                   pl.BlockSpec((tm, kv_lora), lambda i: (i, 0)),
                   pl.BlockSpec((tm, rope), lambda i: (i, 0)),
                   pl.BlockSpec((kw, tm), lambda i: (0, i)),
                   pl.BlockSpec((tm, kv_lora), lambda i: (i, 0))],
        out_shape=[jax.ShapeDtypeStruct((n_heads, m, kw), BF16),
                   jax.ShapeDtypeStruct((m, kv_lora), F32),
                   jax.ShapeDtypeStruct((m, rope), F32),
                   jax.ShapeDtypeStruct((kw, m), BF16),
                   jax.ShapeDtypeStruct((m, kv_lora), BF16)],
        compiler_params=_cparams(("parallel",), VMEM_LIMIT),
        name="mla_proj",
    )(h, cos_t, sin_t, qnw, kvnw, wq, wuk)


def _softmax_step(s, v, m_ref, l_ref, acc_ref):
    m_prev = m_ref[...]
    m_new = jnp.maximum(m_prev, jnp.max(s, -1, keepdims=True))
    alpha = jnp.exp2(m_prev - m_new)
    p = jnp.exp2(s - m_new)
    l_ref[...] = alpha * l_ref[...] + jnp.sum(p, -1, keepdims=True)
    acc_ref[...] = alpha * acc_ref[...] + _dot(p.astype(BF16), v)
    m_ref[...] = m_new


def _attn_prompt_kernel(qf_ref, kt_ref, v_ref, wuv_ref, o_ref, m_ref, l_ref, acc_ref, *, n_heads, tq, tk, dv, vh, n_split):
    i = pl.program_id(1)
    m_ref[...] = jnp.full(m_ref.shape, NEG, F32)
    l_ref[...] = jnp.zeros(l_ref.shape, F32)
    acc_ref[...] = jnp.zeros(acc_ref.shape, F32)
    n_full = (i * tq) // tk
    off = i * tq - n_full * tk
    hs = n_heads // n_split
    rs = hs * tq

    def step(j, masked):
        kt = kt_ref[:, pl.ds(pl.multiple_of(j * tk, tk), tk)]
        v = v_ref[pl.ds(pl.multiple_of(j * tk, tk), tk), :]
        for g in range(n_split):
            q = qf_ref[g * hs:(g + 1) * hs].reshape(rs, qf_ref.shape[-1])
            s = _dot(q, kt)
            if masked:
                row = lax.broadcasted_iota(jnp.int32, (tq, tk), 0)
                col = lax.broadcasted_iota(jnp.int32, (tq, tk), 1)
                s = jnp.where((col <= row + off)[None], s.reshape(hs, tq, tk), NEG).reshape(rs, tk)
            sl = slice(g * rs, (g + 1) * rs)
            _softmax_step(s, v, m_ref.at[sl], l_ref.at[sl], acc_ref.at[sl])

    def body(j, carry):
        step(j, False)
        return carry

    lax.fori_loop(0, n_full, body, 0)
    step(n_full, True)
    o = acc_ref[...] / l_ref[...]
    for h in range(n_heads):
        oh = o[h * tq:(h + 1) * tq].astype(BF16)
        o_ref[:, h * vh:(h + 1) * vh] = _dot(oh, wuv_ref[h])


def _attn_prompt(qf, kt, cb, wuv, *, batch, seq, tq, tk):
    n_heads, _, kw = qf.shape
    dv, vh = wuv.shape[1], wuv.shape[2]
    nq = seq // tq
    rows = n_heads * tq
    assert seq % tk == 0 and tk % tq == 0
    kern = functools.partial(_attn_prompt_kernel, n_heads=n_heads, tq=tq, tk=tk, dv=dv, vh=vh, n_split=2)
    return pl.pallas_call(
        kern,
        grid=(batch, nq),
        in_specs=[pl.BlockSpec((n_heads, tq, kw), lambda b, i: (0, b * nq + i, 0)),
                  pl.BlockSpec((kw, seq), lambda b, i: (0, b)),
                  pl.BlockSpec((seq, dv), lambda b, i: (b, 0)),
                  pl.BlockSpec(wuv.shape, lambda b, i: (0, 0, 0))],
        out_specs=pl.BlockSpec((tq, n_heads * vh), lambda b, i: (b * nq + i, 0)),
        out_shape=jax.ShapeDtypeStruct((batch * seq, n_heads * vh), F32),
        scratch_shapes=[pltpu.VMEM((rows, 1), F32), pltpu.VMEM((rows, 1), F32),
                        pltpu.VMEM((rows, dv), F32)],
        compiler_params=_cparams(("parallel", "arbitrary"), VMEM_LIMIT),
        name="attn_prompt",
    )(qf, kt, cb, wuv)


def _attn_sample_kernel(pt_ref, q_ref, cn_ref, krn_ref, cc_hbm, cr_hbm, wuv_ref, o_ref,
                        cbuf, rbuf, sem, *, n_pages, page, n_heads, ts, dv, dr, vh):
    b = pl.program_id(0)
    nb = pl.num_programs(0)
    slot = b % 2
    rows = n_heads * ts

    def fetch(bb, s):
        for p in range(n_pages):
            pg = pt_ref[bb, p]
            pltpu.make_async_copy(cc_hbm.at[pg], cbuf.at[s, p], sem.at[0, s]).start()
            pltpu.make_async_copy(cr_hbm.at[pg], rbuf.at[s, p], sem.at[1, s]).start()

    def wait(s):
        pltpu.make_async_copy(cc_hbm.at[pl.ds(0, n_pages)], cbuf.at[s], sem.at[0, s]).wait()
        pltpu.make_async_copy(cr_hbm.at[pl.ds(0, n_pages)], rbuf.at[s], sem.at[1, s]).wait()

    @pl.when(b == 0)
    def _():
        fetch(0, 0)

    wait(slot)
    fetch(jnp.minimum(b + 1, nb - 1), 1 - slot)

    q = q_ref[0].astype(F32)
    qc = q[:, :dv]
    qr = q[:, dv:dv + dr]
    n_chunks = n_pages // PAGE_CHUNK
    ck = PAGE_CHUNK * page
    parts = []
    for ch in range(n_chunks):
        c_ch = cbuf[slot, ch * PAGE_CHUNK:(ch + 1) * PAGE_CHUNK].reshape(ck, dv)
        r_ch = jnp.concatenate([rbuf[slot, ch * PAGE_CHUNK + u] for u in range(PAGE_CHUNK)], axis=1)
        parts.append(_dot_nt(qc, c_ch) + _dot(qr, r_ch))
    cn = jnp.concatenate([cn_ref[...], jnp.zeros((LANE - ts, dv), F32)], axis=0)
    krn = jnp.concatenate([krn_ref[...], jnp.zeros((LANE - ts, dr), F32)], axis=0)
    row = lax.broadcasted_iota(jnp.int32, (ts, LANE), 0)
    col = lax.broadcasted_iota(jnp.int32, (ts, LANE), 1)
    s_new = (_dot_nt(qc, cn) + _dot_nt(qr, krn)).reshape(n_heads, ts, LANE)
    s_new = jnp.where((col <= row)[None], s_new, NEG).reshape(rows, LANE)

    m = jnp.max(s_new, -1, keepdims=True)
    for s in parts:
        m = jnp.maximum(m, jnp.max(s, -1, keepdims=True))
    p_new = jnp.exp2(s_new - m)
    l = jnp.sum(p_new, -1, keepdims=True)
    acc = _dot(p_new, cn)
    for ch in range(n_chunks):
        p = jnp.exp2(parts[ch] - m)
        l = l + jnp.sum(p, -1, keepdims=True)
        acc = acc + _dot(p, cbuf[slot, ch * PAGE_CHUNK:(ch + 1) * PAGE_CHUNK].reshape(ck, dv))
    o = acc / l
    for h in range(n_heads):
        oh = o[h * ts:(h + 1) * ts].astype(BF16)
        o_ref[:, h * vh:(h + 1) * vh] = _dot(oh, wuv_ref[h])

    @pl.when(b == nb - 1)
    def _():
        wait(1 - slot)


def _attn_sample(page_table, qs, c_all, kr_all, row_blk_off, cache_c, cache_rt, wuv):
    bs, rows, kw = qs.shape
    n_heads, dv, vh = wuv.shape
    ts = rows // n_heads
    n_pages = page_table.shape[1]
    page = cache_c.shape[1]
    dr = cache_rt.shape[1]
    assert n_pages % PAGE_CHUNK == 0
    kern = functools.partial(_attn_sample_kernel, n_pages=n_pages, page=page, n_heads=n_heads, ts=ts,
                             dv=dv, dr=dr, vh=vh)
    grid_spec = pltpu.PrefetchScalarGridSpec(
        num_scalar_prefetch=1,
        grid=(bs,),
        in_specs=[pl.BlockSpec((1, rows, kw), lambda b, pt: (b, 0, 0)),
                  pl.BlockSpec((ts, dv), lambda b, pt: (row_blk_off + b, 0)),
                  pl.BlockSpec((ts, dr), lambda b, pt: (row_blk_off + b, 0)),
                  pl.BlockSpec(memory_space=pl.ANY),
                  pl.BlockSpec(memory_space=pl.ANY),
                  pl.BlockSpec(wuv.shape, lambda b, pt: (0, 0, 0))],
        out_specs=pl.BlockSpec((ts, n_heads * vh), lambda b, pt: (b, 0)),
        scratch_shapes=[pltpu.VMEM((2, n_pages, page, dv), F32), pltpu.VMEM((2, n_pages, dr, page), F32),
                        pltpu.SemaphoreType.DMA((2, 2))])
    return pl.pallas_call(
        kern,
        grid_spec=grid_spec,
        out_shape=jax.ShapeDtypeStruct((bs * ts, n_heads * vh), F32),
        compiler_params=_cparams(("arbitrary",), VMEM_LIMIT),
        name="attn_sample",
    )(page_table, qs, c_all, kr_all, cache_c, cache_rt, wuv)


def _neumann_inv(low, eye, steps):
    p = eye - low
    x = low
    for _ in range(steps):
        x = _bdot(x, x)
        p = p + _bdot(p, x)
    return p


def _gdn_kernel(*refs, n_in, n_seq, chunk, n_heads, dk, dv, conv_w):
    qkv_refs, ab_refs, z_refs = refs[:n_in], refs[n_in:2 * n_in], refs[2 * n_in:3 * n_in]
    (cs_ref, s0_ref, cw_ref, alog_ref, dtb_ref, nw_ref,
     o_ref, sfin_ref, ncv_ref, xp_ref, s_ref) = refs[3 * n_in:]
    n = pl.program_id(1)
    c = chunk
    per = n_seq // n_in
    n_prob = n_seq * n_heads
    hi = lax.Precision.HIGHEST

    def seq_rows(group, g):
        k = g % per
        return group[g // per][k * c:(k + 1) * c, :]

    @pl.when(n == 0)
    def _():
        s_ref[...] = s0_ref[...].reshape(n_prob, dk, dv)
        xp_ref[:, 0:SUBLANE, :] = cs_ref[...]

    r_i = lax.broadcasted_iota(jnp.int32, (c, c), 0)
    c_i = lax.broadcasted_iota(jnp.int32, (c, c), 1)
    causal = (c_i <= r_i)[None]
    strict = (c_i < r_i)[None]
    eye_c = (r_i == c_i).astype(F32)[None]
    db = min(GDN_DIAG, c)
    n_blk = c // db
    sh = int(math.log2(db))
    same_blk = (jnp.right_shift(r_i, sh) == jnp.right_shift(c_i, sh))[None]
    r_l = lax.broadcasted_iota(jnp.int32, (LANE, LANE), 0)
    c_l = lax.broadcasted_iota(jnp.int32, (LANE, LANE), 1)
    eye_l = (r_l == c_l).astype(F32)
    tril = (c_i <= r_i).astype(F32)
    cw = cw_ref[...]
    hk = n_heads * dk

    q_l, k_l, v_l, z_l, beta_l, gcol_l, grow_l, tails = [], [], [], [], [], [], [], []
    for g in range(n_seq):
        xp_ref[g, SUBLANE:SUBLANE + c, :] = seq_rows(qkv_refs, g)
        conv = xp_ref[g, pl.ds(SUBLANE - (conv_w - 1), c), :] * cw[0:1, :]
        for j in range(1, conv_w):
            conv = conv + xp_ref[g, pl.ds(SUBLANE - (conv_w - 1) + j, c), :] * cw[j:j + 1, :]
        tail = xp_ref[g, c:c + SUBLANE, :]
        xp_ref[g, 0:SUBLANE, :] = tail
        tails.append(tail)
        act = conv * _sigmoid(conv)
        ab = seq_rows(ab_refs, g)
        apb = ab + dtb_ref[...]
        softplus = jnp.maximum(apb, 0.0) + jnp.log(1.0 + jnp.exp(-jnp.abs(apb)))
        gfull = -jnp.exp(alog_ref[...]) * softplus
        betaf = _sigmoid(ab)
        gc = _dot(tril, gfull, precision=hi)
        gc_t = _dot_nt(eye_l, gc, precision=hi)
        zg = seq_rows(z_refs, g)
        for h in range(n_heads):
            q_l.append(act[:, h * dk:(h + 1) * dk])
            k_l.append(act[:, hk + h * dk:hk + (h + 1) * dk])
            v_l.append(act[:, 2 * hk + h * dv:2 * hk + (h + 1) * dv])
            z_l.append(zg[:, h * dv:(h + 1) * dv])
            beta_l.append(betaf[:, n_heads + h:n_heads + h + 1])
            gcol_l.append(gc[:, h:h + 1])
            grow_l.append(gc_t[h:h + 1, :])

    q = jnp.stack(q_l)
    k = jnp.stack(k_l)
    v = jnp.stack(v_l)
    z = jnp.stack(z_l)
    beta = jnp.stack(beta_l)
    gcol = jnp.stack(gcol_l)
    grow = jnp.stack(grow_l)
    q = q * lax.rsqrt(jnp.sum(q * q, -1, keepdims=True) + NORM_EPS) * (dk ** -0.5)
    k = k * lax.rsqrt(jnp.sum(k * k, -1, keepdims=True) + NORM_EPS)
    decay = jnp.where(causal, jnp.exp(jnp.where(causal, gcol - grow, 0.0)), 0.0)
    kb = k * beta
    lower = jnp.where(strict, _bdot_nt(kb, k) * decay, 0.0)
    attn = _bdot_nt(q, k) * decay
    l_d = jnp.where(same_blk, lower, 0.0)
    t_inv = _neumann_inv(l_d, eye_c, int(math.log2(db)) - 1)
    if n_blk > 1:
        m_inv = _neumann_inv(_bdot(t_inv, lower - l_d), eye_c, int(math.log2(n_blk)) - 1)
        t_inv = _bdot(m_inv, t_inv)
    egc = jnp.exp(gcol)
    uw = _bdot(t_inv, jnp.concatenate([v * beta, kb * egc], axis=2))
    s = s_ref[...]
    v_new = uw[:, :, :dv] - _bdot(uw[:, :, dv:], s)
    o = _bdot(q * egc, s) + _bdot(attn, v_new)
    glast = gcol[:, c - 1:c, :]
    kdec = k * jnp.exp(glast - gcol)
    s_new = s * jnp.exp(glast) + _bdot_tn(kdec, v_new)
    s_ref[...] = s_new
    on = o * lax.rsqrt(jnp.mean(o * o, -1, keepdims=True) + NORM_EPS) * nw_ref[...] * (z * _sigmoid(z))
    for g in range(n_seq):
        for h in range(n_heads):
            o_ref[g, :, h * dv:(h + 1) * dv] = on[g * n_heads + h]

    @pl.when(n == pl.num_programs(1) - 1)
    def _():
        sfin_ref[...] = s_new.reshape(n_seq, n_heads, dk, dv)
        for g in range(n_seq):
            ncv_ref[g] = tails[g]


def _gdn(h, row_off, conv_state8, ssm_state, cw, alog_p, dtb_p, nw, *, batch, seq, chunk, group,
         qkv_blk, ab_blk, z_blk, conv_w):
    _, n_heads, dk, dv = ssm_state.shape
    nc = seq // chunk
    ch = cw.shape[1]
    hd = n_heads * dv
    contiguous = nc == 1
    n_in = 1 if contiguous else group
    assert batch % group == 0 and row_off % (group * chunk) == 0
    kern = functools.partial(_gdn_kernel, n_in=n_in, n_seq=group, chunk=chunk, n_heads=n_heads,
                             dk=dk, dv=dv, conv_w=conv_w)
    if contiguous:
        rb = group * chunk
        row_maps = [lambda i, n: row_off // rb + i]
    else:
        rb = chunk
        row_maps = [(lambda i, n, g=g: row_off // rb + (i * group + g) * nc + n) for g in range(group)]

    def specs(width, col_blk):
        return [pl.BlockSpec((rb, width), lambda i, n, r=r: (r(i, n), col_blk)) for r in row_maps]

    in_specs = specs(ch, qkv_blk) + specs(LANE, ab_blk) + specs(hd, z_blk)
    in_specs += [pl.BlockSpec((group, SUBLANE, ch), lambda i, n: (i, 0, 0)),
                 pl.BlockSpec((group, n_heads, dk, dv), lambda i, n: (i, 0, 0, 0)),
                 pl.BlockSpec(cw.shape, lambda i, n: (0, 0)),
                 pl.BlockSpec((1, LANE), lambda i, n: (0, 0)),
                 pl.BlockSpec((1, LANE), lambda i, n: (0, 0)),
                 pl.BlockSpec((1, dv), lambda i, n: (0, 0))]
    o3, sfin, ncv = pl.pallas_call(
        kern,
        grid=(batch // group, nc),
        in_specs=in_specs,
        out_specs=[pl.BlockSpec((group, chunk, hd), lambda i, n: (i, n, 0)),
                   pl.BlockSpec((group, n_heads, dk, dv), lambda i, n: (i, 0, 0, 0)),
                   pl.BlockSpec((group, SUBLANE, ch), lambda i, n: (i, 0, 0))],
        out_shape=[jax.ShapeDtypeStruct((batch, seq, hd), F32),
                   jax.ShapeDtypeStruct(ssm_state.shape, F32),
                   jax.ShapeDtypeStruct((batch, SUBLANE, ch), F32)],
        scratch_shapes=[pltpu.VMEM((group, chunk + SUBLANE, ch), F32),
                        pltpu.VMEM((group * n_heads, dk, dv), F32)],
        compiler_params=_cparams(("parallel", "arbitrary"), VMEM_LIMIT),
        name="gdn",
    )(*([h] * (3 * n_in)), conv_state8, ssm_state, cw, alog_p, dtb_p, nw)
    return o3.reshape(batch * seq, hd), sfin, ncv


def _store_row_tiles(ref, val):
    rows = val.shape[0]
    for j in range(SUBLANE):
        ref[pl.ds(j, rows, stride=SUBLANE), :] = val[:, j * LANE:(j + 1) * LANE]


def _load_row_tiles(ref, start, rows, j):
    return ref[pl.ds(start * SUBLANE + j, rows, stride=SUBLANE), :]


def _post_kernel(x_ref, omp_ref, oms_ref, ogp_ref, ogs_ref, ga_ref, gb_ref, wo_ref, g1_ref, b1_ref, wr_ref, br_ref,
                 x1_ref, x1t_ref, idx_ref, gate_ref, rank_ref, cnt_ref, carry_ref, *, alpha, top_k, n_first):
    @pl.when(pl.program_id(0) == 0)
    def _():
        carry_ref[...] = jnp.zeros(carry_ref.shape, F32)

    first = pl.program_id(0) < n_first
    om = jnp.where(first, omp_ref[...], oms_ref[...])
    og = jnp.where(first, ogp_ref[...], ogs_ref[...])
    mix = _sigmoid(ga_ref[...]) * om + _sigmoid(gb_ref[...]) * og
    y = _dot(mix.astype(BF16), wo_ref[...])
    x1 = _layer_norm(alpha * x_ref[...] + y, g1_ref[...], b1_ref[...])
    x1_ref[...] = x1
    _store_row_tiles(x1t_ref, x1)
    logits = _dot_nt(wr_ref[...], x1, precision=lax.Precision.HIGHEST) + br_ref[...]
    n_exp, tm = logits.shape
    e_i = lax.broadcasted_iota(jnp.int32, (n_exp, tm), 0)
    vals, idxs = [], []
    for _ in range(top_k):
        mx = jnp.max(logits, axis=0, keepdims=True)
        ix = jnp.min(jnp.where(logits == mx, e_i, n_exp), axis=0, keepdims=True)
        vals.append(mx)
        idxs.append(ix)
        logits = jnp.where(e_i == ix, -jnp.inf, logits)
    es = [jnp.exp(v - vals[0]) for v in vals]
    tot = es[0]
    for e in es[1:]:
        tot = tot + e
    onehots = [e_i == ix for ix in idxs]
    sel = onehots[0].astype(F32)
    for oh in onehots[1:]:
        sel = sel + oh.astype(F32)
    t_r = lax.broadcasted_iota(jnp.int32, (tm, tm), 0)
    t_c = lax.broadcasted_iota(jnp.int32, (tm, tm), 1)
    before = _dot(sel.astype(BF16), (t_r < t_c).astype(BF16)) + carry_ref[...]
    carry = carry_ref[...] + jnp.sum(sel, axis=1, keepdims=True)
    carry_ref[...] = carry
    cnt_ref[...] = carry
    r_i = lax.broadcasted_iota(jnp.int32, (SUBLANE, tm), 0)
    idx_o = jnp.zeros((SUBLANE, tm), jnp.int32)
    gate_o = jnp.zeros((SUBLANE, tm), F32)
    rank_o = jnp.zeros((SUBLANE, tm), F32)
    for k in range(top_k):
        idx_o = jnp.where(r_i == k, idxs[k], idx_o)
        gate_o = jnp.where(r_i == k, es[k] / tot, gate_o)
        rank_k = jnp.sum(jnp.where(onehots[k], before, 0.0), axis=0, keepdims=True)
        rank_o = jnp.where(r_i == k, rank_k, rank_o)
    idx_ref[...] = idx_o
    gate_ref[...] = gate_o
    rank_ref[...] = rank_o.astype(jnp.int32)


def _post(x, om_p, om_s, og_p, og_s, h, ga_blk, gb_blk, wo, g1, b1, wr_t, br, *, tm, alpha):
    m, d = x.shape
    n_exp = wr_t.shape[0]
    n_first = om_p.shape[0] // tm
    assert om_p.shape[0] % tm == 0 and om_s.shape[0] % tm == 0 and d == SUBLANE * LANE
    kern = functools.partial(_post_kernel, alpha=alpha, top_k=TOP_K, n_first=n_first)
    row = lambda i: (i, 0)
    const = lambda i: (0, 0)
    first = lambda i: (jnp.minimum(i, n_first - 1), 0)
    second = lambda i: (jnp.maximum(i - n_first, 0), 0)
    return pl.pallas_call(
        kern,
        grid=(m // tm,),
        in_specs=[pl.BlockSpec((tm, d), row),
                  pl.BlockSpec((tm, d), first), pl.BlockSpec((tm, d), second),
                  pl.BlockSpec((tm, d), first), pl.BlockSpec((tm, d), second),
                  pl.BlockSpec((tm, d), lambda i: (i, ga_blk)),
                  pl.BlockSpec((tm, d), lambda i: (i, gb_blk)),
                  pl.BlockSpec(wo.shape, const), pl.BlockSpec((1, d), const), pl.BlockSpec((1, d), const),
                  pl.BlockSpec(wr_t.shape, const), pl.BlockSpec((n_exp, 1), const)],
        out_specs=[pl.BlockSpec((tm, d), row),
                   pl.BlockSpec((tm * SUBLANE, LANE), row),
                   pl.BlockSpec((SUBLANE, tm), lambda i: (0, i)),
                   pl.BlockSpec((SUBLANE, tm), lambda i: (0, i)),
                   pl.BlockSpec((SUBLANE, tm), lambda i: (0, i)),
                   pl.BlockSpec((n_exp, 1), const)],
        out_shape=[jax.ShapeDtypeStruct((m, d), F32),
                   jax.ShapeDtypeStruct((m * SUBLANE, LANE), F32),
                   jax.ShapeDtypeStruct((SUBLANE, m), jnp.int32),
                   jax.ShapeDtypeStruct((SUBLANE, m), F32),
                   jax.ShapeDtypeStruct((SUBLANE, m), jnp.int32),
                   jax.ShapeDtypeStruct((n_exp, 1), F32)],
        scratch_shapes=[pltpu.VMEM((n_exp, 1), F32)],
        compiler_params=_cparams(("arbitrary",), VMEM_LIMIT),
        name="post_mix",
    )(x, om_p, om_s, og_p, og_s, h, h, wo, g1, b1, wr_t, br)


def _moe_kernel(be_ref, rows_ref, nused_ref, x_hbm, wgu_ref, bgu_ref, wd_ref, bd_ref, o_ref,
                xbuf, sem, wgu_bf, wd_bf, xb_ref, *, blk, d_exp):
    i = pl.program_id(0)
    n_used = nused_ref[0]
    slot = i % 2

    def row_copy(tok, s, r):
        return pltpu.make_async_copy(x_hbm.at[pl.ds(tok * SUBLANE, SUBLANE)],
                                     xbuf.at[s, pl.ds(r * SUBLANE, SUBLANE)], sem.at[s])

    def wait(s):
        pltpu.make_async_copy(x_hbm.at[pl.ds(0, blk * SUBLANE)], xbuf.at[s], sem.at[s]).wait()

    @pl.when(jnp.logical_and(i == 0, n_used > 0))
    def _():
        second = jnp.minimum(1, n_used - 1) * blk

        def body(r, carry):
            row_copy(rows_ref[r], 0, r).start()
            row_copy(rows_ref[second + r], 1, r).start()
            return carry
        lax.fori_loop(0, blk, body, 0)

    @pl.when(i < n_used)
    def _():
        e = be_ref[i]
        e_prev = be_ref[jnp.maximum(i - 1, 0)]

        @pl.when(jnp.logical_or(i == 0, e != e_prev))
        def _():
            wgu_bf[...] = wgu_ref[0].astype(BF16)
            wd_bf[...] = wd_ref[0].astype(BF16)

        wait(slot)
        for j in range(SUBLANE):
            xb_ref[:, j * LANE:(j + 1) * LANE] = _load_row_tiles(xbuf.at[slot], 0, blk, j).astype(BF16)
        nxt = jnp.minimum(i + 2, n_used - 1) * blk
        for r in range(blk):
            row_copy(rows_ref[nxt + r], slot, r).start()
        hh = _dot(xb_ref[...], wgu_bf[...]) + bgu_ref[0]
        gate = jnp.minimum(hh[:, :d_exp], SWIGLU_LIMIT)
        up = jnp.clip(hh[:, d_exp:], -SWIGLU_LIMIT, SWIGLU_LIMIT)
        act = (up + 1.0) * gate * _sigmoid(SWIGLU_ALPHA * gate)
        _store_row_tiles(o_ref, _dot(act.astype(BF16), wd_bf[...]) + bd_ref[0])

        @pl.when(i == n_used - 1)
        def _():
            wait(0)
            wait(1)

    @pl.when(i >= n_used)
    def _():
        o_ref[...] = jnp.zeros(o_ref.shape, F32)


def _moe_experts(block_e, rows, n_used, x1, w_gu, b_gu, w_down, b_down, *, blk):
    n_exp, d, d2 = w_gu.shape
    d_exp = d2 // 2
    nb = block_e.shape[0]
    kern = functools.partial(_moe_kernel, blk=blk, d_exp=d_exp)
    grid_spec = pltpu.PrefetchScalarGridSpec(
        num_scalar_prefetch=3,
        grid=(nb,),
        in_specs=[pl.BlockSpec(memory_space=pl.ANY),
                  pl.BlockSpec((1, d, d2), lambda i, be, rw, nu: (be[i], 0, 0)),
                  pl.BlockSpec((1, 1, d2), lambda i, be, rw, nu: (be[i], 0, 0)),
                  pl.BlockSpec((1, d_exp, d), lambda i, be, rw, nu: (be[i], 0, 0)),
                  pl.BlockSpec((1, 1, d), lambda i, be, rw, nu: (be[i], 0, 0))],
        out_specs=pl.BlockSpec((blk * SUBLANE, LANE), lambda i, be, rw, nu: (i, 0)),
        scratch_shapes=[pltpu.VMEM((2, blk * SUBLANE, LANE), F32), pltpu.SemaphoreType.DMA((2,)),
                        pltpu.VMEM((d, d2), BF16), pltpu.VMEM((d_exp, d), BF16),
                        pltpu.VMEM((blk, d), BF16)])
    return pl.pallas_call(
        kern,
        grid_spec=grid_spec,
        out_shape=jax.ShapeDtypeStruct((nb * blk * SUBLANE, LANE), F32),
        compiler_params=_cparams(("arbitrary",), VMEM_LIMIT),
        name="moe_experts",
    )(block_e, rows, n_used, x1, w_gu, b_gu.reshape(n_exp, 1, d2), w_down, b_down.reshape(n_exp, 1, d))


def _combine_kernel(dest_ref, x1_ref, gate_ref, ys_hbm, g2_ref, b2_ref, op_ref, os_ref, ybuf, sem,
                    *, tm, top_k, alpha, m_total, n_first):
    i = pl.program_id(0)
    nsteps = pl.num_programs(0)
    slot = i % 2
    n_rows = top_k * tm

    def row_copy(d, s, r):
        return pltpu.make_async_copy(ys_hbm.at[pl.ds(d * SUBLANE, SUBLANE)],
                                     ybuf.at[s, pl.ds(r * SUBLANE, SUBLANE)], sem.at[s])

    def wait(s):
        pltpu.make_async_copy(ys_hbm.at[pl.ds(0, n_rows * SUBLANE)], ybuf.at[s], sem.at[s]).wait()

    @pl.when(i == 0)
    def _():
        second = jnp.minimum(1, nsteps - 1) * tm

        def body(r, carry):
            k = r // tm
            t = r - k * tm
            row_copy(dest_ref[k * m_total + t], 0, r).start()
            row_copy(dest_ref[k * m_total + second + t], 1, r).start()
            return carry
        lax.fori_loop(0, n_rows, body, 0)

    wait(slot)
    g = gate_ref[...]
    parts = []
    for j in range(SUBLANE):
        acc = g[:, 0:1] * _load_row_tiles(ybuf.at[slot], 0, tm, j)
        for k in range(1, top_k):
            acc = acc + g[:, k:k + 1] * _load_row_tiles(ybuf.at[slot], k * tm, tm, j)
        parts.append(acc)
    y = jnp.concatenate(parts, axis=1)
    nxt = jnp.minimum(i + 2, nsteps - 1) * tm
    for r in range(n_rows):
        k, t = divmod(r, tm)
        row_copy(dest_ref[k * m_total + nxt + t], slot, r).start()
    res = _layer_norm(alpha * x1_ref[...] + y, g2_ref[...], b2_ref[...])

    @pl.when(i < n_first)
    def _():
        op_ref[...] = res

    @pl.when(i >= n_first)
    def _():
        os_ref[...] = res

    @pl.when(i == nsteps - 1)
    def _():
        wait(0)
        wait(1)


def _combine(dest_km, x1, gates_mk, ys, g2, b2, *, tm, alpha, m_first):
    m, d = x1.shape
    n_first = m_first // tm
    assert m_first % tm == 0 and 0 < m_first < m
    kern = functools.partial(_combine_kernel, tm=tm, top_k=TOP_K, alpha=alpha, m_total=m, n_first=n_first)
    grid_spec = pltpu.PrefetchScalarGridSpec(
        num_scalar_prefetch=1,
        grid=(m // tm,),
        in_specs=[pl.BlockSpec((tm, d), lambda i, ds: (i, 0)),
                  pl.BlockSpec((tm, SUBLANE), lambda i, ds: (i, 0)),
                  pl.BlockSpec(memory_space=pl.ANY),
                  pl.BlockSpec((1, d), lambda i, ds: (0, 0)),
                  pl.BlockSpec((1, d), lambda i, ds: (0, 0))],
        out_specs=[pl.BlockSpec((tm, d), lambda i, ds: (jnp.minimum(i, n_first - 1), 0)),
                   pl.BlockSpec((tm, d), lambda i, ds: (jnp.maximum(i - n_first, 0), 0))],
        scratch_shapes=[pltpu.VMEM((2, TOP_K * tm * SUBLANE, LANE), F32), pltpu.SemaphoreType.DMA((2,))])
    return pl.pallas_call(
        kern,
        grid_spec=grid_spec,
        out_shape=[jax.ShapeDtypeStruct((m_first, d), F32), jax.ShapeDtypeStruct((m - m_first, d), F32)],
        compiler_params=_cparams(("arbitrary",), VMEM_LIMIT),
        name="moe_combine",
    )(dest_km, x1, gates_mk, ys, g2, b2)


def _pack_w_in(w_in, splits):
    q_lora, kv_lora, rope, conv_ch, gv, nh, _, d, _ = splits
    offs = [0]
    for s in splits:
        offs.append(offs[-1] + s)
    part = [w_in[:, offs[i]:offs[i + 1]] for i in range(len(splits))]
    q_lat, kv_lat, k_r, qkv, z, a, b, g_a, g_b = part
    dm = w_in.shape[0]
    half = rope // 2
    zpad = lambda n: jnp.zeros((dm, n), w_in.dtype)
    k_sw = jnp.concatenate([k_r[:, half:], k_r[:, :half]], axis=1)
    small = jnp.concatenate([q_lat, kv_lat, k_r, zpad(LANE - rope), k_sw, zpad(LANE - rope),
                             a, b, zpad(LANE - 2 * nh)], axis=1)
    return jnp.concatenate([qkv, small, z, g_a, g_b], axis=1).astype(BF16)


def _pack_w_uq(w_uq, nope, rope):
    w = jnp.transpose(w_uq, (1, 0, 2))
    half = rope // 2
    r = w[..., nope:]
    zp = jnp.zeros(r.shape[:-1] + (LANE - rope,), w.dtype)
    r_sw = jnp.concatenate([r[..., half:], r[..., :half]], axis=-1)
    return jnp.concatenate([w[..., :nope], r, zp, r_sw, zp], axis=-1).astype(BF16)


def _rope_tables(pos, rope):
    half = rope // 2
    inv = ROPE_THETA ** (-jnp.arange(half, dtype=F32) / half)
    ang = pos.astype(F32)[:, None] * inv[None, :]
    cos, sin = jnp.cos(ang), jnp.sin(ang)
    zp = jnp.zeros((pos.shape[0], LANE - rope), F32)
    return (jnp.concatenate([cos, cos, zp], axis=1), jnp.concatenate([-sin, sin, zp], axis=1))


def _route_meta(idx_t, rank_t, counts, m, n_exp, blk):
    a = m * TOP_K
    counts = counts.astype(jnp.int32)
    padded = (counts + blk - 1) // blk * blk
    pad_end = jnp.cumsum(padded)
    pad_start = pad_end - padded
    experts = jnp.arange(n_exp, dtype=jnp.int32)
    e_km = idx_t[:TOP_K]
    start_km = jnp.sum(jnp.where(e_km[:, :, None] == experts, pad_start, 0), axis=-1)
    dest_km = (start_km + rank_t[:TOP_K]).astype(jnp.int32).reshape(a)
    nb = a // blk + n_exp
    tok_km = jnp.tile(jnp.arange(m, dtype=jnp.int32), TOP_K)
    rows = jnp.zeros((nb * blk,), jnp.int32).at[dest_km].set(tok_km)
    first_row = jnp.arange(nb, dtype=jnp.int32) * blk
    block_e = jnp.minimum(jnp.sum((pad_end[None, :] <= first_row[:, None]).astype(jnp.int32), axis=1),
                          n_exp - 1).astype(jnp.int32)
    n_used = (pad_end[-1] // blk).astype(jnp.int32).reshape(1)
    return block_e, rows, n_used, dest_km


def kernel(x_prompt, x_sample, cache_ckv, cache_krope, page_table, state_conv, state_ssm, w_in, q_norm_w, kv_norm_w, w_uq, w_uk, w_uv, conv_w, a_log, dt_bias, gdn_norm_w, w_o, ln1_g, ln1_b, w_router, b_router, w_gu, b_gu, w_down, b_down, ln2_g, ln2_b):
    bp, tp, d = x_prompt.shape
    bs, ts, _ = x_sample.shape
    depth = w_in.shape[0]
    q_lora, n_heads, qk = w_uq.shape[1:]
    kv_lora, _, nope = w_uk.shape[1:]
    rope = qk - nope
    vh = w_uv.shape[3]
    cw_taps, conv_ch = conv_w.shape[1:]
    g_heads = a_log.shape[1]
    dk, dv = state_ssm.shape[3:]
    n_exp = w_router.shape[2]
    page = cache_ckv.shape[2]
    past = page_table.shape[1] * page
    splits = (q_lora, kv_lora, rope, conv_ch, g_heads * dv, g_heads, g_heads, d, d)
    assert sum(splits) == w_in.shape[2]
    alpha = (2 * depth) ** 0.25
    scale = (nope + rope) ** -0.5 * LOG2E
    mp, ms = bp * tp, bs * ts
    m = mp + ms
    small_w = q_lora + kv_lora + 3 * LANE
    assert small_w == d and conv_ch % d == 0
    qkv_blk, small_blk = 0, conv_ch // d
    z_blk, ga_blk, gb_blk = small_blk + 1, small_blk + 2, small_blk + 3
    ab_blk = (conv_ch + q_lora + kv_lora + 2 * LANE) // LANE

    cos_p, sin_p = _rope_tables(jnp.arange(tp, dtype=jnp.int32), rope)
    cos_s, sin_s = _rope_tables(past + jnp.arange(ts, dtype=jnp.int32), rope)
    cos_t = jnp.concatenate([jnp.tile(cos_p, (bp, 1)), jnp.tile(cos_s, (bs, 1))], axis=0)
    sin_t = jnp.concatenate([jnp.tile(sin_p, (bp, 1)), jnp.tile(sin_s, (bs, 1))], axis=0)

    x_p, x_s = x_prompt.reshape(mp, d), x_sample.reshape(ms, d)
    outs = {k: [] for k in ("ckv_p", "kr_p", "conv_p", "ssm_p", "ckv_s", "kr_s", "conv_s", "ssm_s")}
    pad_lanes = lambda v: jnp.pad(v, (0, LANE - v.shape[0])).reshape(1, LANE)
    for l in range(depth):
        x = jnp.concatenate([x_p, x_s], axis=0)
        w_pack = _pack_w_in(w_in[l], splits)
        wq = _pack_w_uq(w_uq[l], nope, rope)
        wuk = jnp.transpose(w_uk[l], (1, 2, 0)).astype(BF16)
        wuv = jnp.transpose(w_uv[l], (1, 0, 2)).astype(BF16)
        h = _in_proj(x, w_pack, tm=_row_tile(m, 2304), tn=512)
        qf, c_all, kr_all, kt, cb = _mla_proj(
            h, small_blk, cos_t, sin_t, q_norm_w[l].reshape(1, q_lora), kv_norm_w[l].reshape(1, kv_lora),
            wq, wuk, tm=256, q_lora=q_lora, kv_lora=kv_lora, nope=nope, rope=rope, scale=scale)
        om_p = _attn_prompt(qf, kt, cb, wuv, batch=bp, seq=tp, tq=128, tk=min(1024, tp))
        qs = qf[:, mp:, :].reshape(n_heads, bs, ts, qf.shape[-1])
        qs = jnp.transpose(qs, (1, 0, 2, 3)).reshape(bs, n_heads * ts, qf.shape[-1])
        om_s = _attn_sample(page_table, qs, c_all, kr_all, mp // ts, cache_ckv[l],
                            jnp.swapaxes(cache_krope[l], 1, 2), wuv)

        alog_p = pad_lanes(a_log[l])
        dtb_p = pad_lanes(dt_bias[l])
        nw = gdn_norm_w[l].reshape(1, dv)
        gdn_kw = dict(qkv_blk=qkv_blk, ab_blk=ab_blk, z_blk=z_blk, conv_w=cw_taps)
        zeros_conv = jnp.zeros((bp, SUBLANE, conv_ch), F32)
        zeros_ssm = jnp.zeros((bp, g_heads, dk, dv), F32)
        og_p, ssm_p, ncv_p = _gdn(h, 0, zeros_conv, zeros_ssm, conv_w[l], alog_p, dtb_p, nw,
                                  batch=bp, seq=tp, chunk=min(GDN_CHUNK, tp), group=bp, **gdn_kw)
        conv8_s = jnp.pad(state_conv[l], ((0, 0), (SUBLANE - (cw_taps - 1), 0), (0, 0)))
        og_s, ssm_s, ncv_s = _gdn(h, mp, conv8_s, state_ssm[l], conv_w[l], alog_p, dtb_p, nw,
                                  batch=bs, seq=ts, chunk=ts, group=math.gcd(bs, GDN_SAMPLE_GROUP), **gdn_kw)

        x1, x1t, idx_t, gate_t, rank_t, counts = _post(
            x, om_p, om_s, og_p, og_s, h, ga_blk, gb_blk, w_o[l].astype(BF16), ln1_g[l].reshape(1, d), ln1_b[l].reshape(1, d),
            w_router[l].T, b_router[l].reshape(n_exp, 1), tm=256, alpha=alpha)
        block_e, rows, n_used, dest_km = _route_meta(idx_t, rank_t, counts[:, 0], m, n_exp, MOE_ROWS)
        ys = _moe_experts(block_e, rows, n_used, x1t, w_gu[l], b_gu[l], w_down[l], b_down[l], blk=MOE_ROWS)
        x_p, x_s = _combine(dest_km, x1, gate_t.T, ys, ln2_g[l].reshape(1, d), ln2_b[l].reshape(1, d),
                            tm=128, alpha=alpha, m_first=mp)

        outs["ckv_p"].append(c_all[:mp].reshape(bp, tp, kv_lora))
        outs["kr_p"].append(kr_all[:mp].reshape(bp, tp, rope))
        outs["conv_p"].append(ncv_p[:, SUBLANE - (cw_taps - 1):, :])
        outs["ssm_p"].append(ssm_p)
        outs["ckv_s"].append(c_all[mp:].reshape(bs, ts, kv_lora))
        outs["kr_s"].append(kr_all[mp:].reshape(bs, ts, rope))
        outs["conv_s"].append(ncv_s[:, SUBLANE - (cw_taps - 1):, :])
        outs["ssm_s"].append(ssm_s)

    return (x_p.reshape(bp, tp, d), x_s.reshape(bs, ts, d),
            jnp.stack(outs["ckv_p"]), jnp.stack(outs["kr_p"]), jnp.stack(outs["conv_p"]), jnp.stack(outs["ssm_p"]),
            jnp.stack(outs["ckv_s"]), jnp.stack(outs["kr_s"]), jnp.stack(outs["conv_s"]), jnp.stack(outs["ssm_s"]))
```

```python
import functools
import math

import jax
import jax.numpy as jnp
from jax import lax
from jax.experimental import pallas as pl
from jax.experimental.pallas import tpu as pltpu

F32 = jnp.float32
BF16 = jnp.bfloat16

ROPE_THETA = 10000.0
NORM_EPS = 1e-6
TOP_K = 4
SWIGLU_LIMIT = 7.0
SWIGLU_ALPHA = 1.702
GDN_CHUNK = 64
GDN_DIAG = 16
GDN_SAMPLE_GROUP = 8
MOE_ROWS = 512
PAGE_CHUNK = 8
LANE = 128
SUBLANE = 8
VMEM_LIMIT = 56 * 1024 * 1024
NEG = -1e30
LOG2E = 1.4426950408889634


def _cparams(sem, vmem=None):
    return pltpu.CompilerParams(dimension_semantics=sem, vmem_limit_bytes=vmem)


def _dot(a, b, **kw):
    return jnp.dot(a, b, preferred_element_type=F32, **kw)


def _dot_nt(a, b, **kw):
    return lax.dot_general(a, b, (((1,), (1,)), ((), ())), preferred_element_type=F32, **kw)


def _bdot(a, b):
    return lax.dot_general(a, b, (((2,), (1,)), ((0,), (0,))), preferred_element_type=F32)


def _bdot_nt(a, b):
    return lax.dot_general(a, b, (((2,), (2,)), ((0,), (0,))), preferred_element_type=F32)


def _bdot_tn(a, b):
    return lax.dot_general(a, b, (((1,), (1,)), ((0,), (0,))), preferred_element_type=F32)


def _sigmoid(x):
    return 1.0 / (1.0 + jnp.exp(-x))


def _layer_norm(v, g, b):
    mu = jnp.mean(v, -1, keepdims=True)
    vc = v - mu
    var = jnp.mean(vc * vc, -1, keepdims=True)
    return vc * lax.rsqrt(var + NORM_EPS) * g + b


def _inproj_kernel(x_ref, w_ref, o_ref, xb_ref):
    @pl.when(pl.program_id(1) == 0)
    def _():
        xb_ref[...] = x_ref[...].astype(BF16)

    o_ref[...] = _dot(xb_ref[...], w_ref[...])


def _row_tile(m, target):
    return max(t for t in range(SUBLANE, target + 1, SUBLANE) if m % t == 0)


def _in_proj(x, w, tm, tn):
    m, k = x.shape
    n = w.shape[1]
    return pl.pallas_call(
        _inproj_kernel,
        grid=(m // tm, n // tn),
        in_specs=[pl.BlockSpec((tm, k), lambda i, j: (i, 0)),
                  pl.BlockSpec((k, tn), lambda i, j: (0, j))],
        out_specs=pl.BlockSpec((tm, tn), lambda i, j: (i, j)),
        out_shape=jax.ShapeDtypeStruct((m, n), F32),
        scratch_shapes=[pltpu.VMEM((tm, k), BF16)],
        compiler_params=_cparams(("parallel", "arbitrary"), VMEM_LIMIT),
        name="in_proj",
    )(x, w)


def _mla_proj_kernel(h_ref, cos_ref, sin_ref, qnw_ref, kvnw_ref, wq_ref, wuk_ref,
                     qf_ref, c_ref, kr_ref, kt_ref, cb_ref, *, n_heads, q_lora, kv_lora, nope, rope, scale):
    hs = h_ref[...]
    cos = cos_ref[...]
    sin = sin_ref[...]
    q_lat = hs[:, :q_lora]
    qn = q_lat * lax.rsqrt(jnp.mean(q_lat * q_lat, -1, keepdims=True) + NORM_EPS) * qnw_ref[...]
    qn = qn.astype(BF16)
    kv = hs[:, q_lora:q_lora + kv_lora]
    c = kv * lax.rsqrt(jnp.mean(kv * kv, -1, keepdims=True) + NORM_EPS) * kvnw_ref[...]
    o = q_lora + kv_lora
    kr = hs[:, o:o + LANE] * cos + hs[:, o + LANE:o + 2 * LANE] * sin
    c_ref[...] = c
    kr_ref[...] = kr[:, :rope]
    cb = c.astype(BF16)
    cb_ref[...] = cb
    kfull = jnp.concatenate([cb, kr.astype(BF16)], axis=1)
    kw = kfull.shape[1]
    eye = (lax.broadcasted_iota(jnp.int32, (kw, kw), 0) == lax.broadcasted_iota(jnp.int32, (kw, kw), 1))
    kt_ref[...] = _dot_nt(eye.astype(BF16), kfull).astype(BF16)
    for h in range(n_heads):
        qh = _dot(qn, wq_ref[h])
        qa = _dot(qh[:, :nope].astype(BF16), wuk_ref[h])
        qr = qh[:, nope:nope + LANE] * cos + qh[:, nope + LANE:nope + 2 * LANE] * sin
        qf_ref[h, :, :kv_lora] = (qa * scale).astype(BF16)
        qf_ref[h, :, kv_lora:] = (qr * scale).astype(BF16)


def _mla_proj(h, col_blk, cos_t, sin_t, qnw, kvnw, wq, wuk, *, tm, q_lora, kv_lora, nope, rope, scale):
    m = h.shape[0]
    n_heads = wq.shape[0]
    wcol = q_lora + kv_lora + 3 * LANE
    kw = kv_lora + LANE
    kern = functools.partial(_mla_proj_kernel, n_heads=n_heads, q_lora=q_lora, kv_lora=kv_lora,
                             nope=nope, rope=rope, scale=scale)
    return pl.pallas_call(
        kern,
        grid=(m // tm,),
        in_specs=[pl.BlockSpec((tm, wcol), lambda i: (i, col_blk)),
                  pl.BlockSpec((tm, LANE), lambda i: (i, 0)),
                  pl.BlockSpec((tm, LANE), lambda i: (i, 0)),
                  pl.BlockSpec((1, q_lora), lambda i: (0, 0)),
                  pl.BlockSpec((1, kv_lora), lambda i: (0, 0)),
                  pl.BlockSpec(wq.shape, lambda i: (0, 0, 0)),
                  pl.BlockSpec(wuk.shape, lambda i: (0, 0, 0))],
        out_specs=[pl.BlockSpec((n_heads, tm, kw), lambda i: (0, i, 0)),
                   pl.BlockSpec((tm, kv_lora), lambda i: (i, 0)),
                   pl.BlockSpec((tm, rope), lambda i: (i, 0)),
                   pl.BlockSpec((kw, tm), lambda i: (0, i)),
                   pl.BlockSpec((tm, kv_lora), lambda i: (i, 0))],
        out_shape=[jax.ShapeDtypeStruct((n_heads, m, kw), BF16),
                   jax.ShapeDtypeStruct((m, kv_lora), F32),
                   jax.ShapeDtypeStruct((m, rope), F32),
                   jax.ShapeDtypeStruct((kw, m), BF16),
                   jax.ShapeDtypeStruct((m, kv_lora), BF16)],
        compiler_params=_cparams(("parallel",), VMEM_LIMIT),
        name="mla_proj",
    )(h, cos_t, sin_t, qnw, kvnw, wq, wuk)


def _softmax_step(s, v, m_ref, l_ref, acc_ref):
    m_prev = m_ref[...]
    m_new = jnp.maximum(m_prev, jnp.max(s, -1, keepdims=True))
    alpha = jnp.exp2(m_prev - m_new)
    p = jnp.exp2(s - m_new)
    l_ref[...] = alpha * l_ref[...] + jnp.sum(p, -1, keepdims=True)
    acc_ref[...] = alpha * acc_ref[...] + _dot(p.astype(BF16), v)
    m_ref[...] = m_new


def _attn_prompt_kernel(qf_ref, kt_ref, v_ref, wuv_ref, o_ref, m_ref, l_ref, acc_ref, *, n_heads, tq, tk, dv, vh, n_split):
    i = pl.program_id(1)
    m_ref[...] = jnp.full(m_ref.shape, NEG, F32)
    l_ref[...] = jnp.zeros(l_ref.shape, F32)
    acc_ref[...] = jnp.zeros(acc_ref.shape, F32)
    n_full = (i * tq) // tk
    off = i * tq - n_full * tk
    hs = n_heads // n_split
    rs = hs * tq

    def step(j, masked):
        kt = kt_ref[:, pl.ds(pl.multiple_of(j * tk, tk), tk)]
        v = v_ref[pl.ds(pl.multiple_of(j * tk, tk), tk), :]
        for g in range(n_split):
            q = qf_ref[g * hs:(g + 1) * hs].reshape(rs, qf_ref.shape[-1])
            s = _dot(q, kt)
            if masked:
                row = lax.broadcasted_iota(jnp.int32, (tq, tk), 0)
                col = lax.broadcasted_iota(jnp.int32, (tq, tk), 1)
                s = jnp.where((col <= row + off)[None], s.reshape(hs, tq, tk), NEG).reshape(rs, tk)
            sl = slice(g * rs, (g + 1) * rs)
            _softmax_step(s, v, m_ref.at[sl], l_ref.at[sl], acc_ref.at[sl])

    def body(j, carry):
        step(j, False)
        return carry

    lax.fori_loop(0, n_full, body, 0)
    step(n_full, True)
    o = acc_ref[...] / l_ref[...]
    for h in range(n_heads):
        oh = o[h * tq:(h + 1) * tq].astype(BF16)
        o_ref[:, h * vh:(h + 1) * vh] = _dot(oh, wuv_ref[h])


def _attn_prompt(qf, kt, cb, wuv, *, batch, seq, tq, tk):
    n_heads, _, kw = qf.shape
    dv, vh = wuv.shape[1], wuv.shape[2]
    nq = seq // tq
    rows = n_heads * tq
    assert seq % tk == 0 and tk % tq == 0
    kern = functools.partial(_attn_prompt_kernel, n_heads=n_heads, tq=tq, tk=tk, dv=dv, vh=vh, n_split=2)
    return pl.pallas_call(
        kern,
        grid=(batch, nq),
        in_specs=[pl.BlockSpec((n_heads, tq, kw), lambda b, i: (0, b * nq + i, 0)),
                  pl.BlockSpec((kw, seq), lambda b, i: (0, b)),
                  pl.BlockSpec((seq, dv), lambda b, i: (b, 0)),
                  pl.BlockSpec(wuv.shape, lambda b, i: (0, 0, 0))],
        out_specs=pl.BlockSpec((tq, n_heads * vh), lambda b, i: (b * nq + i, 0)),
        out_shape=jax.ShapeDtypeStruct((batch * seq, n_heads * vh), F32),
        scratch_shapes=[pltpu.VMEM((rows, 1), F32), pltpu.VMEM((rows, 1), F32),
                        pltpu.VMEM((rows, dv), F32)],
        compiler_params=_cparams(("parallel", "arbitrary"), VMEM_LIMIT),
        name="attn_prompt",
    )(qf, kt, cb, wuv)


def _attn_sample_kernel(pt_ref, q_ref, cn_ref, krn_ref, cc_hbm, cr_hbm, wuv_ref, o_ref,
                        cbuf, rbuf, sem, *, n_pages, page, n_heads, ts, dv, dr, vh):
    b = pl.program_id(0)
    nb = pl.num_programs(0)
    slot = b % 2
    rows = n_heads * ts

    def fetch(bb, s):
        for p in range(n_pages):
            pg = pt_ref[bb, p]
            pltpu.make_async_copy(cc_hbm.at[pg], cbuf.at[s, p], sem.at[0, s]).start()
            pltpu.make_async_copy(cr_hbm.at[pg], rbuf.at[s, p], sem.at[1, s]).start(priority=1)

    def wait(s):
        pltpu.make_async_copy(cc_hbm.at[pl.ds(0, n_pages)], cbuf.at[s], sem.at[0, s]).wait()
        pltpu.make_async_copy(cr_hbm.at[pl.ds(0, n_pages)], rbuf.at[s], sem.at[1, s]).wait()

    @pl.when(b == 0)
    def _():
        fetch(0, 0)

    wait(slot)
    fetch(jnp.minimum(b + 1, nb - 1), 1 - slot)

    q = q_ref[0].astype(F32)
    qc = q[:, :dv]
    qr = q[:, dv:dv + dr]
    n_chunks = n_pages // PAGE_CHUNK
    ck = PAGE_CHUNK * page
    parts = []
    for ch in range(n_chunks):
        c_ch = cbuf[slot, ch * PAGE_CHUNK:(ch + 1) * PAGE_CHUNK].reshape(ck, dv)
        r_ch = jnp.concatenate([rbuf[slot, ch * PAGE_CHUNK + u] for u in range(PAGE_CHUNK)], axis=1)
        parts.append(_dot_nt(qc, c_ch) + _dot(qr, r_ch))
    cn = jnp.concatenate([cn_ref[...], jnp.zeros((LANE - ts, dv), F32)], axis=0)
    krn = jnp.concatenate([krn_ref[...], jnp.zeros((LANE - ts, dr), F32)], axis=0)
    row = lax.broadcasted_iota(jnp.int32, (ts, LANE), 0)
    col = lax.broadcasted_iota(jnp.int32, (ts, LANE), 1)
    s_new = (_dot_nt(qc, cn) + _dot_nt(qr, krn)).reshape(n_heads, ts, LANE)
    s_new = jnp.where((col <= row)[None], s_new, NEG).reshape(rows, LANE)

    m = jnp.max(s_new, -1, keepdims=True)
    for s in parts:
        m = jnp.maximum(m, jnp.max(s, -1, keepdims=True))
    p_new = jnp.exp2(s_new - m)
    l = jnp.sum(p_new, -1, keepdims=True)
    acc = _dot(p_new, cn)
    for ch in range(n_chunks):
        p = jnp.exp2(parts[ch] - m)
        l = l + jnp.sum(p, -1, keepdims=True)
        acc = acc + _dot(p, cbuf[slot, ch * PAGE_CHUNK:(ch + 1) * PAGE_CHUNK].reshape(ck, dv))
    o = acc / l
    for h in range(n_heads):
        oh = o[h * ts:(h + 1) * ts].astype(BF16)
        o_ref[:, h * vh:(h + 1) * vh] = _dot(oh, wuv_ref[h])

    @pl.when(b == nb - 1)
    def _():
        wait(1 - slot)


def _attn_sample(page_table, qs, c_all, kr_all, row_blk_off, cache_c, cache_rt, wuv):
    bs, rows, kw = qs.shape
    n_heads, dv, vh = wuv.shape
    ts = rows // n_heads
    n_pages = page_table.shape[1]
    page = cache_c.shape[1]
    dr = cache_rt.shape[1]
    assert n_pages % PAGE_CHUNK == 0
    kern = functools.partial(_attn_sample_kernel, n_pages=n_pages, page=page, n_heads=n_heads, ts=ts,
                             dv=dv, dr=dr, vh=vh)
    grid_spec = pltpu.PrefetchScalarGridSpec(
        num_scalar_prefetch=1,
        grid=(bs,),
        in_specs=[pl.BlockSpec((1, rows, kw), lambda b, pt: (b, 0, 0)),
                  pl.BlockSpec((ts, dv), lambda b, pt: (row_blk_off + b, 0)),
                  pl.BlockSpec((ts, dr), lambda b, pt: (row_blk_off + b, 0)),
                  pl.BlockSpec(memory_space=pl.ANY),
                  pl.BlockSpec(memory_space=pl.ANY),
                  pl.BlockSpec(wuv.shape, lambda b, pt: (0, 0, 0))],
        out_specs=pl.BlockSpec((ts, n_heads * vh), lambda b, pt: (b, 0)),
        scratch_shapes=[pltpu.VMEM((2, n_pages, page, dv), F32), pltpu.VMEM((2, n_pages, dr, page), F32),
                        pltpu.SemaphoreType.DMA((2, 2))])
    return pl.pallas_call(
        kern,
        grid_spec=grid_spec,
        out_shape=jax.ShapeDtypeStruct((bs * ts, n_heads * vh), F32),
        compiler_params=_cparams(("arbitrary",), VMEM_LIMIT),
        name="attn_sample",
    )(page_table, qs, c_all, kr_all, cache_c, cache_rt, wuv)


def _neumann_inv(low, eye, steps):
    p = eye - low
    x = low
    for _ in range(steps):
        x = _bdot(x, x)
        p = p + _bdot(p, x)
    return p


def _gdn_kernel(*refs, n_in, n_seq, chunk, n_heads, dk, dv, conv_w):
    qkv_refs, ab_refs, z_refs = refs[:n_in], refs[n_in:2 * n_in], refs[2 * n_in:3 * n_in]
    (cs_ref, s0_ref, cw_ref, alog_ref, dtb_ref, nw_ref,
     o_ref, sfin_ref, ncv_ref, xp_ref, s_ref) = refs[3 * n_in:]
    n = pl.program_id(1)
    c = chunk
    per = n_seq // n_in
    n_prob = n_seq * n_heads
    hi = lax.Precision.HIGHEST

    def seq_rows(group, g):
        k = g % per
        return group[g // per][k * c:(k + 1) * c, :]

    @pl.when(n == 0)
    def _():
        s_ref[...] = s0_ref[...].reshape(n_prob, dk, dv)
        xp_ref[:, 0:SUBLANE, :] = cs_ref[...]

    r_i = lax.broadcasted_iota(jnp.int32, (c, c), 0)
    c_i = lax.broadcasted_iota(jnp.int32, (c, c), 1)
    causal = (c_i <= r_i)[None]
    strict = (c_i < r_i)[None]
    eye_c = (r_i == c_i).astype(F32)[None]
    db = min(GDN_DIAG, c)
    n_blk = c // db
    sh = int(math.log2(db))
    same_blk = (jnp.right_shift(r_i, sh) == jnp.right_shift(c_i, sh))[None]
    r_l = lax.broadcasted_iota(jnp.int32, (LANE, LANE), 0)
    c_l = lax.broadcasted_iota(jnp.int32, (LANE, LANE), 1)
    eye_l = (r_l == c_l).astype(F32)
    tril = (c_i <= r_i).astype(F32)
    cw = cw_ref[...]
    hk = n_heads * dk

    q_l, k_l, v_l, z_l, beta_l, gcol_l, grow_l, tails = [], [], [], [], [], [], [], []
    for g in range(n_seq):
        xp_ref[g, SUBLANE:SUBLANE + c, :] = seq_rows(qkv_refs, g)
        conv = xp_ref[g, pl.ds(SUBLANE - (conv_w - 1), c), :] * cw[0:1, :]
        for j in range(1, conv_w):
            conv = conv + xp_ref[g, pl.ds(SUBLANE - (conv_w - 1) + j, c), :] * cw[j:j + 1, :]
        tail = xp_ref[g, c:c + SUBLANE, :]
        xp_ref[g, 0:SUBLANE, :] = tail
        tails.append(tail)
        act = conv * _sigmoid(conv)
        ab = seq_rows(ab_refs, g)
        apb = ab + dtb_ref[...]
        softplus = jnp.maximum(apb, 0.0) + jnp.log(1.0 + jnp.exp(-jnp.abs(apb)))
        gfull = -jnp.exp(alog_ref[...]) * softplus
        betaf = _sigmoid(ab)
        gc = _dot(tril, gfull, precision=hi)
        gc_t = _dot_nt(eye_l, gc, precision=hi)
        zg = seq_rows(z_refs, g)
        for h in range(n_heads):
            q_l.append(act[:, h * dk:(h + 1) * dk])
            k_l.append(act[:, hk + h * dk:hk + (h + 1) * dk])
            v_l.append(act[:, 2 * hk + h * dv:2 * hk + (h + 1) * dv])
            z_l.append(zg[:, h * dv:(h + 1) * dv])
            beta_l.append(betaf[:, n_heads + h:n_heads + h + 1])
            gcol_l.append(gc[:, h:h + 1])
            grow_l.append(gc_t[h:h + 1, :])

    q = jnp.stack(q_l)
    k = jnp.stack(k_l)
    v = jnp.stack(v_l)
    z = jnp.stack(z_l)
    beta = jnp.stack(beta_l)
    gcol = jnp.stack(gcol_l)
    grow = jnp.stack(grow_l)
    q = q * lax.rsqrt(jnp.sum(q * q, -1, keepdims=True) + NORM_EPS) * (dk ** -0.5)
    k = k * lax.rsqrt(jnp.sum(k * k, -1, keepdims=True) + NORM_EPS)
    decay = jnp.where(causal, jnp.exp(jnp.where(causal, gcol - grow, 0.0)), 0.0)
    kb = k * beta
    lower = jnp.where(strict, _bdot_nt(kb, k) * decay, 0.0)
    attn = _bdot_nt(q, k) * decay
    l_d = jnp.where(same_blk, lower, 0.0)
    t_inv = _neumann_inv(l_d, eye_c, int(math.log2(db)) - 1)
    if n_blk > 1:
        m_inv = _neumann_inv(_bdot(t_inv, lower - l_d), eye_c, int(math.log2(n_blk)) - 1)
        t_inv = _bdot(m_inv, t_inv)
    egc = jnp.exp(gcol)
    uw = _bdot(t_inv, jnp.concatenate([v * beta, kb * egc], axis=2))
    s = s_ref[...]
    v_new = uw[:, :, :dv] - _bdot(uw[:, :, dv:], s)
    o = _bdot(q * egc, s) + _bdot(attn, v_new)
    glast = gcol[:, c - 1:c, :]
    kdec = k * jnp.exp(glast - gcol)
    s_new = s * jnp.exp(glast) + _bdot_tn(kdec, v_new)
    s_ref[...] = s_new
    on = o * lax.rsqrt(jnp.mean(o * o, -1, keepdims=True) + NORM_EPS) * nw_ref[...] * (z * _sigmoid(z))
    for g in range(n_seq):
        for h in range(n_heads):
            o_ref[g, :, h * dv:(h + 1) * dv] = on[g * n_heads + h]

    @pl.when(n == pl.num_programs(1) - 1)
    def _():
        sfin_ref[...] = s_new.reshape(n_seq, n_heads, dk, dv)
        for g in range(n_seq):
            ncv_ref[g] = tails[g]


def _gdn(h, row_off, conv_state8, ssm_state, cw, alog_p, dtb_p, nw, *, batch, seq, chunk, group,
         qkv_blk, ab_blk, z_blk, conv_w):
    _, n_heads, dk, dv = ssm_state.shape
    nc = seq // chunk
    ch = cw.shape[1]
    hd = n_heads * dv
    contiguous = nc == 1
    n_in = 1 if contiguous else group
    assert batch % group == 0 and row_off % (group * chunk) == 0
    kern = functools.partial(_gdn_kernel, n_in=n_in, n_seq=group, chunk=chunk, n_heads=n_heads,
                             dk=dk, dv=dv, conv_w=conv_w)
    if contiguous:
        rb = group * chunk
        row_maps = [lambda i, n: row_off // rb + i]
    else:
        rb = chunk
        row_maps = [(lambda i, n, g=g: row_off // rb + (i * group + g) * nc + n) for g in range(group)]

    def specs(width, col_blk):
        return [pl.BlockSpec((rb, width), lambda i, n, r=r: (r(i, n), col_blk)) for r in row_maps]

    in_specs = specs(ch, qkv_blk) + specs(LANE, ab_blk) + specs(hd, z_blk)
    in_specs += [pl.BlockSpec((group, SUBLANE, ch), lambda i, n: (i, 0, 0)),
                 pl.BlockSpec((group, n_heads, dk, dv), lambda i, n: (i, 0, 0, 0)),
                 pl.BlockSpec(cw.shape, lambda i, n: (0, 0)),
                 pl.BlockSpec((1, LANE), lambda i, n: (0, 0)),
                 pl.BlockSpec((1, LANE), lambda i, n: (0, 0)),
                 pl.BlockSpec((1, dv), lambda i, n: (0, 0))]
    o3, sfin, ncv = pl.pallas_call(
        kern,
        grid=(batch // group, nc),
        in_specs=in_specs,
        out_specs=[pl.BlockSpec((group, chunk, hd), lambda i, n: (i, n, 0)),
                   pl.BlockSpec((group, n_heads, dk, dv), lambda i, n: (i, 0, 0, 0)),
                   pl.BlockSpec((group, SUBLANE, ch), lambda i, n: (i, 0, 0))],
        out_shape=[jax.ShapeDtypeStruct((batch, seq, hd), F32),
                   jax.ShapeDtypeStruct(ssm_state.shape, F32),
                   jax.ShapeDtypeStruct((batch, SUBLANE, ch), F32)],
        scratch_shapes=[pltpu.VMEM((group, chunk + SUBLANE, ch), F32),
                        pltpu.VMEM((group * n_heads, dk, dv), F32)],
        compiler_params=_cparams(("parallel", "arbitrary"), VMEM_LIMIT),
        name="gdn",
    )(*([h] * (3 * n_in)), conv_state8, ssm_state, cw, alog_p, dtb_p, nw)
    return o3.reshape(batch * seq, hd), sfin, ncv


def _store_row_tiles(ref, val):
    rows = val.shape[0]
    for j in range(SUBLANE):
        ref[pl.ds(j, rows, stride=SUBLANE), :] = val[:, j * LANE:(j + 1) * LANE]


def _load_row_tiles(ref, start, rows, j):
    return ref[pl.ds(start * SUBLANE + j, rows, stride=SUBLANE), :]


def _post_kernel(x_ref, omp_ref, oms_ref, ogp_ref, ogs_ref, ga_ref, gb_ref, wo_ref, g1_ref, b1_ref, wr_ref, br_ref,
                 x1_ref, x1t_ref, idx_ref, gate_ref, rank_ref, cnt_ref, carry_ref, *, alpha, top_k, n_first):
    @pl.when(pl.program_id(0) == 0)
    def _():
        carry_ref[...] = jnp.zeros(carry_ref.shape, F32)

    first = pl.program_id(0) < n_first
    om = jnp.where(first, omp_ref[...], oms_ref[...])
    og = jnp.where(first, ogp_ref[...], ogs_ref[...])
    mix = _sigmoid(ga_ref[...]) * om + _sigmoid(gb_ref[...]) * og
    y = _dot(mix.astype(BF16), wo_ref[...])
    x1 = _layer_norm(alpha * x_ref[...] + y, g1_ref[...], b1_ref[...])
    x1_ref[...] = x1
    _store_row_tiles(x1t_ref, x1)
    logits = _dot_nt(wr_ref[...], x1, precision=lax.Precision.HIGHEST) + br_ref[...]
    n_exp, tm = logits.shape
    e_i = lax.broadcasted_iota(jnp.int32, (n_exp, tm), 0)
    vals, idxs = [], []
    for _ in range(top_k):
        mx = jnp.max(logits, axis=0, keepdims=True)
        ix = jnp.min(jnp.where(logits == mx, e_i, n_exp), axis=0, keepdims=True)
        vals.append(mx)
        idxs.append(ix)
        logits = jnp.where(e_i == ix, -jnp.inf, logits)
    es = [jnp.exp(v - vals[0]) for v in vals]
    tot = es[0]
    for e in es[1:]:
        tot = tot + e
    onehots = [e_i == ix for ix in idxs]
    sel = onehots[0].astype(F32)
    for oh in onehots[1:]:
        sel = sel + oh.astype(F32)
    t_r = lax.broadcasted_iota(jnp.int32, (tm, tm), 0)
    t_c = lax.broadcasted_iota(jnp.int32, (tm, tm), 1)
    before = _dot(sel.astype(BF16), (t_r < t_c).astype(BF16)) + carry_ref[...]
    carry = carry_ref[...] + jnp.sum(sel, axis=1, keepdims=True)
    carry_ref[...] = carry
    cnt_ref[...] = carry
    r_i = lax.broadcasted_iota(jnp.int32, (SUBLANE, tm), 0)
    idx_o = jnp.zeros((SUBLANE, tm), jnp.int32)
    gate_o = jnp.zeros((SUBLANE, tm), F32)
    rank_o = jnp.zeros((SUBLANE, tm), F32)
    for k in range(top_k):
        idx_o = jnp.where(r_i == k, idxs[k], idx_o)
        gate_o = jnp.where(r_i == k, es[k] / tot, gate_o)
        rank_k = jnp.sum(jnp.where(onehots[k], before, 0.0), axis=0, keepdims=True)
        rank_o = jnp.where(r_i == k, rank_k, rank_o)
    idx_ref[...] = idx_o
    gate_ref[...] = gate_o
    rank_ref[...] = rank_o.astype(jnp.int32)


def _post(x, om_p, om_s, og_p, og_s, h, ga_blk, gb_blk, wo, g1, b1, wr_t, br, *, tm, alpha):
    m, d = x.shape
    n_exp = wr_t.shape[0]
    n_first = om_p.shape[0] // tm
    assert om_p.shape[0] % tm == 0 and om_s.shape[0] % tm == 0 and d == SUBLANE * LANE
    kern = functools.partial(_post_kernel, alpha=alpha, top_k=TOP_K, n_first=n_first)
    row = lambda i: (i, 0)
    const = lambda i: (0, 0)
    first = lambda i: (jnp.minimum(i, n_first - 1), 0)
    second = lambda i: (jnp.maximum(i - n_first, 0), 0)
    return pl.pallas_call(
        kern,
        grid=(m // tm,),
        in_specs=[pl.BlockSpec((tm, d), row),
                  pl.BlockSpec((tm, d), first), pl.BlockSpec((tm, d), second),
                  pl.BlockSpec((tm, d), first), pl.BlockSpec((tm, d), second),
                  pl.BlockSpec((tm, d), lambda i: (i, ga_blk)),
                  pl.BlockSpec((tm, d), lambda i: (i, gb_blk)),
                  pl.BlockSpec(wo.shape, const), pl.BlockSpec((1, d), const), pl.BlockSpec((1, d), const),
                  pl.BlockSpec(wr_t.shape, const), pl.BlockSpec((n_exp, 1), const)],
        out_specs=[pl.BlockSpec((tm, d), row),
                   pl.BlockSpec((tm * SUBLANE, LANE), row),
                   pl.BlockSpec((SUBLANE, tm), lambda i: (0, i)),
                   pl.BlockSpec((SUBLANE, tm), lambda i: (0, i)),
                   pl.BlockSpec((SUBLANE, tm), lambda i: (0, i)),
                   pl.BlockSpec((n_exp, 1), const)],
        out_shape=[jax.ShapeDtypeStruct((m, d), F32),
                   jax.ShapeDtypeStruct((m * SUBLANE, LANE), F32),
                   jax.ShapeDtypeStruct((SUBLANE, m), jnp.int32),
                   jax.ShapeDtypeStruct((SUBLANE, m), F32),
                   jax.ShapeDtypeStruct((SUBLANE, m), jnp.int32),
                   jax.ShapeDtypeStruct((n_exp, 1), F32)],
        scratch_shapes=[pltpu.VMEM((n_exp, 1), F32)],
        compiler_params=_cparams(("arbitrary",), VMEM_LIMIT),
        name="post_mix",
    )(x, om_p, om_s, og_p, og_s, h, h, wo, g1, b1, wr_t, br)


def _moe_kernel(be_ref, rows_ref, nused_ref, x_hbm, wgu_ref, bgu_ref, wd_ref, bd_ref, o_ref,
                xbuf, sem, wgu_bf, wd_bf, xb_ref, *, blk, d_exp):
    i = pl.program_id(0)
    n_used = nused_ref[0]
    slot = i % 2

    def row_copy(tok, s, r):
        return pltpu.make_async_copy(x_hbm.at[pl.ds(tok * SUBLANE, SUBLANE)],
                                     xbuf.at[s, pl.ds(r * SUBLANE, SUBLANE)], sem.at[s])

    def wait(s):
        pltpu.make_async_copy(x_hbm.at[pl.ds(0, blk * SUBLANE)], xbuf.at[s], sem.at[s]).wait()

    @pl.when(jnp.logical_and(i == 0, n_used > 0))
    def _():
        second = jnp.minimum(1, n_used - 1) * blk

        def body(r, carry):
            row_copy(rows_ref[r], 0, r).start()
            row_copy(rows_ref[second + r], 1, r).start()
            return carry
        lax.fori_loop(0, blk, body, 0)

    @pl.when(i < n_used)
    def _():
        e = be_ref[i]
        e_prev = be_ref[jnp.maximum(i - 1, 0)]

        @pl.when(jnp.logical_or(i == 0, e != e_prev))
        def _():
            wgu_bf[...] = wgu_ref[0].astype(BF16)
            wd_bf[...] = wd_ref[0].astype(BF16)

        wait(slot)
        for j in range(SUBLANE):
            xb_ref[:, j * LANE:(j + 1) * LANE] = _load_row_tiles(xbuf.at[slot], 0, blk, j).astype(BF16)
        nxt = jnp.minimum(i + 2, n_used - 1) * blk
        for r in range(blk):
            row_copy(rows_ref[nxt + r], slot, r).start(priority=1)
        hh = _dot(xb_ref[...], wgu_bf[...]) + bgu_ref[0]
        gate = jnp.minimum(hh[:, :d_exp], SWIGLU_LIMIT)
        up = jnp.clip(hh[:, d_exp:], -SWIGLU_LIMIT, SWIGLU_LIMIT)
        act = (up + 1.0) * gate * _sigmoid(SWIGLU_ALPHA * gate)
        _store_row_tiles(o_ref, _dot(act.astype(BF16), wd_bf[...]) + bd_ref[0])

        @pl.when(i == n_used - 1)
        def _():
            wait(0)
            wait(1)

    @pl.when(i >= n_used)
    def _():
        o_ref[...] = jnp.zeros(o_ref.shape, F32)


def _moe_experts(block_e, rows, n_used, x1, w_gu, b_gu, w_down, b_down, *, blk):
    n_exp, d, d2 = w_gu.shape
    d_exp = d2 // 2
    nb = block_e.shape[0]
    kern = functools.partial(_moe_kernel, blk=blk, d_exp=d_exp)
    grid_spec = pltpu.PrefetchScalarGridSpec(
        num_scalar_prefetch=3,
        grid=(nb,),
        in_specs=[pl.BlockSpec(memory_space=pl.ANY),
                  pl.BlockSpec((1, d, d2), lambda i, be, rw, nu: (be[i], 0, 0)),
                  pl.BlockSpec((1, 1, d2), lambda i, be, rw, nu: (be[i], 0, 0)),
                  pl.BlockSpec((1, d_exp, d), lambda i, be, rw, nu: (be[i], 0, 0)),
                  pl.BlockSpec((1, 1, d), lambda i, be, rw, nu: (be[i], 0, 0))],
        out_specs=pl.BlockSpec((blk * SUBLANE, LANE), lambda i, be, rw, nu: (i, 0)),
        scratch_shapes=[pltpu.VMEM((2, blk * SUBLANE, LANE), F32), pltpu.SemaphoreType.DMA((2,)),
                        pltpu.VMEM((d, d2), BF16), pltpu.VMEM((d_exp, d), BF16),
                        pltpu.VMEM((blk, d), BF16)])
    return pl.pallas_call(
        kern,
        grid_spec=grid_spec,
        out_shape=jax.ShapeDtypeStruct((nb * blk * SUBLANE, LANE), F32),
        compiler_params=_cparams(("arbitrary",), VMEM_LIMIT),
        name="moe_experts",
    )(block_e, rows, n_used, x1, w_gu, b_gu.reshape(n_exp, 1, d2), w_down, b_down.reshape(n_exp, 1, d))


def _combine_kernel(dest_ref, x1_ref, gate_ref, ys_hbm, g2_ref, b2_ref, op_ref, os_ref, ybuf, sem,
                    *, tm, top_k, alpha, m_total, n_first):
    i = pl.program_id(0)
    nsteps = pl.num_programs(0)
    slot = i % 2
    n_rows = top_k * tm

    def row_copy(d, s, r):
        return pltpu.make_async_copy(ys_hbm.at[pl.ds(d * SUBLANE, SUBLANE)],
                                     ybuf.at[s, pl.ds(r * SUBLANE, SUBLANE)], sem.at[s])

    def wait(s):
        pltpu.make_async_copy(ys_hbm.at[pl.ds(0, n_rows * SUBLANE)], ybuf.at[s], sem.at[s]).wait()

    @pl.when(i == 0)
    def _():
        second = jnp.minimum(1, nsteps - 1) * tm

        def body(r, carry):
            k = r // tm
            t = r - k * tm
            row_copy(dest_ref[k * m_total + t], 0, r).start()
            row_copy(dest_ref[k * m_total + second + t], 1, r).start()
            return carry
        lax.fori_loop(0, n_rows, body, 0)

    wait(slot)
    g = gate_ref[...]
    parts = []
    for j in range(SUBLANE):
        acc = g[:, 0:1] * _load_row_tiles(ybuf.at[slot], 0, tm, j)
        for k in range(1, top_k):
            acc = acc + g[:, k:k + 1] * _load_row_tiles(ybuf.at[slot], k * tm, tm, j)
        parts.append(acc)
    y = jnp.concatenate(parts, axis=1)
    nxt = jnp.minimum(i + 2, nsteps - 1) * tm
    for r in range(n_rows):
        k, t = divmod(r, tm)
        row_copy(dest_ref[k * m_total + nxt + t], slot, r).start(priority=r % 2)
    res = _layer_norm(alpha * x1_ref[...] + y, g2_ref[...], b2_ref[...])

    @pl.when(i < n_first)
    def _():
        op_ref[...] = res

    @pl.when(i >= n_first)
    def _():
        os_ref[...] = res

    @pl.when(i == nsteps - 1)
    def _():
        wait(0)
        wait(1)


def _combine(dest_km, x1, gates_mk, ys, g2, b2, *, tm, alpha, m_first):
    m, d = x1.shape
    n_first = m_first // tm
    assert m_first % tm == 0 and 0 < m_first < m
    kern = functools.partial(_combine_kernel, tm=tm, top_k=TOP_K, alpha=alpha, m_total=m, n_first=n_first)
    grid_spec = pltpu.PrefetchScalarGridSpec(
        num_scalar_prefetch=1,
        grid=(m // tm,),
        in_specs=[pl.BlockSpec((tm, d), lambda i, ds: (i, 0)),
                  pl.BlockSpec((tm, SUBLANE), lambda i, ds: (i, 0)),
                  pl.BlockSpec(memory_space=pl.ANY),
                  pl.BlockSpec((1, d), lambda i, ds: (0, 0)),
                  pl.BlockSpec((1, d), lambda i, ds: (0, 0))],
        out_specs=[pl.BlockSpec((tm, d), lambda i, ds: (jnp.minimum(i, n_first - 1), 0)),
                   pl.BlockSpec((tm, d), lambda i, ds: (jnp.maximum(i - n_first, 0), 0))],
        scratch_shapes=[pltpu.VMEM((2, TOP_K * tm * SUBLANE, LANE), F32), pltpu.SemaphoreType.DMA((2,))])
    return pl.pallas_call(
        kern,
        grid_spec=grid_spec,
        out_shape=[jax.ShapeDtypeStruct((m_first, d), F32), jax.ShapeDtypeStruct((m - m_first, d), F32)],
        compiler_params=_cparams(("arbitrary",), VMEM_LIMIT),
        name="moe_combine",
    )(dest_km, x1, gates_mk, ys, g2, b2)


def _pack_w_in(w_in, splits):
    q_lora, kv_lora, rope, conv_ch, gv, nh, _, d, _ = splits
    offs = [0]
    for s in splits:
        offs.append(offs[-1] + s)
    part = [w_in[:, offs[i]:offs[i + 1]] for i in range(len(splits))]
    q_lat, kv_lat, k_r, qkv, z, a, b, g_a, g_b = part
    dm = w_in.shape[0]
    half = rope // 2
    zpad = lambda n: jnp.zeros((dm, n), w_in.dtype)
    k_sw = jnp.concatenate([k_r[:, half:], k_r[:, :half]], axis=1)
    small = jnp.concatenate([q_lat, kv_lat, k_r, zpad(LANE - rope), k_sw, zpad(LANE - rope),
                             a, b, zpad(LANE - 2 * nh)], axis=1)
    return jnp.concatenate([qkv, small, z, g_a, g_b], axis=1).astype(BF16)


def _pack_w_uq(w_uq, nope, rope):
    w = jnp.transpose(w_uq, (1, 0, 2))
    half = rope // 2
    r = w[..., nope:]
    zp = jnp.zeros(r.shape[:-1] + (LANE - rope,), w.dtype)
    r_sw = jnp.concatenate([r[..., half:], r[..., :half]], axis=-1)
    return jnp.concatenate([w[..., :nope], r, zp, r_sw, zp], axis=-1).astype(BF16)


def _rope_tables(pos, rope):
    half = rope // 2
    inv = ROPE_THETA ** (-jnp.arange(half, dtype=F32) / half)
    ang = pos.astype(F32)[:, None] * inv[None, :]
    cos, sin = jnp.cos(ang), jnp.sin(ang)
    zp = jnp.zeros((pos.shape[0], LANE - rope), F32)
    return (jnp.concatenate([cos, cos, zp], axis=1), jnp.concatenate([-sin, sin, zp], axis=1))


def _route_meta(idx_t, rank_t, counts, m, n_exp, blk):
    a = m * TOP_K
    counts = counts.astype(jnp.int32)
    padded = (counts + blk - 1) // blk * blk
    pad_end = jnp.cumsum(padded)
    pad_start = pad_end - padded
    experts = jnp.arange(n_exp, dtype=jnp.int32)
    e_km = idx_t[:TOP_K]
    start_km = jnp.sum(jnp.where(e_km[:, :, None] == experts, pad_start, 0), axis=-1)
    dest_km = (start_km + rank_t[:TOP_K]).astype(jnp.int32).reshape(a)
    nb = a // blk + n_exp
    tok_km = jnp.tile(jnp.arange(m, dtype=jnp.int32), TOP_K)
    rows = jnp.zeros((nb * blk,), jnp.int32).at[dest_km].set(tok_km, unique_indices=True,
                                                              mode='promise_in_bounds')
    first_row = jnp.arange(nb, dtype=jnp.int32) * blk
    block_e = jnp.minimum(jnp.sum((pad_end[None, :] <= first_row[:, None]).astype(jnp.int32), axis=1),
                          n_exp - 1).astype(jnp.int32)
    n_used = (pad_end[-1] // blk).astype(jnp.int32).reshape(1)
    return block_e, rows, n_used, dest_km


def kernel(x_prompt, x_sample, cache_ckv, cache_krope, page_table, state_conv, state_ssm, w_in, q_norm_w, kv_norm_w, w_uq, w_uk, w_uv, conv_w, a_log, dt_bias, gdn_norm_w, w_o, ln1_g, ln1_b, w_router, b_router, w_gu, b_gu, w_down, b_down, ln2_g, ln2_b):
    bp, tp, d = x_prompt.shape
    bs, ts, _ = x_sample.shape
    depth = w_in.shape[0]
    q_lora, n_heads, qk = w_uq.shape[1:]
    kv_lora, _, nope = w_uk.shape[1:]
    rope = qk - nope
    vh = w_uv.shape[3]
    cw_taps, conv_ch = conv_w.shape[1:]
    g_heads = a_log.shape[1]
    dk, dv = state_ssm.shape[3:]
    n_exp = w_router.shape[2]
    page = cache_ckv.shape[2]
    past = page_table.shape[1] * page
    splits = (q_lora, kv_lora, rope, conv_ch, g_heads * dv, g_heads, g_heads, d, d)
    assert sum(splits) == w_in.shape[2]
    alpha = (2 * depth) ** 0.25
    scale = (nope + rope) ** -0.5 * LOG2E
    mp, ms = bp * tp, bs * ts
    m = mp + ms
    small_w = q_lora + kv_lora + 3 * LANE
    assert small_w == d and conv_ch % d == 0
    qkv_blk, small_blk = 0, conv_ch // d
    z_blk, ga_blk, gb_blk = small_blk + 1, small_blk + 2, small_blk + 3
    ab_blk = (conv_ch + q_lora + kv_lora + 2 * LANE) // LANE

    cos_p, sin_p = _rope_tables(jnp.arange(tp, dtype=jnp.int32), rope)
    cos_s, sin_s = _rope_tables(past + jnp.arange(ts, dtype=jnp.int32), rope)
    cos_t = jnp.concatenate([jnp.tile(cos_p, (bp, 1)), jnp.tile(cos_s, (bs, 1))], axis=0)
    sin_t = jnp.concatenate([jnp.tile(sin_p, (bp, 1)), jnp.tile(sin_s, (bs, 1))], axis=0)

    x_p, x_s = x_prompt.reshape(mp, d), x_sample.reshape(ms, d)
    outs = {k: [] for k in ("ckv_p", "kr_p", "conv_p", "ssm_p", "ckv_s", "kr_s", "conv_s", "ssm_s")}
    pad_lanes = lambda v: jnp.pad(v, (0, LANE - v.shape[0])).reshape(1, LANE)
    for l in range(depth):
        x = jnp.concatenate([x_p, x_s], axis=0)
        w_pack = _pack_w_in(w_in[l], splits)
        wq = _pack_w_uq(w_uq[l], nope, rope)
        wuk = jnp.transpose(w_uk[l], (1, 2, 0)).astype(BF16)
        wuv = jnp.transpose(w_uv[l], (1, 0, 2)).astype(BF16)
        h = _in_proj(x, w_pack, tm=_row_tile(m, 2304), tn=512)
        qf, c_all, kr_all, kt, cb = _mla_proj(
            h, small_blk, cos_t, sin_t, q_norm_w[l].reshape(1, q_lora), kv_norm_w[l].reshape(1, kv_lora),
            wq, wuk, tm=256, q_lora=q_lora, kv_lora=kv_lora, nope=nope, rope=rope, scale=scale)
        om_p = _attn_prompt(qf, kt, cb, wuv, batch=bp, seq=tp, tq=128, tk=min(1024, tp))
        qs = qf[:, mp:, :].reshape(n_heads, bs, ts, qf.shape[-1])
        qs = jnp.transpose(qs, (1, 0, 2, 3)).reshape(bs, n_heads * ts, qf.shape[-1])
        om_s = _attn_sample(page_table, qs, c_all, kr_all, mp // ts, cache_ckv[l],
                            jnp.swapaxes(cache_krope[l], 1, 2), wuv)

        alog_p = pad_lanes(a_log[l])
        dtb_p = pad_lanes(dt_bias[l])
        nw = gdn_norm_w[l].reshape(1, dv)
        gdn_kw = dict(qkv_blk=qkv_blk, ab_blk=ab_blk, z_blk=z_blk, conv_w=cw_taps)
        zeros_conv = jnp.zeros((bp, SUBLANE, conv_ch), F32)
        zeros_ssm = jnp.zeros((bp, g_heads, dk, dv), F32)
        og_p, ssm_p, ncv_p = _gdn(h, 0, zeros_conv, zeros_ssm, conv_w[l], alog_p, dtb_p, nw,
                                  batch=bp, seq=tp, chunk=min(GDN_CHUNK, tp), group=bp, **gdn_kw)
        conv8_s = jnp.pad(state_conv[l], ((0, 0), (SUBLANE - (cw_taps - 1), 0), (0, 0)))
        og_s, ssm_s, ncv_s = _gdn(h, mp, conv8_s, state_ssm[l], conv_w[l], alog_p, dtb_p, nw,
                                  batch=bs, seq=ts, chunk=ts, group=math.gcd(bs, GDN_SAMPLE_GROUP), **gdn_kw)

        x1, x1t, idx_t, gate_t, rank_t, counts = _post(
            x, om_p, om_s, og_p, og_s, h, ga_blk, gb_blk, w_o[l].astype(BF16), ln1_g[l].reshape(1, d), ln1_b[l].reshape(1, d),
            w_router[l].T, b_router[l].reshape(n_exp, 1), tm=256, alpha=alpha)
        block_e, rows, n_used, dest_km = _route_meta(idx_t, rank_t, counts[:, 0], m, n_exp, MOE_ROWS)
        ys = _moe_experts(block_e, rows, n_used, x1t, w_gu[l], b_gu[l], w_down[l], b_down[l], blk=MOE_ROWS)
        x_p, x_s = _combine(dest_km, x1, gate_t.T, ys, ln2_g[l].reshape(1, d), ln2_b[l].reshape(1, d),
                            tm=128, alpha=alpha, m_first=mp)

        outs["ckv_p"].append(c_all[:mp].reshape(bp, tp, kv_lora))
        outs["kr_p"].append(kr_all[:mp].reshape(bp, tp, rope))
        outs["conv_p"].append(ncv_p[:, SUBLANE - (cw_taps - 1):, :])
        outs["ssm_p"].append(ssm_p)
        outs["ckv_s"].append(c_all[mp:].reshape(bs, ts, kv_lora))
        outs["kr_s"].append(kr_all[mp:].reshape(bs, ts, rope))
        outs["conv_s"].append(ncv_s[:, SUBLANE - (cw_taps - 1):, :])
        outs["ssm_s"].append(ssm_s)

    return (x_p.reshape(bp, tp, d), x_s.reshape(bs, ts, d),
            jnp.stack(outs["ckv_p"]), jnp.stack(outs["kr_p"]), jnp.stack(outs["conv_p"]), jnp.stack(outs["ssm_p"]),
            jnp.stack(outs["ckv_s"]), jnp.stack(outs["kr_s"]), jnp.stack(outs["conv_s"]), jnp.stack(outs["ssm_s"]))
```

```python
import functools
import math

import jax
import jax.numpy as jnp
from jax import lax
from jax.experimental import pallas as pl
from jax.experimental.pallas import tpu as pltpu

F32 = jnp.float32
BF16 = jnp.bfloat16

ROPE_THETA = 10000.0
NORM_EPS = 1e-6
TOP_K = 4
SWIGLU_LIMIT = 7.0
SWIGLU_ALPHA = 1.702
GDN_CHUNK = 64
GDN_DIAG = 16
GDN_SAMPLE_GROUP = 8
MOE_ROWS = 512
MOE_ISSUE_CHUNKS = 8
PAGE_CHUNK = 8
LANE = 128
SUBLANE = 8
VMEM_LIMIT = 56 * 1024 * 1024
NEG = -1e30
LOG2E = 1.4426950408889634


def _cparams(sem, vmem=None):
    return pltpu.CompilerParams(dimension_semantics=sem, vmem_limit_bytes=vmem)


def _dot(a, b, **kw):
    return jnp.dot(a, b, preferred_element_type=F32, **kw)


def _dot_nt(a, b, **kw):
    return lax.dot_general(a, b, (((1,), (1,)), ((), ())), preferred_element_type=F32, **kw)


def _bdot(a, b):
    return lax.dot_general(a, b, (((2,), (1,)), ((0,), (0,))), preferred_element_type=F32)


def _bdot_nt(a, b):
    return lax.dot_general(a, b, (((2,), (2,)), ((0,), (0,))), preferred_element_type=F32)


def _bdot_tn(a, b):
    return lax.dot_general(a, b, (((1,), (1,)), ((0,), (0,))), preferred_element_type=F32)


def _sigmoid(x):
    return 1.0 / (1.0 + jnp.exp(-x))


def _layer_norm(v, g, b):
    mu = jnp.mean(v, -1, keepdims=True)
    vc = v - mu
    var = jnp.mean(vc * vc, -1, keepdims=True)
    return vc * lax.rsqrt(var + NORM_EPS) * g + b


def _inproj_kernel(x_ref, w_ref, o_ref, xb_ref):
    @pl.when(pl.program_id(1) == 0)
    def _():
        xb_ref[...] = x_ref[...].astype(BF16)

    o_ref[...] = _dot(xb_ref[...], w_ref[...])


def _row_tile(m, target):
    return max(t for t in range(SUBLANE, target + 1, SUBLANE) if m % t == 0)


def _in_proj(x, w, tm, tn):
    m, k = x.shape
    n = w.shape[1]
    return pl.pallas_call(
        _inproj_kernel,
        grid=(m // tm, n // tn),
        in_specs=[pl.BlockSpec((tm, k), lambda i, j: (i, 0)),
                  pl.BlockSpec((k, tn), lambda i, j: (0, j))],
        out_specs=pl.BlockSpec((tm, tn), lambda i, j: (i, j)),
        out_shape=jax.ShapeDtypeStruct((m, n), F32),
        scratch_shapes=[pltpu.VMEM((tm, k), BF16)],
        compiler_params=_cparams(("parallel", "arbitrary"), VMEM_LIMIT),
        name="in_proj",
    )(x, w)


def _mla_proj_kernel(h_ref, cos_ref, sin_ref, qnw_ref, kvnw_ref, wq_ref, wuk_ref,
                     qf_ref, c_ref, kr_ref, kt_ref, cb_ref, *, n_heads, q_lora, kv_lora, nope, rope, scale):
    hs = h_ref[...]
    cos = cos_ref[...]
    sin = sin_ref[...]
    q_lat = hs[:, :q_lora]
    qn = q_lat * lax.rsqrt(jnp.mean(q_lat * q_lat, -1, keepdims=True) + NORM_EPS) * qnw_ref[...]
    qn = qn.astype(BF16)
    kv = hs[:, q_lora:q_lora + kv_lora]
    c = kv * lax.rsqrt(jnp.mean(kv * kv, -1, keepdims=True) + NORM_EPS) * kvnw_ref[...]
    o = q_lora + kv_lora
    kr = hs[:, o:o + LANE] * cos + hs[:, o + LANE:o + 2 * LANE] * sin
    c_ref[...] = c
    kr_ref[...] = kr[:, :rope]
    cb = c.astype(BF16)
    cb_ref[...] = cb
    kfull = jnp.concatenate([cb, kr.astype(BF16)], axis=1)
    kw = kfull.shape[1]
    eye = (lax.broadcasted_iota(jnp.int32, (kw, kw), 0) == lax.broadcasted_iota(jnp.int32, (kw, kw), 1))
    kt_ref[...] = _dot_nt(eye.astype(BF16), kfull).astype(BF16)
    for h in range(n_heads):
        qh = _dot(qn, wq_ref[h])
        qa = _dot(qh[:, :nope].astype(BF16), wuk_ref[h])
        qr = qh[:, nope:nope + LANE] * cos + qh[:, nope + LANE:nope + 2 * LANE] * sin
        qf_ref[h, :, :kv_lora] = (qa * scale).astype(BF16)
        qf_ref[h, :, kv_lora:] = (qr * scale).astype(BF16)


def _mla_proj(h, col_blk, cos_t, sin_t, qnw, kvnw, wq, wuk, *, tm, q_lora, kv_lora, nope, rope, scale):
    m = h.shape[0]
    n_heads = wq.shape[0]
    wcol = q_lora + kv_lora + 3 * LANE
    kw = kv_lora + LANE
    kern = functools.partial(_mla_proj_kernel, n_heads=n_heads, q_lora=q_lora, kv_lora=kv_lora,
                             nope=nope, rope=rope, scale=scale)
    return pl.pallas_call(
        kern,
        grid=(m // tm,),
        in_specs=[pl.BlockSpec((tm, wcol), lambda i: (i, col_blk)),
                  pl.BlockSpec((tm, LANE), lambda i: (i, 0)),
                  pl.BlockSpec((tm, LANE), lambda i: (i, 0)),
                  pl.BlockSpec((1, q_lora), lambda i: (0, 0)),
                  pl.BlockSpec((1, kv_lora), lambda i: (0, 0)),
                  pl.BlockSpec(wq.shape, lambda i: (0, 0, 0)),
                  pl.BlockSpec(wuk.shape, lambda i: (0, 0, 0))],
        out_specs=[pl.BlockSpec((n_heads, tm, kw), lambda i: (0, i, 0)),
                   pl.BlockSpec((tm, kv_lora), lambda i: (i, 0)),
                   pl.BlockSpec((tm, rope), lambda i: (i, 0)),
                   pl.BlockSpec((kw, tm), lambda i: (0, i)),
                   pl.BlockSpec((tm, kv_lora), lambda i: (i, 0))],
        out_shape=[jax.ShapeDtypeStruct((n_heads, m, kw), BF16),
                   jax.ShapeDtypeStruct((m, kv_lora), F32),
                   jax.ShapeDtypeStruct((m, rope), F32),
                   jax.ShapeDtypeStruct((kw, m), BF16),
                   jax.ShapeDtypeStruct((m, kv_lora), BF16)],
        compiler_params=_cparams(("parallel",), VMEM_LIMIT),
        name="mla_proj",
    )(h, cos_t, sin_t, qnw, kvnw, wq, wuk)


def _softmax_step(s, v, m_ref, l_ref, acc_ref):
    m_prev = m_ref[...]
    m_new = jnp.maximum(m_prev, jnp.max(s, -1, keepdims=True))
    alpha = jnp.exp2(m_prev - m_new)
    p = jnp.exp2(s - m_new)
    l_ref[...] = alpha * l_ref[...] + jnp.sum(p, -1, keepdims=True)
    acc_ref[...] = alpha * acc_ref[...] + _dot(p.astype(BF16), v)
    m_ref[...] = m_new


def _attn_prompt_kernel(qf_ref, kt_ref, v_ref, wuv_ref, o_ref, m_ref, l_ref, acc_ref, *, n_heads, tq, tk, dv, vh, n_split):
    i = pl.program_id(1)
    m_ref[...] = jnp.full(m_ref.shape, NEG, F32)
    l_ref[...] = jnp.zeros(l_ref.shape, F32)
    acc_ref[...] = jnp.zeros(acc_ref.shape, F32)
    n_full = (i * tq) // tk
    off = i * tq - n_full * tk
    hs = n_heads // n_split
    rs = hs * tq

    def step(j, masked):
        kt = kt_ref[:, pl.ds(pl.multiple_of(j * tk, tk), tk)]
        v = v_ref[pl.ds(pl.multiple_of(j * tk, tk), tk), :]
        for g in range(n_split):
            q = qf_ref[g * hs:(g + 1) * hs].reshape(rs, qf_ref.shape[-1])
            s = _dot(q, kt)
            if masked:
                row = lax.broadcasted_iota(jnp.int32, (tq, tk), 0)
                col = lax.broadcasted_iota(jnp.int32, (tq, tk), 1)
                s = jnp.where((col <= row + off)[None], s.reshape(hs, tq, tk), NEG).reshape(rs, tk)
            sl = slice(g * rs, (g + 1) * rs)
            _softmax_step(s, v, m_ref.at[sl], l_ref.at[sl], acc_ref.at[sl])

    def body(j, carry):
        step(j, False)
        return carry

    lax.fori_loop(0, n_full, body, 0)
    step(n_full, True)
    o = acc_ref[...] / l_ref[...]
    for h in range(n_heads):
        oh = o[h * tq:(h + 1) * tq].astype(BF16)
        o_ref[:, h * vh:(h + 1) * vh] = _dot(oh, wuv_ref[h])


def _attn_prompt(qf, kt, cb, wuv, *, batch, seq, tq, tk):
    n_heads, _, kw = qf.shape
    dv, vh = wuv.shape[1], wuv.shape[2]
    nq = seq // tq
    rows = n_heads * tq
    assert seq % tk == 0 and tk % tq == 0
    kern = functools.partial(_attn_prompt_kernel, n_heads=n_heads, tq=tq, tk=tk, dv=dv, vh=vh, n_split=2)
    return pl.pallas_call(
        kern,
        grid=(batch, nq),
        in_specs=[pl.BlockSpec((n_heads, tq, kw), lambda b, i: (0, b * nq + i, 0)),
                  pl.BlockSpec((kw, seq), lambda b, i: (0, b)),
                  pl.BlockSpec((seq, dv), lambda b, i: (b, 0)),
                  pl.BlockSpec(wuv.shape, lambda b, i: (0, 0, 0))],
        out_specs=pl.BlockSpec((tq, n_heads * vh), lambda b, i: (b * nq + i, 0)),
        out_shape=jax.ShapeDtypeStruct((batch * seq, n_heads * vh), F32),
        scratch_shapes=[pltpu.VMEM((rows, 1), F32), pltpu.VMEM((rows, 1), F32),
                        pltpu.VMEM((rows, dv), F32)],
        compiler_params=_cparams(("parallel", "arbitrary"), VMEM_LIMIT),
        name="attn_prompt",
    )(qf, kt, cb, wuv)


def _attn_sample_kernel(pt_ref, q_ref, cn_ref, krn_ref, cc_hbm, cr_hbm, wuv_ref, o_ref,
                        cbuf, rbuf, sem, *, n_pages, page, n_heads, ts, dv, dr, vh):
    b = pl.program_id(0)
    nb = pl.num_programs(0)
    slot = b % 2
    rows = n_heads * ts

    def fetch(bb, s):
        for p in range(n_pages):
            pg = pt_ref[bb, p]
            pltpu.make_async_copy(cc_hbm.at[pg], cbuf.at[s, p], sem.at[0, s]).start()
            pltpu.make_async_copy(cr_hbm.at[pg], rbuf.at[s, p], sem.at[1, s]).start(priority=1)

    def wait(s):
        pltpu.make_async_copy(cc_hbm.at[pl.ds(0, n_pages)], cbuf.at[s], sem.at[0, s]).wait()
        pltpu.make_async_copy(cr_hbm.at[pl.ds(0, n_pages)], rbuf.at[s], sem.at[1, s]).wait()

    @pl.when(b == 0)
    def _():
        fetch(0, 0)

    wait(slot)
    fetch(jnp.minimum(b + 1, nb - 1), 1 - slot)

    q = q_ref[0].astype(F32)
    qc = q[:, :dv]
    qr = q[:, dv:dv + dr]
    n_chunks = n_pages // PAGE_CHUNK
    ck = PAGE_CHUNK * page
    parts = []
    for ch in range(n_chunks):
        c_ch = cbuf[slot, ch * PAGE_CHUNK:(ch + 1) * PAGE_CHUNK].reshape(ck, dv)
        r_ch = jnp.concatenate([rbuf[slot, ch * PAGE_CHUNK + u] for u in range(PAGE_CHUNK)], axis=1)
        parts.append(_dot_nt(qc, c_ch) + _dot(qr, r_ch))
    cn = jnp.concatenate([cn_ref[...], jnp.zeros((LANE - ts, dv), F32)], axis=0)
    krn = jnp.concatenate([krn_ref[...], jnp.zeros((LANE - ts, dr), F32)], axis=0)
    row = lax.broadcasted_iota(jnp.int32, (ts, LANE), 0)
    col = lax.broadcasted_iota(jnp.int32, (ts, LANE), 1)
    s_new = (_dot_nt(qc, cn) + _dot_nt(qr, krn)).reshape(n_heads, ts, LANE)
    s_new = jnp.where((col <= row)[None], s_new, NEG).reshape(rows, LANE)

    m = jnp.max(s_new, -1, keepdims=True)
    for s in parts:
        m = jnp.maximum(m, jnp.max(s, -1, keepdims=True))
    p_new = jnp.exp2(s_new - m)
    l = jnp.sum(p_new, -1, keepdims=True)
    acc = _dot(p_new, cn)
    for ch in range(n_chunks):
        p = jnp.exp2(parts[ch] - m)
        l = l + jnp.sum(p, -1, keepdims=True)
        acc = acc + _dot(p, cbuf[slot, ch * PAGE_CHUNK:(ch + 1) * PAGE_CHUNK].reshape(ck, dv))
    o = acc / l
    for h in range(n_heads):
        oh = o[h * ts:(h + 1) * ts].astype(BF16)
        o_ref[:, h * vh:(h + 1) * vh] = _dot(oh, wuv_ref[h])

    @pl.when(b == nb - 1)
    def _():
        wait(1 - slot)


def _attn_sample(page_table, qs, c_all, kr_all, row_blk_off, cache_c, cache_rt, wuv):
    bs, rows, kw = qs.shape
    n_heads, dv, vh = wuv.shape
    ts = rows // n_heads
    n_pages = page_table.shape[1]
    page = cache_c.shape[1]
    dr = cache_rt.shape[1]
    assert n_pages % PAGE_CHUNK == 0
    kern = functools.partial(_attn_sample_kernel, n_pages=n_pages, page=page, n_heads=n_heads, ts=ts,
                             dv=dv, dr=dr, vh=vh)
    grid_spec = pltpu.PrefetchScalarGridSpec(
        num_scalar_prefetch=1,
        grid=(bs,),
        in_specs=[pl.BlockSpec((1, rows, kw), lambda b, pt: (b, 0, 0)),
                  pl.BlockSpec((ts, dv), lambda b, pt: (row_blk_off + b, 0)),
                  pl.BlockSpec((ts, dr), lambda b, pt: (row_blk_off + b, 0)),
                  pl.BlockSpec(memory_space=pl.ANY),
                  pl.BlockSpec(memory_space=pl.ANY),
                  pl.BlockSpec(wuv.shape, lambda b, pt: (0, 0, 0))],
        out_specs=pl.BlockSpec((ts, n_heads * vh), lambda b, pt: (b, 0)),
        scratch_shapes=[pltpu.VMEM((2, n_pages, page, dv), F32), pltpu.VMEM((2, n_pages, dr, page), F32),
                        pltpu.SemaphoreType.DMA((2, 2))])
    return pl.pallas_call(
        kern,
        grid_spec=grid_spec,
        out_shape=jax.ShapeDtypeStruct((bs * ts, n_heads * vh), F32),
        compiler_params=_cparams(("arbitrary",), VMEM_LIMIT),
        name="attn_sample",
    )(page_table, qs, c_all, kr_all, cache_c, cache_rt, wuv)


def _neumann_inv(low, eye, steps):
    p = eye - low
    x = low
    for _ in range(steps):
        x = _bdot(x, x)
        p = p + _bdot(p, x)
    return p


def _gdn_kernel(*refs, n_in, n_seq, chunk, n_heads, dk, dv, conv_w):
    qkv_refs, ab_refs, z_refs = refs[:n_in], refs[n_in:2 * n_in], refs[2 * n_in:3 * n_in]
    (cs_ref, s0_ref, cw_ref, alog_ref, dtb_ref, nw_ref,
     o_ref, sfin_ref, ncv_ref, xp_ref, s_ref) = refs[3 * n_in:]
    n = pl.program_id(1)
    c = chunk
    per = n_seq // n_in
    n_prob = n_seq * n_heads
    hi = lax.Precision.HIGHEST

    def seq_rows(group, g):
        k = g % per
        return group[g // per][k * c:(k + 1) * c, :]

    @pl.when(n == 0)
    def _():
        s_ref[...] = s0_ref[...].reshape(n_prob, dk, dv)
        xp_ref[:, 0:SUBLANE, :] = cs_ref[...]

    r_i = lax.broadcasted_iota(jnp.int32, (c, c), 0)
    c_i = lax.broadcasted_iota(jnp.int32, (c, c), 1)
    causal = (c_i <= r_i)[None]
    strict = (c_i < r_i)[None]
    eye_c = (r_i == c_i).astype(F32)[None]
    db = min(GDN_DIAG, c)
    n_blk = c // db
    sh = int(math.log2(db))
    same_blk = (jnp.right_shift(r_i, sh) == jnp.right_shift(c_i, sh))[None]
    r_l = lax.broadcasted_iota(jnp.int32, (LANE, LANE), 0)
    c_l = lax.broadcasted_iota(jnp.int32, (LANE, LANE), 1)
    eye_l = (r_l == c_l).astype(F32)
    tril = (c_i <= r_i).astype(F32)
    cw = cw_ref[...]
    hk = n_heads * dk

    q_l, k_l, v_l, z_l, beta_l, gcol_l, grow_l, tails = [], [], [], [], [], [], [], []
    for g in range(n_seq):
        xp_ref[g, SUBLANE:SUBLANE + c, :] = seq_rows(qkv_refs, g)
        conv = xp_ref[g, pl.ds(SUBLANE - (conv_w - 1), c), :] * cw[0:1, :]
        for j in range(1, conv_w):
            conv = conv + xp_ref[g, pl.ds(SUBLANE - (conv_w - 1) + j, c), :] * cw[j:j + 1, :]
        tail = xp_ref[g, c:c + SUBLANE, :]
        xp_ref[g, 0:SUBLANE, :] = tail
        tails.append(tail)
        act = conv * _sigmoid(conv)
        ab = seq_rows(ab_refs, g)
        apb = ab + dtb_ref[...]
        softplus = jnp.maximum(apb, 0.0) + jnp.log(1.0 + jnp.exp(-jnp.abs(apb)))
        gfull = -jnp.exp(alog_ref[...]) * softplus
        betaf = _sigmoid(ab)
        gc = _dot(tril, gfull, precision=hi)
        gc_t = _dot_nt(eye_l, gc, precision=hi)
        zg = seq_rows(z_refs, g)
        for h in range(n_heads):
            q_l.append(act[:, h * dk:(h + 1) * dk])
            k_l.append(act[:, hk + h * dk:hk + (h + 1) * dk])
            v_l.append(act[:, 2 * hk + h * dv:2 * hk + (h + 1) * dv])
            z_l.append(zg[:, h * dv:(h + 1) * dv])
            beta_l.append(betaf[:, n_heads + h:n_heads + h + 1])
            gcol_l.append(gc[:, h:h + 1])
            grow_l.append(gc_t[h:h + 1, :])

    q = jnp.stack(q_l)
    k = jnp.stack(k_l)
    v = jnp.stack(v_l)
    z = jnp.stack(z_l)
    beta = jnp.stack(beta_l)
    gcol = jnp.stack(gcol_l)
    grow = jnp.stack(grow_l)
    q = q * lax.rsqrt(jnp.sum(q * q, -1, keepdims=True) + NORM_EPS) * (dk ** -0.5)
    k = k * lax.rsqrt(jnp.sum(k * k, -1, keepdims=True) + NORM_EPS)
    decay = jnp.where(causal, jnp.exp(jnp.where(causal, gcol - grow, 0.0)), 0.0)
    kb = k * beta
    lower = jnp.where(strict, _bdot_nt(kb, k) * decay, 0.0)
    attn = _bdot_nt(q, k) * decay
    l_d = jnp.where(same_blk, lower, 0.0)
    t_inv = _neumann_inv(l_d, eye_c, int(math.log2(db)) - 1)
    if n_blk > 1:
        m_inv = _neumann_inv(_bdot(t_inv, lower - l_d), eye_c, int(math.log2(n_blk)) - 1)
        t_inv = _bdot(m_inv, t_inv)
    egc = jnp.exp(gcol)
    uw = _bdot(t_inv, jnp.concatenate([v * beta, kb * egc], axis=2))
    s = s_ref[...]
    v_new = uw[:, :, :dv] - _bdot(uw[:, :, dv:], s)
    o = _bdot(q * egc, s) + _bdot(attn, v_new)
    glast = gcol[:, c - 1:c, :]
    kdec = k * jnp.exp(glast - gcol)
    s_new = s * jnp.exp(glast) + _bdot_tn(kdec, v_new)
    s_ref[...] = s_new
    on = o * lax.rsqrt(jnp.mean(o * o, -1, keepdims=True) + NORM_EPS) * nw_ref[...] * (z * _sigmoid(z))
    for g in range(n_seq):
        for h in range(n_heads):
            o_ref[g, :, h * dv:(h + 1) * dv] = on[g * n_heads + h]

    @pl.when(n == pl.num_programs(1) - 1)
    def _():
        sfin_ref[...] = s_new.reshape(n_seq, n_heads, dk, dv)
        for g in range(n_seq):
            ncv_ref[g] = tails[g]


def _gdn(h, row_off, conv_state8, ssm_state, cw, alog_p, dtb_p, nw, *, batch, seq, chunk, group,
         qkv_blk, ab_blk, z_blk, conv_w):
    _, n_heads, dk, dv = ssm_state.shape
    nc = seq // chunk
    ch = cw.shape[1]
    hd = n_heads * dv
    contiguous = nc == 1
    n_in = 1 if contiguous else group
    assert batch % group == 0 and row_off % (group * chunk) == 0
    kern = functools.partial(_gdn_kernel, n_in=n_in, n_seq=group, chunk=chunk, n_heads=n_heads,
                             dk=dk, dv=dv, conv_w=conv_w)
    if contiguous:
        rb = group * chunk
        row_maps = [lambda i, n: row_off // rb + i]
    else:
        rb = chunk
        row_maps = [(lambda i, n, g=g: row_off // rb + (i * group + g) * nc + n) for g in range(group)]

    def specs(width, col_blk):
        return [pl.BlockSpec((rb, width), lambda i, n, r=r: (r(i, n), col_blk)) for r in row_maps]

    in_specs = specs(ch, qkv_blk) + specs(LANE, ab_blk) + specs(hd, z_blk)
    in_specs += [pl.BlockSpec((group, SUBLANE, ch), lambda i, n: (i, 0, 0)),
                 pl.BlockSpec((group, n_heads, dk, dv), lambda i, n: (i, 0, 0, 0)),
                 pl.BlockSpec(cw.shape, lambda i, n: (0, 0)),
                 pl.BlockSpec((1, LANE), lambda i, n: (0, 0)),
                 pl.BlockSpec((1, LANE), lambda i, n: (0, 0)),
                 pl.BlockSpec((1, dv), lambda i, n: (0, 0))]
    o3, sfin, ncv = pl.pallas_call(
        kern,
        grid=(batch // group, nc),
        in_specs=in_specs,
        out_specs=[pl.BlockSpec((group, chunk, hd), lambda i, n: (i, n, 0)),
                   pl.BlockSpec((group, n_heads, dk, dv), lambda i, n: (i, 0, 0, 0)),
                   pl.BlockSpec((group, SUBLANE, ch), lambda i, n: (i, 0, 0))],
        out_shape=[jax.ShapeDtypeStruct((batch, seq, hd), F32),
                   jax.ShapeDtypeStruct(ssm_state.shape, F32),
                   jax.ShapeDtypeStruct((batch, SUBLANE, ch), F32)],
        scratch_shapes=[pltpu.VMEM((group, chunk + SUBLANE, ch), F32),
                        pltpu.VMEM((group * n_heads, dk, dv), F32)],
        compiler_params=_cparams(("parallel", "arbitrary"), VMEM_LIMIT),
        name="gdn",
    )(*([h] * (3 * n_in)), conv_state8, ssm_state, cw, alog_p, dtb_p, nw)
    return o3.reshape(batch * seq, hd), sfin, ncv


def _store_row_tiles(ref, val):
    rows = val.shape[0]
    for j in range(SUBLANE):
        ref[pl.ds(j, rows, stride=SUBLANE), :] = val[:, j * LANE:(j + 1) * LANE]


def _load_row_tiles(ref, start, rows, j):
    return ref[pl.ds(start * SUBLANE + j, rows, stride=SUBLANE), :]


def _post_kernel(x_ref, omp_ref, oms_ref, ogp_ref, ogs_ref, ga_ref, gb_ref, wo_ref, g1_ref, b1_ref, wr_ref, br_ref,
                 x1_ref, x1t_ref, idx_ref, gate_ref, rank_ref, cnt_ref, carry_ref, *, alpha, top_k, n_first):
    @pl.when(pl.program_id(0) == 0)
    def _():
        carry_ref[...] = jnp.zeros(carry_ref.shape, F32)

    first = pl.program_id(0) < n_first
    om = jnp.where(first, omp_ref[...], oms_ref[...])
    og = jnp.where(first, ogp_ref[...], ogs_ref[...])
    mix = _sigmoid(ga_ref[...]) * om + _sigmoid(gb_ref[...]) * og
    y = _dot(mix.astype(BF16), wo_ref[...])
    x1 = _layer_norm(alpha * x_ref[...] + y, g1_ref[...], b1_ref[...])
    x1_ref[...] = x1
    _store_row_tiles(x1t_ref, x1)
    logits = _dot_nt(wr_ref[...], x1, precision=lax.Precision.HIGHEST) + br_ref[...]
    n_exp, tm = logits.shape
    e_i = lax.broadcasted_iota(jnp.int32, (n_exp, tm), 0)
    vals, idxs = [], []
    for _ in range(top_k):
        mx = jnp.max(logits, axis=0, keepdims=True)
        ix = jnp.min(jnp.where(logits == mx, e_i, n_exp), axis=0, keepdims=True)
        vals.append(mx)
        idxs.append(ix)
        logits = jnp.where(e_i == ix, -jnp.inf, logits)
    es = [jnp.exp(v - vals[0]) for v in vals]
    tot = es[0]
    for e in es[1:]:
        tot = tot + e
    onehots = [e_i == ix for ix in idxs]
    sel = onehots[0].astype(F32)
    for oh in onehots[1:]:
        sel = sel + oh.astype(F32)
    t_r = lax.broadcasted_iota(jnp.int32, (tm, tm), 0)
    t_c = lax.broadcasted_iota(jnp.int32, (tm, tm), 1)
    before = _dot(sel.astype(BF16), (t_r < t_c).astype(BF16)) + carry_ref[...]
    carry = carry_ref[...] + jnp.sum(sel, axis=1, keepdims=True)
    carry_ref[...] = carry
    cnt_ref[...] = carry
    r_i = lax.broadcasted_iota(jnp.int32, (SUBLANE, tm), 0)
    idx_o = jnp.zeros((SUBLANE, tm), jnp.int32)
    gate_o = jnp.zeros((SUBLANE, tm), F32)
    rank_o = jnp.zeros((SUBLANE, tm), F32)
    for k in range(top_k):
        idx_o = jnp.where(r_i == k, idxs[k], idx_o)
        gate_o = jnp.where(r_i == k, es[k] / tot, gate_o)
        rank_k = jnp.sum(jnp.where(onehots[k], before, 0.0), axis=0, keepdims=True)
        rank_o = jnp.where(r_i == k, rank_k, rank_o)
    idx_ref[...] = idx_o
    gate_ref[...] = gate_o
    rank_ref[...] = rank_o.astype(jnp.int32)


def _post(x, om_p, om_s, og_p, og_s, h, ga_blk, gb_blk, wo, g1, b1, wr_t, br, *, tm, alpha):
    m, d = x.shape
    n_exp = wr_t.shape[0]
    n_first = om_p.shape[0] // tm
    assert om_p.shape[0] % tm == 0 and om_s.shape[0] % tm == 0 and d == SUBLANE * LANE
    kern = functools.partial(_post_kernel, alpha=alpha, top_k=TOP_K, n_first=n_first)
    row = lambda i: (i, 0)
    const = lambda i: (0, 0)
    first = lambda i: (jnp.minimum(i, n_first - 1), 0)
    second = lambda i: (jnp.maximum(i - n_first, 0), 0)
    return pl.pallas_call(
        kern,
        grid=(m // tm,),
        in_specs=[pl.BlockSpec((tm, d), row),
                  pl.BlockSpec((tm, d), first), pl.BlockSpec((tm, d), second),
                  pl.BlockSpec((tm, d), first), pl.BlockSpec((tm, d), second),
                  pl.BlockSpec((tm, d), lambda i: (i, ga_blk)),
                  pl.BlockSpec((tm, d), lambda i: (i, gb_blk)),
                  pl.BlockSpec(wo.shape, const), pl.BlockSpec((1, d), const), pl.BlockSpec((1, d), const),
                  pl.BlockSpec(wr_t.shape, const), pl.BlockSpec((n_exp, 1), const)],
        out_specs=[pl.BlockSpec((tm, d), row),
                   pl.BlockSpec((tm * SUBLANE, LANE), row),
                   pl.BlockSpec((SUBLANE, tm), lambda i: (0, i)),
                   pl.BlockSpec((SUBLANE, tm), lambda i: (0, i)),
                   pl.BlockSpec((SUBLANE, tm), lambda i: (0, i)),
                   pl.BlockSpec((n_exp, 1), const)],
        out_shape=[jax.ShapeDtypeStruct((m, d), F32),
                   jax.ShapeDtypeStruct((m * SUBLANE, LANE), F32),
                   jax.ShapeDtypeStruct((SUBLANE, m), jnp.int32),
                   jax.ShapeDtypeStruct((SUBLANE, m), F32),
                   jax.ShapeDtypeStruct((SUBLANE, m), jnp.int32),
                   jax.ShapeDtypeStruct((n_exp, 1), F32)],
        scratch_shapes=[pltpu.VMEM((n_exp, 1), F32)],
        compiler_params=_cparams(("arbitrary",), VMEM_LIMIT),
        name="post_mix",
    )(x, om_p, om_s, og_p, og_s, h, h, wo, g1, b1, wr_t, br)


def _moe_kernel(be_ref, rows_ref, nused_ref, x_hbm, wgu_ref, bgu_ref, wd_ref, bd_ref, o_ref,
                xbuf, sem, wgu_bf, wd_bf, xb_ref, hh_ref, *, blk, d_exp):
    i = pl.program_id(0)
    n_used = nused_ref[0]
    slot = i % 2

    def row_copy(tok, s, r):
        return pltpu.make_async_copy(x_hbm.at[pl.ds(tok * SUBLANE, SUBLANE)],
                                     xbuf.at[s, pl.ds(r * SUBLANE, SUBLANE)], sem.at[s])

    def wait(s):
        pltpu.make_async_copy(x_hbm.at[pl.ds(0, blk * SUBLANE)], xbuf.at[s], sem.at[s]).wait()

    @pl.when(jnp.logical_and(i == 0, n_used > 0))
    def _():
        second = jnp.minimum(1, n_used - 1) * blk

        def body(r, carry):
            row_copy(rows_ref[r], 0, r).start()
            row_copy(rows_ref[second + r], 1, r).start()
            return carry
        lax.fori_loop(0, blk, body, 0)

    @pl.when(i < n_used)
    def _():
        e = be_ref[i]
        e_prev = be_ref[jnp.maximum(i - 1, 0)]

        @pl.when(jnp.logical_or(i == 0, e != e_prev))
        def _():
            wgu_bf[...] = wgu_ref[0].astype(BF16)
            wd_bf[...] = wd_ref[0].astype(BF16)

        wait(slot)
        for j in range(SUBLANE):
            xb_ref[:, j * LANE:(j + 1) * LANE] = _load_row_tiles(xbuf.at[slot], 0, blk, j).astype(BF16)
        nxt = jnp.minimum(i + 2, n_used - 1) * blk
        d2 = wgu_bf.shape[1]
        cw = d2 // MOE_ISSUE_CHUNKS
        rpc = blk // MOE_ISSUE_CHUNKS

        def chunk(c, carry):
            base = c * rpc
            for r in range(rpc):
                row_copy(rows_ref[nxt + base + r], slot, base + r).start(priority=r % 2)
            col = pl.multiple_of(c * cw, cw)
            hh_ref[:, pl.ds(col, cw)] = (_dot(xb_ref[...], wgu_bf[:, pl.ds(col, cw)])
                                         + bgu_ref[0, :, pl.ds(col, cw)])
            return carry

        lax.fori_loop(0, MOE_ISSUE_CHUNKS, chunk, 0)
        gate = jnp.minimum(hh_ref[:, :d_exp], SWIGLU_LIMIT)
        up = jnp.clip(hh_ref[:, d_exp:], -SWIGLU_LIMIT, SWIGLU_LIMIT)
        act = (up + 1.0) * gate * _sigmoid(SWIGLU_ALPHA * gate)
        _store_row_tiles(o_ref, _dot(act.astype(BF16), wd_bf[...]) + bd_ref[0])

        @pl.when(i == n_used - 1)
        def _():
            wait(0)
            wait(1)

    @pl.when(i >= n_used)
    def _():
        o_ref[...] = jnp.zeros(o_ref.shape, F32)


def _moe_experts(block_e, rows, n_used, x1, w_gu, b_gu, w_down, b_down, *, blk):
    n_exp, d, d2 = w_gu.shape
    d_exp = d2 // 2
    nb = block_e.shape[0]
    kern = functools.partial(_moe_kernel, blk=blk, d_exp=d_exp)
    grid_spec = pltpu.PrefetchScalarGridSpec(
        num_scalar_prefetch=3,
        grid=(nb,),
        in_specs=[pl.BlockSpec(memory_space=pl.ANY),
                  pl.BlockSpec((1, d, d2), lambda i, be, rw, nu: (be[i], 0, 0)),
                  pl.BlockSpec((1, 1, d2), lambda i, be, rw, nu: (be[i], 0, 0)),
                  pl.BlockSpec((1, d_exp, d), lambda i, be, rw, nu: (be[i], 0, 0)),
                  pl.BlockSpec((1, 1, d), lambda i, be, rw, nu: (be[i], 0, 0))],
        out_specs=pl.BlockSpec((blk * SUBLANE, LANE), lambda i, be, rw, nu: (i, 0)),
        scratch_shapes=[pltpu.VMEM((2, blk * SUBLANE, LANE), F32), pltpu.SemaphoreType.DMA((2,)),
                        pltpu.VMEM((d, d2), BF16), pltpu.VMEM((d_exp, d), BF16),
                        pltpu.VMEM((blk, d), BF16), pltpu.VMEM((blk, d2), F32)])
    return pl.pallas_call(
        kern,
        grid_spec=grid_spec,
        out_shape=jax.ShapeDtypeStruct((nb * blk * SUBLANE, LANE), F32),
        compiler_params=_cparams(("arbitrary",), VMEM_LIMIT),
        name="moe_experts",
    )(block_e, rows, n_used, x1, w_gu, b_gu.reshape(n_exp, 1, d2), w_down, b_down.reshape(n_exp, 1, d))


def _combine_kernel(dest_ref, x1_ref, gate_ref, ys_hbm, g2_ref, b2_ref, op_ref, os_ref, ybuf, sem,
                    *, tm, top_k, alpha, m_total, n_first):
    i = pl.program_id(0)
    nsteps = pl.num_programs(0)
    slot = i % 2
    n_rows = top_k * tm

    def row_copy(d, s, r):
        return pltpu.make_async_copy(ys_hbm.at[pl.ds(d * SUBLANE, SUBLANE)],
                                     ybuf.at[s, pl.ds(r * SUBLANE, SUBLANE)], sem.at[s])

    def wait(s):
        pltpu.make_async_copy(ys_hbm.at[pl.ds(0, n_rows * SUBLANE)], ybuf.at[s], sem.at[s]).wait()

    @pl.when(i == 0)
    def _():
        second = jnp.minimum(1, nsteps - 1) * tm

        def body(r, carry):
            k = r // tm
            t = r - k * tm
            row_copy(dest_ref[k * m_total + t], 0, r).start()
            row_copy(dest_ref[k * m_total + second + t], 1, r).start()
            return carry
        lax.fori_loop(0, n_rows, body, 0)

    wait(slot)
    g = gate_ref[...]
    parts = []
    for j in range(SUBLANE):
        acc = g[:, 0:1] * _load_row_tiles(ybuf.at[slot], 0, tm, j)
        for k in range(1, top_k):
            acc = acc + g[:, k:k + 1] * _load_row_tiles(ybuf.at[slot], k * tm, tm, j)
        parts.append(acc)
    y = jnp.concatenate(parts, axis=1)
    nxt = jnp.minimum(i + 2, nsteps - 1) * tm
    for r in range(n_rows):
        k, t = divmod(r, tm)
        row_copy(dest_ref[k * m_total + nxt + t], slot, r).start(priority=r % 2)
    res = _layer_norm(alpha * x1_ref[...] + y, g2_ref[...], b2_ref[...])

    @pl.when(i < n_first)
    def _():
        op_ref[...] = res

    @pl.when(i >= n_first)
    def _():
        os_ref[...] = res

    @pl.when(i == nsteps - 1)
    def _():
        wait(0)
        wait(1)


def _combine(dest_km, x1, gates_mk, ys, g2, b2, *, tm, alpha, m_first):
    m, d = x1.shape
    n_first = m_first // tm
    assert m_first % tm == 0 and 0 < m_first < m
    kern = functools.partial(_combine_kernel, tm=tm, top_k=TOP_K, alpha=alpha, m_total=m, n_first=n_first)
    grid_spec = pltpu.PrefetchScalarGridSpec(
        num_scalar_prefetch=1,
        grid=(m // tm,),
        in_specs=[pl.BlockSpec((tm, d), lambda i, ds: (i, 0)),
                  pl.BlockSpec((tm, SUBLANE), lambda i, ds: (i, 0)),
                  pl.BlockSpec(memory_space=pl.ANY),
                  pl.BlockSpec((1, d), lambda i, ds: (0, 0)),
                  pl.BlockSpec((1, d), lambda i, ds: (0, 0))],
        out_specs=[pl.BlockSpec((tm, d), lambda i, ds: (jnp.minimum(i, n_first - 1), 0)),
                   pl.BlockSpec((tm, d), lambda i, ds: (jnp.maximum(i - n_first, 0), 0))],
        scratch_shapes=[pltpu.VMEM((2, TOP_K * tm * SUBLANE, LANE), F32), pltpu.SemaphoreType.DMA((2,))])
    return pl.pallas_call(
        kern,
        grid_spec=grid_spec,
        out_shape=[jax.ShapeDtypeStruct((m_first, d), F32), jax.ShapeDtypeStruct((m - m_first, d), F32)],
        compiler_params=_cparams(("arbitrary",), VMEM_LIMIT),
        name="moe_combine",
    )(dest_km, x1, gates_mk, ys, g2, b2)


def _pack_w_in(w_in, splits):
    q_lora, kv_lora, rope, conv_ch, gv, nh, _, d, _ = splits
    offs = [0]
    for s in splits:
        offs.append(offs[-1] + s)
    part = [w_in[:, offs[i]:offs[i + 1]] for i in range(len(splits))]
    q_lat, kv_lat, k_r, qkv, z, a, b, g_a, g_b = part
    dm = w_in.shape[0]
    half = rope // 2
    zpad = lambda n: jnp.zeros((dm, n), w_in.dtype)
    k_sw = jnp.concatenate([k_r[:, half:], k_r[:, :half]], axis=1)
    small = jnp.concatenate([q_lat, kv_lat, k_r, zpad(LANE - rope), k_sw, zpad(LANE - rope),
                             a, b, zpad(LANE - 2 * nh)], axis=1)
    return jnp.concatenate([qkv, small, z, g_a, g_b], axis=1).astype(BF16)


def _pack_w_uq(w_uq, nope, rope):
    w = jnp.transpose(w_uq, (1, 0, 2))
    half = rope // 2
    r = w[..., nope:]
    zp = jnp.zeros(r.shape[:-1] + (LANE - rope,), w.dtype)
    r_sw = jnp.concatenate([r[..., half:], r[..., :half]], axis=-1)
    return jnp.concatenate([w[..., :nope], r, zp, r_sw, zp], axis=-1).astype(BF16)


def _rope_tables(pos, rope):
    half = rope // 2
    inv = ROPE_THETA ** (-jnp.arange(half, dtype=F32) / half)
    ang = pos.astype(F32)[:, None] * inv[None, :]
    cos, sin = jnp.cos(ang), jnp.sin(ang)
    zp = jnp.zeros((pos.shape[0], LANE - rope), F32)
    return (jnp.concatenate([cos, cos, zp], axis=1), jnp.concatenate([-sin, sin, zp], axis=1))


def _route_meta(idx_t, rank_t, counts, m, n_exp, blk):
    a = m * TOP_K
    counts = counts.astype(jnp.int32)
    padded = (counts + blk - 1) // blk * blk
    pad_end = jnp.cumsum(padded)
    pad_start = pad_end - padded
    experts = jnp.arange(n_exp, dtype=jnp.int32)
    e_km = idx_t[:TOP_K]
    start_km = jnp.sum(jnp.where(e_km[:, :, None] == experts, pad_start, 0), axis=-1)
    dest_km = (start_km + rank_t[:TOP_K]).astype(jnp.int32).reshape(a)
    nb = a // blk + n_exp
    tok_km = jnp.tile(jnp.arange(m, dtype=jnp.int32), TOP_K)
    rows = jnp.zeros((nb * blk,), jnp.int32).at[dest_km].set(tok_km, unique_indices=True,
                                                              mode='promise_in_bounds')
    first_row = jnp.arange(nb, dtype=jnp.int32) * blk
    block_e = jnp.minimum(jnp.sum((pad_end[None, :] <= first_row[:, None]).astype(jnp.int32), axis=1),
                          n_exp - 1).astype(jnp.int32)
    n_used = (pad_end[-1] // blk).astype(jnp.int32).reshape(1)
    return block_e, rows, n_used, dest_km


def kernel(x_prompt, x_sample, cache_ckv, cache_krope, page_table, state_conv, state_ssm, w_in, q_norm_w, kv_norm_w, w_uq, w_uk, w_uv, conv_w, a_log, dt_bias, gdn_norm_w, w_o, ln1_g, ln1_b, w_router, b_router, w_gu, b_gu, w_down, b_down, ln2_g, ln2_b):
    bp, tp, d = x_prompt.shape
    bs, ts, _ = x_sample.shape
    depth = w_in.shape[0]
    q_lora, n_heads, qk = w_uq.shape[1:]
    kv_lora, _, nope = w_uk.shape[1:]
    rope = qk - nope
    vh = w_uv.shape[3]
    cw_taps, conv_ch = conv_w.shape[1:]
    g_heads = a_log.shape[1]
    dk, dv = state_ssm.shape[3:]
    n_exp = w_router.shape[2]
    page = cache_ckv.shape[2]
    past = page_table.shape[1] * page
    splits = (q_lora, kv_lora, rope, conv_ch, g_heads * dv, g_heads, g_heads, d, d)
    assert sum(splits) == w_in.shape[2]
    alpha = (2 * depth) ** 0.25
    scale = (nope + rope) ** -0.5 * LOG2E
    mp, ms = bp * tp, bs * ts
    m = mp + ms
    small_w = q_lora + kv_lora + 3 * LANE
    assert small_w == d and conv_ch % d == 0
    qkv_blk, small_blk = 0, conv_ch // d
    z_blk, ga_blk, gb_blk = small_blk + 1, small_blk + 2, small_blk + 3
    ab_blk = (conv_ch + q_lora + kv_lora + 2 * LANE) // LANE

    cos_p, sin_p = _rope_tables(jnp.arange(tp, dtype=jnp.int32), rope)
    cos_s, sin_s = _rope_tables(past + jnp.arange(ts, dtype=jnp.int32), rope)
    cos_t = jnp.concatenate([jnp.tile(cos_p, (bp, 1)), jnp.tile(cos_s, (bs, 1))], axis=0)
    sin_t = jnp.concatenate([jnp.tile(sin_p, (bp, 1)), jnp.tile(sin_s, (bs, 1))], axis=0)

    x_p, x_s = x_prompt.reshape(mp, d), x_sample.reshape(ms, d)
    outs = {k: [] for k in ("ckv_p", "kr_p", "conv_p", "ssm_p", "ckv_s", "kr_s", "conv_s", "ssm_s")}
    pad_lanes = lambda v: jnp.pad(v, (0, LANE - v.shape[0])).reshape(1, LANE)
    for l in range(depth):
        x = jnp.concatenate([x_p, x_s], axis=0)
        w_pack = _pack_w_in(w_in[l], splits)
        wq = _pack_w_uq(w_uq[l], nope, rope)
        wuk = jnp.transpose(w_uk[l], (1, 2, 0)).astype(BF16)
        wuv = jnp.transpose(w_uv[l], (1, 0, 2)).astype(BF16)
        h = _in_proj(x, w_pack, tm=_row_tile(m, 2304), tn=512)
        qf, c_all, kr_all, kt, cb = _mla_proj(
            h, small_blk, cos_t, sin_t, q_norm_w[l].reshape(1, q_lora), kv_norm_w[l].reshape(1, kv_lora),
            wq, wuk, tm=256, q_lora=q_lora, kv_lora=kv_lora, nope=nope, rope=rope, scale=scale)
        om_p = _attn_prompt(qf, kt, cb, wuv, batch=bp, seq=tp, tq=128, tk=min(1024, tp))
        qs = qf[:, mp:, :].reshape(n_heads, bs, ts, qf.shape[-1])
        qs = jnp.transpose(qs, (1, 0, 2, 3)).reshape(bs, n_heads * ts, qf.shape[-1])
        om_s = _attn_sample(page_table, qs, c_all, kr_all, mp // ts, cache_ckv[l],
                            jnp.swapaxes(cache_krope[l], 1, 2), wuv)

        alog_p = pad_lanes(a_log[l])
        dtb_p = pad_lanes(dt_bias[l])
        nw = gdn_norm_w[l].reshape(1, dv)
        gdn_kw = dict(qkv_blk=qkv_blk, ab_blk=ab_blk, z_blk=z_blk, conv_w=cw_taps)
        zeros_conv = jnp.zeros((bp, SUBLANE, conv_ch), F32)
        zeros_ssm = jnp.zeros((bp, g_heads, dk, dv), F32)
        og_p, ssm_p, ncv_p = _gdn(h, 0, zeros_conv, zeros_ssm, conv_w[l], alog_p, dtb_p, nw,
                                  batch=bp, seq=tp, chunk=min(GDN_CHUNK, tp), group=bp, **gdn_kw)
        conv8_s = jnp.pad(state_conv[l], ((0, 0), (SUBLANE - (cw_taps - 1), 0), (0, 0)))
        og_s, ssm_s, ncv_s = _gdn(h, mp, conv8_s, state_ssm[l], conv_w[l], alog_p, dtb_p, nw,
                                  batch=bs, seq=ts, chunk=ts, group=math.gcd(bs, GDN_SAMPLE_GROUP), **gdn_kw)

        x1, x1t, idx_t, gate_t, rank_t, counts = _post(
            x, om_p, om_s, og_p, og_s, h, ga_blk, gb_blk, w_o[l].astype(BF16), ln1_g[l].reshape(1, d), ln1_b[l].reshape(1, d),
            w_router[l].T, b_router[l].reshape(n_exp, 1), tm=256, alpha=alpha)
        block_e, rows, n_used, dest_km = _route_meta(idx_t, rank_t, counts[:, 0], m, n_exp, MOE_ROWS)
        ys = _moe_experts(block_e, rows, n_used, x1t, w_gu[l], b_gu[l], w_down[l], b_down[l], blk=MOE_ROWS)
        x_p, x_s = _combine(dest_km, x1, gate_t.T, ys, ln2_g[l].reshape(1, d), ln2_b[l].reshape(1, d),
                            tm=128, alpha=alpha, m_first=mp)

        outs["ckv_p"].append(c_all[:mp].reshape(bp, tp, kv_lora))
        outs["kr_p"].append(kr_all[:mp].reshape(bp, tp, rope))
        outs["conv_p"].append(ncv_p[:, SUBLANE - (cw_taps - 1):, :])
        outs["ssm_p"].append(ssm_p)
        outs["ckv_s"].append(c_all[mp:].reshape(bs, ts, kv_lora))
        outs["kr_s"].append(kr_all[mp:].reshape(bs, ts, rope))
        outs["conv_s"].append(ncv_s[:, SUBLANE - (cw_taps - 1):, :])
        outs["ssm_s"].append(ssm_s)

    return (x_p.reshape(bp, tp, d), x_s.reshape(bs, ts, d),
            jnp.stack(outs["ckv_p"]), jnp.stack(outs["kr_p"]), jnp.stack(outs["conv_p"]), jnp.stack(outs["ssm_p"]),
            jnp.stack(outs["ckv_s"]), jnp.stack(outs["kr_s"]), jnp.stack(outs["conv_s"]), jnp.stack(outs["ssm_s"]))
```

```python
import functools
import math

import jax
import jax.numpy as jnp
from jax import lax
from jax.experimental import pallas as pl
from jax.experimental.pallas import tpu as pltpu

F32 = jnp.float32
BF16 = jnp.bfloat16

ROPE_THETA = 10000.0
NORM_EPS = 1e-6
TOP_K = 4
SWIGLU_LIMIT = 7.0
SWIGLU_ALPHA = 1.702
GDN_CHUNK = 64
GDN_DIAG = 16
GDN_SAMPLE_GROUP = 8
MOE_ROWS = 512
PAGE_CHUNK = 8
DIAG_VARIANTS = 4
LANE = 128
SUBLANE = 8
VMEM_LIMIT = 56 * 1024 * 1024
NEG = -1e30
LOG2E = 1.4426950408889634


def _cparams(sem, vmem=None):
    return pltpu.CompilerParams(dimension_semantics=sem, vmem_limit_bytes=vmem)


def _dot(a, b, **kw):
    return jnp.dot(a, b, preferred_element_type=F32, **kw)


def _dot_nt(a, b, **kw):
    return lax.dot_general(a, b, (((1,), (1,)), ((), ())), preferred_element_type=F32, **kw)


def _bdot(a, b):
    return lax.dot_general(a, b, (((2,), (1,)), ((0,), (0,))), preferred_element_type=F32)


def _bdot_nt(a, b):
    return lax.dot_general(a, b, (((2,), (2,)), ((0,), (0,))), preferred_element_type=F32)


def _bdot_tn(a, b):
    return lax.dot_general(a, b, (((1,), (1,)), ((0,), (0,))), preferred_element_type=F32)


def _sigmoid(x):
    return 1.0 / (1.0 + jnp.exp(-x))


def _layer_norm(v, g, b):
    mu = jnp.mean(v, -1, keepdims=True)
    vc = v - mu
    var = jnp.mean(vc * vc, -1, keepdims=True)
    return vc * lax.rsqrt(var + NORM_EPS) * g + b


def _inproj_kernel(x_ref, w_ref, o_ref, xb_ref):
    @pl.when(pl.program_id(1) == 0)
    def _():
        xb_ref[...] = x_ref[...].astype(BF16)

    o_ref[...] = _dot(xb_ref[...], w_ref[...])


def _row_tile(m, target):
    return max(t for t in range(SUBLANE, target + 1, SUBLANE) if m % t == 0)


def _in_proj(x, w, tm, tn):
    m, k = x.shape
    n = w.shape[1]
    return pl.pallas_call(
        _inproj_kernel,
        grid=(m // tm, n // tn),
        in_specs=[pl.BlockSpec((tm, k), lambda i, j: (i, 0)),
                  pl.BlockSpec((k, tn), lambda i, j: (0, j))],
        out_specs=pl.BlockSpec((tm, tn), lambda i, j: (i, j)),
        out_shape=jax.ShapeDtypeStruct((m, n), F32),
        scratch_shapes=[pltpu.VMEM((tm, k), BF16)],
        compiler_params=_cparams(("parallel", "arbitrary"), VMEM_LIMIT),
        name="in_proj",
    )(x, w)


def _mla_proj_kernel(h_ref, cos_ref, sin_ref, qnw_ref, kvnw_ref, wq_ref, wuk_ref,
                     qf_ref, c_ref, kr_ref, kt_ref, cb_ref, *, n_heads, q_lora, kv_lora, nope, rope, scale):
    hs = h_ref[...]
    cos = cos_ref[...]
    sin = sin_ref[...]
    q_lat = hs[:, :q_lora]
    qn = q_lat * lax.rsqrt(jnp.mean(q_lat * q_lat, -1, keepdims=True) + NORM_EPS) * qnw_ref[...]
    qn = qn.astype(BF16)
    kv = hs[:, q_lora:q_lora + kv_lora]
    c = kv * lax.rsqrt(jnp.mean(kv * kv, -1, keepdims=True) + NORM_EPS) * kvnw_ref[...]
    o = q_lora + kv_lora
    kr = hs[:, o:o + LANE] * cos + hs[:, o + LANE:o + 2 * LANE] * sin
    c_ref[...] = c
    kr_ref[...] = kr[:, :rope]
    cb = c.astype(BF16)
    cb_ref[...] = cb
    kfull = jnp.concatenate([cb, kr.astype(BF16)], axis=1)
    kw = kfull.shape[1]
    eye = (lax.broadcasted_iota(jnp.int32, (kw, kw), 0) == lax.broadcasted_iota(jnp.int32, (kw, kw), 1))
    kt_ref[...] = _dot_nt(eye.astype(BF16), kfull).astype(BF16)
    for h in range(n_heads):
        qh = _dot(qn, wq_ref[h])
        qa = _dot(qh[:, :nope].astype(BF16), wuk_ref[h])
        qr = qh[:, nope:nope + LANE] * cos + qh[:, nope + LANE:nope + 2 * LANE] * sin
        qf_ref[h, :, :kv_lora] = (qa * scale).astype(BF16)
        qf_ref[h, :, kv_lora:] = (qr * scale).astype(BF16)


def _mla_proj(h, col_blk, cos_t, sin_t, qnw, kvnw, wq, wuk, *, tm, q_lora, kv_lora, nope, rope, scale):
    m = h.shape[0]
    n_heads = wq.shape[0]
    wcol = q_lora + kv_lora + 3 * LANE
    kw = kv_lora + LANE
    kern = functools.partial(_mla_proj_kernel, n_heads=n_heads, q_lora=q_lora, kv_lora=kv_lora,
                             nope=nope, rope=rope, scale=scale)
    return pl.pallas_call(
        kern,
        grid=(m // tm,),
        in_specs=[pl.BlockSpec((tm, wcol), lambda i: (i, col_blk)),
                  pl.BlockSpec((tm, LANE), lambda i: (i, 0)),
                  pl.BlockSpec((tm, LANE), lambda i: (i, 0)),
                  pl.BlockSpec((1, q_lora), lambda i: (0, 0)),
                  pl.BlockSpec((1, kv_lora), lambda i: (0, 0)),
                  pl.BlockSpec(wq.shape, lambda i: (0, 0, 0)),
                  pl.BlockSpec(wuk.shape, lambda i: (0, 0, 0))],
        out_specs=[pl.BlockSpec((n_heads, tm, kw), lambda i: (0, i, 0)),
                   pl.BlockSpec((tm, kv_lora), lambda i: (i, 0)),
                   pl.BlockSpec((tm, rope), lambda i: (i, 0)),
                   pl.BlockSpec((kw, tm), lambda i: (0, i)),
                   pl.BlockSpec((tm, kv_lora), lambda i: (i, 0))],
        out_shape=[jax.ShapeDtypeStruct((n_heads, m, kw), BF16),
                   jax.ShapeDtypeStruct((m, kv_lora), F32),
                   jax.ShapeDtypeStruct((m, rope), F32),
                   jax.ShapeDtypeStruct((kw, m), BF16),
                   jax.ShapeDtypeStruct((m, kv_lora), BF16)],
        compiler_params=_cparams(("parallel",), VMEM_LIMIT),
        name="mla_proj",
    )(h, cos_t, sin_t, qnw, kvnw, wq, wuk)


def _softmax_step(s, v, m_ref, l_ref, acc_ref):
    m_prev = m_ref[...]
    m_new = jnp.maximum(m_prev, jnp.max(s, -1, keepdims=True))
    alpha = jnp.exp2(m_prev - m_new)
    p = jnp.exp2(s - m_new)
    l_ref[...] = alpha * l_ref[...] + jnp.sum(p, -1, keepdims=True)
    acc_ref[...] = alpha * acc_ref[...] + _dot(p.astype(BF16), v)
    m_ref[...] = m_new


def _attn_prompt_kernel(qf_ref, kt_ref, v_ref, wuv_ref, o_ref, m_ref, l_ref, acc_ref, *, n_heads, tq, tk, dv, vh, n_split):
    i = pl.program_id(1)
    m_ref[...] = jnp.full(m_ref.shape, NEG, F32)
    l_ref[...] = jnp.zeros(l_ref.shape, F32)
    acc_ref[...] = jnp.zeros(acc_ref.shape, F32)
    n_full = (i * tq) // tk
    off = i * tq - n_full * tk
    hs = n_heads // n_split
    rs = hs * tq

    def step(j, width, masked):
        start = pl.multiple_of(j * tk, tk)
        kt = kt_ref[:, pl.ds(start, width)]
        v = v_ref[pl.ds(start, width), :]
        for g in range(n_split):
            q = qf_ref[g * hs:(g + 1) * hs].reshape(rs, qf_ref.shape[-1])
            s = _dot(q, kt)
            if masked:
                row = lax.broadcasted_iota(jnp.int32, (tq, width), 0)
                col = lax.broadcasted_iota(jnp.int32, (tq, width), 1)
                s = jnp.where((col <= row + off)[None], s.reshape(hs, tq, width), NEG).reshape(rs, width)
            sl = slice(g * rs, (g + 1) * rs)
            _softmax_step(s, v, m_ref.at[sl], l_ref.at[sl], acc_ref.at[sl])

    def body(j, carry):
        step(j, tk, False)
        return carry

    lax.fori_loop(0, n_full, body, 0)
    need = off + tq
    widths = [tk * (v + 1) // DIAG_VARIANTS for v in range(DIAG_VARIANTS)]
    for v, width in enumerate(widths):
        lo = widths[v - 1] if v else 0

        @pl.when(jnp.logical_and(need > lo, need <= width))
        def _(width=width):
            step(n_full, width, True)

    o = acc_ref[...] / l_ref[...]
    for h in range(n_heads):
        oh = o[h * tq:(h + 1) * tq].astype(BF16)
        o_ref[:, h * vh:(h + 1) * vh] = _dot(oh, wuv_ref[h])


def _attn_prompt(qf, kt, cb, wuv, *, batch, seq, tq, tk):
    n_heads, _, kw = qf.shape
    dv, vh = wuv.shape[1], wuv.shape[2]
    nq = seq // tq
    rows = n_heads * tq
    assert seq % tk == 0 and tk % tq == 0 and tk % (DIAG_VARIANTS * LANE) == 0
    kern = functools.partial(_attn_prompt_kernel, n_heads=n_heads, tq=tq, tk=tk, dv=dv, vh=vh, n_split=2)
    return pl.pallas_call(
        kern,
        grid=(batch, nq),
        in_specs=[pl.BlockSpec((n_heads, tq, kw), lambda b, i: (0, b * nq + i, 0)),
                  pl.BlockSpec((kw, seq), lambda b, i: (0, b)),
                  pl.BlockSpec((seq, dv), lambda b, i: (b, 0)),
                  pl.BlockSpec(wuv.shape, lambda b, i: (0, 0, 0))],
        out_specs=pl.BlockSpec((tq, n_heads * vh), lambda b, i: (b * nq + i, 0)),
        out_shape=jax.ShapeDtypeStruct((batch * seq, n_heads * vh), F32),
        scratch_shapes=[pltpu.VMEM((rows, 1), F32), pltpu.VMEM((rows, 1), F32),
                        pltpu.VMEM((rows, dv), F32)],
        compiler_params=_cparams(("parallel", "arbitrary"), VMEM_LIMIT),
        name="attn_prompt",
    )(qf, kt, cb, wuv)


def _attn_sample_kernel(pt_ref, q_ref, cn_ref, krn_ref, cc_hbm, cr_hbm, wuv_ref, o_ref,
                        cbuf, rbuf, sem, *, n_pages, page, n_heads, ts, dv, dr, vh):
    b = pl.program_id(0)
    nb = pl.num_programs(0)
    slot = b % 2
    rows = n_heads * ts

    def fetch(bb, s):
        for p in range(n_pages):
            pg = pt_ref[bb, p]
            pltpu.make_async_copy(cc_hbm.at[pg], cbuf.at[s, p], sem.at[0, s]).start()
            pltpu.make_async_copy(cr_hbm.at[pg], rbuf.at[s, p], sem.at[1, s]).start(priority=1)

    def wait(s):
        pltpu.make_async_copy(cc_hbm.at[pl.ds(0, n_pages)], cbuf.at[s], sem.at[0, s]).wait()
        pltpu.make_async_copy(cr_hbm.at[pl.ds(0, n_pages)], rbuf.at[s], sem.at[1, s]).wait()

    @pl.when(b == 0)
    def _():
        fetch(0, 0)

    wait(slot)
    fetch(jnp.minimum(b + 1, nb - 1), 1 - slot)

    q = q_ref[0].astype(F32)
    qc = q[:, :dv]
    qr = q[:, dv:dv + dr]
    n_chunks = n_pages // PAGE_CHUNK
    ck = PAGE_CHUNK * page
    parts = []
    for ch in range(n_chunks):
        c_ch = cbuf[slot, ch * PAGE_CHUNK:(ch + 1) * PAGE_CHUNK].reshape(ck, dv)
        r_ch = jnp.concatenate([rbuf[slot, ch * PAGE_CHUNK + u] for u in range(PAGE_CHUNK)], axis=1)
        parts.append(_dot_nt(qc, c_ch) + _dot(qr, r_ch))
    cn = jnp.concatenate([cn_ref[...], jnp.zeros((LANE - ts, dv), F32)], axis=0)
    krn = jnp.concatenate([krn_ref[...], jnp.zeros((LANE - ts, dr), F32)], axis=0)
    row = lax.broadcasted_iota(jnp.int32, (ts, LANE), 0)
    col = lax.broadcasted_iota(jnp.int32, (ts, LANE), 1)
    s_new = (_dot_nt(qc, cn) + _dot_nt(qr, krn)).reshape(n_heads, ts, LANE)
    s_new = jnp.where((col <= row)[None], s_new, NEG).reshape(rows, LANE)

    m = jnp.max(s_new, -1, keepdims=True)
    for s in parts:
        m = jnp.maximum(m, jnp.max(s, -1, keepdims=True))
    p_new = jnp.exp2(s_new - m)
    l = jnp.sum(p_new, -1, keepdims=True)
    acc = _dot(p_new, cn)
    for ch in range(n_chunks):
        p = jnp.exp2(parts[ch] - m)
        l = l + jnp.sum(p, -1, keepdims=True)
        acc = acc + _dot(p, cbuf[slot, ch * PAGE_CHUNK:(ch + 1) * PAGE_CHUNK].reshape(ck, dv))
    o = acc / l
    for h in range(n_heads):
        oh = o[h * ts:(h + 1) * ts].astype(BF16)
        o_ref[:, h * vh:(h + 1) * vh] = _dot(oh, wuv_ref[h])

    @pl.when(b == nb - 1)
    def _():
        wait(1 - slot)


def _attn_sample(page_table, qs, c_all, kr_all, row_blk_off, cache_c, cache_rt, wuv):
    bs, rows, kw = qs.shape
    n_heads, dv, vh = wuv.shape
    ts = rows // n_heads
    n_pages = page_table.shape[1]
    page = cache_c.shape[1]
    dr = cache_rt.shape[1]
    assert n_pages % PAGE_CHUNK == 0
    kern = functools.partial(_attn_sample_kernel, n_pages=n_pages, page=page, n_heads=n_heads, ts=ts,
                             dv=dv, dr=dr, vh=vh)
    grid_spec = pltpu.PrefetchScalarGridSpec(
        num_scalar_prefetch=1,
        grid=(bs,),
        in_specs=[pl.BlockSpec((1, rows, kw), lambda b, pt: (b, 0, 0)),
                  pl.BlockSpec((ts, dv), lambda b, pt: (row_blk_off + b, 0)),
                  pl.BlockSpec((ts, dr), lambda b, pt: (row_blk_off + b, 0)),
                  pl.BlockSpec(memory_space=pl.ANY),
                  pl.BlockSpec(memory_space=pl.ANY),
                  pl.BlockSpec(wuv.shape, lambda b, pt: (0, 0, 0))],
        out_specs=pl.BlockSpec((ts, n_heads * vh), lambda b, pt: (b, 0)),
        scratch_shapes=[pltpu.VMEM((2, n_pages, page, dv), F32), pltpu.VMEM((2, n_pages, dr, page), F32),
                        pltpu.SemaphoreType.DMA((2, 2))])
    return pl.pallas_call(
        kern,
        grid_spec=grid_spec,
        out_shape=jax.ShapeDtypeStruct((bs * ts, n_heads * vh), F32),
        compiler_params=_cparams(("arbitrary",), VMEM_LIMIT),
        name="attn_sample",
    )(page_table, qs, c_all, kr_all, cache_c, cache_rt, wuv)


def _neumann_inv(low, eye, steps):
    p = eye - low
    x = low
    for _ in range(steps):
        x = _bdot(x, x)
        p = p + _bdot(p, x)
    return p


def _gdn_kernel(*refs, n_in, n_seq, chunk, n_heads, dk, dv, conv_w):
    qkv_refs, ab_refs, z_refs = refs[:n_in], refs[n_in:2 * n_in], refs[2 * n_in:3 * n_in]
    (cs_ref, s0_ref, cw_ref, alog_ref, dtb_ref, nw_ref,
     o_ref, sfin_ref, ncv_ref, xp_ref, s_ref) = refs[3 * n_in:]
    n = pl.program_id(1)
    c = chunk
    per = n_seq // n_in
    n_prob = n_seq * n_heads
    hi = lax.Precision.HIGHEST

    def seq_rows(group, g):
        k = g % per
        return group[g // per][k * c:(k + 1) * c, :]

    @pl.when(n == 0)
    def _():
        s_ref[...] = s0_ref[...].reshape(n_prob, dk, dv)
        xp_ref[:, 0:SUBLANE, :] = cs_ref[...]

    r_i = lax.broadcasted_iota(jnp.int32, (c, c), 0)
    c_i = lax.broadcasted_iota(jnp.int32, (c, c), 1)
    causal = (c_i <= r_i)[None]
    strict = (c_i < r_i)[None]
    eye_c = (r_i == c_i).astype(F32)[None]
    db = min(GDN_DIAG, c)
    n_blk = c // db
    sh = int(math.log2(db))
    same_blk = (jnp.right_shift(r_i, sh) == jnp.right_shift(c_i, sh))[None]
    r_l = lax.broadcasted_iota(jnp.int32, (LANE, LANE), 0)
    c_l = lax.broadcasted_iota(jnp.int32, (LANE, LANE), 1)
    eye_l = (r_l == c_l).astype(F32)
    tril = (c_i <= r_i).astype(F32)
    cw = cw_ref[...]
    hk = n_heads * dk

    q_l, k_l, v_l, z_l, beta_l, gcol_l, grow_l, tails = [], [], [], [], [], [], [], []
    for g in range(n_seq):
        xp_ref[g, SUBLANE:SUBLANE + c, :] = seq_rows(qkv_refs, g)
        conv = xp_ref[g, pl.ds(SUBLANE - (conv_w - 1), c), :] * cw[0:1, :]
        for j in range(1, conv_w):
            conv = conv + xp_ref[g, pl.ds(SUBLANE - (conv_w - 1) + j, c), :] * cw[j:j + 1, :]
        tail = xp_ref[g, c:c + SUBLANE, :]
        xp_ref[g, 0:SUBLANE, :] = tail
        tails.append(tail)
        act = conv * _sigmoid(conv)
        ab = seq_rows(ab_refs, g)
        apb = ab + dtb_ref[...]
        softplus = jnp.maximum(apb, 0.0) + jnp.log(1.0 + jnp.exp(-jnp.abs(apb)))
        gfull = -jnp.exp(alog_ref[...]) * softplus
        betaf = _sigmoid(ab)
        gc = _dot(tril, gfull, precision=hi)
        gc_t = _dot_nt(eye_l, gc, precision=hi)
        zg = seq_rows(z_refs, g)
        for h in range(n_heads):
            q_l.append(act[:, h * dk:(h + 1) * dk])
            k_l.append(act[:, hk + h * dk:hk + (h + 1) * dk])
            v_l.append(act[:, 2 * hk + h * dv:2 * hk + (h + 1) * dv])
            z_l.append(zg[:, h * dv:(h + 1) * dv])
            beta_l.append(betaf[:, n_heads + h:n_heads + h + 1])
            gcol_l.append(gc[:, h:h + 1])
            grow_l.append(gc_t[h:h + 1, :])

    q = jnp.stack(q_l)
    k = jnp.stack(k_l)
    v = jnp.stack(v_l)
    z = jnp.stack(z_l)
    beta = jnp.stack(beta_l)
    gcol = jnp.stack(gcol_l)
    grow = jnp.stack(grow_l)
    q = q * lax.rsqrt(jnp.sum(q * q, -1, keepdims=True) + NORM_EPS) * (dk ** -0.5)
    k = k * lax.rsqrt(jnp.sum(k * k, -1, keepdims=True) + NORM_EPS)
    decay = jnp.where(causal, jnp.exp(jnp.where(causal, gcol - grow, 0.0)), 0.0)
    kb = k * beta
    lower = jnp.where(strict, _bdot_nt(kb, k) * decay, 0.0)
    attn = _bdot_nt(q, k) * decay
    l_d = jnp.where(same_blk, lower, 0.0)
    t_inv = _neumann_inv(l_d, eye_c, int(math.log2(db)) - 1)
    if n_blk > 1:
        m_inv = _neumann_inv(_bdot(t_inv, lower - l_d), eye_c, int(math.log2(n_blk)) - 1)
        t_inv = _bdot(m_inv, t_inv)
    egc = jnp.exp(gcol)
    uw = _bdot(t_inv, jnp.concatenate([v * beta, kb * egc], axis=2))
    s = s_ref[...]
    v_new = uw[:, :, :dv] - _bdot(uw[:, :, dv:], s)
    o = _bdot(q * egc, s) + _bdot(attn, v_new)
    glast = gcol[:, c - 1:c, :]
    kdec = k * jnp.exp(glast - gcol)
    s_new = s * jnp.exp(glast) + _bdot_tn(kdec, v_new)
    s_ref[...] = s_new
    on = o * lax.rsqrt(jnp.mean(o * o, -1, keepdims=True) + NORM_EPS) * nw_ref[...] * (z * _sigmoid(z))
    for g in range(n_seq):
        for h in range(n_heads):
            o_ref[g, :, h * dv:(h + 1) * dv] = on[g * n_heads + h]

    @pl.when(n == pl.num_programs(1) - 1)
    def _():
        sfin_ref[...] = s_new.reshape(n_seq, n_heads, dk, dv)
        for g in range(n_seq):
            ncv_ref[g] = tails[g]


def _gdn(h, row_off, conv_state8, ssm_state, cw, alog_p, dtb_p, nw, *, batch, seq, chunk, group,
         qkv_blk, ab_blk, z_blk, conv_w):
    _, n_heads, dk, dv = ssm_state.shape
    nc = seq // chunk
    ch = cw.shape[1]
    hd = n_heads * dv
    contiguous = nc == 1
    n_in = 1 if contiguous else group
    assert batch % group == 0 and row_off % (group * chunk) == 0
    kern = functools.partial(_gdn_kernel, n_in=n_in, n_seq=group, chunk=chunk, n_heads=n_heads,
                             dk=dk, dv=dv, conv_w=conv_w)
    if contiguous:
        rb = group * chunk
        row_maps = [lambda i, n: row_off // rb + i]
    else:
        rb = chunk
        row_maps = [(lambda i, n, g=g: row_off // rb + (i * group + g) * nc + n) for g in range(group)]

    def specs(width, col_blk):
        return [pl.BlockSpec((rb, width), lambda i, n, r=r: (r(i, n), col_blk)) for r in row_maps]

    in_specs = specs(ch, qkv_blk) + specs(LANE, ab_blk) + specs(hd, z_blk)
    in_specs += [pl.BlockSpec((group, SUBLANE, ch), lambda i, n: (i, 0, 0)),
                 pl.BlockSpec((group, n_heads, dk, dv), lambda i, n: (i, 0, 0, 0)),
                 pl.BlockSpec(cw.shape, lambda i, n: (0, 0)),
                 pl.BlockSpec((1, LANE), lambda i, n: (0, 0)),
                 pl.BlockSpec((1, LANE), lambda i, n: (0, 0)),
                 pl.BlockSpec((1, dv), lambda i, n: (0, 0))]
    o3, sfin, ncv = pl.pallas_call(
        kern,
        grid=(batch // group, nc),
        in_specs=in_specs,
        out_specs=[pl.BlockSpec((group, chunk, hd), lambda i, n: (i, n, 0)),
                   pl.BlockSpec((group, n_heads, dk, dv), lambda i, n: (i, 0, 0, 0)),
                   pl.BlockSpec((group, SUBLANE, ch), lambda i, n: (i, 0, 0))],
        out_shape=[jax.ShapeDtypeStruct((batch, seq, hd), F32),
                   jax.ShapeDtypeStruct(ssm_state.shape, F32),
                   jax.ShapeDtypeStruct((batch, SUBLANE, ch), F32)],
        scratch_shapes=[pltpu.VMEM((group, chunk + SUBLANE, ch), F32),
                        pltpu.VMEM((group * n_heads, dk, dv), F32)],
        compiler_params=_cparams(("parallel", "arbitrary"), VMEM_LIMIT),
        name="gdn",
    )(*([h] * (3 * n_in)), conv_state8, ssm_state, cw, alog_p, dtb_p, nw)
    return o3.reshape(batch * seq, hd), sfin, ncv


def _store_row_tiles(ref, val):
    rows = val.shape[0]
    for j in range(SUBLANE):
        ref[pl.ds(j, rows, stride=SUBLANE), :] = val[:, j * LANE:(j + 1) * LANE]


def _load_row_tiles(ref, start, rows, j):
    return ref[pl.ds(start * SUBLANE + j, rows, stride=SUBLANE), :]


def _post_kernel(x_ref, omp_ref, oms_ref, ogp_ref, ogs_ref, ga_ref, gb_ref, wo_ref, g1_ref, b1_ref, wr_ref, br_ref,
                 x1_ref, x1t_ref, idx_ref, gate_ref, rank_ref, cnt_ref, carry_ref, *, alpha, top_k, n_first):
    @pl.when(pl.program_id(0) == 0)
    def _():
        carry_ref[...] = jnp.zeros(carry_ref.shape, F32)

    first = pl.program_id(0) < n_first
    om = jnp.where(first, omp_ref[...], oms_ref[...])
    og = jnp.where(first, ogp_ref[...], ogs_ref[...])
    mix = _sigmoid(ga_ref[...]) * om + _sigmoid(gb_ref[...]) * og
    y = _dot(mix.astype(BF16), wo_ref[...])
    x1 = _layer_norm(alpha * x_ref[...] + y, g1_ref[...], b1_ref[...])
    x1_ref[...] = x1
    _store_row_tiles(x1t_ref, x1)
    logits = _dot_nt(wr_ref[...], x1, precision=lax.Precision.HIGHEST) + br_ref[...]
    n_exp, tm = logits.shape
    e_i = lax.broadcasted_iota(jnp.int32, (n_exp, tm), 0)
    vals, idxs = [], []
    for _ in range(top_k):
        mx = jnp.max(logits, axis=0, keepdims=True)
        ix = jnp.min(jnp.where(logits == mx, e_i, n_exp), axis=0, keepdims=True)
        vals.append(mx)
        idxs.append(ix)
        logits = jnp.where(e_i == ix, -jnp.inf, logits)
    es = [jnp.exp(v - vals[0]) for v in vals]
    tot = es[0]
    for e in es[1:]:
        tot = tot + e
    onehots = [e_i == ix for ix in idxs]
    sel = onehots[0].astype(F32)
    for oh in onehots[1:]:
        sel = sel + oh.astype(F32)
    t_r = lax.broadcasted_iota(jnp.int32, (tm, tm), 0)
    t_c = lax.broadcasted_iota(jnp.int32, (tm, tm), 1)
    before = _dot(sel.astype(BF16), (t_r < t_c).astype(BF16)) + carry_ref[...]
    carry = carry_ref[...] + jnp.sum(sel, axis=1, keepdims=True)
    carry_ref[...] = carry
    cnt_ref[...] = carry
    r_i = lax.broadcasted_iota(jnp.int32, (SUBLANE, tm), 0)
    idx_o = jnp.zeros((SUBLANE, tm), jnp.int32)
    gate_o = jnp.zeros((SUBLANE, tm), F32)
    rank_o = jnp.zeros((SUBLANE, tm), F32)
    for k in range(top_k):
        idx_o = jnp.where(r_i == k, idxs[k], idx_o)
        gate_o = jnp.where(r_i == k, es[k] / tot, gate_o)
        rank_k = jnp.sum(jnp.where(onehots[k], before, 0.0), axis=0, keepdims=True)
        rank_o = jnp.where(r_i == k, rank_k, rank_o)
    idx_ref[...] = idx_o
    gate_ref[...] = gate_o
    rank_ref[...] = rank_o.astype(jnp.int32)


def _post(x, om_p, om_s, og_p, og_s, h, ga_blk, gb_blk, wo, g1, b1, wr_t, br, *, tm, alpha):
    m, d = x.shape
    n_exp = wr_t.shape[0]
    n_first = om_p.shape[0] // tm
    assert om_p.shape[0] % tm == 0 and om_s.shape[0] % tm == 0 and d == SUBLANE * LANE
    kern = functools.partial(_post_kernel, alpha=alpha, top_k=TOP_K, n_first=n_first)
    row = lambda i: (i, 0)
    const = lambda i: (0, 0)
    first = lambda i: (jnp.minimum(i, n_first - 1), 0)
    second = lambda i: (jnp.maximum(i - n_first, 0), 0)
    return pl.pallas_call(
        kern,
        grid=(m // tm,),
        in_specs=[pl.BlockSpec((tm, d), row),
                  pl.BlockSpec((tm, d), first), pl.BlockSpec((tm, d), second),
                  pl.BlockSpec((tm, d), first), pl.BlockSpec((tm, d), second),
                  pl.BlockSpec((tm, d), lambda i: (i, ga_blk)),
                  pl.BlockSpec((tm, d), lambda i: (i, gb_blk)),
                  pl.BlockSpec(wo.shape, const), pl.BlockSpec((1, d), const), pl.BlockSpec((1, d), const),
                  pl.BlockSpec(wr_t.shape, const), pl.BlockSpec((n_exp, 1), const)],
        out_specs=[pl.BlockSpec((tm, d), row),
                   pl.BlockSpec((tm * SUBLANE, LANE), row),
                   pl.BlockSpec((SUBLANE, tm), lambda i: (0, i)),
                   pl.BlockSpec((SUBLANE, tm), lambda i: (0, i)),
                   pl.BlockSpec((SUBLANE, tm), lambda i: (0, i)),
                   pl.BlockSpec((n_exp, 1), const)],
        out_shape=[jax.ShapeDtypeStruct((m, d), F32),
                   jax.ShapeDtypeStruct((m * SUBLANE, LANE), F32),
                   jax.ShapeDtypeStruct((SUBLANE, m), jnp.int32),
                   jax.ShapeDtypeStruct((SUBLANE, m), F32),
                   jax.ShapeDtypeStruct((SUBLANE, m), jnp.int32),
                   jax.ShapeDtypeStruct((n_exp, 1), F32)],
        scratch_shapes=[pltpu.VMEM((n_exp, 1), F32)],
        compiler_params=_cparams(("arbitrary",), VMEM_LIMIT),
        name="post_mix",
    )(x, om_p, om_s, og_p, og_s, h, h, wo, g1, b1, wr_t, br)


def _moe_kernel(be_ref, rows_ref, nused_ref, x_hbm, wgu_ref, bgu_ref, wd_ref, bd_ref, o_ref,
                xbuf, sem, wgu_bf, wd_bf, xb_ref, *, blk, d_exp):
    i = pl.program_id(0)
    n_used = nused_ref[0]
    slot = i % 2

    def row_copy(tok, s, r):
        return pltpu.make_async_copy(x_hbm.at[pl.ds(tok * SUBLANE, SUBLANE)],
                                     xbuf.at[s, pl.ds(r * SUBLANE, SUBLANE)], sem.at[s])

    def wait(s):
        pltpu.make_async_copy(x_hbm.at[pl.ds(0, blk * SUBLANE)], xbuf.at[s], sem.at[s]).wait()

    @pl.when(jnp.logical_and(i == 0, n_used > 0))
    def _():
        second = jnp.minimum(1, n_used - 1) * blk

        def body(r, carry):
            row_copy(rows_ref[r], 0, r).start()
            row_copy(rows_ref[second + r], 1, r).start()
            return carry
        lax.fori_loop(0, blk, body, 0)

    @pl.when(i < n_used)
    def _():
        e = be_ref[i]
        e_prev = be_ref[jnp.maximum(i - 1, 0)]

        @pl.when(jnp.logical_or(i == 0, e != e_prev))
        def _():
            wgu_bf[...] = wgu_ref[0].astype(BF16)
            wd_bf[...] = wd_ref[0].astype(BF16)

        wait(slot)
        for j in range(SUBLANE):
            xb_ref[:, j * LANE:(j + 1) * LANE] = _load_row_tiles(xbuf.at[slot], 0, blk, j).astype(BF16)
        nxt = jnp.minimum(i + 2, n_used - 1) * blk
        for r in range(blk):
            row_copy(rows_ref[nxt + r], slot, r).start(priority=r % 2)
        hh = _dot(xb_ref[...], wgu_bf[...]) + bgu_ref[0]
        gate = jnp.minimum(hh[:, :d_exp], SWIGLU_LIMIT)
        up = jnp.clip(hh[:, d_exp:], -SWIGLU_LIMIT, SWIGLU_LIMIT)
        act = (up + 1.0) * gate * _sigmoid(SWIGLU_ALPHA * gate)
        _store_row_tiles(o_ref, _dot(act.astype(BF16), wd_bf[...]) + bd_ref[0])

        @pl.when(i == n_used - 1)
        def _():
            wait(0)
            wait(1)

    @pl.when(i >= n_used)
    def _():
        o_ref[...] = jnp.zeros(o_ref.shape, F32)


def _moe_experts(block_e, rows, n_used, x1, w_gu, b_gu, w_down, b_down, *, blk):
    n_exp, d, d2 = w_gu.shape
    d_exp = d2 // 2
    nb = block_e.shape[0]
    kern = functools.partial(_moe_kernel, blk=blk, d_exp=d_exp)
    grid_spec = pltpu.PrefetchScalarGridSpec(
        num_scalar_prefetch=3,
        grid=(nb,),
        in_specs=[pl.BlockSpec(memory_space=pl.ANY),
                  pl.BlockSpec((1, d, d2), lambda i, be, rw, nu: (be[i], 0, 0)),
                  pl.BlockSpec((1, 1, d2), lambda i, be, rw, nu: (be[i], 0, 0)),
                  pl.BlockSpec((1, d_exp, d), lambda i, be, rw, nu: (be[i], 0, 0)),
                  pl.BlockSpec((1, 1, d), lambda i, be, rw, nu: (be[i], 0, 0))],
        out_specs=pl.BlockSpec((blk * SUBLANE, LANE), lambda i, be, rw, nu: (i, 0)),
        scratch_shapes=[pltpu.VMEM((2, blk * SUBLANE, LANE), F32), pltpu.SemaphoreType.DMA((2,)),
                        pltpu.VMEM((d, d2), BF16), pltpu.VMEM((d_exp, d), BF16),
                        pltpu.VMEM((blk, d), BF16)])
    return pl.pallas_call(
        kern,
        grid_spec=grid_spec,
        out_shape=jax.ShapeDtypeStruct((nb * blk * SUBLANE, LANE), F32),
        compiler_params=_cparams(("arbitrary",), VMEM_LIMIT),
        name="moe_experts",
    )(block_e, rows, n_used, x1, w_gu, b_gu.reshape(n_exp, 1, d2), w_down, b_down.reshape(n_exp, 1, d))


def _combine_kernel(dest_ref, x1_ref, gate_ref, ys_hbm, g2_ref, b2_ref, op_ref, os_ref, ybuf, sem,
                    *, tm, top_k, alpha, m_total, n_first):
    i = pl.program_id(0)
    nsteps = pl.num_programs(0)
    slot = i % 2
    n_rows = top_k * tm

    def row_copy(d, s, r):
        return pltpu.make_async_copy(ys_hbm.at[pl.ds(d * SUBLANE, SUBLANE)],
                                     ybuf.at[s, pl.ds(r * SUBLANE, SUBLANE)], sem.at[s])

    def wait(s):
        pltpu.make_async_copy(ys_hbm.at[pl.ds(0, n_rows * SUBLANE)], ybuf.at[s], sem.at[s]).wait()

    @pl.when(i == 0)
    def _():
        second = jnp.minimum(1, nsteps - 1) * tm

        def body(r, carry):
            k = r // tm
            t = r - k * tm
            row_copy(dest_ref[k * m_total + t], 0, r).start()
            row_copy(dest_ref[k * m_total + second + t], 1, r).start()
            return carry
        lax.fori_loop(0, n_rows, body, 0)

    wait(slot)
    g = gate_ref[...]
    parts = []
    for j in range(SUBLANE):
        acc = g[:, 0:1] * _load_row_tiles(ybuf.at[slot], 0, tm, j)
        for k in range(1, top_k):
            acc = acc + g[:, k:k + 1] * _load_row_tiles(ybuf.at[slot], k * tm, tm, j)
        parts.append(acc)
    y = jnp.concatenate(parts, axis=1)
    nxt = jnp.minimum(i + 2, nsteps - 1) * tm
    for r in range(n_rows):
        k, t = divmod(r, tm)
        row_copy(dest_ref[k * m_total + nxt + t], slot, r).start(priority=r % 2)
    res = _layer_norm(alpha * x1_ref[...] + y, g2_ref[...], b2_ref[...])

    @pl.when(i < n_first)
    def _():
        op_ref[...] = res

    @pl.when(i >= n_first)
    def _():
        os_ref[...] = res

    @pl.when(i == nsteps - 1)
    def _():
        wait(0)
        wait(1)


def _combine(dest_km, x1, gates_mk, ys, g2, b2, *, tm, alpha, m_first):
    m, d = x1.shape
    n_first = m_first // tm
    assert m_first % tm == 0 and 0 < m_first < m
    kern = functools.partial(_combine_kernel, tm=tm, top_k=TOP_K, alpha=alpha, m_total=m, n_first=n_first)
    grid_spec = pltpu.PrefetchScalarGridSpec(
        num_scalar_prefetch=1,
        grid=(m // tm,),
        in_specs=[pl.BlockSpec((tm, d), lambda i, ds: (i, 0)),
                  pl.BlockSpec((tm, SUBLANE), lambda i, ds: (i, 0)),
                  pl.BlockSpec(memory_space=pl.ANY),
                  pl.BlockSpec((1, d), lambda i, ds: (0, 0)),
                  pl.BlockSpec((1, d), lambda i, ds: (0, 0))],
        out_specs=[pl.BlockSpec((tm, d), lambda i, ds: (jnp.minimum(i, n_first - 1), 0)),
                   pl.BlockSpec((tm, d), lambda i, ds: (jnp.maximum(i - n_first, 0), 0))],
        scratch_shapes=[pltpu.VMEM((2, TOP_K * tm * SUBLANE, LANE), F32), pltpu.SemaphoreType.DMA((2,))])
    return pl.pallas_call(
        kern,
        grid_spec=grid_spec,
        out_shape=[jax.ShapeDtypeStruct((m_first, d), F32), jax.ShapeDtypeStruct((m - m_first, d), F32)],
        compiler_params=_cparams(("arbitrary",), VMEM_LIMIT),
        name="moe_combine",
    )(dest_km, x1, gates_mk, ys, g2, b2)


def _pack_w_in(w_in, splits):
    q_lora, kv_lora, rope, conv_ch, gv, nh, _, d, _ = splits
    offs = [0]
    for s in splits:
        offs.append(offs[-1] + s)
    part = [w_in[:, offs[i]:offs[i + 1]] for i in range(len(splits))]
    q_lat, kv_lat, k_r, qkv, z, a, b, g_a, g_b = part
    dm = w_in.shape[0]
    half = rope // 2
    zpad = lambda n: jnp.zeros((dm, n), w_in.dtype)
    k_sw = jnp.concatenate([k_r[:, half:], k_r[:, :half]], axis=1)
    small = jnp.concatenate([q_lat, kv_lat, k_r, zpad(LANE - rope), k_sw, zpad(LANE - rope),
                             a, b, zpad(LANE - 2 * nh)], axis=1)
    return jnp.concatenate([qkv, small, z, g_a, g_b], axis=1).astype(BF16)


def _pack_w_uq(w_uq, nope, rope):
    w = jnp.transpose(w_uq, (1, 0, 2))
    half = rope // 2
    r = w[..., nope:]
    zp = jnp.zeros(r.shape[:-1] + (LANE - rope,), w.dtype)
    r_sw = jnp.concatenate([r[..., half:], r[..., :half]], axis=-1)
    return jnp.concatenate([w[..., :nope], r, zp, r_sw, zp], axis=-1).astype(BF16)


def _rope_tables(pos, rope):
    half = rope // 2
    inv = ROPE_THETA ** (-jnp.arange(half, dtype=F32) / half)
    ang = pos.astype(F32)[:, None] * inv[None, :]
    cos, sin = jnp.cos(ang), jnp.sin(ang)
    zp = jnp.zeros((pos.shape[0], LANE - rope), F32)
    return (jnp.concatenate([cos, cos, zp], axis=1), jnp.concatenate([-sin, sin, zp], axis=1))


def _route_meta(idx_t, rank_t, counts, m, n_exp, blk):
    a = m * TOP_K
    counts = counts.astype(jnp.int32)
    padded = (counts + blk - 1) // blk * blk
    pad_end = jnp.cumsum(padded)
    pad_start = pad_end - padded
    experts = jnp.arange(n_exp, dtype=jnp.int32)
    e_km = idx_t[:TOP_K]
    start_km = jnp.sum(jnp.where(e_km[:, :, None] == experts, pad_start, 0), axis=-1)
    dest_km = (start_km + rank_t[:TOP_K]).astype(jnp.int32).reshape(a)
    nb = a // blk + n_exp
    tok_km = jnp.tile(jnp.arange(m, dtype=jnp.int32), TOP_K)
    rows = jnp.zeros((nb * blk,), jnp.int32).at[dest_km].set(tok_km, unique_indices=True,
                                                              mode='promise_in_bounds')
    first_row = jnp.arange(nb, dtype=jnp.int32) * blk
    block_e = jnp.minimum(jnp.sum((pad_end[None, :] <= first_row[:, None]).astype(jnp.int32), axis=1),
                          n_exp - 1).astype(jnp.int32)
    n_used = (pad_end[-1] // blk).astype(jnp.int32).reshape(1)
    return block_e, rows, n_used, dest_km


def kernel(x_prompt, x_sample, cache_ckv, cache_krope, page_table, state_conv, state_ssm, w_in, q_norm_w, kv_norm_w, w_uq, w_uk, w_uv, conv_w, a_log, dt_bias, gdn_norm_w, w_o, ln1_g, ln1_b, w_router, b_router, w_gu, b_gu, w_down, b_down, ln2_g, ln2_b):
    bp, tp, d = x_prompt.shape
    bs, ts, _ = x_sample.shape
    depth = w_in.shape[0]
    q_lora, n_heads, qk = w_uq.shape[1:]
    kv_lora, _, nope = w_uk.shape[1:]
    rope = qk - nope
    vh = w_uv.shape[3]
    cw_taps, conv_ch = conv_w.shape[1:]
    g_heads = a_log.shape[1]
    dk, dv = state_ssm.shape[3:]
    n_exp = w_router.shape[2]
    page = cache_ckv.shape[2]
    past = page_table.shape[1] * page
    splits = (q_lora, kv_lora, rope, conv_ch, g_heads * dv, g_heads, g_heads, d, d)
    assert sum(splits) == w_in.shape[2]
    alpha = (2 * depth) ** 0.25
    scale = (nope + rope) ** -0.5 * LOG2E
    mp, ms = bp * tp, bs * ts
    m = mp + ms
    small_w = q_lora + kv_lora + 3 * LANE
    assert small_w == d and conv_ch % d == 0
    qkv_blk, small_blk = 0, conv_ch // d
    z_blk, ga_blk, gb_blk = small_blk + 1, small_blk + 2, small_blk + 3
    ab_blk = (conv_ch + q_lora + kv_lora + 2 * LANE) // LANE

    cos_p, sin_p = _rope_tables(jnp.arange(tp, dtype=jnp.int32), rope)
    cos_s, sin_s = _rope_tables(past + jnp.arange(ts, dtype=jnp.int32), rope)
    cos_t = jnp.concatenate([jnp.tile(cos_p, (bp, 1)), jnp.tile(cos_s, (bs, 1))], axis=0)
    sin_t = jnp.concatenate([jnp.tile(sin_p, (bp, 1)), jnp.tile(sin_s, (bs, 1))], axis=0)

    x_p, x_s = x_prompt.reshape(mp, d), x_sample.reshape(ms, d)
    outs = {k: [] for k in ("ckv_p", "kr_p", "conv_p", "ssm_p", "ckv_s", "kr_s", "conv_s", "ssm_s")}
    pad_lanes = lambda v: jnp.pad(v, (0, LANE - v.shape[0])).reshape(1, LANE)
    for l in range(depth):
        x = jnp.concatenate([x_p, x_s], axis=0)
        w_pack = _pack_w_in(w_in[l], splits)
        wq = _pack_w_uq(w_uq[l], nope, rope)
        wuk = jnp.transpose(w_uk[l], (1, 2, 0)).astype(BF16)
        wuv = jnp.transpose(w_uv[l], (1, 0, 2)).astype(BF16)
        h = _in_proj(x, w_pack, tm=_row_tile(m, 2304), tn=512)
        qf, c_all, kr_all, kt, cb = _mla_proj(
            h, small_blk, cos_t, sin_t, q_norm_w[l].reshape(1, q_lora), kv_norm_w[l].reshape(1, kv_lora),
            wq, wuk, tm=256, q_lora=q_lora, kv_lora=kv_lora, nope=nope, rope=rope, scale=scale)
        om_p = _attn_prompt(qf, kt, cb, wuv, batch=bp, seq=tp, tq=128, tk=min(1024, tp))
        qs = qf[:, mp:, :].reshape(n_heads, bs, ts, qf.shape[-1])
        qs = jnp.transpose(qs, (1, 0, 2, 3)).reshape(bs, n_heads * ts, qf.shape[-1])
        om_s = _attn_sample(page_table, qs, c_all, kr_all, mp // ts, cache_ckv[l],
                            jnp.swapaxes(cache_krope[l], 1, 2), wuv)

        alog_p = pad_lanes(a_log[l])
        dtb_p = pad_lanes(dt_bias[l])
        nw = gdn_norm_w[l].reshape(1, dv)
        gdn_kw = dict(qkv_blk=qkv_blk, ab_blk=ab_blk, z_blk=z_blk, conv_w=cw_taps)
        zeros_conv = jnp.zeros((bp, SUBLANE, conv_ch), F32)
        zeros_ssm = jnp.zeros((bp, g_heads, dk, dv), F32)
        og_p, ssm_p, ncv_p = _gdn(h, 0, zeros_conv, zeros_ssm, conv_w[l], alog_p, dtb_p, nw,
                                  batch=bp, seq=tp, chunk=min(GDN_CHUNK, tp), group=bp, **gdn_kw)
        conv8_s = jnp.pad(state_conv[l], ((0, 0), (SUBLANE - (cw_taps - 1), 0), (0, 0)))
        og_s, ssm_s, ncv_s = _gdn(h, mp, conv8_s, state_ssm[l], conv_w[l], alog_p, dtb_p, nw,
                                  batch=bs, seq=ts, chunk=ts, group=math.gcd(bs, GDN_SAMPLE_GROUP), **gdn_kw)

        x1, x1t, idx_t, gate_t, rank_t, counts = _post(
            x, om_p, om_s, og_p, og_s, h, ga_blk, gb_blk, w_o[l].astype(BF16), ln1_g[l].reshape(1, d), ln1_b[l].reshape(1, d),
            w_router[l].T, b_router[l].reshape(n_exp, 1), tm=256, alpha=alpha)
        block_e, rows, n_used, dest_km = _route_meta(idx_t, rank_t, counts[:, 0], m, n_exp, MOE_ROWS)
        ys = _moe_experts(block_e, rows, n_used, x1t, w_gu[l], b_gu[l], w_down[l], b_down[l], blk=MOE_ROWS)
        x_p, x_s = _combine(dest_km, x1, gate_t.T, ys, ln2_g[l].reshape(1, d), ln2_b[l].reshape(1, d),
                            tm=128, alpha=alpha, m_first=mp)

        outs["ckv_p"].append(c_all[:mp].reshape(bp, tp, kv_lora))
        outs["kr_p"].append(kr_all[:mp].reshape(bp, tp, rope))
        outs["conv_p"].append(ncv_p[:, SUBLANE - (cw_taps - 1):, :])
        outs["ssm_p"].append(ssm_p)
        outs["ckv_s"].append(c_all[mp:].reshape(bs, ts, kv_lora))
        outs["kr_s"].append(kr_all[mp:].reshape(bs, ts, rope))
        outs["conv_s"].append(ncv_s[:, SUBLANE - (cw_taps - 1):, :])
        outs["ssm_s"].append(ssm_s)

    return (x_p.reshape(bp, tp, d), x_s.reshape(bs, ts, d),
            jnp.stack(outs["ckv_p"]), jnp.stack(outs["kr_p"]), jnp.stack(outs["conv_p"]), jnp.stack(outs["ssm_p"]),
            jnp.stack(outs["ckv_s"]), jnp.stack(outs["kr_s"]), jnp.stack(outs["conv_s"]), jnp.stack(outs["ssm_s"]))
```

```python
import functools
import math

import jax
import jax.numpy as jnp
from jax import lax
from jax.experimental import pallas as pl
from jax.experimental.pallas import tpu as pltpu

F32 = jnp.float32
BF16 = jnp.bfloat16

ROPE_THETA = 10000.0
NORM_EPS = 1e-6
TOP_K = 4
SWIGLU_LIMIT = 7.0
SWIGLU_ALPHA = 1.702
GDN_CHUNK = 64
GDN_DIAG = 16
GDN_SAMPLE_GROUP = 8
MOE_ROWS = 512
PAGE_CHUNK = 8
DIAG_VARIANTS = 4
LANE = 128
SUBLANE = 8
VMEM_LIMIT = 56 * 1024 * 1024
NEG = -1e30
LOG2E = 1.4426950408889634


def _cparams(sem, vmem=None):
    return pltpu.CompilerParams(dimension_semantics=sem, vmem_limit_bytes=vmem)


def _dot(a, b, **kw):
    return jnp.dot(a, b, preferred_element_type=F32, **kw)


def _dot_nt(a, b, **kw):
    return lax.dot_general(a, b, (((1,), (1,)), ((), ())), preferred_element_type=F32, **kw)


def _bdot(a, b):
    return lax.dot_general(a, b, (((2,), (1,)), ((0,), (0,))), preferred_element_type=F32)


def _bdot_nt(a, b):
    return lax.dot_general(a, b, (((2,), (2,)), ((0,), (0,))), preferred_element_type=F32)


def _bdot_tn(a, b):
    return lax.dot_general(a, b, (((1,), (1,)), ((0,), (0,))), preferred_element_type=F32)


def _sigmoid(x):
    return 1.0 / (1.0 + jnp.exp(-x))


def _layer_norm(v, g, b):
    mu = jnp.mean(v, -1, keepdims=True)
    vc = v - mu
    var = jnp.mean(vc * vc, -1, keepdims=True)
    return vc * lax.rsqrt(var + NORM_EPS) * g + b


def _inproj_kernel(x_ref, w_ref, o_ref, xb_ref):
    @pl.when(pl.program_id(1) == 0)
    def _():
        xb_ref[...] = x_ref[...].astype(BF16)

    o_ref[...] = _dot(xb_ref[...], w_ref[...])


def _row_tile(m, target):
    return max(t for t in range(SUBLANE, target + 1, SUBLANE) if m % t == 0)


def _in_proj(x, w, tm, tn):
    m, k = x.shape
    n = w.shape[1]
    return pl.pallas_call(
        _inproj_kernel,
        grid=(m // tm, n // tn),
        in_specs=[pl.BlockSpec((tm, k), lambda i, j: (i, 0)),
                  pl.BlockSpec((k, tn), lambda i, j: (0, j))],
        out_specs=pl.BlockSpec((tm, tn), lambda i, j: (i, j)),
        out_shape=jax.ShapeDtypeStruct((m, n), F32),
        scratch_shapes=[pltpu.VMEM((tm, k), BF16)],
        compiler_params=_cparams(("parallel", "arbitrary"), VMEM_LIMIT),
        name="in_proj",
    )(x, w)


def _mla_proj_kernel(h_ref, cos_ref, sin_ref, qnw_ref, kvnw_ref, wq_ref, wuk_ref,
                     qf_ref, c_ref, kr_ref, kt_ref, cb_ref, *, n_heads, q_lora, kv_lora, nope, rope, scale):
    hs = h_ref[...]
    cos = cos_ref[...]
    sin = sin_ref[...]
    q_lat = hs[:, :q_lora]
    qn = q_lat * lax.rsqrt(jnp.mean(q_lat * q_lat, -1, keepdims=True) + NORM_EPS) * qnw_ref[...]
    qn = qn.astype(BF16)
    kv = hs[:, q_lora:q_lora + kv_lora]
    c = kv * lax.rsqrt(jnp.mean(kv * kv, -1, keepdims=True) + NORM_EPS) * kvnw_ref[...]
    o = q_lora + kv_lora
    kr = hs[:, o:o + LANE] * cos + hs[:, o + LANE:o + 2 * LANE] * sin
    c_ref[...] = c
    kr_ref[...] = kr[:, :rope]
    cb = c.astype(BF16)
    cb_ref[...] = cb
    kfull = jnp.concatenate([cb, kr.astype(BF16)], axis=1)
    kw = kfull.shape[1]
    eye = (lax.broadcasted_iota(jnp.int32, (kw, kw), 0) == lax.broadcasted_iota(jnp.int32, (kw, kw), 1))
    kt_ref[...] = _dot_nt(eye.astype(BF16), kfull).astype(BF16)
    for h in range(n_heads):
        qh = _dot(qn, wq_ref[h])
        qa = _dot(qh[:, :nope].astype(BF16), wuk_ref[h])
        qr = qh[:, nope:nope + LANE] * cos + qh[:, nope + LANE:nope + 2 * LANE] * sin
        qf_ref[h, :, :kv_lora] = (qa * scale).astype(BF16)
        qf_ref[h, :, kv_lora:] = (qr * scale).astype(BF16)


def _mla_proj(h, col_blk, cos_t, sin_t, qnw, kvnw, wq, wuk, *, tm, q_lora, kv_lora, nope, rope, scale):
    m = h.shape[0]
    n_heads = wq.shape[0]
    wcol = q_lora + kv_lora + 3 * LANE
    kw = kv_lora + LANE
    kern = functools.partial(_mla_proj_kernel, n_heads=n_heads, q_lora=q_lora, kv_lora=kv_lora,
                             nope=nope, rope=rope, scale=scale)
    return pl.pallas_call(
        kern,
        grid=(m // tm,),
        in_specs=[pl.BlockSpec((tm, wcol), lambda i: (i, col_blk)),
                  pl.BlockSpec((tm, LANE), lambda i: (i, 0)),
                  pl.BlockSpec((tm, LANE), lambda i: (i, 0)),
                  pl.BlockSpec((1, q_lora), lambda i: (0, 0)),
                  pl.BlockSpec((1, kv_lora), lambda i: (0, 0)),
                  pl.BlockSpec(wq.shape, lambda i: (0, 0, 0)),
                  pl.BlockSpec(wuk.shape, lambda i: (0, 0, 0))],
        out_specs=[pl.BlockSpec((n_heads, tm, kw), lambda i: (0, i, 0)),
                   pl.BlockSpec((tm, kv_lora), lambda i: (i, 0)),
                   pl.BlockSpec((tm, rope), lambda i: (i, 0)),
                   pl.BlockSpec((kw, tm), lambda i: (0, i)),
                   pl.BlockSpec((tm, kv_lora), lambda i: (i, 0))],
        out_shape=[jax.ShapeDtypeStruct((n_heads, m, kw), BF16),
                   jax.ShapeDtypeStruct((m, kv_lora), F32),
                   jax.ShapeDtypeStruct((m, rope), F32),
                   jax.ShapeDtypeStruct((kw, m), BF16),
                   jax.ShapeDtypeStruct((m, kv_lora), BF16)],
        compiler_params=_cparams(("parallel",), VMEM_LIMIT),
        name="mla_proj",
    )(h, cos_t, sin_t, qnw, kvnw, wq, wuk)


def _softmax_step(s, v, m_ref, l_ref, acc_ref):
    m_prev = m_ref[...]
    m_new = jnp.maximum(m_prev, jnp.max(s, -1, keepdims=True))
    alpha = jnp.exp2(m_prev - m_new)
    p = jnp.exp2(s - m_new)
    l_ref[...] = alpha * l_ref[...] + jnp.sum(p, -1, keepdims=True)
    acc_ref[...] = alpha * acc_ref[...] + _dot(p.astype(BF16), v)
    m_ref[...] = m_new


def _attn_prompt_kernel(qf_ref, kt_ref, v_ref, wuv_ref, o_ref, m_ref, l_ref, acc_ref, *, n_heads, tq, tk, dv, vh, n_split):
    i = pl.program_id(1)
    m_ref[...] = jnp.full(m_ref.shape, NEG, F32)
    l_ref[...] = jnp.zeros(l_ref.shape, F32)
    acc_ref[...] = jnp.zeros(acc_ref.shape, F32)
    n_full = (i * tq) // tk
    off = i * tq - n_full * tk
    hs = n_heads // n_split
    rs = hs * tq

    def step(j, width, masked):
        start = pl.multiple_of(j * tk, tk)
        kt = kt_ref[:, pl.ds(start, width)]
        v = v_ref[pl.ds(start, width), :]
        for g in range(n_split):
            q = qf_ref[g * hs:(g + 1) * hs].reshape(rs, qf_ref.shape[-1])
            s = _dot(q, kt)
            if masked:
                row = lax.broadcasted_iota(jnp.int32, (tq, width), 0)
                col = lax.broadcasted_iota(jnp.int32, (tq, width), 1)
                s = jnp.where((col <= row + off)[None], s.reshape(hs, tq, width), NEG).reshape(rs, width)
            sl = slice(g * rs, (g + 1) * rs)
            _softmax_step(s, v, m_ref.at[sl], l_ref.at[sl], acc_ref.at[sl])

    def body(j, carry):
        step(j, tk, False)
        return carry

    lax.fori_loop(0, n_full, body, 0)
    need = off + tq
    widths = [tk * (v + 1) // DIAG_VARIANTS for v in range(DIAG_VARIANTS)]
    for v, width in enumerate(widths):
        lo = widths[v - 1] if v else 0

        @pl.when(jnp.logical_and(need > lo, need <= width))
        def _(width=width):
            step(n_full, width, True)

    o = acc_ref[...] / l_ref[...]
    for h in range(n_heads):
        oh = o[h * tq:(h + 1) * tq].astype(BF16)
        o_ref[:, h * vh:(h + 1) * vh] = _dot(oh, wuv_ref[h])


def _attn_prompt(qf, kt, cb, wuv, *, batch, seq, tq, tk):
    n_heads, _, kw = qf.shape
    dv, vh = wuv.shape[1], wuv.shape[2]
    nq = seq // tq
    rows = n_heads * tq
    assert seq % tk == 0 and tk % tq == 0 and tk % (DIAG_VARIANTS * LANE) == 0
    kern = functools.partial(_attn_prompt_kernel, n_heads=n_heads, tq=tq, tk=tk, dv=dv, vh=vh, n_split=2)
    return pl.pallas_call(
        kern,
        grid=(batch, nq),
        in_specs=[pl.BlockSpec((n_heads, tq, kw), lambda b, i: (0, b * nq + i, 0)),
                  pl.BlockSpec((kw, seq), lambda b, i: (0, b)),
                  pl.BlockSpec((seq, dv), lambda b, i: (b, 0)),
                  pl.BlockSpec(wuv.shape, lambda b, i: (0, 0, 0))],
        out_specs=pl.BlockSpec((tq, n_heads * vh), lambda b, i: (b * nq + i, 0)),
        out_shape=jax.ShapeDtypeStruct((batch * seq, n_heads * vh), F32),
        scratch_shapes=[pltpu.VMEM((rows, 1), F32), pltpu.VMEM((rows, 1), F32),
                        pltpu.VMEM((rows, dv), F32)],
        compiler_params=_cparams(("parallel", "arbitrary"), VMEM_LIMIT),
        name="attn_prompt",
    )(qf, kt, cb, wuv)


def _attn_sample_kernel(pt_ref, q_ref, cn_ref, krn_ref, cc_hbm, cr_hbm, wuv_ref, o_ref,
                        cbuf, rbuf, sem, *, n_pages, page, n_heads, ts, dv, dr, vh):
    b = pl.program_id(0)
    nb = pl.num_programs(0)
    slot = b % 2
    rows = n_heads * ts

    def fetch(bb, s):
        for p in range(n_pages):
            pg = pt_ref[bb, p]
            pltpu.make_async_copy(cc_hbm.at[pg], cbuf.at[s, p], sem.at[0, s]).start()
            pltpu.make_async_copy(cr_hbm.at[pg], rbuf.at[s, p], sem.at[1, s]).start(priority=1)

    def wait(s):
        pltpu.make_async_copy(cc_hbm.at[pl.ds(0, n_pages)], cbuf.at[s], sem.at[0, s]).wait()
        pltpu.make_async_copy(cr_hbm.at[pl.ds(0, n_pages)], rbuf.at[s], sem.at[1, s]).wait()

    @pl.when(b == 0)
    def _():
        fetch(0, 0)

    wait(slot)
    fetch(jnp.minimum(b + 1, nb - 1), 1 - slot)

    q = q_ref[0].astype(F32)
    qc = q[:, :dv]
    qr = q[:, dv:dv + dr]
    n_chunks = n_pages // PAGE_CHUNK
    ck = PAGE_CHUNK * page
    parts = []
    for ch in range(n_chunks):
        c_ch = cbuf[slot, ch * PAGE_CHUNK:(ch + 1) * PAGE_CHUNK].reshape(ck, dv)
        r_ch = jnp.concatenate([rbuf[slot, ch * PAGE_CHUNK + u] for u in range(PAGE_CHUNK)], axis=1)
        parts.append(_dot_nt(qc, c_ch) + _dot(qr, r_ch))
    cn = jnp.concatenate([cn_ref[...], jnp.zeros((LANE - ts, dv), F32)], axis=0)
    krn = jnp.concatenate([krn_ref[...], jnp.zeros((LANE - ts, dr), F32)], axis=0)
    row = lax.broadcasted_iota(jnp.int32, (ts, LANE), 0)
    col = lax.broadcasted_iota(jnp.int32, (ts, LANE), 1)
    s_new = (_dot_nt(qc, cn) + _dot_nt(qr, krn)).reshape(n_heads, ts, LANE)
    s_new = jnp.where((col <= row)[None], s_new, NEG).reshape(rows, LANE)

    m = jnp.max(s_new, -1, keepdims=True)
    for s in parts:
        m = jnp.maximum(m, jnp.max(s, -1, keepdims=True))
    p_new = jnp.exp2(s_new - m)
    l = jnp.sum(p_new, -1, keepdims=True)
    acc = _dot(p_new, cn)
    for ch in range(n_chunks):
        p = jnp.exp2(parts[ch] - m)
        l = l + jnp.sum(p, -1, keepdims=True)
        acc = acc + _dot(p, cbuf[slot, ch * PAGE_CHUNK:(ch + 1) * PAGE_CHUNK].reshape(ck, dv))
    o = acc / l
    for h in range(n_heads):
        oh = o[h * ts:(h + 1) * ts].astype(BF16)
        o_ref[:, h * vh:(h + 1) * vh] = _dot(oh, wuv_ref[h])

    @pl.when(b == nb - 1)
    def _():
        wait(1 - slot)


def _attn_sample(page_table, qs, c_all, kr_all, row_blk_off, cache_c, cache_rt, wuv):
    bs, rows, kw = qs.shape
    n_heads, dv, vh = wuv.shape
    ts = rows // n_heads
    n_pages = page_table.shape[1]
    page = cache_c.shape[1]
    dr = cache_rt.shape[1]
    assert n_pages % PAGE_CHUNK == 0
    kern = functools.partial(_attn_sample_kernel, n_pages=n_pages, page=page, n_heads=n_heads, ts=ts,
                             dv=dv, dr=dr, vh=vh)
    grid_spec = pltpu.PrefetchScalarGridSpec(
        num_scalar_prefetch=1,
        grid=(bs,),
        in_specs=[pl.BlockSpec((1, rows, kw), lambda b, pt: (b, 0, 0)),
                  pl.BlockSpec((ts, dv), lambda b, pt: (row_blk_off + b, 0)),
                  pl.BlockSpec((ts, dr), lambda b, pt: (row_blk_off + b, 0)),
                  pl.BlockSpec(memory_space=pl.ANY),
                  pl.BlockSpec(memory_space=pl.ANY),
                  pl.BlockSpec(wuv.shape, lambda b, pt: (0, 0, 0))],
        out_specs=pl.BlockSpec((ts, n_heads * vh), lambda b, pt: (b, 0)),
        scratch_shapes=[pltpu.VMEM((2, n_pages, page, dv), F32), pltpu.VMEM((2, n_pages, dr, page), F32),
                        pltpu.SemaphoreType.DMA((2, 2))])
    return pl.pallas_call(
        kern,
        grid_spec=grid_spec,
        out_shape=jax.ShapeDtypeStruct((bs * ts, n_heads * vh), F32),
        compiler_params=_cparams(("arbitrary",), VMEM_LIMIT),
        name="attn_sample",
    )(page_table, qs, c_all, kr_all, cache_c, cache_rt, wuv)


def _neumann_inv(low, eye, steps):
    p = eye - low
    x = low
    for _ in range(steps):
        x = _bdot(x, x)
        p = p + _bdot(p, x)
    return p


def _gdn_kernel(*refs, n_in, n_seq, chunk, n_heads, dk, dv, conv_w):
    qkv_refs, ab_refs, z_refs = refs[:n_in], refs[n_in:2 * n_in], refs[2 * n_in:3 * n_in]
    (cs_ref, s0_ref, cw_ref, alog_ref, dtb_ref, nw_ref,
     o_ref, sfin_ref, ncv_ref, xp_ref, s_ref) = refs[3 * n_in:]
    n = pl.program_id(1)
    c = chunk
    per = n_seq // n_in
    n_prob = n_seq * n_heads
    hi = lax.Precision.HIGHEST

    def seq_rows(group, g):
        k = g % per
        return group[g // per][k * c:(k + 1) * c, :]

    @pl.when(n == 0)
    def _():
        s_ref[...] = s0_ref[...].reshape(n_prob, dk, dv)
        xp_ref[:, 0:SUBLANE, :] = cs_ref[...]

    r_i = lax.broadcasted_iota(jnp.int32, (c, c), 0)
    c_i = lax.broadcasted_iota(jnp.int32, (c, c), 1)
    causal = (c_i <= r_i)[None]
    strict = (c_i < r_i)[None]
    eye_c = (r_i == c_i).astype(F32)[None]
    db = min(GDN_DIAG, c)
    n_blk = c // db
    sh = int(math.log2(db))
    same_blk = (jnp.right_shift(r_i, sh) == jnp.right_shift(c_i, sh))[None]
    r_l = lax.broadcasted_iota(jnp.int32, (LANE, LANE), 0)
    c_l = lax.broadcasted_iota(jnp.int32, (LANE, LANE), 1)
    eye_l = (r_l == c_l).astype(F32)
    tril = (c_i <= r_i).astype(F32)
    cw = cw_ref[...]
    hk = n_heads * dk

    q_l, k_l, v_l, z_l, beta_l, gcol_l, grow_l, tails = [], [], [], [], [], [], [], []
    for g in range(n_seq):
        xp_ref[g, SUBLANE:SUBLANE + c, :] = seq_rows(qkv_refs, g)
        conv = xp_ref[g, pl.ds(SUBLANE - (conv_w - 1), c), :] * cw[0:1, :]
        for j in range(1, conv_w):
            conv = conv + xp_ref[g, pl.ds(SUBLANE - (conv_w - 1) + j, c), :] * cw[j:j + 1, :]
        tail = xp_ref[g, c:c + SUBLANE, :]
        xp_ref[g, 0:SUBLANE, :] = tail
        tails.append(tail)
        act = conv * _sigmoid(conv)
        ab = seq_rows(ab_refs, g)
        apb = ab + dtb_ref[...]
        softplus = jnp.maximum(apb, 0.0) + jnp.log(1.0 + jnp.exp(-jnp.abs(apb)))
        gfull = -jnp.exp(alog_ref[...]) * softplus
        betaf = _sigmoid(ab)
        gc = _dot(tril, gfull, precision=hi)
        gc_t = _dot_nt(eye_l, gc, precision=hi)
        zg = seq_rows(z_refs, g)
        for h in range(n_heads):
            q_l.append(act[:, h * dk:(h + 1) * dk])
            k_l.append(act[:, hk + h * dk:hk + (h + 1) * dk])
            v_l.append(act[:, 2 * hk + h * dv:2 * hk + (h + 1) * dv])
            z_l.append(zg[:, h * dv:(h + 1) * dv])
            beta_l.append(betaf[:, n_heads + h:n_heads + h + 1])
            gcol_l.append(gc[:, h:h + 1])
            grow_l.append(gc_t[h:h + 1, :])

    q = jnp.stack(q_l)
    k = jnp.stack(k_l)
    v = jnp.stack(v_l)
    z = jnp.stack(z_l)
    beta = jnp.stack(beta_l)
    gcol = jnp.stack(gcol_l)
    grow = jnp.stack(grow_l)
    q = q * lax.rsqrt(jnp.sum(q * q, -1, keepdims=True) + NORM_EPS) * (dk ** -0.5)
    k = k * lax.rsqrt(jnp.sum(k * k, -1, keepdims=True) + NORM_EPS)
    decay = jnp.where(causal, jnp.exp(jnp.where(causal, gcol - grow, 0.0)), 0.0)
    kb = k * beta
    lower = jnp.where(strict, _bdot_nt(kb, k) * decay, 0.0)
    attn = _bdot_nt(q, k) * decay
    l_d = jnp.where(same_blk, lower, 0.0)
    t_inv = _neumann_inv(l_d, eye_c, int(math.log2(db)) - 1)
    if n_blk > 1:
        m_inv = _neumann_inv(_bdot(t_inv, lower - l_d), eye_c, int(math.log2(n_blk)) - 1)
        t_inv = _bdot(m_inv, t_inv)
    egc = jnp.exp(gcol)
    uw = _bdot(t_inv, jnp.concatenate([v * beta, kb * egc], axis=2))
    s = s_ref[...]
    v_new = uw[:, :, :dv] - _bdot(uw[:, :, dv:], s)
    o = _bdot(q * egc, s) + _bdot(attn, v_new)
    glast = gcol[:, c - 1:c, :]
    kdec = k * jnp.exp(glast - gcol)
    s_new = s * jnp.exp(glast) + _bdot_tn(kdec, v_new)
    s_ref[...] = s_new
    on = o * lax.rsqrt(jnp.mean(o * o, -1, keepdims=True) + NORM_EPS) * nw_ref[...] * (z * _sigmoid(z))
    for g in range(n_seq):
        for h in range(n_heads):
            o_ref[g, :, h * dv:(h + 1) * dv] = on[g * n_heads + h]

    @pl.when(n == pl.num_programs(1) - 1)
    def _():
        sfin_ref[...] = s_new.reshape(n_seq, n_heads, dk, dv)
        for g in range(n_seq):
            ncv_ref[g] = tails[g]


def _gdn(h, row_off, conv_state8, ssm_state, cw, alog_p, dtb_p, nw, *, batch, seq, chunk, group,
         qkv_blk, ab_blk, z_blk, conv_w):
    _, n_heads, dk, dv = ssm_state.shape
    nc = seq // chunk
    ch = cw.shape[1]
    hd = n_heads * dv
    contiguous = nc == 1
    n_in = 1 if contiguous else group
    assert batch % group == 0 and row_off % (group * chunk) == 0
    kern = functools.partial(_gdn_kernel, n_in=n_in, n_seq=group, chunk=chunk, n_heads=n_heads,
                             dk=dk, dv=dv, conv_w=conv_w)
    if contiguous:
        rb = group * chunk
        row_maps = [lambda i, n: row_off // rb + i]
    else:
        rb = chunk
        row_maps = [(lambda i, n, g=g: row_off // rb + (i * group + g) * nc + n) for g in range(group)]

    def specs(width, col_blk):
        return [pl.BlockSpec((rb, width), lambda i, n, r=r: (r(i, n), col_blk)) for r in row_maps]

    in_specs = specs(ch, qkv_blk) + specs(LANE, ab_blk) + specs(hd, z_blk)
    in_specs += [pl.BlockSpec((group, SUBLANE, ch), lambda i, n: (i, 0, 0)),
                 pl.BlockSpec((group, n_heads, dk, dv), lambda i, n: (i, 0, 0, 0)),
                 pl.BlockSpec(cw.shape, lambda i, n: (0, 0)),
                 pl.BlockSpec((1, LANE), lambda i, n: (0, 0)),
                 pl.BlockSpec((1, LANE), lambda i, n: (0, 0)),
                 pl.BlockSpec((1, dv), lambda i, n: (0, 0))]
    o3, sfin, ncv = pl.pallas_call(
        kern,
        grid=(batch // group, nc),
        in_specs=in_specs,
        out_specs=[pl.BlockSpec((group, chunk, hd), lambda i, n: (i, n, 0)),
                   pl.BlockSpec((group, n_heads, dk, dv), lambda i, n: (i, 0, 0, 0)),
                   pl.BlockSpec((group, SUBLANE, ch), lambda i, n: (i, 0, 0))],
        out_shape=[jax.ShapeDtypeStruct((batch, seq, hd), F32),
                   jax.ShapeDtypeStruct(ssm_state.shape, F32),
                   jax.ShapeDtypeStruct((batch, SUBLANE, ch), F32)],
        scratch_shapes=[pltpu.VMEM((group, chunk + SUBLANE, ch), F32),
                        pltpu.VMEM((group * n_heads, dk, dv), F32)],
        compiler_params=_cparams(("parallel", "arbitrary"), VMEM_LIMIT),
        name="gdn",
    )(*([h] * (3 * n_in)), conv_state8, ssm_state, cw, alog_p, dtb_p, nw)
    return o3.reshape(batch * seq, hd), sfin, ncv


def _store_row_tiles(ref, val):
    rows = val.shape[0]
    for j in range(SUBLANE):
        ref[pl.ds(j, rows, stride=SUBLANE), :] = val[:, j * LANE:(j + 1) * LANE]


def _load_row_tiles(ref, start, rows, j):
    return ref[pl.ds(start * SUBLANE + j, rows, stride=SUBLANE), :]


def _post_kernel(x_ref, omp_ref, oms_ref, ogp_ref, ogs_ref, ga_ref, gb_ref, wo_ref, g1_ref, b1_ref, wr_ref, br_ref,
                 x1_ref, x1t_ref, idx_ref, gate_ref, rank_ref, cnt_ref, carry_ref, *, alpha, top_k, n_first):
    @pl.when(pl.program_id(0) == 0)
    def _():
        carry_ref[...] = jnp.zeros(carry_ref.shape, F32)

    first = pl.program_id(0) < n_first
    om = jnp.where(first, omp_ref[...], oms_ref[...])
    og = jnp.where(first, ogp_ref[...], ogs_ref[...])
    mix = _sigmoid(ga_ref[...]) * om + _sigmoid(gb_ref[...]) * og
    y = _dot(mix.astype(BF16), wo_ref[...])
    x1 = _layer_norm(alpha * x_ref[...] + y, g1_ref[...], b1_ref[...])
    x1_ref[...] = x1
    _store_row_tiles(x1t_ref, x1)
    logits = _dot_nt(wr_ref[...], x1, precision=lax.Precision.HIGHEST) + br_ref[...]
    n_exp, tm = logits.shape
    e_i = lax.broadcasted_iota(jnp.int32, (n_exp, tm), 0)
    vals, idxs = [], []
    for _ in range(top_k):
        mx = jnp.max(logits, axis=0, keepdims=True)
        ix = jnp.min(jnp.where(logits == mx, e_i, n_exp), axis=0, keepdims=True)
        vals.append(mx)
        idxs.append(ix)
        logits = jnp.where(e_i == ix, -jnp.inf, logits)
    es = [jnp.exp(v - vals[0]) for v in vals]
    tot = es[0]
    for e in es[1:]:
        tot = tot + e
    onehots = [e_i == ix for ix in idxs]
    sel = onehots[0].astype(F32)
    for oh in onehots[1:]:
        sel = sel + oh.astype(F32)
    t_r = lax.broadcasted_iota(jnp.int32, (tm, tm), 0)
    t_c = lax.broadcasted_iota(jnp.int32, (tm, tm), 1)
    before = _dot(sel.astype(BF16), (t_r < t_c).astype(BF16)) + carry_ref[...]
    carry = carry_ref[...] + jnp.sum(sel, axis=1, keepdims=True)
    carry_ref[...] = carry
    cnt_ref[...] = carry
    r_i = lax.broadcasted_iota(jnp.int32, (SUBLANE, tm), 0)
    idx_o = jnp.zeros((SUBLANE, tm), jnp.int32)
    gate_o = jnp.zeros((SUBLANE, tm), F32)
    rank_o = jnp.zeros((SUBLANE, tm), F32)
    for k in range(top_k):
        idx_o = jnp.where(r_i == k, idxs[k], idx_o)
        gate_o = jnp.where(r_i == k, es[k] / tot, gate_o)
        rank_k = jnp.sum(jnp.where(onehots[k], before, 0.0), axis=0, keepdims=True)
        rank_o = jnp.where(r_i == k, rank_k, rank_o)
    idx_ref[...] = idx_o
    gate_ref[...] = gate_o
    rank_ref[...] = rank_o.astype(jnp.int32)


def _post(x, om_p, om_s, og_p, og_s, h, ga_blk, gb_blk, wo, g1, b1, wr_t, br, *, tm, alpha):
    m, d = x.shape
    n_exp = wr_t.shape[0]
    n_first = om_p.shape[0] // tm
    assert om_p.shape[0] % tm == 0 and om_s.shape[0] % tm == 0 and d == SUBLANE * LANE
    kern = functools.partial(_post_kernel, alpha=alpha, top_k=TOP_K, n_first=n_first)
    row = lambda i: (i, 0)
    const = lambda i: (0, 0)
    first = lambda i: (jnp.minimum(i, n_first - 1), 0)
    second = lambda i: (jnp.maximum(i - n_first, 0), 0)
    return pl.pallas_call(
        kern,
        grid=(m // tm,),
        in_specs=[pl.BlockSpec((tm, d), row),
                  pl.BlockSpec((tm, d), first), pl.BlockSpec((tm, d), second),
                  pl.BlockSpec((tm, d), first), pl.BlockSpec((tm, d), second),
                  pl.BlockSpec((tm, d), lambda i: (i, ga_blk)),
                  pl.BlockSpec((tm, d), lambda i: (i, gb_blk)),
                  pl.BlockSpec(wo.shape, const), pl.BlockSpec((1, d), const), pl.BlockSpec((1, d), const),
                  pl.BlockSpec(wr_t.shape, const), pl.BlockSpec((n_exp, 1), const)],
        out_specs=[pl.BlockSpec((tm, d), row),
                   pl.BlockSpec((tm * SUBLANE, LANE), row),
                   pl.BlockSpec((SUBLANE, tm), lambda i: (0, i)),
                   pl.BlockSpec((SUBLANE, tm), lambda i: (0, i)),
                   pl.BlockSpec((SUBLANE, tm), lambda i: (0, i)),
                   pl.BlockSpec((n_exp, 1), const)],
        out_shape=[jax.ShapeDtypeStruct((m, d), F32),
                   jax.ShapeDtypeStruct((m * SUBLANE, LANE), F32),
                   jax.ShapeDtypeStruct((SUBLANE, m), jnp.int32),
                   jax.ShapeDtypeStruct((SUBLANE, m), F32),
                   jax.ShapeDtypeStruct((SUBLANE, m), jnp.int32),
                   jax.ShapeDtypeStruct((n_exp, 1), F32)],
        scratch_shapes=[pltpu.VMEM((n_exp, 1), F32)],
        compiler_params=_cparams(("arbitrary",), VMEM_LIMIT),
        name="post_mix",
    )(x, om_p, om_s, og_p, og_s, h, h, wo, g1, b1, wr_t, br)


def _dispatch_kernel(rows_ref, nused_ref, x_hbm, o_ref, xbuf, sem, *, blk):
    i = pl.program_id(0)
    n_used = nused_ref[0]
    slot = i % 2

    def row_copy(tok, s, r):
        return pltpu.make_async_copy(x_hbm.at[pl.ds(tok * SUBLANE, SUBLANE)],
                                     xbuf.at[s, pl.ds(r * SUBLANE, SUBLANE)], sem.at[s])

    def wait(s):
        pltpu.make_async_copy(x_hbm.at[pl.ds(0, blk * SUBLANE)], xbuf.at[s], sem.at[s]).wait()

    @pl.when(jnp.logical_and(i == 0, n_used > 0))
    def _():
        second = jnp.minimum(1, n_used - 1) * blk

        def body(r, carry):
            row_copy(rows_ref[r], 0, r).start()
            row_copy(rows_ref[second + r], 1, r).start()
            return carry
        lax.fori_loop(0, blk, body, 0)

    @pl.when(i < n_used)
    def _():
        wait(slot)
        for j in range(SUBLANE):
            o_ref[:, j * LANE:(j + 1) * LANE] = _load_row_tiles(xbuf.at[slot], 0, blk, j).astype(BF16)
        nxt = jnp.minimum(i + 2, n_used - 1) * blk
        for r in range(blk):
            row_copy(rows_ref[nxt + r], slot, r).start(priority=r % 2)

        @pl.when(i == n_used - 1)
        def _():
            wait(0)
            wait(1)

    @pl.when(i >= n_used)
    def _():
        o_ref[...] = jnp.zeros(o_ref.shape, BF16)


def _moe_dispatch(rows, n_used, x1t, *, blk, nb, d):
    kern = functools.partial(_dispatch_kernel, blk=blk)
    grid_spec = pltpu.PrefetchScalarGridSpec(
        num_scalar_prefetch=2,
        grid=(nb,),
        in_specs=[pl.BlockSpec(memory_space=pl.ANY)],
        out_specs=pl.BlockSpec((blk, d), lambda i, rw, nu: (i, 0)),
        scratch_shapes=[pltpu.VMEM((2, blk * SUBLANE, LANE), F32), pltpu.SemaphoreType.DMA((2,))])
    return pl.pallas_call(
        kern,
        grid_spec=grid_spec,
        out_shape=jax.ShapeDtypeStruct((nb * blk, d), BF16),
        compiler_params=_cparams(("arbitrary",), VMEM_LIMIT),
        name="moe_dispatch",
    )(rows, n_used, x1t)


def _moe_kernel(be_ref, nused_ref, xs_ref, wgu_ref, bgu_ref, wd_ref, bd_ref, o_ref,
                wgu_bf, wd_bf, *, d_exp):
    i = pl.program_id(0)
    n_used = nused_ref[0]

    @pl.when(i < n_used)
    def _():
        e = be_ref[i]
        e_prev = be_ref[jnp.maximum(i - 1, 0)]

        @pl.when(jnp.logical_or(i == 0, e != e_prev))
        def _():
            wgu_bf[...] = wgu_ref[0].astype(BF16)
            wd_bf[...] = wd_ref[0].astype(BF16)

        hh = _dot(xs_ref[...], wgu_bf[...]) + bgu_ref[0]
        gate = jnp.minimum(hh[:, :d_exp], SWIGLU_LIMIT)
        up = jnp.clip(hh[:, d_exp:], -SWIGLU_LIMIT, SWIGLU_LIMIT)
        act = (up + 1.0) * gate * _sigmoid(SWIGLU_ALPHA * gate)
        _store_row_tiles(o_ref, _dot(act.astype(BF16), wd_bf[...]) + bd_ref[0])

    @pl.when(i >= n_used)
    def _():
        o_ref[...] = jnp.zeros(o_ref.shape, F32)


def _moe_experts(block_e, n_used, xs, w_gu, b_gu, w_down, b_down, *, blk):
    n_exp, d, d2 = w_gu.shape
    d_exp = d2 // 2
    nb = block_e.shape[0]
    kern = functools.partial(_moe_kernel, d_exp=d_exp)
    grid_spec = pltpu.PrefetchScalarGridSpec(
        num_scalar_prefetch=2,
        grid=(nb,),
        in_specs=[pl.BlockSpec((blk, d), lambda i, be, nu: (i, 0)),
                  pl.BlockSpec((1, d, d2), lambda i, be, nu: (be[i], 0, 0)),
                  pl.BlockSpec((1, 1, d2), lambda i, be, nu: (be[i], 0, 0)),
                  pl.BlockSpec((1, d_exp, d), lambda i, be, nu: (be[i], 0, 0)),
                  pl.BlockSpec((1, 1, d), lambda i, be, nu: (be[i], 0, 0))],
        out_specs=pl.BlockSpec((blk * SUBLANE, LANE), lambda i, be, nu: (i, 0)),
        scratch_shapes=[pltpu.VMEM((d, d2), BF16), pltpu.VMEM((d_exp, d), BF16)])
    return pl.pallas_call(
        kern,
        grid_spec=grid_spec,
        out_shape=jax.ShapeDtypeStruct((nb * blk * SUBLANE, LANE), F32),
        compiler_params=_cparams(("arbitrary",), VMEM_LIMIT),
        name="moe_experts",
    )(block_e, n_used, xs, w_gu, b_gu.reshape(n_exp, 1, d2), w_down, b_down.reshape(n_exp, 1, d))


def _combine_kernel(dest_ref, x1_ref, gate_ref, ys_hbm, g2_ref, b2_ref, op_ref, os_ref, ybuf, sem,
                    *, tm, top_k, alpha, m_total, n_first):
    i = pl.program_id(0)
    nsteps = pl.num_programs(0)
    slot = i % 2
    n_rows = top_k * tm

    def row_copy(d, s, r):
        return pltpu.make_async_copy(ys_hbm.at[pl.ds(d * SUBLANE, SUBLANE)],
                                     ybuf.at[s, pl.ds(r * SUBLANE, SUBLANE)], sem.at[s])

    def wait(s):
        pltpu.make_async_copy(ys_hbm.at[pl.ds(0, n_rows * SUBLANE)], ybuf.at[s], sem.at[s]).wait()

    @pl.when(i == 0)
    def _():
        second = jnp.minimum(1, nsteps - 1) * tm

        def body(r, carry):
            k = r // tm
            t = r - k * tm
            row_copy(dest_ref[k * m_total + t], 0, r).start()
            row_copy(dest_ref[k * m_total + second + t], 1, r).start()
            return carry
        lax.fori_loop(0, n_rows, body, 0)

    wait(slot)
    g = gate_ref[...]
    parts = []
    for j in range(SUBLANE):
        acc = g[:, 0:1] * _load_row_tiles(ybuf.at[slot], 0, tm, j)
        for k in range(1, top_k):
            acc = acc + g[:, k:k + 1] * _load_row_tiles(ybuf.at[slot], k * tm, tm, j)
        parts.append(acc)
    y = jnp.concatenate(parts, axis=1)
    nxt = jnp.minimum(i + 2, nsteps - 1) * tm
    for r in range(n_rows):
        k, t = divmod(r, tm)
        row_copy(dest_ref[k * m_total + nxt + t], slot, r).start(priority=r % 2)
    res = _layer_norm(alpha * x1_ref[...] + y, g2_ref[...], b2_ref[...])

    @pl.when(i < n_first)
    def _():
        op_ref[...] = res

    @pl.when(i >= n_first)
    def _():
        os_ref[...] = res

    @pl.when(i == nsteps - 1)
    def _():
        wait(0)
        wait(1)


def _combine(dest_km, x1, gates_mk, ys, g2, b2, *, tm, alpha, m_first):
    m, d = x1.shape
    n_first = m_first // tm
    assert m_first % tm == 0 and 0 < m_first < m
    kern = functools.partial(_combine_kernel, tm=tm, top_k=TOP_K, alpha=alpha, m_total=m, n_first=n_first)
    grid_spec = pltpu.PrefetchScalarGridSpec(
        num_scalar_prefetch=1,
        grid=(m // tm,),
        in_specs=[pl.BlockSpec((tm, d), lambda i, ds: (i, 0)),
                  pl.BlockSpec((tm, SUBLANE), lambda i, ds: (i, 0)),
                  pl.BlockSpec(memory_space=pl.ANY),
                  pl.BlockSpec((1, d), lambda i, ds: (0, 0)),
                  pl.BlockSpec((1, d), lambda i, ds: (0, 0))],
        out_specs=[pl.BlockSpec((tm, d), lambda i, ds: (jnp.minimum(i, n_first - 1), 0)),
                   pl.BlockSpec((tm, d), lambda i, ds: (jnp.maximum(i - n_first, 0), 0))],
        scratch_shapes=[pltpu.VMEM((2, TOP_K * tm * SUBLANE, LANE), F32), pltpu.SemaphoreType.DMA((2,))])
    return pl.pallas_call(
        kern,
        grid_spec=grid_spec,
        out_shape=[jax.ShapeDtypeStruct((m_first, d), F32), jax.ShapeDtypeStruct((m - m_first, d), F32)],
        compiler_params=_cparams(("arbitrary",), VMEM_LIMIT),
        name="moe_combine",
    )(dest_km, x1, gates_mk, ys, g2, b2)


def _pack_w_in(w_in, splits):
    q_lora, kv_lora, rope, conv_ch, gv, nh, _, d, _ = splits
    offs = [0]
    for s in splits:
        offs.append(offs[-1] + s)
    part = [w_in[:, offs[i]:offs[i + 1]] for i in range(len(splits))]
    q_lat, kv_lat, k_r, qkv, z, a, b, g_a, g_b = part
    dm = w_in.shape[0]
    half = rope // 2
    zpad = lambda n: jnp.zeros((dm, n), w_in.dtype)
    k_sw = jnp.concatenate([k_r[:, half:], k_r[:, :half]], axis=1)
    small = jnp.concatenate([q_lat, kv_lat, k_r, zpad(LANE - rope), k_sw, zpad(LANE - rope),
                             a, b, zpad(LANE - 2 * nh)], axis=1)
    return jnp.concatenate([qkv, small, z, g_a, g_b], axis=1).astype(BF16)


def _pack_w_uq(w_uq, nope, rope):
    w = jnp.transpose(w_uq, (1, 0, 2))
    half = rope // 2
    r = w[..., nope:]
    zp = jnp.zeros(r.shape[:-1] + (LANE - rope,), w.dtype)
    r_sw = jnp.concatenate([r[..., half:], r[..., :half]], axis=-1)
    return jnp.concatenate([w[..., :nope], r, zp, r_sw, zp], axis=-1).astype(BF16)


def _rope_tables(pos, rope):
    half = rope // 2
    inv = ROPE_THETA ** (-jnp.arange(half, dtype=F32) / half)
    ang = pos.astype(F32)[:, None] * inv[None, :]
    cos, sin = jnp.cos(ang), jnp.sin(ang)
    zp = jnp.zeros((pos.shape[0], LANE - rope), F32)
    return (jnp.concatenate([cos, cos, zp], axis=1), jnp.concatenate([-sin, sin, zp], axis=1))


def _route_meta(idx_t, rank_t, counts, m, n_exp, blk):
    a = m * TOP_K
    counts = counts.astype(jnp.int32)
    padded = (counts + blk - 1) // blk * blk
    pad_end = jnp.cumsum(padded)
    pad_start = pad_end - padded
    experts = jnp.arange(n_exp, dtype=jnp.int32)
    e_km = idx_t[:TOP_K]
    start_km = jnp.sum(jnp.where(e_km[:, :, None] == experts, pad_start, 0), axis=-1)
    dest_km = (start_km + rank_t[:TOP_K]).astype(jnp.int32).reshape(a)
    nb = a // blk + n_exp
    tok_km = jnp.tile(jnp.arange(m, dtype=jnp.int32), TOP_K)
    rows = jnp.zeros((nb * blk,), jnp.int32).at[dest_km].set(tok_km, unique_indices=True,
                                                              mode='promise_in_bounds')
    first_row = jnp.arange(nb, dtype=jnp.int32) * blk
    block_e = jnp.minimum(jnp.sum((pad_end[None, :] <= first_row[:, None]).astype(jnp.int32), axis=1),
                          n_exp - 1).astype(jnp.int32)
    n_used = (pad_end[-1] // blk).astype(jnp.int32).reshape(1)
    return block_e, rows, n_used, dest_km


def kernel(x_prompt, x_sample, cache_ckv, cache_krope, page_table, state_conv, state_ssm, w_in, q_norm_w, kv_norm_w, w_uq, w_uk, w_uv, conv_w, a_log, dt_bias, gdn_norm_w, w_o, ln1_g, ln1_b, w_router, b_router, w_gu, b_gu, w_down, b_down, ln2_g, ln2_b):
    bp, tp, d = x_prompt.shape
    bs, ts, _ = x_sample.shape
    depth = w_in.shape[0]
    q_lora, n_heads, qk = w_uq.shape[1:]
    kv_lora, _, nope = w_uk.shape[1:]
    rope = qk - nope
    vh = w_uv.shape[3]
    cw_taps, conv_ch = conv_w.shape[1:]
    g_heads = a_log.shape[1]
    dk, dv = state_ssm.shape[3:]
    n_exp = w_router.shape[2]
    page = cache_ckv.shape[2]
    past = page_table.shape[1] * page
    splits = (q_lora, kv_lora, rope, conv_ch, g_heads * dv, g_heads, g_heads, d, d)
    assert sum(splits) == w_in.shape[2]
    alpha = (2 * depth) ** 0.25
    scale = (nope + rope) ** -0.5 * LOG2E
    mp, ms = bp * tp, bs * ts
    m = mp + ms
    small_w = q_lora + kv_lora + 3 * LANE
    assert small_w == d and conv_ch % d == 0
    qkv_blk, small_blk = 0, conv_ch // d
    z_blk, ga_blk, gb_blk = small_blk + 1, small_blk + 2, small_blk + 3
    ab_blk = (conv_ch + q_lora + kv_lora + 2 * LANE) // LANE

    cos_p, sin_p = _rope_tables(jnp.arange(tp, dtype=jnp.int32), rope)
    cos_s, sin_s = _rope_tables(past + jnp.arange(ts, dtype=jnp.int32), rope)
    cos_t = jnp.concatenate([jnp.tile(cos_p, (bp, 1)), jnp.tile(cos_s, (bs, 1))], axis=0)
    sin_t = jnp.concatenate([jnp.tile(sin_p, (bp, 1)), jnp.tile(sin_s, (bs, 1))], axis=0)

    x_p, x_s = x_prompt.reshape(mp, d), x_sample.reshape(ms, d)
    outs = {k: [] for k in ("ckv_p", "kr_p", "conv_p", "ssm_p", "ckv_s", "kr_s", "conv_s", "ssm_s")}
    pad_lanes = lambda v: jnp.pad(v, (0, LANE - v.shape[0])).reshape(1, LANE)
    for l in range(depth):
        x = jnp.concatenate([x_p, x_s], axis=0)
        w_pack = _pack_w_in(w_in[l], splits)
        wq = _pack_w_uq(w_uq[l], nope, rope)
        wuk = jnp.transpose(w_uk[l], (1, 2, 0)).astype(BF16)
        wuv = jnp.transpose(w_uv[l], (1, 0, 2)).astype(BF16)
        h = _in_proj(x, w_pack, tm=_row_tile(m, 2304), tn=512)
        qf, c_all, kr_all, kt, cb = _mla_proj(
            h, small_blk, cos_t, sin_t, q_norm_w[l].reshape(1, q_lora), kv_norm_w[l].reshape(1, kv_lora),
            wq, wuk, tm=256, q_lora=q_lora, kv_lora=kv_lora, nope=nope, rope=rope, scale=scale)
        om_p = _attn_prompt(qf, kt, cb, wuv, batch=bp, seq=tp, tq=128, tk=min(1024, tp))
        qs = qf[:, mp:, :].reshape(n_heads, bs, ts, qf.shape[-1])
        qs = jnp.transpose(qs, (1, 0, 2, 3)).reshape(bs, n_heads * ts, qf.shape[-1])
        om_s = _attn_sample(page_table, qs, c_all, kr_all, mp // ts, cache_ckv[l],
                            jnp.swapaxes(cache_krope[l], 1, 2), wuv)

        alog_p = pad_lanes(a_log[l])
        dtb_p = pad_lanes(dt_bias[l])
        nw = gdn_norm_w[l].reshape(1, dv)
        gdn_kw = dict(qkv_blk=qkv_blk, ab_blk=ab_blk, z_blk=z_blk, conv_w=cw_taps)
        zeros_conv = jnp.zeros((bp, SUBLANE, conv_ch), F32)
        zeros_ssm = jnp.zeros((bp, g_heads, dk, dv), F32)
        og_p, ssm_p, ncv_p = _gdn(h, 0, zeros_conv, zeros_ssm, conv_w[l], alog_p, dtb_p, nw,
                                  batch=bp, seq=tp, chunk=min(GDN_CHUNK, tp), group=bp, **gdn_kw)
        conv8_s = jnp.pad(state_conv[l], ((0, 0), (SUBLANE - (cw_taps - 1), 0), (0, 0)))
        og_s, ssm_s, ncv_s = _gdn(h, mp, conv8_s, state_ssm[l], conv_w[l], alog_p, dtb_p, nw,
                                  batch=bs, seq=ts, chunk=ts, group=math.gcd(bs, GDN_SAMPLE_GROUP), **gdn_kw)

        x1, x1t, idx_t, gate_t, rank_t, counts = _post(
            x, om_p, om_s, og_p, og_s, h, ga_blk, gb_blk, w_o[l].astype(BF16), ln1_g[l].reshape(1, d), ln1_b[l].reshape(1, d),
            w_router[l].T, b_router[l].reshape(n_exp, 1), tm=256, alpha=alpha)
        block_e, rows, n_used, dest_km = _route_meta(idx_t, rank_t, counts[:, 0], m, n_exp, MOE_ROWS)
        xs = _moe_dispatch(rows, n_used, x1t, blk=MOE_ROWS, nb=block_e.shape[0], d=d)
        ys = _moe_experts(block_e, n_used, xs, w_gu[l], b_gu[l], w_down[l], b_down[l], blk=MOE_ROWS)
        x_p, x_s = _combine(dest_km, x1, gate_t.T, ys, ln2_g[l].reshape(1, d), ln2_b[l].reshape(1, d),
                            tm=128, alpha=alpha, m_first=mp)

        outs["ckv_p"].append(c_all[:mp].reshape(bp, tp, kv_lora))
        outs["kr_p"].append(kr_all[:mp].reshape(bp, tp, rope))
        outs["conv_p"].append(ncv_p[:, SUBLANE - (cw_taps - 1):, :])
        outs["ssm_p"].append(ssm_p)
        outs["ckv_s"].append(c_all[mp:].reshape(bs, ts, kv_lora))
        outs["kr_s"].append(kr_all[mp:].reshape(bs, ts, rope))
        outs["conv_s"].append(ncv_s[:, SUBLANE - (cw_taps - 1):, :])
        outs["ssm_s"].append(ssm_s)

    return (x_p.reshape(bp, tp, d), x_s.reshape(bs, ts, d),
            jnp.stack(outs["ckv_p"]), jnp.stack(outs["kr_p"]), jnp.stack(outs["conv_p"]), jnp.stack(outs["ssm_p"]),
            jnp.stack(outs["ckv_s"]), jnp.stack(outs["kr_s"]), jnp.stack(outs["conv_s"]), jnp.stack(outs["ssm_s"]))
```

```python
import functools
import math

import jax
import jax.numpy as jnp
from jax import lax
from jax.experimental import pallas as pl
from jax.experimental.pallas import tpu as pltpu

F32 = jnp.float32
BF16 = jnp.bfloat16

ROPE_THETA = 10000.0
NORM_EPS = 1e-6
TOP_K = 4
SWIGLU_LIMIT = 7.0
SWIGLU_ALPHA = 1.702
GDN_CHUNK = 64
GDN_DIAG = 16
GDN_SAMPLE_GROUP = 8
MOE_ROWS = 512
PAGE_CHUNK = 8
DIAG_VARIANTS = 4
LANE = 128
SUBLANE = 8
VMEM_LIMIT = 56 * 1024 * 1024
NEG = -1e30
LOG2E = 1.4426950408889634


def _cparams(sem, vmem=None):
    return pltpu.CompilerParams(dimension_semantics=sem, vmem_limit_bytes=vmem)


def _dot(a, b, **kw):
    return jnp.dot(a, b, preferred_element_type=F32, **kw)


def _dot_nt(a, b, **kw):
    return lax.dot_general(a, b, (((1,), (1,)), ((), ())), preferred_element_type=F32, **kw)


def _bdot(a, b):
    return lax.dot_general(a, b, (((2,), (1,)), ((0,), (0,))), preferred_element_type=F32)


def _bdot_nt(a, b):
    return lax.dot_general(a, b, (((2,), (2,)), ((0,), (0,))), preferred_element_type=F32)


def _bdot_tn(a, b):
    return lax.dot_general(a, b, (((1,), (1,)), ((0,), (0,))), preferred_element_type=F32)


def _sigmoid(x):
    return 1.0 / (1.0 + jnp.exp(-x))


def _layer_norm(v, g, b):
    mu = jnp.mean(v, -1, keepdims=True)
    vc = v - mu
    var = jnp.mean(vc * vc, -1, keepdims=True)
    return vc * lax.rsqrt(var + NORM_EPS) * g + b


def _inproj_kernel(x_ref, w_ref, o_ref, xb_ref):
    @pl.when(pl.program_id(1) == 0)
    def _():
        xb_ref[...] = x_ref[...].astype(BF16)

    o_ref[...] = _dot(xb_ref[...], w_ref[...])


def _row_tile(m, target):
    return max(t for t in range(SUBLANE, target + 1, SUBLANE) if m % t == 0)


def _in_proj(x, w, tm, tn):
    m, k = x.shape
    n = w.shape[1]
    return pl.pallas_call(
        _inproj_kernel,
        grid=(m // tm, n // tn),
        in_specs=[pl.BlockSpec((tm, k), lambda i, j: (i, 0)),
                  pl.BlockSpec((k, tn), lambda i, j: (0, j))],
        out_specs=pl.BlockSpec((tm, tn), lambda i, j: (i, j)),
        out_shape=jax.ShapeDtypeStruct((m, n), F32),
        scratch_shapes=[pltpu.VMEM((tm, k), BF16)],
        compiler_params=_cparams(("parallel", "arbitrary"), VMEM_LIMIT),
        name="in_proj",
    )(x, w)


def _mla_proj_kernel(h_ref, cos_ref, sin_ref, qnw_ref, kvnw_ref, wq_ref, wuk_ref,
                     qf_ref, c_ref, kr_ref, kt_ref, cb_ref, *, n_heads, q_lora, kv_lora, nope, rope, scale):
    hs = h_ref[...]
    cos = cos_ref[...]
    sin = sin_ref[...]
    q_lat = hs[:, :q_lora]
    qn = q_lat * lax.rsqrt(jnp.mean(q_lat * q_lat, -1, keepdims=True) + NORM_EPS) * qnw_ref[...]
    qn = qn.astype(BF16)
    kv = hs[:, q_lora:q_lora + kv_lora]
    c = kv * lax.rsqrt(jnp.mean(kv * kv, -1, keepdims=True) + NORM_EPS) * kvnw_ref[...]
    o = q_lora + kv_lora
    kr = hs[:, o:o + LANE] * cos + hs[:, o + LANE:o + 2 * LANE] * sin
    c_ref[...] = c
    kr_ref[...] = kr[:, :rope]
    cb = c.astype(BF16)
    cb_ref[...] = cb
    kfull = jnp.concatenate([cb, kr.astype(BF16)], axis=1)
    kw = kfull.shape[1]
    eye = (lax.broadcasted_iota(jnp.int32, (kw, kw), 0) == lax.broadcasted_iota(jnp.int32, (kw, kw), 1))
    kt_ref[...] = _dot_nt(eye.astype(BF16), kfull).astype(BF16)
    for h in range(n_heads):
        qh = _dot(qn, wq_ref[h])
        qa = _dot(qh[:, :nope].astype(BF16), wuk_ref[h])
        qr = qh[:, nope:nope + LANE] * cos + qh[:, nope + LANE:nope + 2 * LANE] * sin
        qf_ref[h, :, :kv_lora] = (qa * scale).astype(BF16)
        qf_ref[h, :, kv_lora:] = (qr * scale).astype(BF16)


def _mla_proj(h, col_blk, cos_t, sin_t, qnw, kvnw, wq, wuk, *, tm, q_lora, kv_lora, nope, rope, scale):
    m = h.shape[0]
    n_heads = wq.shape[0]
    wcol = q_lora + kv_lora + 3 * LANE
    kw = kv_lora + LANE
    kern = functools.partial(_mla_proj_kernel, n_heads=n_heads, q_lora=q_lora, kv_lora=kv_lora,
                             nope=nope, rope=rope, scale=scale)
    return pl.pallas_call(
        kern,
        grid=(m // tm,),
        in_specs=[pl.BlockSpec((tm, wcol), lambda i: (i, col_blk)),
                  pl.BlockSpec((tm, LANE), lambda i: (i, 0)),
                  pl.BlockSpec((tm, LANE), lambda i: (i, 0)),
                  pl.BlockSpec((1, q_lora), lambda i: (0, 0)),
                  pl.BlockSpec((1, kv_lora), lambda i: (0, 0)),
                  pl.BlockSpec(wq.shape, lambda i: (0, 0, 0)),
                  pl.BlockSpec(wuk.shape, lambda i: (0, 0, 0))],
        out_specs=[pl.BlockSpec((n_heads, tm, kw), lambda i: (0, i, 0)),
                   pl.BlockSpec((tm, kv_lora), lambda i: (i, 0)),
                   pl.BlockSpec((tm, rope), lambda i: (i, 0)),
                   pl.BlockSpec((kw, tm), lambda i: (0, i)),
                   pl.BlockSpec((tm, kv_lora), lambda i: (i, 0))],
        out_shape=[jax.ShapeDtypeStruct((n_heads, m, kw), BF16),
                   jax.ShapeDtypeStruct((m, kv_lora), F32),
                   jax.ShapeDtypeStruct((m, rope), F32),
                   jax.ShapeDtypeStruct((kw, m), BF16),
                   jax.ShapeDtypeStruct((m, kv_lora), BF16)],
        compiler_params=_cparams(("parallel",), VMEM_LIMIT),
        name="mla_proj",
    )(h, cos_t, sin_t, qnw, kvnw, wq, wuk)


def _softmax_step(s, v, m_ref, l_ref, acc_ref):
    m_prev = m_ref[...]
    m_new = jnp.maximum(m_prev, jnp.max(s, -1, keepdims=True))
    alpha = jnp.exp2(m_prev - m_new)
    p = jnp.exp2(s - m_new)
    l_ref[...] = alpha * l_ref[...] + jnp.sum(p, -1, keepdims=True)
    acc_ref[...] = alpha * acc_ref[...] + _dot(p.astype(BF16), v)
    m_ref[...] = m_new


def _attn_prompt_kernel(qf_ref, kt_ref, v_ref, wuv_ref, o_ref, m_ref, l_ref, acc_ref, *, n_heads, tq, tk, dv, vh, n_split):
    i = pl.program_id(1)
    m_ref[...] = jnp.full(m_ref.shape, NEG, F32)
    l_ref[...] = jnp.zeros(l_ref.shape, F32)
    acc_ref[...] = jnp.zeros(acc_ref.shape, F32)
    n_full = (i * tq) // tk
    off = i * tq - n_full * tk
    hs = n_heads // n_split
    rs = hs * tq

    def step(j, width, masked):
        start = pl.multiple_of(j * tk, tk)
        kt = kt_ref[:, pl.ds(start, width)]
        v = v_ref[pl.ds(start, width), :]
        for g in range(n_split):
            q = qf_ref[g * hs:(g + 1) * hs].reshape(rs, qf_ref.shape[-1])
            s = _dot(q, kt)
            if masked:
                row = lax.broadcasted_iota(jnp.int32, (tq, width), 0)
                col = lax.broadcasted_iota(jnp.int32, (tq, width), 1)
                s = jnp.where((col <= row + off)[None], s.reshape(hs, tq, width), NEG).reshape(rs, width)
            sl = slice(g * rs, (g + 1) * rs)
            _softmax_step(s, v, m_ref.at[sl], l_ref.at[sl], acc_ref.at[sl])

    def body(j, carry):
        step(j, tk, False)
        return carry

    lax.fori_loop(0, n_full, body, 0)
    need = off + tq
    widths = [tk * (v + 1) // DIAG_VARIANTS for v in range(DIAG_VARIANTS)]
    for v, width in enumerate(widths):
        lo = widths[v - 1] if v else 0

        @pl.when(jnp.logical_and(need > lo, need <= width))
        def _(width=width):
            step(n_full, width, True)

    o = acc_ref[...] / l_ref[...]
    for h in range(n_heads):
        oh = o[h * tq:(h + 1) * tq].astype(BF16)
        o_ref[:, h * vh:(h + 1) * vh] = _dot(oh, wuv_ref[h])


def _attn_prompt(qf, kt, cb, wuv, *, batch, seq, tq, tk):
    n_heads, _, kw = qf.shape
    dv, vh = wuv.shape[1], wuv.shape[2]
    nq = seq // tq
    rows = n_heads * tq
    assert seq % tk == 0 and tk % tq == 0 and tk % (DIAG_VARIANTS * LANE) == 0
    kern = functools.partial(_attn_prompt_kernel, n_heads=n_heads, tq=tq, tk=tk, dv=dv, vh=vh, n_split=2)
    return pl.pallas_call(
        kern,
        grid=(batch, nq),
        in_specs=[pl.BlockSpec((n_heads, tq, kw), lambda b, i: (0, b * nq + i, 0)),
                  pl.BlockSpec((kw, seq), lambda b, i: (0, b)),
                  pl.BlockSpec((seq, dv), lambda b, i: (b, 0)),
                  pl.BlockSpec(wuv.shape, lambda b, i: (0, 0, 0))],
        out_specs=pl.BlockSpec((tq, n_heads * vh), lambda b, i: (b * nq + i, 0)),
        out_shape=jax.ShapeDtypeStruct((batch * seq, n_heads * vh), F32),
        scratch_shapes=[pltpu.VMEM((rows, 1), F32), pltpu.VMEM((rows, 1), F32),
                        pltpu.VMEM((rows, dv), F32)],
        compiler_params=_cparams(("parallel", "arbitrary"), VMEM_LIMIT),
        name="attn_prompt",
    )(qf, kt, cb, wuv)


def _attn_sample_kernel(pt_ref, q_ref, cn_ref, krn_ref, cc_hbm, cr_hbm, wuv_ref, o_ref,
                        cbuf, rbuf, sem, *, n_pages, page, n_heads, ts, dv, dr, vh):
    b = pl.program_id(0)
    nb = pl.num_programs(0)
    slot = b % 2
    rows = n_heads * ts

    def fetch(bb, s):
        for p in range(n_pages):
            pg = pt_ref[bb, p]
            pltpu.make_async_copy(cc_hbm.at[pg], cbuf.at[s, p], sem.at[0, s]).start()
            pltpu.make_async_copy(cr_hbm.at[pg], rbuf.at[s, p], sem.at[1, s]).start(priority=1)

    def wait(s):
        pltpu.make_async_copy(cc_hbm.at[pl.ds(0, n_pages)], cbuf.at[s], sem.at[0, s]).wait()
        pltpu.make_async_copy(cr_hbm.at[pl.ds(0, n_pages)], rbuf.at[s], sem.at[1, s]).wait()

    @pl.when(b == 0)
    def _():
        fetch(0, 0)

    wait(slot)
    fetch(jnp.minimum(b + 1, nb - 1), 1 - slot)

    q = q_ref[0].astype(F32)
    qc = q[:, :dv]
    qr = q[:, dv:dv + dr]
    n_chunks = n_pages // PAGE_CHUNK
    ck = PAGE_CHUNK * page
    parts = []
    for ch in range(n_chunks):
        c_ch = cbuf[slot, ch * PAGE_CHUNK:(ch + 1) * PAGE_CHUNK].reshape(ck, dv)
        r_ch = jnp.concatenate([rbuf[slot, ch * PAGE_CHUNK + u] for u in range(PAGE_CHUNK)], axis=1)
        parts.append(_dot_nt(qc, c_ch) + _dot(qr, r_ch))
    cn = jnp.concatenate([cn_ref[...], jnp.zeros((LANE - ts, dv), F32)], axis=0)
    krn = jnp.concatenate([krn_ref[...], jnp.zeros((LANE - ts, dr), F32)], axis=0)
    row = lax.broadcasted_iota(jnp.int32, (ts, LANE), 0)
    col = lax.broadcasted_iota(jnp.int32, (ts, LANE), 1)
    s_new = (_dot_nt(qc, cn) + _dot_nt(qr, krn)).reshape(n_heads, ts, LANE)
    s_new = jnp.where((col <= row)[None], s_new, NEG).reshape(rows, LANE)

    m = jnp.max(s_new, -1, keepdims=True)
    for s in parts:
        m = jnp.maximum(m, jnp.max(s, -1, keepdims=True))
    p_new = jnp.exp2(s_new - m)
    l = jnp.sum(p_new, -1, keepdims=True)
    acc = _dot(p_new, cn)
    for ch in range(n_chunks):
        p = jnp.exp2(parts[ch] - m)
        l = l + jnp.sum(p, -1, keepdims=True)
        acc = acc + _dot(p, cbuf[slot, ch * PAGE_CHUNK:(ch + 1) * PAGE_CHUNK].reshape(ck, dv))
    o = acc / l
    for h in range(n_heads):
        oh = o[h * ts:(h + 1) * ts].astype(BF16)
        o_ref[:, h * vh:(h + 1) * vh] = _dot(oh, wuv_ref[h])

    @pl.when(b == nb - 1)
    def _():
        wait(1 - slot)


def _attn_sample(page_table, qs, c_all, kr_all, row_blk_off, cache_c, cache_rt, wuv):
    bs, rows, kw = qs.shape
    n_heads, dv, vh = wuv.shape
    ts = rows // n_heads
    n_pages = page_table.shape[1]
    page = cache_c.shape[1]
    dr = cache_rt.shape[1]
    assert n_pages % PAGE_CHUNK == 0
    kern = functools.partial(_attn_sample_kernel, n_pages=n_pages, page=page, n_heads=n_heads, ts=ts,
                             dv=dv, dr=dr, vh=vh)
    grid_spec = pltpu.PrefetchScalarGridSpec(
        num_scalar_prefetch=1,
        grid=(bs,),
        in_specs=[pl.BlockSpec((1, rows, kw), lambda b, pt: (b, 0, 0)),
                  pl.BlockSpec((ts, dv), lambda b, pt: (row_blk_off + b, 0)),
                  pl.BlockSpec((ts, dr), lambda b, pt: (row_blk_off + b, 0)),
                  pl.BlockSpec(memory_space=pl.ANY),
                  pl.BlockSpec(memory_space=pl.ANY),
                  pl.BlockSpec(wuv.shape, lambda b, pt: (0, 0, 0))],
        out_specs=pl.BlockSpec((ts, n_heads * vh), lambda b, pt: (b, 0)),
        scratch_shapes=[pltpu.VMEM((2, n_pages, page, dv), F32), pltpu.VMEM((2, n_pages, dr, page), F32),
                        pltpu.SemaphoreType.DMA((2, 2))])
    return pl.pallas_call(
        kern,
        grid_spec=grid_spec,
        out_shape=jax.ShapeDtypeStruct((bs * ts, n_heads * vh), F32),
        compiler_params=_cparams(("arbitrary",), VMEM_LIMIT),
        name="attn_sample",
    )(page_table, qs, c_all, kr_all, cache_c, cache_rt, wuv)


def _neumann_inv(low, eye, steps):
    p = eye - low
    x = low
    for _ in range(steps):
        x = _bdot(x, x)
        p = p + _bdot(p, x)
    return p


def _gdn_kernel(*refs, n_in, n_seq, chunk, n_heads, dk, dv, conv_w):
    qkv_refs, ab_refs, z_refs = refs[:n_in], refs[n_in:2 * n_in], refs[2 * n_in:3 * n_in]
    (cs_ref, s0_ref, cw_ref, alog_ref, dtb_ref, nw_ref,
     o_ref, sfin_ref, ncv_ref, xp_ref, s_ref) = refs[3 * n_in:]
    n = pl.program_id(1)
    c = chunk
    per = n_seq // n_in
    n_prob = n_seq * n_heads
    hi = lax.Precision.HIGHEST

    def seq_rows(group, g):
        k = g % per
        return group[g // per][k * c:(k + 1) * c, :]

    @pl.when(n == 0)
    def _():
        s_ref[...] = s0_ref[...].reshape(n_prob, dk, dv)
        xp_ref[:, 0:SUBLANE, :] = cs_ref[...]

    r_i = lax.broadcasted_iota(jnp.int32, (c, c), 0)
    c_i = lax.broadcasted_iota(jnp.int32, (c, c), 1)
    causal = (c_i <= r_i)[None]
    strict = (c_i < r_i)[None]
    eye_c = (r_i == c_i).astype(F32)[None]
    db = min(GDN_DIAG, c)
    n_blk = c // db
    sh = int(math.log2(db))
    same_blk = (jnp.right_shift(r_i, sh) == jnp.right_shift(c_i, sh))[None]
    r_l = lax.broadcasted_iota(jnp.int32, (LANE, LANE), 0)
    c_l = lax.broadcasted_iota(jnp.int32, (LANE, LANE), 1)
    eye_l = (r_l == c_l).astype(F32)
    tril = (c_i <= r_i).astype(F32)
    cw = cw_ref[...]
    hk = n_heads * dk

    q_l, k_l, v_l, z_l, beta_l, gcol_l, grow_l, tails = [], [], [], [], [], [], [], []
    for g in range(n_seq):
        xp_ref[g, SUBLANE:SUBLANE + c, :] = seq_rows(qkv_refs, g)
        conv = xp_ref[g, pl.ds(SUBLANE - (conv_w - 1), c), :] * cw[0:1, :]
        for j in range(1, conv_w):
            conv = conv + xp_ref[g, pl.ds(SUBLANE - (conv_w - 1) + j, c), :] * cw[j:j + 1, :]
        tail = xp_ref[g, c:c + SUBLANE, :]
        xp_ref[g, 0:SUBLANE, :] = tail
        tails.append(tail)
        act = conv * _sigmoid(conv)
        ab = seq_rows(ab_refs, g)
        apb = ab + dtb_ref[...]
        softplus = jnp.maximum(apb, 0.0) + jnp.log(1.0 + jnp.exp(-jnp.abs(apb)))
        gfull = -jnp.exp(alog_ref[...]) * softplus
        betaf = _sigmoid(ab)
        gc = _dot(tril, gfull, precision=hi)
        gc_t = _dot_nt(eye_l, gc, precision=hi)
        zg = seq_rows(z_refs, g)
        for h in range(n_heads):
            q_l.append(act[:, h * dk:(h + 1) * dk])
            k_l.append(act[:, hk + h * dk:hk + (h + 1) * dk])
            v_l.append(act[:, 2 * hk + h * dv:2 * hk + (h + 1) * dv])
            z_l.append(zg[:, h * dv:(h + 1) * dv])
            beta_l.append(betaf[:, n_heads + h:n_heads + h + 1])
            gcol_l.append(gc[:, h:h + 1])
            grow_l.append(gc_t[h:h + 1, :])

    q = jnp.stack(q_l)
    k = jnp.stack(k_l)
    v = jnp.stack(v_l)
    z = jnp.stack(z_l)
    beta = jnp.stack(beta_l)
    gcol = jnp.stack(gcol_l)
    grow = jnp.stack(grow_l)
    q = q * lax.rsqrt(jnp.sum(q * q, -1, keepdims=True) + NORM_EPS) * (dk ** -0.5)
    k = k * lax.rsqrt(jnp.sum(k * k, -1, keepdims=True) + NORM_EPS)
    decay = jnp.where(causal, jnp.exp(jnp.where(causal, gcol - grow, 0.0)), 0.0)
    kb = k * beta
    lower = jnp.where(strict, _bdot_nt(kb, k) * decay, 0.0)
    attn = _bdot_nt(q, k) * decay
    l_d = jnp.where(same_blk, lower, 0.0)
    t_inv = _neumann_inv(l_d, eye_c, int(math.log2(db)) - 1)
    if n_blk > 1:
        m_inv = _neumann_inv(_bdot(t_inv, lower - l_d), eye_c, int(math.log2(n_blk)) - 1)
        t_inv = _bdot(m_inv, t_inv)
    egc = jnp.exp(gcol)
    uw = _bdot(t_inv, jnp.concatenate([v * beta, kb * egc], axis=2))
    s = s_ref[...]
    v_new = uw[:, :, :dv] - _bdot(uw[:, :, dv:], s)
    o = _bdot(q * egc, s) + _bdot(attn, v_new)
    glast = gcol[:, c - 1:c, :]
    kdec = k * jnp.exp(glast - gcol)
    s_new = s * jnp.exp(glast) + _bdot_tn(kdec, v_new)
    s_ref[...] = s_new
    on = o * lax.rsqrt(jnp.mean(o * o, -1, keepdims=True) + NORM_EPS) * nw_ref[...] * (z * _sigmoid(z))
    for g in range(n_seq):
        for h in range(n_heads):
            o_ref[g, :, h * dv:(h + 1) * dv] = on[g * n_heads + h]

    @pl.when(n == pl.num_programs(1) - 1)
    def _():
        sfin_ref[...] = s_new.reshape(n_seq, n_heads, dk, dv)
        for g in range(n_seq):
            ncv_ref[g] = tails[g]


def _gdn(h, row_off, conv_state8, ssm_state, cw, alog_p, dtb_p, nw, *, batch, seq, chunk, group,
         qkv_blk, ab_blk, z_blk, conv_w):
    _, n_heads, dk, dv = ssm_state.shape
    nc = seq // chunk
    ch = cw.shape[1]
    hd = n_heads * dv
    contiguous = nc == 1
    n_in = 1 if contiguous else group
    assert batch % group == 0 and row_off % (group * chunk) == 0
    kern = functools.partial(_gdn_kernel, n_in=n_in, n_seq=group, chunk=chunk, n_heads=n_heads,
                             dk=dk, dv=dv, conv_w=conv_w)
    if contiguous:
        rb = group * chunk
        row_maps = [lambda i, n: row_off // rb + i]
    else:
        rb = chunk
        row_maps = [(lambda i, n, g=g: row_off // rb + (i * group + g) * nc + n) for g in range(group)]

    def specs(width, col_blk):
        return [pl.BlockSpec((rb, width), lambda i, n, r=r: (r(i, n), col_blk)) for r in row_maps]

    in_specs = specs(ch, qkv_blk) + specs(LANE, ab_blk) + specs(hd, z_blk)
    in_specs += [pl.BlockSpec((group, SUBLANE, ch), lambda i, n: (i, 0, 0)),
                 pl.BlockSpec((group, n_heads, dk, dv), lambda i, n: (i, 0, 0, 0)),
                 pl.BlockSpec(cw.shape, lambda i, n: (0, 0)),
                 pl.BlockSpec((1, LANE), lambda i, n: (0, 0)),
                 pl.BlockSpec((1, LANE), lambda i, n: (0, 0)),
                 pl.BlockSpec((1, dv), lambda i, n: (0, 0))]
    o3, sfin, ncv = pl.pallas_call(
        kern,
        grid=(batch // group, nc),
        in_specs=in_specs,
        out_specs=[pl.BlockSpec((group, chunk, hd), lambda i, n: (i, n, 0)),
                   pl.BlockSpec((group, n_heads, dk, dv), lambda i, n: (i, 0, 0, 0)),
                   pl.BlockSpec((group, SUBLANE, ch), lambda i, n: (i, 0, 0))],
        out_shape=[jax.ShapeDtypeStruct((batch, seq, hd), F32),
                   jax.ShapeDtypeStruct(ssm_state.shape, F32),
                   jax.ShapeDtypeStruct((batch, SUBLANE, ch), F32)],
        scratch_shapes=[pltpu.VMEM((group, chunk + SUBLANE, ch), F32),
                        pltpu.VMEM((group * n_heads, dk, dv), F32)],
        compiler_params=_cparams(("parallel", "arbitrary"), VMEM_LIMIT),
        name="gdn",
    )(*([h] * (3 * n_in)), conv_state8, ssm_state, cw, alog_p, dtb_p, nw)
    return o3.reshape(batch * seq, hd), sfin, ncv


def _store_row_tiles(ref, val):
    rows = val.shape[0]
    for j in range(SUBLANE):
        ref[pl.ds(j, rows, stride=SUBLANE), :] = val[:, j * LANE:(j + 1) * LANE]


def _load_row_tiles(ref, start, rows, j):
    return ref[pl.ds(start * SUBLANE + j, rows, stride=SUBLANE), :]


def _post_kernel(x_ref, omp_ref, oms_ref, ogp_ref, ogs_ref, ga_ref, gb_ref, wo_ref, g1_ref, b1_ref, wr_ref, br_ref,
                 x1_ref, x1t_ref, idx_ref, gate_ref, rank_ref, cnt_ref, carry_ref, *, alpha, top_k, n_first):
    @pl.when(pl.program_id(0) == 0)
    def _():
        carry_ref[...] = jnp.zeros(carry_ref.shape, F32)

    first = pl.program_id(0) < n_first
    om = jnp.where(first, omp_ref[...], oms_ref[...])
    og = jnp.where(first, ogp_ref[...], ogs_ref[...])
    mix = _sigmoid(ga_ref[...]) * om + _sigmoid(gb_ref[...]) * og
    y = _dot(mix.astype(BF16), wo_ref[...])
    x1 = _layer_norm(alpha * x_ref[...] + y, g1_ref[...], b1_ref[...])
    x1_ref[...] = x1
    _store_row_tiles(x1t_ref, x1)
    logits = _dot_nt(wr_ref[...], x1, precision=lax.Precision.HIGHEST) + br_ref[...]
    n_exp, tm = logits.shape
    e_i = lax.broadcasted_iota(jnp.int32, (n_exp, tm), 0)
    vals, idxs = [], []
    for _ in range(top_k):
        mx = jnp.max(logits, axis=0, keepdims=True)
        ix = jnp.min(jnp.where(logits == mx, e_i, n_exp), axis=0, keepdims=True)
        vals.append(mx)
        idxs.append(ix)
        logits = jnp.where(e_i == ix, -jnp.inf, logits)
    es = [jnp.exp(v - vals[0]) for v in vals]
    tot = es[0]
    for e in es[1:]:
        tot = tot + e
    onehots = [e_i == ix for ix in idxs]
    sel = onehots[0].astype(F32)
    for oh in onehots[1:]:
        sel = sel + oh.astype(F32)
    t_r = lax.broadcasted_iota(jnp.int32, (tm, tm), 0)
    t_c = lax.broadcasted_iota(jnp.int32, (tm, tm), 1)
    before = _dot(sel.astype(BF16), (t_r < t_c).astype(BF16)) + carry_ref[...]
    carry = carry_ref[...] + jnp.sum(sel, axis=1, keepdims=True)
    carry_ref[...] = carry
    cnt_ref[...] = carry
    r_i = lax.broadcasted_iota(jnp.int32, (SUBLANE, tm), 0)
    idx_o = jnp.zeros((SUBLANE, tm), jnp.int32)
    gate_o = jnp.zeros((SUBLANE, tm), F32)
    rank_o = jnp.zeros((SUBLANE, tm), F32)
    for k in range(top_k):
        idx_o = jnp.where(r_i == k, idxs[k], idx_o)
        gate_o = jnp.where(r_i == k, es[k] / tot, gate_o)
        rank_k = jnp.sum(jnp.where(onehots[k], before, 0.0), axis=0, keepdims=True)
        rank_o = jnp.where(r_i == k, rank_k, rank_o)
    idx_ref[...] = idx_o
    gate_ref[...] = gate_o
    rank_ref[...] = rank_o.astype(jnp.int32)


def _post(x, om_p, om_s, og_p, og_s, h, ga_blk, gb_blk, wo, g1, b1, wr_t, br, *, tm, alpha):
    m, d = x.shape
    n_exp = wr_t.shape[0]
    n_first = om_p.shape[0] // tm
    assert om_p.shape[0] % tm == 0 and om_s.shape[0] % tm == 0 and d == SUBLANE * LANE
    kern = functools.partial(_post_kernel, alpha=alpha, top_k=TOP_K, n_first=n_first)
    row = lambda i: (i, 0)
    const = lambda i: (0, 0)
    first = lambda i: (jnp.minimum(i, n_first - 1), 0)
    second = lambda i: (jnp.maximum(i - n_first, 0), 0)
    return pl.pallas_call(
        kern,
        grid=(m // tm,),
        in_specs=[pl.BlockSpec((tm, d), row),
                  pl.BlockSpec((tm, d), first), pl.BlockSpec((tm, d), second),
                  pl.BlockSpec((tm, d), first), pl.BlockSpec((tm, d), second),
                  pl.BlockSpec((tm, d), lambda i: (i, ga_blk)),
                  pl.BlockSpec((tm, d), lambda i: (i, gb_blk)),
                  pl.BlockSpec(wo.shape, const), pl.BlockSpec((1, d), const), pl.BlockSpec((1, d), const),
                  pl.BlockSpec(wr_t.shape, const), pl.BlockSpec((n_exp, 1), const)],
        out_specs=[pl.BlockSpec((tm, d), row),
                   pl.BlockSpec((tm * SUBLANE, LANE), row),
                   pl.BlockSpec((SUBLANE, tm), lambda i: (0, i)),
                   pl.BlockSpec((SUBLANE, tm), lambda i: (0, i)),
                   pl.BlockSpec((SUBLANE, tm), lambda i: (0, i)),
                   pl.BlockSpec((n_exp, 1), const)],
        out_shape=[jax.ShapeDtypeStruct((m, d), F32),
                   jax.ShapeDtypeStruct((m * SUBLANE, LANE), F32),
                   jax.ShapeDtypeStruct((SUBLANE, m), jnp.int32),
                   jax.ShapeDtypeStruct((SUBLANE, m), F32),
                   jax.ShapeDtypeStruct((SUBLANE, m), jnp.int32),
                   jax.ShapeDtypeStruct((n_exp, 1), F32)],
        scratch_shapes=[pltpu.VMEM((n_exp, 1), F32)],
        compiler_params=_cparams(("arbitrary",), VMEM_LIMIT),
        name="post_mix",
    )(x, om_p, om_s, og_p, og_s, h, h, wo, g1, b1, wr_t, br)


def _moe_kernel(be_ref, rows_ref, nused_ref, x_hbm, wgu_ref, bgu_ref, wd_ref, bd_ref, o_ref,
                xbuf, sem, wgu_bf, wd_bf, xb_ref, *, blk, d_exp):
    i = pl.program_id(0)
    n_used = nused_ref[0]
    slot = i % 2

    def row_copy(tok, s, r):
        return pltpu.make_async_copy(x_hbm.at[pl.ds(tok * SUBLANE, SUBLANE)],
                                     xbuf.at[s, pl.ds(r * SUBLANE, SUBLANE)], sem.at[s])

    def wait(s):
        pltpu.make_async_copy(x_hbm.at[pl.ds(0, blk * SUBLANE)], xbuf.at[s], sem.at[s]).wait()

    @pl.when(jnp.logical_and(i == 0, n_used > 0))
    def _():
        second = jnp.minimum(1, n_used - 1) * blk

        def body(r, carry):
            row_copy(rows_ref[r], 0, r).start()
            row_copy(rows_ref[second + r], 1, r).start()
            return carry
        lax.fori_loop(0, blk, body, 0)

    @pl.when(i < n_used)
    def _():
        e = be_ref[i]
        e_prev = be_ref[jnp.maximum(i - 1, 0)]

        @pl.when(jnp.logical_or(i == 0, e != e_prev))
        def _():
            wgu_bf[...] = wgu_ref[0].astype(BF16)
            wd_bf[...] = wd_ref[0].astype(BF16)

        wait(slot)
        for j in range(SUBLANE):
            xb_ref[:, j * LANE:(j + 1) * LANE] = _load_row_tiles(xbuf.at[slot], 0, blk, j).astype(BF16)
        nxt = jnp.minimum(i + 2, n_used - 1) * blk
        for r in range(blk):
            row_copy(rows_ref[nxt + r], slot, r).start(priority=r % 2)
        hh = _dot(xb_ref[...], wgu_bf[...]) + bgu_ref[0]
        gate = jnp.minimum(hh[:, :d_exp], SWIGLU_LIMIT)
        up = jnp.clip(hh[:, d_exp:], -SWIGLU_LIMIT, SWIGLU_LIMIT)
        act = (up + 1.0) * gate * _sigmoid(SWIGLU_ALPHA * gate)
        _store_row_tiles(o_ref, _dot(act.astype(BF16), wd_bf[...]) + bd_ref[0])

        @pl.when(i == n_used - 1)
        def _():
            wait(0)
            wait(1)

    @pl.when(i >= n_used)
    def _():
        o_ref[...] = jnp.zeros(o_ref.shape, F32)


def _moe_experts(block_e, rows, n_used, x1, w_gu, b_gu, w_down, b_down, *, blk):
    n_exp, d, d2 = w_gu.shape
    d_exp = d2 // 2
    nb = block_e.shape[0]
    kern = functools.partial(_moe_kernel, blk=blk, d_exp=d_exp)
    grid_spec = pltpu.PrefetchScalarGridSpec(
        num_scalar_prefetch=3,
        grid=(nb,),
        in_specs=[pl.BlockSpec(memory_space=pl.ANY),
                  pl.BlockSpec((1, d, d2), lambda i, be, rw, nu: (be[i], 0, 0)),
                  pl.BlockSpec((1, 1, d2), lambda i, be, rw, nu: (be[i], 0, 0)),
                  pl.BlockSpec((1, d_exp, d), lambda i, be, rw, nu: (be[i], 0, 0)),
                  pl.BlockSpec((1, 1, d), lambda i, be, rw, nu: (be[i], 0, 0))],
        out_specs=pl.BlockSpec((blk * SUBLANE, LANE), lambda i, be, rw, nu: (i, 0)),
        scratch_shapes=[pltpu.VMEM((2, blk * SUBLANE, LANE), F32), pltpu.SemaphoreType.DMA((2,)),
                        pltpu.VMEM((d, d2), BF16), pltpu.VMEM((d_exp, d), BF16),
                        pltpu.VMEM((blk, d), BF16)])
    return pl.pallas_call(
        kern,
        grid_spec=grid_spec,
        out_shape=jax.ShapeDtypeStruct((nb * blk * SUBLANE, LANE), F32),
        compiler_params=_cparams(("arbitrary",), VMEM_LIMIT),
        name="moe_experts",
    )(block_e, rows, n_used, x1, w_gu, b_gu.reshape(n_exp, 1, d2), w_down, b_down.reshape(n_exp, 1, d))


def _combine_kernel(dest_ref, x1_ref, gate_ref, ys_hbm, g2_ref, b2_ref, op_ref, os_ref, ybuf, sem,
                    *, tm, top_k, alpha, m_total, n_first):
    i = pl.program_id(0)
    nsteps = pl.num_programs(0)
    slot = i % 2
    n_rows = top_k * tm

    def row_copy(d, s, r):
        return pltpu.make_async_copy(ys_hbm.at[pl.ds(d * SUBLANE, SUBLANE)],
                                     ybuf.at[s, pl.ds(r * SUBLANE, SUBLANE)], sem.at[s])

    def wait(s):
        pltpu.make_async_copy(ys_hbm.at[pl.ds(0, n_rows * SUBLANE)], ybuf.at[s], sem.at[s]).wait()

    @pl.when(i == 0)
    def _():
        second = jnp.minimum(1, nsteps - 1) * tm

        def body(r, carry):
            k = r // tm
            t = r - k * tm
            row_copy(dest_ref[k * m_total + t], 0, r).start()
            row_copy(dest_ref[k * m_total + second + t], 1, r).start()
            return carry
        lax.fori_loop(0, n_rows, body, 0)

    wait(slot)
    g = gate_ref[...]
    parts = []
    for j in range(SUBLANE):
        acc = g[:, 0:1] * _load_row_tiles(ybuf.at[slot], 0, tm, j)
        for k in range(1, top_k):
            acc = acc + g[:, k:k + 1] * _load_row_tiles(ybuf.at[slot], k * tm, tm, j)
        parts.append(acc)
    y = jnp.concatenate(parts, axis=1)
    nxt = jnp.minimum(i + 2, nsteps - 1) * tm
    for r in range(n_rows):
        k, t = divmod(r, tm)
        row_copy(dest_ref[k * m_total + nxt + t], slot, r).start(priority=r % 2)
    res = _layer_norm(alpha * x1_ref[...] + y, g2_ref[...], b2_ref[...])

    @pl.when(i < n_first)
    def _():
        op_ref[...] = res

    @pl.when(i >= n_first)
    def _():
        os_ref[...] = res

    @pl.when(i == nsteps - 1)
    def _():
        wait(0)
        wait(1)


def _combine(dest_km, x1, gates_mk, ys, g2, b2, *, tm, alpha, m_first):
    m, d = x1.shape
    n_first = m_first // tm
    assert m_first % tm == 0 and 0 < m_first < m
    kern = functools.partial(_combine_kernel, tm=tm, top_k=TOP_K, alpha=alpha, m_total=m, n_first=n_first)
    grid_spec = pltpu.PrefetchScalarGridSpec(
        num_scalar_prefetch=1,
        grid=(m // tm,),
        in_specs=[pl.BlockSpec((tm, d), lambda i, ds: (i, 0)),
                  pl.BlockSpec((tm, SUBLANE), lambda i, ds: (i, 0)),
                  pl.BlockSpec(memory_space=pl.ANY),
                  pl.BlockSpec((1, d), lambda i, ds: (0, 0)),
                  pl.BlockSpec((1, d), lambda i, ds: (0, 0))],
        out_specs=[pl.BlockSpec((tm, d), lambda i, ds: (jnp.minimum(i, n_first - 1), 0)),
                   pl.BlockSpec((tm, d), lambda i, ds: (jnp.maximum(i - n_first, 0), 0))],
        scratch_shapes=[pltpu.VMEM((2, TOP_K * tm * SUBLANE, LANE), F32), pltpu.SemaphoreType.DMA((2,))])
    return pl.pallas_call(
        kern,
        grid_spec=grid_spec,
        out_shape=[jax.ShapeDtypeStruct((m_first, d), F32), jax.ShapeDtypeStruct((m - m_first, d), F32)],
        compiler_params=_cparams(("arbitrary",), VMEM_LIMIT),
        name="moe_combine",
    )(dest_km, x1, gates_mk, ys, g2, b2)


def _pack_w_in(w_in, splits):
    q_lora, kv_lora, rope, conv_ch, gv, nh, _, d, _ = splits
    offs = [0]
    for s in splits:
        offs.append(offs[-1] + s)
    part = [w_in[:, offs[i]:offs[i + 1]] for i in range(len(splits))]
    q_lat, kv_lat, k_r, qkv, z, a, b, g_a, g_b = part
    dm = w_in.shape[0]
    half = rope // 2
    zpad = lambda n: jnp.zeros((dm, n), w_in.dtype)
    k_sw = jnp.concatenate([k_r[:, half:], k_r[:, :half]], axis=1)
    small = jnp.concatenate([q_lat, kv_lat, k_r, zpad(LANE - rope), k_sw, zpad(LANE - rope),
                             a, b, zpad(LANE - 2 * nh)], axis=1)
    return jnp.concatenate([qkv, small, z, g_a, g_b], axis=1).astype(BF16)


def _pack_w_uq(w_uq, nope, rope):
    w = jnp.transpose(w_uq, (1, 0, 2))
    half = rope // 2
    r = w[..., nope:]
    zp = jnp.zeros(r.shape[:-1] + (LANE - rope,), w.dtype)
    r_sw = jnp.concatenate([r[..., half:], r[..., :half]], axis=-1)
    return jnp.concatenate([w[..., :nope], r, zp, r_sw, zp], axis=-1).astype(BF16)


def _rope_tables(pos, rope):
    half = rope // 2
    inv = ROPE_THETA ** (-jnp.arange(half, dtype=F32) / half)
    ang = pos.astype(F32)[:, None] * inv[None, :]
    cos, sin = jnp.cos(ang), jnp.sin(ang)
    zp = jnp.zeros((pos.shape[0], LANE - rope), F32)
    return (jnp.concatenate([cos, cos, zp], axis=1), jnp.concatenate([-sin, sin, zp], axis=1))


def _route_meta(idx_t, rank_t, counts, m, n_exp, blk):
    a = m * TOP_K
    counts = counts.astype(jnp.int32)
    padded = (counts + blk - 1) // blk * blk
    pad_end = jnp.cumsum(padded)
    pad_start = pad_end - padded
    experts = jnp.arange(n_exp, dtype=jnp.int32)
    e_km = idx_t[:TOP_K]
    start_km = jnp.sum(jnp.where(e_km[:, :, None] == experts, pad_start, 0), axis=-1)
    dest_km = (start_km + rank_t[:TOP_K]).astype(jnp.int32).reshape(a)
    nb = a // blk + n_exp
    tok_km = jnp.tile(jnp.arange(m, dtype=jnp.int32), TOP_K)
    filler = jnp.arange(nb * blk, dtype=jnp.int32) % m
    rows = filler.at[dest_km].set(tok_km, unique_indices=True, mode='promise_in_bounds')
    first_row = jnp.arange(nb, dtype=jnp.int32) * blk
    block_e = jnp.minimum(jnp.sum((pad_end[None, :] <= first_row[:, None]).astype(jnp.int32), axis=1),
                          n_exp - 1).astype(jnp.int32)
    n_used = (pad_end[-1] // blk).astype(jnp.int32).reshape(1)
    return block_e, rows, n_used, dest_km


def kernel(x_prompt, x_sample, cache_ckv, cache_krope, page_table, state_conv, state_ssm, w_in, q_norm_w, kv_norm_w, w_uq, w_uk, w_uv, conv_w, a_log, dt_bias, gdn_norm_w, w_o, ln1_g, ln1_b, w_router, b_router, w_gu, b_gu, w_down, b_down, ln2_g, ln2_b):
    bp, tp, d = x_prompt.shape
    bs, ts, _ = x_sample.shape
    depth = w_in.shape[0]
    q_lora, n_heads, qk = w_uq.shape[1:]
    kv_lora, _, nope = w_uk.shape[1:]
    rope = qk - nope
    vh = w_uv.shape[3]
    cw_taps, conv_ch = conv_w.shape[1:]
    g_heads = a_log.shape[1]
    dk, dv = state_ssm.shape[3:]
    n_exp = w_router.shape[2]
    page = cache_ckv.shape[2]
    past = page_table.shape[1] * page
    splits = (q_lora, kv_lora, rope, conv_ch, g_heads * dv, g_heads, g_heads, d, d)
    assert sum(splits) == w_in.shape[2]
    alpha = (2 * depth) ** 0.25
    scale = (nope + rope) ** -0.5 * LOG2E
    mp, ms = bp * tp, bs * ts
    m = mp + ms
    small_w = q_lora + kv_lora + 3 * LANE
    assert small_w == d and conv_ch % d == 0
    qkv_blk, small_blk = 0, conv_ch // d
    z_blk, ga_blk, gb_blk = small_blk + 1, small_blk + 2, small_blk + 3
    ab_blk = (conv_ch + q_lora + kv_lora + 2 * LANE) // LANE

    cos_p, sin_p = _rope_tables(jnp.arange(tp, dtype=jnp.int32), rope)
    cos_s, sin_s = _rope_tables(past + jnp.arange(ts, dtype=jnp.int32), rope)
    cos_t = jnp.concatenate([jnp.tile(cos_p, (bp, 1)), jnp.tile(cos_s, (bs, 1))], axis=0)
    sin_t = jnp.concatenate([jnp.tile(sin_p, (bp, 1)), jnp.tile(sin_s, (bs, 1))], axis=0)

    x_p, x_s = x_prompt.reshape(mp, d), x_sample.reshape(ms, d)
    outs = {k: [] for k in ("ckv_p", "kr_p", "conv_p", "ssm_p", "ckv_s", "kr_s", "conv_s", "ssm_s")}
    pad_lanes = lambda v: jnp.pad(v, (0, LANE - v.shape[0])).reshape(1, LANE)
    for l in range(depth):
        x = jnp.concatenate([x_p, x_s], axis=0)
        w_pack = _pack_w_in(w_in[l], splits)
        wq = _pack_w_uq(w_uq[l], nope, rope)
        wuk = jnp.transpose(w_uk[l], (1, 2, 0)).astype(BF16)
        wuv = jnp.transpose(w_uv[l], (1, 0, 2)).astype(BF16)
        h = _in_proj(x, w_pack, tm=_row_tile(m, 2304), tn=512)
        qf, c_all, kr_all, kt, cb = _mla_proj(
            h, small_blk, cos_t, sin_t, q_norm_w[l].reshape(1, q_lora), kv_norm_w[l].reshape(1, kv_lora),
            wq, wuk, tm=256, q_lora=q_lora, kv_lora=kv_lora, nope=nope, rope=rope, scale=scale)
        om_p = _attn_prompt(qf, kt, cb, wuv, batch=bp, seq=tp, tq=128, tk=min(1024, tp))
        qs = qf[:, mp:, :].reshape(n_heads, bs, ts, qf.shape[-1])
        qs = jnp.transpose(qs, (1, 0, 2, 3)).reshape(bs, n_heads * ts, qf.shape[-1])
        om_s = _attn_sample(page_table, qs, c_all, kr_all, mp // ts, cache_ckv[l],
                            jnp.swapaxes(cache_krope[l], 1, 2), wuv)

        alog_p = pad_lanes(a_log[l])
        dtb_p = pad_lanes(dt_bias[l])
        nw = gdn_norm_w[l].reshape(1, dv)
        gdn_kw = dict(qkv_blk=qkv_blk, ab_blk=ab_blk, z_blk=z_blk, conv_w=cw_taps)
        zeros_conv = jnp.zeros((bp, SUBLANE, conv_ch), F32)
        zeros_ssm = jnp.zeros((bp, g_heads, dk, dv), F32)
        og_p, ssm_p, ncv_p = _gdn(h, 0, zeros_conv, zeros_ssm, conv_w[l], alog_p, dtb_p, nw,
                                  batch=bp, seq=tp, chunk=min(GDN_CHUNK, tp), group=bp, **gdn_kw)
        conv8_s = jnp.pad(state_conv[l], ((0, 0), (SUBLANE - (cw_taps - 1), 0), (0, 0)))
        og_s, ssm_s, ncv_s = _gdn(h, mp, conv8_s, state_ssm[l], conv_w[l], alog_p, dtb_p, nw,
                                  batch=bs, seq=ts, chunk=ts, group=math.gcd(bs, GDN_SAMPLE_GROUP), **gdn_kw)

        x1, x1t, idx_t, gate_t, rank_t, counts = _post(
            x, om_p, om_s, og_p, og_s, h, ga_blk, gb_blk, w_o[l].astype(BF16), ln1_g[l].reshape(1, d), ln1_b[l].reshape(1, d),
            w_router[l].T, b_router[l].reshape(n_exp, 1), tm=256, alpha=alpha)
        block_e, rows, n_used, dest_km = _route_meta(idx_t, rank_t, counts[:, 0], m, n_exp, MOE_ROWS)
        ys = _moe_experts(block_e, rows, n_used, x1t, w_gu[l], b_gu[l], w_down[l], b_down[l], blk=MOE_ROWS)
        x_p, x_s = _combine(dest_km, x1, gate_t.T, ys, ln2_g[l].reshape(1, d), ln2_b[l].reshape(1, d),
                            tm=128, alpha=alpha, m_first=mp)

        outs["ckv_p"].append(c_all[:mp].reshape(bp, tp, kv_lora))
        outs["kr_p"].append(kr_all[:mp].reshape(bp, tp, rope))
        outs["conv_p"].append(ncv_p[:, SUBLANE - (cw_taps - 1):, :])
        outs["ssm_p"].append(ssm_p)
        outs["ckv_s"].append(c_all[mp:].reshape(bs, ts, kv_lora))
        outs["kr_s"].append(kr_all[mp:].reshape(bs, ts, rope))
        outs["conv_s"].append(ncv_s[:, SUBLANE - (cw_taps - 1):, :])
        outs["ssm_s"].append(ssm_s)

    return (x_p.reshape(bp, tp, d), x_s.reshape(bs, ts, d),
            jnp.stack(outs["ckv_p"]), jnp.stack(outs["kr_p"]), jnp.stack(outs["conv_p"]), jnp.stack(outs["ssm_p"]),
            jnp.stack(outs["ckv_s"]), jnp.stack(outs["kr_s"]), jnp.stack(outs["conv_s"]), jnp.stack(outs["ssm_s"]))
```

```python
import functools
import math

import jax
import jax.numpy as jnp
from jax import lax
from jax.experimental import pallas as pl
from jax.experimental.pallas import tpu as pltpu

F32 = jnp.float32
BF16 = jnp.bfloat16

ROPE_THETA = 10000.0
NORM_EPS = 1e-6
TOP_K = 4
SWIGLU_LIMIT = 7.0
SWIGLU_ALPHA = 1.702
GDN_CHUNK = 128
GDN_DIAG = 16
GDN_SAMPLE_GROUP = 8
MOE_ROWS = 512
PAGE_CHUNK = 8
DIAG_VARIANTS = 4
LANE = 128
SUBLANE = 8
VMEM_LIMIT = 56 * 1024 * 1024
NEG = -1e30
LOG2E = 1.4426950408889634


def _cparams(sem, vmem=None):
    return pltpu.CompilerParams(dimension_semantics=sem, vmem_limit_bytes=vmem)


def _dot(a, b, **kw):
    return jnp.dot(a, b, preferred_element_type=F32, **kw)


def _dot_nt(a, b, **kw):
    return lax.dot_general(a, b, (((1,), (1,)), ((), ())), preferred_element_type=F32, **kw)


def _bdot(a, b):
    return lax.dot_general(a, b, (((2,), (1,)), ((0,), (0,))), preferred_element_type=F32)


def _bdot_nt(a, b):
    return lax.dot_general(a, b, (((2,), (2,)), ((0,), (0,))), preferred_element_type=F32)


def _bdot_tn(a, b):
    return lax.dot_general(a, b, (((1,), (1,)), ((0,), (0,))), preferred_element_type=F32)


def _sigmoid(x):
    return 1.0 / (1.0 + jnp.exp(-x))


def _layer_norm(v, g, b):
    mu = jnp.mean(v, -1, keepdims=True)
    vc = v - mu
    var = jnp.mean(vc * vc, -1, keepdims=True)
    return vc * lax.rsqrt(var + NORM_EPS) * g + b


def _inproj_kernel(x_ref, w_ref, o_ref, xb_ref):
    @pl.when(pl.program_id(1) == 0)
    def _():
        xb_ref[...] = x_ref[...].astype(BF16)

    o_ref[...] = _dot(xb_ref[...], w_ref[...])


def _row_tile(m, target):
    return max(t for t in range(SUBLANE, target + 1, SUBLANE) if m % t == 0)


def _in_proj(x, w, tm, tn):
    m, k = x.shape
    n = w.shape[1]
    return pl.pallas_call(
        _inproj_kernel,
        grid=(m // tm, n // tn),
        in_specs=[pl.BlockSpec((tm, k), lambda i, j: (i, 0)),
                  pl.BlockSpec((k, tn), lambda i, j: (0, j))],
        out_specs=pl.BlockSpec((tm, tn), lambda i, j: (i, j)),
        out_shape=jax.ShapeDtypeStruct((m, n), F32),
        scratch_shapes=[pltpu.VMEM((tm, k), BF16)],
        compiler_params=_cparams(("parallel", "arbitrary"), VMEM_LIMIT),
        name="in_proj",
    )(x, w)


def _mla_proj_kernel(h_ref, cos_ref, sin_ref, qnw_ref, kvnw_ref, wq_ref, wuk_ref,
                     qf_ref, c_ref, kr_ref, kt_ref, cb_ref, *, n_heads, q_lora, kv_lora, nope, rope, scale):
    hs = h_ref[...]
    cos = cos_ref[...]
    sin = sin_ref[...]
    q_lat = hs[:, :q_lora]
    qn = q_lat * lax.rsqrt(jnp.mean(q_lat * q_lat, -1, keepdims=True) + NORM_EPS) * qnw_ref[...]
    qn = qn.astype(BF16)
    kv = hs[:, q_lora:q_lora + kv_lora]
    c = kv * lax.rsqrt(jnp.mean(kv * kv, -1, keepdims=True) + NORM_EPS) * kvnw_ref[...]
    o = q_lora + kv_lora
    kr = hs[:, o:o + LANE] * cos + hs[:, o + LANE:o + 2 * LANE] * sin
    c_ref[...] = c
    kr_ref[...] = kr[:, :rope]
    cb = c.astype(BF16)
    cb_ref[...] = cb
    kfull = jnp.concatenate([cb, kr.astype(BF16)], axis=1)
    kw = kfull.shape[1]
    eye = (lax.broadcasted_iota(jnp.int32, (kw, kw), 0) == lax.broadcasted_iota(jnp.int32, (kw, kw), 1))
    kt_ref[...] = _dot_nt(eye.astype(BF16), kfull).astype(BF16)
    for h in range(n_heads):
        qh = _dot(qn, wq_ref[h])
        qa = _dot(qh[:, :nope].astype(BF16), wuk_ref[h])
        qr = qh[:, nope:nope + LANE] * cos + qh[:, nope + LANE:nope + 2 * LANE] * sin
        qf_ref[h, :, :kv_lora] = (qa * scale).astype(BF16)
        qf_ref[h, :, kv_lora:] = (qr * scale).astype(BF16)


def _mla_proj(h, col_blk, cos_t, sin_t, qnw, kvnw, wq, wuk, *, tm, q_lora, kv_lora, nope, rope, scale):
    m = h.shape[0]
    n_heads = wq.shape[0]
    wcol = q_lora + kv_lora + 3 * LANE
    kw = kv_lora + LANE
    kern = functools.partial(_mla_proj_kernel, n_heads=n_heads, q_lora=q_lora, kv_lora=kv_lora,
                             nope=nope, rope=rope, scale=scale)
    return pl.pallas_call(
        kern,
        grid=(m // tm,),
        in_specs=[pl.BlockSpec((tm, wcol), lambda i: (i, col_blk)),
                  pl.BlockSpec((tm, LANE), lambda i: (i, 0)),
                  pl.BlockSpec((tm, LANE), lambda i: (i, 0)),
                  pl.BlockSpec((1, q_lora), lambda i: (0, 0)),
                  pl.BlockSpec((1, kv_lora), lambda i: (0, 0)),
                  pl.BlockSpec(wq.shape, lambda i: (0, 0, 0)),
                  pl.BlockSpec(wuk.shape, lambda i: (0, 0, 0))],
        out_specs=[pl.BlockSpec((n_heads, tm, kw), lambda i: (0, i, 0)),
                   pl.BlockSpec((tm, kv_lora), lambda i: (i, 0)),
                   pl.BlockSpec((tm, rope), lambda i: (i, 0)),
                   pl.BlockSpec((kw, tm), lambda i: (0, i)),
                   pl.BlockSpec((tm, kv_lora), lambda i: (i, 0))],
        out_shape=[jax.ShapeDtypeStruct((n_heads, m, kw), BF16),
                   jax.ShapeDtypeStruct((m, kv_lora), F32),
                   jax.ShapeDtypeStruct((m, rope), F32),
                   jax.ShapeDtypeStruct((kw, m), BF16),
                   jax.ShapeDtypeStruct((m, kv_lora), BF16)],
        compiler_params=_cparams(("parallel",), VMEM_LIMIT),
        name="mla_proj",
    )(h, cos_t, sin_t, qnw, kvnw, wq, wuk)


def _softmax_step(s, v, m_ref, l_ref, acc_ref):
    m_prev = m_ref[...]
    m_new = jnp.maximum(m_prev, jnp.max(s, -1, keepdims=True))
    alpha = jnp.exp2(m_prev - m_new)
    p = jnp.exp2(s - m_new)
    l_ref[...] = alpha * l_ref[...] + jnp.sum(p, -1, keepdims=True)
    acc_ref[...] = alpha * acc_ref[...] + _dot(p.astype(BF16), v)
    m_ref[...] = m_new


def _attn_prompt_kernel(qf_ref, kt_ref, v_ref, wuv_ref, o_ref, m_ref, l_ref, acc_ref, *, n_heads, tq, tk, dv, vh, n_split):
    i = pl.program_id(1)
    m_ref[...] = jnp.full(m_ref.shape, NEG, F32)
    l_ref[...] = jnp.zeros(l_ref.shape, F32)
    acc_ref[...] = jnp.zeros(acc_ref.shape, F32)
    n_full = (i * tq) // tk
    off = i * tq - n_full * tk
    hs = n_heads // n_split
    rs = hs * tq

    def step(j, width, masked):
        start = pl.multiple_of(j * tk, tk)
        kt = kt_ref[:, pl.ds(start, width)]
        v = v_ref[pl.ds(start, width), :]
        for g in range(n_split):
            q = qf_ref[g * hs:(g + 1) * hs].reshape(rs, qf_ref.shape[-1])
            s = _dot(q, kt)
            if masked:
                row = lax.broadcasted_iota(jnp.int32, (tq, width), 0)
                col = lax.broadcasted_iota(jnp.int32, (tq, width), 1)
                s = jnp.where((col <= row + off)[None], s.reshape(hs, tq, width), NEG).reshape(rs, width)
            sl = slice(g * rs, (g + 1) * rs)
            _softmax_step(s, v, m_ref.at[sl], l_ref.at[sl], acc_ref.at[sl])

    def body(j, carry):
        step(j, tk, False)
        return carry

    lax.fori_loop(0, n_full, body, 0)
    need = off + tq
    widths = [tk * (v + 1) // DIAG_VARIANTS for v in range(DIAG_VARIANTS)]
    for v, width in enumerate(widths):
        lo = widths[v - 1] if v else 0

        @pl.when(jnp.logical_and(need > lo, need <= width))
        def _(width=width):
            step(n_full, width, True)

    o = acc_ref[...] / l_ref[...]
    for h in range(n_heads):
        oh = o[h * tq:(h + 1) * tq].astype(BF16)
        o_ref[:, h * vh:(h + 1) * vh] = _dot(oh, wuv_ref[h])


def _attn_prompt(qf, kt, cb, wuv, *, batch, seq, tq, tk):
    n_heads, _, kw = qf.shape
    dv, vh = wuv.shape[1], wuv.shape[2]
    nq = seq // tq
    rows = n_heads * tq
    assert seq % tk == 0 and tk % tq == 0 and tk % (DIAG_VARIANTS * LANE) == 0
    kern = functools.partial(_attn_prompt_kernel, n_heads=n_heads, tq=tq, tk=tk, dv=dv, vh=vh, n_split=2)
    return pl.pallas_call(
        kern,
        grid=(batch, nq),
        in_specs=[pl.BlockSpec((n_heads, tq, kw), lambda b, i: (0, b * nq + i, 0)),
                  pl.BlockSpec((kw, seq), lambda b, i: (0, b)),
                  pl.BlockSpec((seq, dv), lambda b, i: (b, 0)),
                  pl.BlockSpec(wuv.shape, lambda b, i: (0, 0, 0))],
        out_specs=pl.BlockSpec((tq, n_heads * vh), lambda b, i: (b * nq + i, 0)),
        out_shape=jax.ShapeDtypeStruct((batch * seq, n_heads * vh), F32),
        scratch_shapes=[pltpu.VMEM((rows, 1), F32), pltpu.VMEM((rows, 1), F32),
                        pltpu.VMEM((rows, dv), F32)],
        compiler_params=_cparams(("parallel", "arbitrary"), VMEM_LIMIT),
        name="attn_prompt",
    )(qf, kt, cb, wuv)


def _attn_sample_kernel(pt_ref, q_ref, cn_ref, krn_ref, cc_hbm, cr_hbm, wuv_ref, o_ref,
                        cbuf, rbuf, sem, *, n_pages, page, n_heads, ts, dv, dr, vh):
    b = pl.program_id(0)
    nb = pl.num_programs(0)
    slot = b % 2
    rows = n_heads * ts

    def fetch(bb, s):
        for p in range(n_pages):
            pg = pt_ref[bb, p]
            pltpu.make_async_copy(cc_hbm.at[pg], cbuf.at[s, p], sem.at[0, s]).start()
            pltpu.make_async_copy(cr_hbm.at[pg], rbuf.at[s, p], sem.at[1, s]).start(priority=1)

    def wait(s):
        pltpu.make_async_copy(cc_hbm.at[pl.ds(0, n_pages)], cbuf.at[s], sem.at[0, s]).wait()
        pltpu.make_async_copy(cr_hbm.at[pl.ds(0, n_pages)], rbuf.at[s], sem.at[1, s]).wait()

    @pl.when(b == 0)
    def _():
        fetch(0, 0)

    wait(slot)
    fetch(jnp.minimum(b + 1, nb - 1), 1 - slot)

    q = q_ref[0].astype(F32)
    qc = q[:, :dv]
    qr = q[:, dv:dv + dr]
    n_chunks = n_pages // PAGE_CHUNK
    ck = PAGE_CHUNK * page
    parts = []
    for ch in range(n_chunks):
        c_ch = cbuf[slot, ch * PAGE_CHUNK:(ch + 1) * PAGE_CHUNK].reshape(ck, dv)
        r_ch = jnp.concatenate([rbuf[slot, ch * PAGE_CHUNK + u] for u in range(PAGE_CHUNK)], axis=1)
        parts.append(_dot_nt(qc, c_ch) + _dot(qr, r_ch))
    cn = jnp.concatenate([cn_ref[...], jnp.zeros((LANE - ts, dv), F32)], axis=0)
    krn = jnp.concatenate([krn_ref[...], jnp.zeros((LANE - ts, dr), F32)], axis=0)
    row = lax.broadcasted_iota(jnp.int32, (ts, LANE), 0)
    col = lax.broadcasted_iota(jnp.int32, (ts, LANE), 1)
    s_new = (_dot_nt(qc, cn) + _dot_nt(qr, krn)).reshape(n_heads, ts, LANE)
    s_new = jnp.where((col <= row)[None], s_new, NEG).reshape(rows, LANE)

    m = jnp.max(s_new, -1, keepdims=True)
    for s in parts:
        m = jnp.maximum(m, jnp.max(s, -1, keepdims=True))
    p_new = jnp.exp2(s_new - m)
    l = jnp.sum(p_new, -1, keepdims=True)
    acc = _dot(p_new, cn)
    for ch in range(n_chunks):
        p = jnp.exp2(parts[ch] - m)
        l = l + jnp.sum(p, -1, keepdims=True)
        acc = acc + _dot(p, cbuf[slot, ch * PAGE_CHUNK:(ch + 1) * PAGE_CHUNK].reshape(ck, dv))
    o = acc / l
    for h in range(n_heads):
        oh = o[h * ts:(h + 1) * ts].astype(BF16)
        o_ref[:, h * vh:(h + 1) * vh] = _dot(oh, wuv_ref[h])

    @pl.when(b == nb - 1)
    def _():
        wait(1 - slot)


def _attn_sample(page_table, qs, c_all, kr_all, row_blk_off, cache_c, cache_rt, wuv):
    bs, rows, kw = qs.shape
    n_heads, dv, vh = wuv.shape
    ts = rows // n_heads
    n_pages = page_table.shape[1]
    page = cache_c.shape[1]
    dr = cache_rt.shape[1]
    assert n_pages % PAGE_CHUNK == 0
    kern = functools.partial(_attn_sample_kernel, n_pages=n_pages, page=page, n_heads=n_heads, ts=ts,
                             dv=dv, dr=dr, vh=vh)
    grid_spec = pltpu.PrefetchScalarGridSpec(
        num_scalar_prefetch=1,
        grid=(bs,),
        in_specs=[pl.BlockSpec((1, rows, kw), lambda b, pt: (b, 0, 0)),
                  pl.BlockSpec((ts, dv), lambda b, pt: (row_blk_off + b, 0)),
                  pl.BlockSpec((ts, dr), lambda b, pt: (row_blk_off + b, 0)),
                  pl.BlockSpec(memory_space=pl.ANY),
                  pl.BlockSpec(memory_space=pl.ANY),
                  pl.BlockSpec(wuv.shape, lambda b, pt: (0, 0, 0))],
        out_specs=pl.BlockSpec((ts, n_heads * vh), lambda b, pt: (b, 0)),
        scratch_shapes=[pltpu.VMEM((2, n_pages, page, dv), F32), pltpu.VMEM((2, n_pages, dr, page), F32),
                        pltpu.SemaphoreType.DMA((2, 2))])
    return pl.pallas_call(
        kern,
        grid_spec=grid_spec,
        out_shape=jax.ShapeDtypeStruct((bs * ts, n_heads * vh), F32),
        compiler_params=_cparams(("arbitrary",), VMEM_LIMIT),
        name="attn_sample",
    )(page_table, qs, c_all, kr_all, cache_c, cache_rt, wuv)


def _neumann_inv(low, eye, steps):
    p = eye - low
    x = low
    for _ in range(steps):
        x = _bdot(x, x)
        p = p + _bdot(p, x)
    return p


def _gdn_kernel(*refs, n_in, n_seq, chunk, n_heads, dk, dv, conv_w):
    qkv_refs, ab_refs, z_refs = refs[:n_in], refs[n_in:2 * n_in], refs[2 * n_in:3 * n_in]
    (cs_ref, s0_ref, cw_ref, alog_ref, dtb_ref, nw_ref,
     o_ref, sfin_ref, ncv_ref, xp_ref, s_ref) = refs[3 * n_in:]
    n = pl.program_id(1)
    c = chunk
    per = n_seq // n_in
    n_prob = n_seq * n_heads
    hi = lax.Precision.HIGHEST

    def seq_rows(group, g):
        k = g % per
        return group[g // per][k * c:(k + 1) * c, :]

    @pl.when(n == 0)
    def _():
        s_ref[...] = s0_ref[...].reshape(n_prob, dk, dv)
        xp_ref[:, 0:SUBLANE, :] = cs_ref[...]

    r_i = lax.broadcasted_iota(jnp.int32, (c, c), 0)
    c_i = lax.broadcasted_iota(jnp.int32, (c, c), 1)
    causal = (c_i <= r_i)[None]
    strict = (c_i < r_i)[None]
    eye_c = (r_i == c_i).astype(F32)[None]
    db = min(GDN_DIAG, c)
    n_blk = c // db
    sh = int(math.log2(db))
    same_blk = (jnp.right_shift(r_i, sh) == jnp.right_shift(c_i, sh))[None]
    r_l = lax.broadcasted_iota(jnp.int32, (LANE, LANE), 0)
    c_l = lax.broadcasted_iota(jnp.int32, (LANE, LANE), 1)
    eye_l = (r_l == c_l).astype(F32)
    tril = (c_i <= r_i).astype(F32)
    cw = cw_ref[...]
    hk = n_heads * dk

    q_l, k_l, v_l, z_l, beta_l, gcol_l, grow_l, tails = [], [], [], [], [], [], [], []
    for g in range(n_seq):
        xp_ref[g, SUBLANE:SUBLANE + c, :] = seq_rows(qkv_refs, g)
        conv = xp_ref[g, pl.ds(SUBLANE - (conv_w - 1), c), :] * cw[0:1, :]
        for j in range(1, conv_w):
            conv = conv + xp_ref[g, pl.ds(SUBLANE - (conv_w - 1) + j, c), :] * cw[j:j + 1, :]
        tail = xp_ref[g, c:c + SUBLANE, :]
        xp_ref[g, 0:SUBLANE, :] = tail
        tails.append(tail)
        act = conv * _sigmoid(conv)
        ab = seq_rows(ab_refs, g)
        apb = ab + dtb_ref[...]
        softplus = jnp.maximum(apb, 0.0) + jnp.log(1.0 + jnp.exp(-jnp.abs(apb)))
        gfull = -jnp.exp(alog_ref[...]) * softplus
        betaf = _sigmoid(ab)
        gc = _dot(tril, gfull, precision=hi)
        gc_t = _dot_nt(eye_l, gc, precision=hi)
        zg = seq_rows(z_refs, g)
        for h in range(n_heads):
            q_l.append(act[:, h * dk:(h + 1) * dk])
            k_l.append(act[:, hk + h * dk:hk + (h + 1) * dk])
            v_l.append(act[:, 2 * hk + h * dv:2 * hk + (h + 1) * dv])
            z_l.append(zg[:, h * dv:(h + 1) * dv])
            beta_l.append(betaf[:, n_heads + h:n_heads + h + 1])
            gcol_l.append(gc[:, h:h + 1])
            grow_l.append(gc_t[h:h + 1, :])

    q = jnp.stack(q_l)
    k = jnp.stack(k_l)
    v = jnp.stack(v_l)
    z = jnp.stack(z_l)
    beta = jnp.stack(beta_l)
    gcol = jnp.stack(gcol_l)
    grow = jnp.stack(grow_l)
    q = q * lax.rsqrt(jnp.sum(q * q, -1, keepdims=True) + NORM_EPS) * (dk ** -0.5)
    k = k * lax.rsqrt(jnp.sum(k * k, -1, keepdims=True) + NORM_EPS)
    decay = jnp.where(causal, jnp.exp(jnp.where(causal, gcol - grow, 0.0)), 0.0)
    kb = k * beta
    lower = jnp.where(strict, _bdot_nt(kb, k) * decay, 0.0)
    attn = _bdot_nt(q, k) * decay
    l_d = jnp.where(same_blk, lower, 0.0)
    t_inv = _neumann_inv(l_d, eye_c, int(math.log2(db)) - 1)
    if n_blk > 1:
        m_inv = _neumann_inv(_bdot(t_inv, lower - l_d), eye_c, int(math.log2(n_blk)) - 1)
        t_inv = _bdot(m_inv, t_inv)
    egc = jnp.exp(gcol)
    uw = _bdot(t_inv, jnp.concatenate([v * beta, kb * egc], axis=2))
    s = s_ref[...]
    v_new = uw[:, :, :dv] - _bdot(uw[:, :, dv:], s)
    o = _bdot(q * egc, s) + _bdot(attn, v_new)
    glast = gcol[:, c - 1:c, :]
    kdec = k * jnp.exp(glast - gcol)
    s_new = s * jnp.exp(glast) + _bdot_tn(kdec, v_new)
    s_ref[...] = s_new
    on = o * lax.rsqrt(jnp.mean(o * o, -1, keepdims=True) + NORM_EPS) * nw_ref[...] * (z * _sigmoid(z))
    for g in range(n_seq):
        for h in range(n_heads):
            o_ref[g, :, h * dv:(h + 1) * dv] = on[g * n_heads + h]

    @pl.when(n == pl.num_programs(1) - 1)
    def _():
        sfin_ref[...] = s_new.reshape(n_seq, n_heads, dk, dv)
        for g in range(n_seq):
            ncv_ref[g] = tails[g]


def _gdn(h, row_off, conv_state8, ssm_state, cw, alog_p, dtb_p, nw, *, batch, seq, chunk, group,
         qkv_blk, ab_blk, z_blk, conv_w):
    _, n_heads, dk, dv = ssm_state.shape
    nc = seq // chunk
    ch = cw.shape[1]
    hd = n_heads * dv
    contiguous = nc == 1
    n_in = 1 if contiguous else group
    assert batch % group == 0 and row_off % (group * chunk) == 0
    kern = functools.partial(_gdn_kernel, n_in=n_in, n_seq=group, chunk=chunk, n_heads=n_heads,
                             dk=dk, dv=dv, conv_w=conv_w)
    if contiguous:
        rb = group * chunk
        row_maps = [lambda i, n: row_off // rb + i]
    else:
        rb = chunk
        row_maps = [(lambda i, n, g=g: row_off // rb + (i * group + g) * nc + n) for g in range(group)]

    def specs(width, col_blk):
        return [pl.BlockSpec((rb, width), lambda i, n, r=r: (r(i, n), col_blk)) for r in row_maps]

    in_specs = specs(ch, qkv_blk) + specs(LANE, ab_blk) + specs(hd, z_blk)
    in_specs += [pl.BlockSpec((group, SUBLANE, ch), lambda i, n: (i, 0, 0)),
                 pl.BlockSpec((group, n_heads, dk, dv), lambda i, n: (i, 0, 0, 0)),
                 pl.BlockSpec(cw.shape, lambda i, n: (0, 0)),
                 pl.BlockSpec((1, LANE), lambda i, n: (0, 0)),
                 pl.BlockSpec((1, LANE), lambda i, n: (0, 0)),
                 pl.BlockSpec((1, dv), lambda i, n: (0, 0))]
    o3, sfin, ncv = pl.pallas_call(
        kern,
        grid=(batch // group, nc),
        in_specs=in_specs,
        out_specs=[pl.BlockSpec((group, chunk, hd), lambda i, n: (i, n, 0)),
                   pl.BlockSpec((group, n_heads, dk, dv), lambda i, n: (i, 0, 0, 0)),
                   pl.BlockSpec((group, SUBLANE, ch), lambda i, n: (i, 0, 0))],
        out_shape=[jax.ShapeDtypeStruct((batch, seq, hd), F32),
                   jax.ShapeDtypeStruct(ssm_state.shape, F32),
                   jax.ShapeDtypeStruct((batch, SUBLANE, ch), F32)],
        scratch_shapes=[pltpu.VMEM((group, chunk + SUBLANE, ch), F32),
                        pltpu.VMEM((group * n_heads, dk, dv), F32)],
        compiler_params=_cparams(("parallel", "arbitrary"), VMEM_LIMIT),
        name="gdn",
    )(*([h] * (3 * n_in)), conv_state8, ssm_state, cw, alog_p, dtb_p, nw)
    return o3.reshape(batch * seq, hd), sfin, ncv


def _store_row_tiles(ref, val):
    rows = val.shape[0]
    for j in range(SUBLANE):
        ref[pl.ds(j, rows, stride=SUBLANE), :] = val[:, j * LANE:(j + 1) * LANE]


def _load_row_tiles(ref, start, rows, j):
    return ref[pl.ds(start * SUBLANE + j, rows, stride=SUBLANE), :]


def _post_kernel(x_ref, omp_ref, oms_ref, ogp_ref, ogs_ref, ga_ref, gb_ref, wo_ref, g1_ref, b1_ref, wr_ref, br_ref,
                 x1_ref, x1t_ref, idx_ref, gate_ref, rank_ref, cnt_ref, carry_ref, *, alpha, top_k, n_first):
    @pl.when(pl.program_id(0) == 0)
    def _():
        carry_ref[...] = jnp.zeros(carry_ref.shape, F32)

    first = pl.program_id(0) < n_first
    om = jnp.where(first, omp_ref[...], oms_ref[...])
    og = jnp.where(first, ogp_ref[...], ogs_ref[...])
    mix = _sigmoid(ga_ref[...]) * om + _sigmoid(gb_ref[...]) * og
    y = _dot(mix.astype(BF16), wo_ref[...])
    x1 = _layer_norm(alpha * x_ref[...] + y, g1_ref[...], b1_ref[...])
    x1_ref[...] = x1
    _store_row_tiles(x1t_ref, x1)
    logits = _dot_nt(wr_ref[...], x1, precision=lax.Precision.HIGHEST) + br_ref[...]
    n_exp, tm = logits.shape
    e_i = lax.broadcasted_iota(jnp.int32, (n_exp, tm), 0)
    vals, idxs = [], []
    for _ in range(top_k):
        mx = jnp.max(logits, axis=0, keepdims=True)
        ix = jnp.min(jnp.where(logits == mx, e_i, n_exp), axis=0, keepdims=True)
        vals.append(mx)
        idxs.append(ix)
        logits = jnp.where(e_i == ix, -jnp.inf, logits)
    es = [jnp.exp(v - vals[0]) for v in vals]
    tot = es[0]
    for e in es[1:]:
        tot = tot + e
    onehots = [e_i == ix for ix in idxs]
    sel = onehots[0].astype(F32)
    for oh in onehots[1:]:
        sel = sel + oh.astype(F32)
    t_r = lax.broadcasted_iota(jnp.int32, (tm, tm), 0)
    t_c = lax.broadcasted_iota(jnp.int32, (tm, tm), 1)
    before = _dot(sel.astype(BF16), (t_r < t_c).astype(BF16)) + carry_ref[...]
    carry = carry_ref[...] + jnp.sum(sel, axis=1, keepdims=True)
    carry_ref[...] = carry
    cnt_ref[...] = carry
    r_i = lax.broadcasted_iota(jnp.int32, (SUBLANE, tm), 0)
    idx_o = jnp.zeros((SUBLANE, tm), jnp.int32)
    gate_o = jnp.zeros((SUBLANE, tm), F32)
    rank_o = jnp.zeros((SUBLANE, tm), F32)
    for k in range(top_k):
        idx_o = jnp.where(r_i == k, idxs[k], idx_o)
        gate_o = jnp.where(r_i == k, es[k] / tot, gate_o)
        rank_k = jnp.sum(jnp.where(onehots[k], before, 0.0), axis=0, keepdims=True)
        rank_o = jnp.where(r_i == k, rank_k, rank_o)
    idx_ref[...] = idx_o
    gate_ref[...] = gate_o
    rank_ref[...] = rank_o.astype(jnp.int32)


def _post(x, om_p, om_s, og_p, og_s, h, ga_blk, gb_blk, wo, g1, b1, wr_t, br, *, tm, alpha):
    m, d = x.shape
    n_exp = wr_t.shape[0]
    n_first = om_p.shape[0] // tm
    assert om_p.shape[0] % tm == 0 and om_s.shape[0] % tm == 0 and d == SUBLANE * LANE
    kern = functools.partial(_post_kernel, alpha=alpha, top_k=TOP_K, n_first=n_first)
    row = lambda i: (i, 0)
    const = lambda i: (0, 0)
    first = lambda i: (jnp.minimum(i, n_first - 1), 0)
    second = lambda i: (jnp.maximum(i - n_first, 0), 0)
    return pl.pallas_call(
        kern,
        grid=(m // tm,),
        in_specs=[pl.BlockSpec((tm, d), row),
                  pl.BlockSpec((tm, d), first), pl.BlockSpec((tm, d), second),
                  pl.BlockSpec((tm, d), first), pl.BlockSpec((tm, d), second),
                  pl.BlockSpec((tm, d), lambda i: (i, ga_blk)),
                  pl.BlockSpec((tm, d), lambda i: (i, gb_blk)),
                  pl.BlockSpec(wo.shape, const), pl.BlockSpec((1, d), const), pl.BlockSpec((1, d), const),
                  pl.BlockSpec(wr_t.shape, const), pl.BlockSpec((n_exp, 1), const)],
        out_specs=[pl.BlockSpec((tm, d), row),
                   pl.BlockSpec((tm * SUBLANE, LANE), row),
                   pl.BlockSpec((SUBLANE, tm), lambda i: (0, i)),
                   pl.BlockSpec((SUBLANE, tm), lambda i: (0, i)),
                   pl.BlockSpec((SUBLANE, tm), lambda i: (0, i)),
                   pl.BlockSpec((n_exp, 1), const)],
        out_shape=[jax.ShapeDtypeStruct((m, d), F32),
                   jax.ShapeDtypeStruct((m * SUBLANE, LANE), F32),
                   jax.ShapeDtypeStruct((SUBLANE, m), jnp.int32),
                   jax.ShapeDtypeStruct((SUBLANE, m), F32),
                   jax.ShapeDtypeStruct((SUBLANE, m), jnp.int32),
                   jax.ShapeDtypeStruct((n_exp, 1), F32)],
        scratch_shapes=[pltpu.VMEM((n_exp, 1), F32)],
        compiler_params=_cparams(("arbitrary",), VMEM_LIMIT),
        name="post_mix",
    )(x, om_p, om_s, og_p, og_s, h, h, wo, g1, b1, wr_t, br)


def _moe_kernel(be_ref, rows_ref, nused_ref, x_hbm, wgu_ref, bgu_ref, wd_ref, bd_ref, o_ref,
                xbuf, sem, wgu_bf, wd_bf, xb_ref, *, blk, d_exp):
    i = pl.program_id(0)
    n_used = nused_ref[0]
    slot = i % 2

    def row_copy(tok, s, r):
        return pltpu.make_async_copy(x_hbm.at[pl.ds(tok * SUBLANE, SUBLANE)],
                                     xbuf.at[s, pl.ds(r * SUBLANE, SUBLANE)], sem.at[s])

    def wait(s):
        pltpu.make_async_copy(x_hbm.at[pl.ds(0, blk * SUBLANE)], xbuf.at[s], sem.at[s]).wait()

    @pl.when(jnp.logical_and(i == 0, n_used > 0))
    def _():
        second = jnp.minimum(1, n_used - 1) * blk

        def body(r, carry):
            row_copy(rows_ref[r], 0, r).start()
            row_copy(rows_ref[second + r], 1, r).start()
            return carry
        lax.fori_loop(0, blk, body, 0)

    @pl.when(i < n_used)
    def _():
        e = be_ref[i]
        e_prev = be_ref[jnp.maximum(i - 1, 0)]

        @pl.when(jnp.logical_or(i == 0, e != e_prev))
        def _():
            wgu_bf[...] = wgu_ref[0].astype(BF16)
            wd_bf[...] = wd_ref[0].astype(BF16)

        wait(slot)
        for j in range(SUBLANE):
            xb_ref[:, j * LANE:(j + 1) * LANE] = _load_row_tiles(xbuf.at[slot], 0, blk, j).astype(BF16)
        nxt = jnp.minimum(i + 2, n_used - 1) * blk
        for r in range(blk):
            row_copy(rows_ref[nxt + r], slot, r).start(priority=r % 2)
        hh = _dot(xb_ref[...], wgu_bf[...]) + bgu_ref[0]
        gate = jnp.minimum(hh[:, :d_exp], SWIGLU_LIMIT)
        up = jnp.clip(hh[:, d_exp:], -SWIGLU_LIMIT, SWIGLU_LIMIT)
        act = (up + 1.0) * gate * _sigmoid(SWIGLU_ALPHA * gate)
        _store_row_tiles(o_ref, _dot(act.astype(BF16), wd_bf[...]) + bd_ref[0])

        @pl.when(i == n_used - 1)
        def _():
            wait(0)
            wait(1)

    @pl.when(i >= n_used)
    def _():
        o_ref[...] = jnp.zeros(o_ref.shape, F32)


def _moe_experts(block_e, rows, n_used, x1, w_gu, b_gu, w_down, b_down, *, blk):
    n_exp, d, d2 = w_gu.shape
    d_exp = d2 // 2
    nb = block_e.shape[0]
    kern = functools.partial(_moe_kernel, blk=blk, d_exp=d_exp)
    grid_spec = pltpu.PrefetchScalarGridSpec(
        num_scalar_prefetch=3,
        grid=(nb,),
        in_specs=[pl.BlockSpec(memory_space=pl.ANY),
                  pl.BlockSpec((1, d, d2), lambda i, be, rw, nu: (be[i], 0, 0)),
                  pl.BlockSpec((1, 1, d2), lambda i, be, rw, nu: (be[i], 0, 0)),
                  pl.BlockSpec((1, d_exp, d), lambda i, be, rw, nu: (be[i], 0, 0)),
                  pl.BlockSpec((1, 1, d), lambda i, be, rw, nu: (be[i], 0, 0))],
        out_specs=pl.BlockSpec((blk * SUBLANE, LANE), lambda i, be, rw, nu: (i, 0)),
        scratch_shapes=[pltpu.VMEM((2, blk * SUBLANE, LANE), F32), pltpu.SemaphoreType.DMA((2,)),
                        pltpu.VMEM((d, d2), BF16), pltpu.VMEM((d_exp, d), BF16),
                        pltpu.VMEM((blk, d), BF16)])
    return pl.pallas_call(
        kern,
        grid_spec=grid_spec,
        out_shape=jax.ShapeDtypeStruct((nb * blk * SUBLANE, LANE), F32),
        compiler_params=_cparams(("arbitrary",), VMEM_LIMIT),
        name="moe_experts",
    )(block_e, rows, n_used, x1, w_gu, b_gu.reshape(n_exp, 1, d2), w_down, b_down.reshape(n_exp, 1, d))


def _combine_kernel(dest_ref, x1_ref, gate_ref, ys_hbm, g2_ref, b2_ref, op_ref, os_ref, ybuf, sem,
                    *, tm, top_k, alpha, m_total, n_first):
    i = pl.program_id(0)
    nsteps = pl.num_programs(0)
    slot = i % 2
    n_rows = top_k * tm

    def row_copy(d, s, r):
        return pltpu.make_async_copy(ys_hbm.at[pl.ds(d * SUBLANE, SUBLANE)],
                                     ybuf.at[s, pl.ds(r * SUBLANE, SUBLANE)], sem.at[s])

    def wait(s):
        pltpu.make_async_copy(ys_hbm.at[pl.ds(0, n_rows * SUBLANE)], ybuf.at[s], sem.at[s]).wait()

    @pl.when(i == 0)
    def _():
        second = jnp.minimum(1, nsteps - 1) * tm

        def body(r, carry):
            k = r // tm
            t = r - k * tm
            row_copy(dest_ref[k * m_total + t], 0, r).start()
            row_copy(dest_ref[k * m_total + second + t], 1, r).start()
            return carry
        lax.fori_loop(0, n_rows, body, 0)

    wait(slot)
    g = gate_ref[...]
    parts = []
    for j in range(SUBLANE):
        acc = g[:, 0:1] * _load_row_tiles(ybuf.at[slot], 0, tm, j)
        for k in range(1, top_k):
            acc = acc + g[:, k:k + 1] * _load_row_tiles(ybuf.at[slot], k * tm, tm, j)
        parts.append(acc)
    y = jnp.concatenate(parts, axis=1)
    nxt = jnp.minimum(i + 2, nsteps - 1) * tm
    for r in range(n_rows):
        k, t = divmod(r, tm)
        row_copy(dest_ref[k * m_total + nxt + t], slot, r).start(priority=r % 2)
    res = _layer_norm(alpha * x1_ref[...] + y, g2_ref[...], b2_ref[...])

    @pl.when(i < n_first)
    def _():
        op_ref[...] = res

    @pl.when(i >= n_first)
    def _():
        os_ref[...] = res

    @pl.when(i == nsteps - 1)
    def _():
        wait(0)
        wait(1)


def _combine(dest_km, x1, gates_mk, ys, g2, b2, *, tm, alpha, m_first):
    m, d = x1.shape
    n_first = m_first // tm
    assert m_first % tm == 0 and 0 < m_first < m
    kern = functools.partial(_combine_kernel, tm=tm, top_k=TOP_K, alpha=alpha, m_total=m, n_first=n_first)
    grid_spec = pltpu.PrefetchScalarGridSpec(
        num_scalar_prefetch=1,
        grid=(m // tm,),
        in_specs=[pl.BlockSpec((tm, d), lambda i, ds: (i, 0)),
                  pl.BlockSpec((tm, SUBLANE), lambda i, ds: (i, 0)),
                  pl.BlockSpec(memory_space=pl.ANY),
                  pl.BlockSpec((1, d), lambda i, ds: (0, 0)),
                  pl.BlockSpec((1, d), lambda i, ds: (0, 0))],
        out_specs=[pl.BlockSpec((tm, d), lambda i, ds: (jnp.minimum(i, n_first - 1), 0)),
                   pl.BlockSpec((tm, d), lambda i, ds: (jnp.maximum(i - n_first, 0), 0))],
        scratch_shapes=[pltpu.VMEM((2, TOP_K * tm * SUBLANE, LANE), F32), pltpu.SemaphoreType.DMA((2,))])
    return pl.pallas_call(
        kern,
        grid_spec=grid_spec,
        out_shape=[jax.ShapeDtypeStruct((m_first, d), F32), jax.ShapeDtypeStruct((m - m_first, d), F32)],
        compiler_params=_cparams(("arbitrary",), VMEM_LIMIT),
        name="moe_combine",
    )(dest_km, x1, gates_mk, ys, g2, b2)


def _pack_w_in(w_in, splits):
    q_lora, kv_lora, rope, conv_ch, gv, nh, _, d, _ = splits
    offs = [0]
    for s in splits:
        offs.append(offs[-1] + s)
    part = [w_in[:, offs[i]:offs[i + 1]] for i in range(len(splits))]
    q_lat, kv_lat, k_r, qkv, z, a, b, g_a, g_b = part
    dm = w_in.shape[0]
    half = rope // 2
    zpad = lambda n: jnp.zeros((dm, n), w_in.dtype)
    k_sw = jnp.concatenate([k_r[:, half:], k_r[:, :half]], axis=1)
    small = jnp.concatenate([q_lat, kv_lat, k_r, zpad(LANE - rope), k_sw, zpad(LANE - rope),
                             a, b, zpad(LANE - 2 * nh)], axis=1)
    return jnp.concatenate([qkv, small, z, g_a, g_b], axis=1).astype(BF16)


def _pack_w_uq(w_uq, nope, rope):
    w = jnp.transpose(w_uq, (1, 0, 2))
    half = rope // 2
    r = w[..., nope:]
    zp = jnp.zeros(r.shape[:-1] + (LANE - rope,), w.dtype)
    r_sw = jnp.concatenate([r[..., half:], r[..., :half]], axis=-1)
    return jnp.concatenate([w[..., :nope], r, zp, r_sw, zp], axis=-1).astype(BF16)


def _rope_tables(pos, rope):
    half = rope // 2
    inv = ROPE_THETA ** (-jnp.arange(half, dtype=F32) / half)
    ang = pos.astype(F32)[:, None] * inv[None, :]
    cos, sin = jnp.cos(ang), jnp.sin(ang)
    zp = jnp.zeros((pos.shape[0], LANE - rope), F32)
    return (jnp.concatenate([cos, cos, zp], axis=1), jnp.concatenate([-sin, sin, zp], axis=1))


def _route_meta(idx_t, rank_t, counts, m, n_exp, blk):
    a = m * TOP_K
    counts = counts.astype(jnp.int32)
    padded = (counts + blk - 1) // blk * blk
    pad_end = jnp.cumsum(padded)
    pad_start = pad_end - padded
    experts = jnp.arange(n_exp, dtype=jnp.int32)
    e_km = idx_t[:TOP_K]
    start_km = jnp.sum(jnp.where(e_km[:, :, None] == experts, pad_start, 0), axis=-1)
    dest_km = (start_km + rank_t[:TOP_K]).astype(jnp.int32).reshape(a)
    nb = a // blk + n_exp
    tok_km = jnp.tile(jnp.arange(m, dtype=jnp.int32), TOP_K)
    filler = jnp.arange(nb * blk, dtype=jnp.int32) % m
    rows = filler.at[dest_km].set(tok_km, unique_indices=True, mode='promise_in_bounds')
    first_row = jnp.arange(nb, dtype=jnp.int32) * blk
    block_e = jnp.minimum(jnp.sum((pad_end[None, :] <= first_row[:, None]).astype(jnp.int32), axis=1),
                          n_exp - 1).astype(jnp.int32)
    n_used = (pad_end[-1] // blk).astype(jnp.int32).reshape(1)
    return block_e, rows, n_used, dest_km


def kernel(x_prompt, x_sample, cache_ckv, cache_krope, page_table, state_conv, state_ssm, w_in, q_norm_w, kv_norm_w, w_uq, w_uk, w_uv, conv_w, a_log, dt_bias, gdn_norm_w, w_o, ln1_g, ln1_b, w_router, b_router, w_gu, b_gu, w_down, b_down, ln2_g, ln2_b):
    bp, tp, d = x_prompt.shape
    bs, ts, _ = x_sample.shape
    depth = w_in.shape[0]
    q_lora, n_heads, qk = w_uq.shape[1:]
    kv_lora, _, nope = w_uk.shape[1:]
    rope = qk - nope
    vh = w_uv.shape[3]
    cw_taps, conv_ch = conv_w.shape[1:]
    g_heads = a_log.shape[1]
    dk, dv = state_ssm.shape[3:]
    n_exp = w_router.shape[2]
    page = cache_ckv.shape[2]
    past = page_table.shape[1] * page
    splits = (q_lora, kv_lora, rope, conv_ch, g_heads * dv, g_heads, g_heads, d, d)
    assert sum(splits) == w_in.shape[2]
    alpha = (2 * depth) ** 0.25
    scale = (nope + rope) ** -0.5 * LOG2E
    mp, ms = bp * tp, bs * ts
    m = mp + ms
    small_w = q_lora + kv_lora + 3 * LANE
    assert small_w == d and conv_ch % d == 0
    qkv_blk, small_blk = 0, conv_ch // d
    z_blk, ga_blk, gb_blk = small_blk + 1, small_blk + 2, small_blk + 3
    ab_blk = (conv_ch + q_lora + kv_lora + 2 * LANE) // LANE

    cos_p, sin_p = _rope_tables(jnp.arange(tp, dtype=jnp.int32), rope)
    cos_s, sin_s = _rope_tables(past + jnp.arange(ts, dtype=jnp.int32), rope)
    cos_t = jnp.concatenate([jnp.tile(cos_p, (bp, 1)), jnp.tile(cos_s, (bs, 1))], axis=0)
    sin_t = jnp.concatenate([jnp.tile(sin_p, (bp, 1)), jnp.tile(sin_s, (bs, 1))], axis=0)

    x_p, x_s = x_prompt.reshape(mp, d), x_sample.reshape(ms, d)
    outs = {k: [] for k in ("ckv_p", "kr_p", "conv_p", "ssm_p", "ckv_s", "kr_s", "conv_s", "ssm_s")}
    pad_lanes = lambda v: jnp.pad(v, (0, LANE - v.shape[0])).reshape(1, LANE)
    for l in range(depth):
        x = jnp.concatenate([x_p, x_s], axis=0)
        w_pack = _pack_w_in(w_in[l], splits)
        wq = _pack_w_uq(w_uq[l], nope, rope)
        wuk = jnp.transpose(w_uk[l], (1, 2, 0)).astype(BF16)
        wuv = jnp.transpose(w_uv[l], (1, 0, 2)).astype(BF16)
        h = _in_proj(x, w_pack, tm=_row_tile(m, 2304), tn=512)
        qf, c_all, kr_all, kt, cb = _mla_proj(
            h, small_blk, cos_t, sin_t, q_norm_w[l].reshape(1, q_lora), kv_norm_w[l].reshape(1, kv_lora),
            wq, wuk, tm=256, q_lora=q_lora, kv_lora=kv_lora, nope=nope, rope=rope, scale=scale)
        om_p = _attn_prompt(qf, kt, cb, wuv, batch=bp, seq=tp, tq=128, tk=min(1024, tp))
        qs = qf[:, mp:, :].reshape(n_heads, bs, ts, qf.shape[-1])
        qs = jnp.transpose(qs, (1, 0, 2, 3)).reshape(bs, n_heads * ts, qf.shape[-1])
        om_s = _attn_sample(page_table, qs, c_all, kr_all, mp // ts, cache_ckv[l],
                            jnp.swapaxes(cache_krope[l], 1, 2), wuv)

        alog_p = pad_lanes(a_log[l])
        dtb_p = pad_lanes(dt_bias[l])
        nw = gdn_norm_w[l].reshape(1, dv)
        gdn_kw = dict(qkv_blk=qkv_blk, ab_blk=ab_blk, z_blk=z_blk, conv_w=cw_taps)
        zeros_conv = jnp.zeros((bp, SUBLANE, conv_ch), F32)
        zeros_ssm = jnp.zeros((bp, g_heads, dk, dv), F32)
        og_p, ssm_p, ncv_p = _gdn(h, 0, zeros_conv, zeros_ssm, conv_w[l], alog_p, dtb_p, nw,
                                  batch=bp, seq=tp, chunk=min(GDN_CHUNK, tp), group=bp, **gdn_kw)
        conv8_s = jnp.pad(state_conv[l], ((0, 0), (SUBLANE - (cw_taps - 1), 0), (0, 0)))
        og_s, ssm_s, ncv_s = _gdn(h, mp, conv8_s, state_ssm[l], conv_w[l], alog_p, dtb_p, nw,
                                  batch=bs, seq=ts, chunk=ts, group=math.gcd(bs, GDN_SAMPLE_GROUP), **gdn_kw)

        x1, x1t, idx_t, gate_t, rank_t, counts = _post(
            x, om_p, om_s, og_p, og_s, h, ga_blk, gb_blk, w_o[l].astype(BF16), ln1_g[l].reshape(1, d), ln1_b[l].reshape(1, d),
            w_router[l].T, b_router[l].reshape(n_exp, 1), tm=256, alpha=alpha)
        block_e, rows, n_used, dest_km = _route_meta(idx_t, rank_t, counts[:, 0], m, n_exp, MOE_ROWS)
        ys = _moe_experts(block_e, rows, n_used, x1t, w_gu[l], b_gu[l], w_down[l], b_down[l], blk=MOE_ROWS)
        x_p, x_s = _combine(dest_km, x1, gate_t.T, ys, ln2_g[l].reshape(1, d), ln2_b[l].reshape(1, d),
                            tm=128, alpha=alpha, m_first=mp)

        outs["ckv_p"].append(c_all[:mp].reshape(bp, tp, kv_lora))
        outs["kr_p"].append(kr_all[:mp].reshape(bp, tp, rope))
        outs["conv_p"].append(ncv_p[:, SUBLANE - (cw_taps - 1):, :])
        outs["ssm_p"].append(ssm_p)
        outs["ckv_s"].append(c_all[mp:].reshape(bs, ts, kv_lora))
        outs["kr_s"].append(kr_all[mp:].reshape(bs, ts, rope))
        outs["conv_s"].append(ncv_s[:, SUBLANE - (cw_taps - 1):, :])
        outs["ssm_s"].append(ssm_s)

    return (x_p.reshape(bp, tp, d), x_s.reshape(bs, ts, d),
            jnp.stack(outs["ckv_p"]), jnp.stack(outs["kr_p"]), jnp.stack(outs["conv_p"]), jnp.stack(outs["ssm_p"]),
            jnp.stack(outs["ckv_s"]), jnp.stack(outs["kr_s"]), jnp.stack(outs["conv_s"]), jnp.stack(outs["ssm_s"]))
```

```python
import functools
import math

import jax
import jax.numpy as jnp
from jax import lax
from jax.experimental import pallas as pl
from jax.experimental.pallas import tpu as pltpu

F32 = jnp.float32
BF16 = jnp.bfloat16

ROPE_THETA = 10000.0
NORM_EPS = 1e-6
TOP_K = 4
SWIGLU_LIMIT = 7.0
SWIGLU_ALPHA = 1.702
GDN_CHUNK = 128
GDN_DIAG = 16
GDN_SAMPLE_GROUP = 8
MOE_ROWS = 512
PAGE_CHUNK = 8
DIAG_VARIANTS = 4
LANE = 128
SUBLANE = 8
VMEM_LIMIT = 56 * 1024 * 1024
NEG = -1e30
LOG2E = 1.4426950408889634


def _cparams(sem, vmem=None):
    return pltpu.CompilerParams(dimension_semantics=sem, vmem_limit_bytes=vmem)


def _dot(a, b, **kw):
    return jnp.dot(a, b, preferred_element_type=F32, **kw)


def _dot_nt(a, b, **kw):
    return lax.dot_general(a, b, (((1,), (1,)), ((), ())), preferred_element_type=F32, **kw)


def _bdot(a, b):
    return lax.dot_general(a, b, (((2,), (1,)), ((0,), (0,))), preferred_element_type=F32)


def _bdot_nt(a, b):
    return lax.dot_general(a, b, (((2,), (2,)), ((0,), (0,))), preferred_element_type=F32)


def _bdot_tn(a, b):
    return lax.dot_general(a, b, (((1,), (1,)), ((0,), (0,))), preferred_element_type=F32)


def _sigmoid(x):
    return 1.0 / (1.0 + jnp.exp(-x))


def _layer_norm(v, g, b):
    mu = jnp.mean(v, -1, keepdims=True)
    vc = v - mu
    var = jnp.mean(vc * vc, -1, keepdims=True)
    return vc * lax.rsqrt(var + NORM_EPS) * g + b


def _inproj_kernel(x_ref, w_ref, o_ref, xb_ref):
    @pl.when(pl.program_id(1) == 0)
    def _():
        xb_ref[...] = x_ref[...].astype(BF16)

    o_ref[...] = _dot(xb_ref[...], w_ref[...])


def _row_tile(m, target):
    return max(t for t in range(SUBLANE, target + 1, SUBLANE) if m % t == 0)


def _in_proj(x, w, tm, tn):
    m, k = x.shape
    n = w.shape[1]
    return pl.pallas_call(
        _inproj_kernel,
        grid=(m // tm, n // tn),
        in_specs=[pl.BlockSpec((tm, k), lambda i, j: (i, 0)),
                  pl.BlockSpec((k, tn), lambda i, j: (0, j))],
        out_specs=pl.BlockSpec((tm, tn), lambda i, j: (i, j)),
        out_shape=jax.ShapeDtypeStruct((m, n), F32),
        scratch_shapes=[pltpu.VMEM((tm, k), BF16)],
        compiler_params=_cparams(("parallel", "arbitrary"), VMEM_LIMIT),
        name="in_proj",
    )(x, w)


def _mla_proj_kernel(h_ref, cos_ref, sin_ref, qnw_ref, kvnw_ref, wq_ref, wuk_ref,
                     qf_ref, c_ref, kr_ref, kt_ref, cb_ref, *, n_heads, q_lora, kv_lora, nope, rope, scale):
    hs = h_ref[...]
    cos = cos_ref[...]
    sin = sin_ref[...]
    q_lat = hs[:, :q_lora]
    qn = q_lat * lax.rsqrt(jnp.mean(q_lat * q_lat, -1, keepdims=True) + NORM_EPS) * qnw_ref[...]
    qn = qn.astype(BF16)
    kv = hs[:, q_lora:q_lora + kv_lora]
    c = kv * lax.rsqrt(jnp.mean(kv * kv, -1, keepdims=True) + NORM_EPS) * kvnw_ref[...]
    o = q_lora + kv_lora
    kr = hs[:, o:o + LANE] * cos + hs[:, o + LANE:o + 2 * LANE] * sin
    c_ref[...] = c
    kr_ref[...] = kr[:, :rope]
    cb = c.astype(BF16)
    cb_ref[...] = cb
    kfull = jnp.concatenate([cb, kr.astype(BF16)], axis=1)
    kw = kfull.shape[1]
    eye = (lax.broadcasted_iota(jnp.int32, (kw, kw), 0) == lax.broadcasted_iota(jnp.int32, (kw, kw), 1))
    kt_ref[...] = _dot_nt(eye.astype(BF16), kfull).astype(BF16)
    for h in range(n_heads):
        qh = _dot(qn, wq_ref[h])
        qa = _dot(qh[:, :nope].astype(BF16), wuk_ref[h])
        qr = qh[:, nope:nope + LANE] * cos + qh[:, nope + LANE:nope + 2 * LANE] * sin
        qf_ref[h, :, :kv_lora] = (qa * scale).astype(BF16)
        qf_ref[h, :, kv_lora:] = (qr * scale).astype(BF16)


def _mla_proj(h, col_blk, cos_t, sin_t, qnw, kvnw, wq, wuk, *, tm, q_lora, kv_lora, nope, rope, scale):
    m = h.shape[0]
    n_heads = wq.shape[0]
    wcol = q_lora + kv_lora + 3 * LANE
    kw = kv_lora + LANE
    kern = functools.partial(_mla_proj_kernel, n_heads=n_heads, q_lora=q_lora, kv_lora=kv_lora,
                             nope=nope, rope=rope, scale=scale)
    return pl.pallas_call(
        kern,
        grid=(m // tm,),
        in_specs=[pl.BlockSpec((tm, wcol), lambda i: (i, col_blk)),
                  pl.BlockSpec((tm, LANE), lambda i: (i, 0)),
                  pl.BlockSpec((tm, LANE), lambda i: (i, 0)),
                  pl.BlockSpec((1, q_lora), lambda i: (0, 0)),
                  pl.BlockSpec((1, kv_lora), lambda i: (0, 0)),
                  pl.BlockSpec(wq.shape, lambda i: (0, 0, 0)),
                  pl.BlockSpec(wuk.shape, lambda i: (0, 0, 0))],
        out_specs=[pl.BlockSpec((n_heads, tm, kw), lambda i: (0, i, 0)),
                   pl.BlockSpec((tm, kv_lora), lambda i: (i, 0)),
                   pl.BlockSpec((tm, rope), lambda i: (i, 0)),
                   pl.BlockSpec((kw, tm), lambda i: (0, i)),
                   pl.BlockSpec((tm, kv_lora), lambda i: (i, 0))],
        out_shape=[jax.ShapeDtypeStruct((n_heads, m, kw), BF16),
                   jax.ShapeDtypeStruct((m, kv_lora), F32),
                   jax.ShapeDtypeStruct((m, rope), F32),
                   jax.ShapeDtypeStruct((kw, m), BF16),
                   jax.ShapeDtypeStruct((m, kv_lora), BF16)],
        compiler_params=_cparams(("parallel",), VMEM_LIMIT),
        name="mla_proj",
    )(h, cos_t, sin_t, qnw, kvnw, wq, wuk)


def _softmax_step(s, v, m_ref, l_ref, acc_ref):
    m_prev = m_ref[...]
    m_new = jnp.maximum(m_prev, jnp.max(s, -1, keepdims=True))
    alpha = jnp.exp2(m_prev - m_new)
    p = jnp.exp2(s - m_new)
    l_ref[...] = alpha * l_ref[...] + jnp.sum(p, -1, keepdims=True)
    acc_ref[...] = alpha * acc_ref[...] + _dot(p.astype(BF16), v)
    m_ref[...] = m_new


def _attn_prompt_kernel(qf_ref, kt_ref, v_ref, wuv_ref, o_ref, m_ref, l_ref, acc_ref, *, n_heads, tq, tk, dv, vh, n_split):
    i = pl.program_id(1)
    m_ref[...] = jnp.full(m_ref.shape, NEG, F32)
    l_ref[...] = jnp.zeros(l_ref.shape, F32)
    acc_ref[...] = jnp.zeros(acc_ref.shape, F32)
    n_full = (i * tq) // tk
    off = i * tq - n_full * tk
    hs = n_heads // n_split
    rs = hs * tq

    def step(j, width, masked):
        start = pl.multiple_of(j * tk, tk)
        kt = kt_ref[:, pl.ds(start, width)]
        v = v_ref[pl.ds(start, width), :]
        for g in range(n_split):
            q = qf_ref[g * hs:(g + 1) * hs].reshape(rs, qf_ref.shape[-1])
            s = _dot(q, kt)
            if masked:
                row = lax.broadcasted_iota(jnp.int32, (tq, width), 0)
                col = lax.broadcasted_iota(jnp.int32, (tq, width), 1)
                s = jnp.where((col <= row + off)[None], s.reshape(hs, tq, width), NEG).reshape(rs, width)
            sl = slice(g * rs, (g + 1) * rs)
            _softmax_step(s, v, m_ref.at[sl], l_ref.at[sl], acc_ref.at[sl])

    def body(j, carry):
        step(j, tk, False)
        return carry

    lax.fori_loop(0, n_full, body, 0)
    need = off + tq
    widths = [tk * (v + 1) // DIAG_VARIANTS for v in range(DIAG_VARIANTS)]
    for v, width in enumerate(widths):
        lo = widths[v - 1] if v else 0

        @pl.when(jnp.logical_and(need > lo, need <= width))
        def _(width=width):
            step(n_full, width, True)

    o = acc_ref[...] / l_ref[...]
    for h in range(n_heads):
        oh = o[h * tq:(h + 1) * tq].astype(BF16)
        o_ref[:, h * vh:(h + 1) * vh] = _dot(oh, wuv_ref[h])


def _attn_prompt(qf, kt, cb, wuv, *, batch, seq, tq, tk):
    n_heads, _, kw = qf.shape
    dv, vh = wuv.shape[1], wuv.shape[2]
    nq = seq // tq
    rows = n_heads * tq
    assert seq % tk == 0 and tk % tq == 0 and tk % (DIAG_VARIANTS * LANE) == 0
    kern = functools.partial(_attn_prompt_kernel, n_heads=n_heads, tq=tq, tk=tk, dv=dv, vh=vh, n_split=2)
    return pl.pallas_call(
        kern,
        grid=(batch, nq),
        in_specs=[pl.BlockSpec((n_heads, tq, kw), lambda b, i: (0, b * nq + i, 0)),
                  pl.BlockSpec((kw, seq), lambda b, i: (0, b)),
                  pl.BlockSpec((seq, dv), lambda b, i: (b, 0)),
                  pl.BlockSpec(wuv.shape, lambda b, i: (0, 0, 0))],
        out_specs=pl.BlockSpec((tq, n_heads * vh), lambda b, i: (b * nq + i, 0)),
        out_shape=jax.ShapeDtypeStruct((batch * seq, n_heads * vh), F32),
        scratch_shapes=[pltpu.VMEM((rows, 1), F32), pltpu.VMEM((rows, 1), F32),
                        pltpu.VMEM((rows, dv), F32)],
        compiler_params=_cparams(("parallel", "arbitrary"), VMEM_LIMIT),
        name="attn_prompt",
    )(qf, kt, cb, wuv)


def _attn_sample_kernel(pt_ref, q_ref, cn_ref, krn_ref, cc_hbm, cr_hbm, wuv_ref, o_ref,
                        cbuf, rbuf, sem, *, n_pages, page, n_heads, ts, dv, dr, vh):
    b = pl.program_id(0)
    nb = pl.num_programs(0)
    slot = b % 2
    rows = n_heads * ts

    def fetch(bb, s):
        for p in range(n_pages):
            pg = pt_ref[bb, p]
            pltpu.make_async_copy(cc_hbm.at[pg], cbuf.at[s, p], sem.at[0, s]).start()
            pltpu.make_async_copy(cr_hbm.at[pg], rbuf.at[s, p], sem.at[1, s]).start(priority=1)

    def wait(s):
        pltpu.make_async_copy(cc_hbm.at[pl.ds(0, n_pages)], cbuf.at[s], sem.at[0, s]).wait()
        pltpu.make_async_copy(cr_hbm.at[pl.ds(0, n_pages)], rbuf.at[s], sem.at[1, s]).wait()

    @pl.when(b == 0)
    def _():
        fetch(0, 0)

    wait(slot)
    fetch(jnp.minimum(b + 1, nb - 1), 1 - slot)

    q = q_ref[0].astype(F32)
    qc = q[:, :dv]
    qr = q[:, dv:dv + dr]
    n_chunks = n_pages // PAGE_CHUNK
    ck = PAGE_CHUNK * page
    parts = []
    for ch in range(n_chunks):
        c_ch = cbuf[slot, ch * PAGE_CHUNK:(ch + 1) * PAGE_CHUNK].reshape(ck, dv)
        r_ch = jnp.concatenate([rbuf[slot, ch * PAGE_CHUNK + u] for u in range(PAGE_CHUNK)], axis=1)
        parts.append(_dot_nt(qc, c_ch) + _dot(qr, r_ch))
    cn = jnp.concatenate([cn_ref[...], jnp.zeros((LANE - ts, dv), F32)], axis=0)
    krn = jnp.concatenate([krn_ref[...], jnp.zeros((LANE - ts, dr), F32)], axis=0)
    row = lax.broadcasted_iota(jnp.int32, (ts, LANE), 0)
    col = lax.broadcasted_iota(jnp.int32, (ts, LANE), 1)
    s_new = (_dot_nt(qc, cn) + _dot_nt(qr, krn)).reshape(n_heads, ts, LANE)
    s_new = jnp.where((col <= row)[None], s_new, NEG).reshape(rows, LANE)

    m = jnp.max(s_new, -1, keepdims=True)
    for s in parts:
        m = jnp.maximum(m, jnp.max(s, -1, keepdims=True))
    p_new = jnp.exp2(s_new - m)
    l = jnp.sum(p_new, -1, keepdims=True)
    acc = _dot(p_new, cn)
    for ch in range(n_chunks):
        p = jnp.exp2(parts[ch] - m)
        l = l + jnp.sum(p, -1, keepdims=True)
        acc = acc + _dot(p, cbuf[slot, ch * PAGE_CHUNK:(ch + 1) * PAGE_CHUNK].reshape(ck, dv))
    o = acc / l
    for h in range(n_heads):
        oh = o[h * ts:(h + 1) * ts].astype(BF16)
        o_ref[:, h * vh:(h + 1) * vh] = _dot(oh, wuv_ref[h])

    @pl.when(b == nb - 1)
    def _():
        wait(1 - slot)


def _attn_sample(page_table, qs, c_all, kr_all, row_blk_off, cache_c, cache_rt, wuv):
    bs, rows, kw = qs.shape
    n_heads, dv, vh = wuv.shape
    ts = rows // n_heads
    n_pages = page_table.shape[1]
    page = cache_c.shape[1]
    dr = cache_rt.shape[1]
    assert n_pages % PAGE_CHUNK == 0
    kern = functools.partial(_attn_sample_kernel, n_pages=n_pages, page=page, n_heads=n_heads, ts=ts,
                             dv=dv, dr=dr, vh=vh)
    grid_spec = pltpu.PrefetchScalarGridSpec(
        num_scalar_prefetch=1,
        grid=(bs,),
        in_specs=[pl.BlockSpec((1, rows, kw), lambda b, pt: (b, 0, 0)),
                  pl.BlockSpec((ts, dv), lambda b, pt: (row_blk_off + b, 0)),
                  pl.BlockSpec((ts, dr), lambda b, pt: (row_blk_off + b, 0)),
                  pl.BlockSpec(memory_space=pl.ANY),
                  pl.BlockSpec(memory_space=pl.ANY),
                  pl.BlockSpec(wuv.shape, lambda b, pt: (0, 0, 0))],
        out_specs=pl.BlockSpec((ts, n_heads * vh), lambda b, pt: (b, 0)),
        scratch_shapes=[pltpu.VMEM((2, n_pages, page, dv), F32), pltpu.VMEM((2, n_pages, dr, page), F32),
                        pltpu.SemaphoreType.DMA((2, 2))])
    return pl.pallas_call(
        kern,
        grid_spec=grid_spec,
        out_shape=jax.ShapeDtypeStruct((bs * ts, n_heads * vh), F32),
        compiler_params=_cparams(("arbitrary",), VMEM_LIMIT),
        name="attn_sample",
    )(page_table, qs, c_all, kr_all, cache_c, cache_rt, wuv)


def _neumann_inv(low, eye, steps):
    p = eye - low
    x = low
    for _ in range(steps):
        x = _bdot(x, x)
        p = p + _bdot(p, x)
    return p


def _gdn_kernel(*refs, n_in, n_seq, chunk, n_heads, dk, dv, conv_w):
    qkv_refs, ab_refs, z_refs = refs[:n_in], refs[n_in:2 * n_in], refs[2 * n_in:3 * n_in]
    (cs_ref, s0_ref, cw_ref, alog_ref, dtb_ref, nw_ref,
     o_ref, sfin_ref, ncv_ref, xp_ref, s_ref) = refs[3 * n_in:]
    n = pl.program_id(1)
    c = chunk
    per = n_seq // n_in
    n_prob = n_seq * n_heads
    hi = lax.Precision.HIGHEST

    def seq_rows(group, g):
        k = g % per
        return group[g // per][k * c:(k + 1) * c, :]

    @pl.when(n == 0)
    def _():
        s_ref[...] = s0_ref[...].reshape(n_prob, dk, dv)
        xp_ref[:, 0:SUBLANE, :] = cs_ref[...]

    r_i = lax.broadcasted_iota(jnp.int32, (c, c), 0)
    c_i = lax.broadcasted_iota(jnp.int32, (c, c), 1)
    causal = (c_i <= r_i)[None]
    strict = (c_i < r_i)[None]
    eye_c = (r_i == c_i).astype(F32)[None]
    db = min(GDN_DIAG, c)
    n_blk = c // db
    sh = int(math.log2(db))
    same_blk = (jnp.right_shift(r_i, sh) == jnp.right_shift(c_i, sh))[None]
    r_l = lax.broadcasted_iota(jnp.int32, (LANE, LANE), 0)
    c_l = lax.broadcasted_iota(jnp.int32, (LANE, LANE), 1)
    eye_l = (r_l == c_l).astype(F32)
    tril = (c_i <= r_i).astype(F32)
    cw = cw_ref[...]
    hk = n_heads * dk

    q_l, k_l, v_l, z_l, beta_l, gcol_l, grow_l, tails = [], [], [], [], [], [], [], []
    for g in range(n_seq):
        xp_ref[g, SUBLANE:SUBLANE + c, :] = seq_rows(qkv_refs, g)
        conv = xp_ref[g, pl.ds(SUBLANE - (conv_w - 1), c), :] * cw[0:1, :]
        for j in range(1, conv_w):
            conv = conv + xp_ref[g, pl.ds(SUBLANE - (conv_w - 1) + j, c), :] * cw[j:j + 1, :]
        tail = xp_ref[g, c:c + SUBLANE, :]
        xp_ref[g, 0:SUBLANE, :] = tail
        tails.append(tail)
        act = conv * _sigmoid(conv)
        ab = seq_rows(ab_refs, g)
        apb = ab + dtb_ref[...]
        softplus = jnp.maximum(apb, 0.0) + jnp.log(1.0 + jnp.exp(-jnp.abs(apb)))
        gfull = -jnp.exp(alog_ref[...]) * softplus
        betaf = _sigmoid(ab)
        gc = _dot(tril, gfull, precision=hi)
        gc_t = _dot_nt(eye_l, gc, precision=hi)
        zg = seq_rows(z_refs, g)
        for h in range(n_heads):
            q_l.append(act[:, h * dk:(h + 1) * dk])
            k_l.append(act[:, hk + h * dk:hk + (h + 1) * dk])
            v_l.append(act[:, 2 * hk + h * dv:2 * hk + (h + 1) * dv])
            z_l.append(zg[:, h * dv:(h + 1) * dv])
            beta_l.append(betaf[:, n_heads + h:n_heads + h + 1])
            gcol_l.append(gc[:, h:h + 1])
            grow_l.append(gc_t[h:h + 1, :])

    q = jnp.stack(q_l)
    k = jnp.stack(k_l)
    v = jnp.stack(v_l)
    z = jnp.stack(z_l)
    beta = jnp.stack(beta_l)
    gcol = jnp.stack(gcol_l)
    grow = jnp.stack(grow_l)
    q = q * lax.rsqrt(jnp.sum(q * q, -1, keepdims=True) + NORM_EPS) * (dk ** -0.5)
    k = k * lax.rsqrt(jnp.sum(k * k, -1, keepdims=True) + NORM_EPS)
    decay = jnp.where(causal, jnp.exp(jnp.where(causal, gcol - grow, 0.0)), 0.0)
    kb = k * beta
    lower = jnp.where(strict, _bdot_nt(kb, k) * decay, 0.0)
    attn = _bdot_nt(q, k) * decay
    l_d = jnp.where(same_blk, lower, 0.0)
    t_inv = _neumann_inv(l_d, eye_c, int(math.log2(db)) - 1)
    if n_blk > 1:
        m_inv = _neumann_inv(_bdot(t_inv, lower - l_d), eye_c, int(math.log2(n_blk)) - 1)
        t_inv = _bdot(m_inv, t_inv)
    egc = jnp.exp(gcol)
    uw = _bdot(t_inv, jnp.concatenate([v * beta, kb * egc], axis=2))
    s = s_ref[...]
    v_new = uw[:, :, :dv] - _bdot(uw[:, :, dv:], s)
    o = _bdot(q * egc, s) + _bdot(attn, v_new)
    glast = gcol[:, c - 1:c, :]
    kdec = k * jnp.exp(glast - gcol)
    s_new = s * jnp.exp(glast) + _bdot_tn(kdec, v_new)
    s_ref[...] = s_new
    on = o * lax.rsqrt(jnp.mean(o * o, -1, keepdims=True) + NORM_EPS) * nw_ref[...] * (z * _sigmoid(z))
    for g in range(n_seq):
        for h in range(n_heads):
            o_ref[g, :, h * dv:(h + 1) * dv] = on[g * n_heads + h]

    @pl.when(n == pl.num_programs(1) - 1)
    def _():
        sfin_ref[...] = s_new.reshape(n_seq, n_heads, dk, dv)
        for g in range(n_seq):
            ncv_ref[g] = tails[g]


def _gdn(h, row_off, conv_state8, ssm_state, cw, alog_p, dtb_p, nw, *, batch, seq, chunk, group,
         qkv_blk, ab_blk, z_blk, conv_w):
    _, n_heads, dk, dv = ssm_state.shape
    nc = seq // chunk
    ch = cw.shape[1]
    hd = n_heads * dv
    contiguous = nc == 1
    n_in = 1 if contiguous else group
    assert batch % group == 0 and row_off % (group * chunk) == 0
    kern = functools.partial(_gdn_kernel, n_in=n_in, n_seq=group, chunk=chunk, n_heads=n_heads,
                             dk=dk, dv=dv, conv_w=conv_w)
    if contiguous:
        rb = group * chunk
        row_maps = [lambda i, n: row_off // rb + i]
    else:
        rb = chunk
        row_maps = [(lambda i, n, g=g: row_off // rb + (i * group + g) * nc + n) for g in range(group)]

    def specs(width, col_blk):
        return [pl.BlockSpec((rb, width), lambda i, n, r=r: (r(i, n), col_blk)) for r in row_maps]

    in_specs = specs(ch, qkv_blk) + specs(LANE, ab_blk) + specs(hd, z_blk)
    in_specs += [pl.BlockSpec((group, SUBLANE, ch), lambda i, n: (i, 0, 0)),
                 pl.BlockSpec((group, n_heads, dk, dv), lambda i, n: (i, 0, 0, 0)),
                 pl.BlockSpec(cw.shape, lambda i, n: (0, 0)),
                 pl.BlockSpec((1, LANE), lambda i, n: (0, 0)),
                 pl.BlockSpec((1, LANE), lambda i, n: (0, 0)),
                 pl.BlockSpec((1, dv), lambda i, n: (0, 0))]
    o3, sfin, ncv = pl.pallas_call(
        kern,
        grid=(batch // group, nc),
        in_specs=in_specs,
        out_specs=[pl.BlockSpec((group, chunk, hd), lambda i, n: (i, n, 0)),
                   pl.BlockSpec((group, n_heads, dk, dv), lambda i, n: (i, 0, 0, 0)),
                   pl.BlockSpec((group, SUBLANE, ch), lambda i, n: (i, 0, 0))],
        out_shape=[jax.ShapeDtypeStruct((batch, seq, hd), F32),
                   jax.ShapeDtypeStruct(ssm_state.shape, F32),
                   jax.ShapeDtypeStruct((batch, SUBLANE, ch), F32)],
        scratch_shapes=[pltpu.VMEM((group, chunk + SUBLANE, ch), F32),
                        pltpu.VMEM((group * n_heads, dk, dv), F32)],
        compiler_params=_cparams(("parallel", "arbitrary"), VMEM_LIMIT),
        name="gdn",
    )(*([h] * (3 * n_in)), conv_state8, ssm_state, cw, alog_p, dtb_p, nw)
    return o3.reshape(batch * seq, hd), sfin, ncv


def _store_row_tiles(ref, val):
    rows = val.shape[0]
    for j in range(SUBLANE):
        ref[pl.ds(j, rows, stride=SUBLANE), :] = val[:, j * LANE:(j + 1) * LANE]


def _load_row_tiles(ref, start, rows, j):
    return ref[pl.ds(start * SUBLANE + j, rows, stride=SUBLANE), :]


def _post_kernel(x_ref, omp_ref, oms_ref, ogp_ref, ogs_ref, ga_ref, gb_ref, wo_ref, g1_ref, b1_ref, wr_ref, br_ref,
                 x1_ref, x1t_ref, idx_ref, gate_ref, rank_ref, cnt_ref, carry_ref, *, alpha, top_k, n_first):
    @pl.when(pl.program_id(0) == 0)
    def _():
        carry_ref[...] = jnp.zeros(carry_ref.shape, F32)

    first = pl.program_id(0) < n_first
    om = jnp.where(first, omp_ref[...], oms_ref[...])
    og = jnp.where(first, ogp_ref[...], ogs_ref[...])
    mix = _sigmoid(ga_ref[...]) * om + _sigmoid(gb_ref[...]) * og
    y = _dot(mix.astype(BF16), wo_ref[...])
    x1 = _layer_norm(alpha * x_ref[...] + y, g1_ref[...], b1_ref[...])
    x1_ref[...] = x1
    _store_row_tiles(x1t_ref, x1)
    logits = _dot_nt(wr_ref[...], x1, precision=lax.Precision.HIGHEST) + br_ref[...]
    n_exp, tm = logits.shape
    e_i = lax.broadcasted_iota(jnp.int32, (n_exp, tm), 0)
    vals, idxs = [], []
    for _ in range(top_k):
        mx = jnp.max(logits, axis=0, keepdims=True)
        ix = jnp.min(jnp.where(logits == mx, e_i, n_exp), axis=0, keepdims=True)
        vals.append(mx)
        idxs.append(ix)
        logits = jnp.where(e_i == ix, -jnp.inf, logits)
    es = [jnp.exp(v - vals[0]) for v in vals]
    tot = es[0]
    for e in es[1:]:
        tot = tot + e
    onehots = [e_i == ix for ix in idxs]
    sel = onehots[0].astype(F32)
    for oh in onehots[1:]:
        sel = sel + oh.astype(F32)
    t_r = lax.broadcasted_iota(jnp.int32, (tm, tm), 0)
    t_c = lax.broadcasted_iota(jnp.int32, (tm, tm), 1)
    before = _dot(sel.astype(BF16), (t_r < t_c).astype(BF16)) + carry_ref[...]
    carry = carry_ref[...] + jnp.sum(sel, axis=1, keepdims=True)
    carry_ref[...] = carry
    cnt_ref[...] = carry
    r_i = lax.broadcasted_iota(jnp.int32, (SUBLANE, tm), 0)
    idx_o = jnp.zeros((SUBLANE, tm), jnp.int32)
    gate_o = jnp.zeros((SUBLANE, tm), F32)
    rank_o = jnp.zeros((SUBLANE, tm), F32)
    for k in range(top_k):
        idx_o = jnp.where(r_i == k, idxs[k], idx_o)
        gate_o = jnp.where(r_i == k, es[k] / tot, gate_o)
        rank_k = jnp.sum(jnp.where(onehots[k], before, 0.0), axis=0, keepdims=True)
        rank_o = jnp.where(r_i == k, rank_k, rank_o)
    idx_ref[...] = idx_o
    gate_ref[...] = gate_o
    rank_ref[...] = rank_o.astype(jnp.int32)


def _post(x, om_p, om_s, og_p, og_s, h, ga_blk, gb_blk, wo, g1, b1, wr_t, br, *, tm, alpha):
    m, d = x.shape
    n_exp = wr_t.shape[0]
    n_first = om_p.shape[0] // tm
    assert om_p.shape[0] % tm == 0 and om_s.shape[0] % tm == 0 and d == SUBLANE * LANE
    kern = functools.partial(_post_kernel, alpha=alpha, top_k=TOP_K, n_first=n_first)
    row = lambda i: (i, 0)
    const = lambda i: (0, 0)
    first = lambda i: (jnp.minimum(i, n_first - 1), 0)
    second = lambda i: (jnp.maximum(i - n_first, 0), 0)
    return pl.pallas_call(
        kern,
        grid=(m // tm,),
        in_specs=[pl.BlockSpec((tm, d), row),
                  pl.BlockSpec((tm, d), first), pl.BlockSpec((tm, d), second),
                  pl.BlockSpec((tm, d), first), pl.BlockSpec((tm, d), second),
                  pl.BlockSpec((tm, d), lambda i: (i, ga_blk)),
                  pl.BlockSpec((tm, d), lambda i: (i, gb_blk)),
                  pl.BlockSpec(wo.shape, const), pl.BlockSpec((1, d), const), pl.BlockSpec((1, d), const),
                  pl.BlockSpec(wr_t.shape, const), pl.BlockSpec((n_exp, 1), const)],
        out_specs=[pl.BlockSpec((tm, d), row),
                   pl.BlockSpec((tm * SUBLANE, LANE), row),
                   pl.BlockSpec((SUBLANE, tm), lambda i: (0, i)),
                   pl.BlockSpec((SUBLANE, tm), lambda i: (0, i)),
                   pl.BlockSpec((SUBLANE, tm), lambda i: (0, i)),
                   pl.BlockSpec((n_exp, 1), const)],
        out_shape=[jax.ShapeDtypeStruct((m, d), F32),
                   jax.ShapeDtypeStruct((m * SUBLANE, LANE), F32),
                   jax.ShapeDtypeStruct((SUBLANE, m), jnp.int32),
                   jax.ShapeDtypeStruct((SUBLANE, m), F32),
                   jax.ShapeDtypeStruct((SUBLANE, m), jnp.int32),
                   jax.ShapeDtypeStruct((n_exp, 1), F32)],
        scratch_shapes=[pltpu.VMEM((n_exp, 1), F32)],
        compiler_params=_cparams(("arbitrary",), VMEM_LIMIT),
        name="post_mix",
    )(x, om_p, om_s, og_p, og_s, h, h, wo, g1, b1, wr_t, br)


def _moe_kernel(be_ref, rows_ref, nused_ref, x_hbm, wgu_ref, bgu_ref, wd_ref, bd_ref, o_ref,
                xbuf, sem, wgu_bf, wd_bf, xb_ref, *, blk, d_exp):
    i = pl.program_id(0)
    n_used = nused_ref[0]
    slot = i % 2

    def row_copy(tok, s, r):
        return pltpu.make_async_copy(x_hbm.at[pl.ds(tok * SUBLANE, SUBLANE)],
                                     xbuf.at[s, pl.ds(r * SUBLANE, SUBLANE)], sem.at[s])

    def wait(s):
        pltpu.make_async_copy(x_hbm.at[pl.ds(0, blk * SUBLANE)], xbuf.at[s], sem.at[s]).wait()

    @pl.when(jnp.logical_and(i == 0, n_used > 0))
    def _():
        second = jnp.minimum(1, n_used - 1) * blk

        def body(r, carry):
            row_copy(rows_ref[r], 0, r).start()
            row_copy(rows_ref[second + r], 1, r).start()
            return carry
        lax.fori_loop(0, blk, body, 0)

    @pl.when(i < n_used)
    def _():
        e = be_ref[i]
        e_prev = be_ref[jnp.maximum(i - 1, 0)]

        @pl.when(jnp.logical_or(i == 0, e != e_prev))
        def _():
            wgu_bf[...] = wgu_ref[0].astype(BF16)
            wd_bf[...] = wd_ref[0].astype(BF16)

        wait(slot)
        for j in range(SUBLANE):
            xb_ref[:, j * LANE:(j + 1) * LANE] = _load_row_tiles(xbuf.at[slot], 0, blk, j).astype(BF16)
        nxt = jnp.minimum(i + 2, n_used - 1) * blk
        for r in range(blk):
            row_copy(rows_ref[nxt + r], slot, r).start(priority=r % 2)
        hh = _dot(xb_ref[...], wgu_bf[...]) + bgu_ref[0]
        gate = jnp.minimum(hh[:, :d_exp], SWIGLU_LIMIT)
        up = jnp.clip(hh[:, d_exp:], -SWIGLU_LIMIT, SWIGLU_LIMIT)
        act = (up + 1.0) * gate * _sigmoid(SWIGLU_ALPHA * gate)
        _store_row_tiles(o_ref, _dot(act.astype(BF16), wd_bf[...]) + bd_ref[0])

        @pl.when(i == n_used - 1)
        def _():
            wait(0)
            wait(1)

    @pl.when(i >= n_used)
    def _():
        o_ref[...] = jnp.zeros(o_ref.shape, F32)


def _moe_experts(block_e, rows, n_used, x1, w_gu, b_gu, w_down, b_down, *, blk):
    n_exp, d, d2 = w_gu.shape
    d_exp = d2 // 2
    nb = block_e.shape[0]
    kern = functools.partial(_moe_kernel, blk=blk, d_exp=d_exp)
    grid_spec = pltpu.PrefetchScalarGridSpec(
        num_scalar_prefetch=3,
        grid=(nb,),
        in_specs=[pl.BlockSpec(memory_space=pl.ANY),
                  pl.BlockSpec((1, d, d2), lambda i, be, rw, nu: (be[i], 0, 0)),
                  pl.BlockSpec((1, 1, d2), lambda i, be, rw, nu: (be[i], 0, 0)),
                  pl.BlockSpec((1, d_exp, d), lambda i, be, rw, nu: (be[i], 0, 0)),
                  pl.BlockSpec((1, 1, d), lambda i, be, rw, nu: (be[i], 0, 0))],
        out_specs=pl.BlockSpec((blk * SUBLANE, LANE), lambda i, be, rw, nu: (i, 0)),
        scratch_shapes=[pltpu.VMEM((2, blk * SUBLANE, LANE), F32), pltpu.SemaphoreType.DMA((2,)),
                        pltpu.VMEM((d, d2), BF16), pltpu.VMEM((d_exp, d), BF16),
                        pltpu.VMEM((blk, d), BF16)])
    return pl.pallas_call(
        kern,
        grid_spec=grid_spec,
        out_shape=jax.ShapeDtypeStruct((nb * blk * SUBLANE, LANE), F32),
        compiler_params=_cparams(("arbitrary",), VMEM_LIMIT),
        name="moe_experts",
    )(block_e, rows, n_used, x1, w_gu, b_gu.reshape(n_exp, 1, d2), w_down, b_down.reshape(n_exp, 1, d))


def _combine_kernel(dest_ref, x1_ref, gate_ref, ys_hbm, g2_ref, b2_ref, op_ref, os_ref, ybuf, sem,
                    *, tm, top_k, alpha, m_total, n_first):
    i = pl.program_id(0)
    nsteps = pl.num_programs(0)
    slot = i % 2
    n_rows = top_k * tm

    def row_copy(d, s, r):
        return pltpu.make_async_copy(ys_hbm.at[pl.ds(d * SUBLANE, SUBLANE)],
                                     ybuf.at[s, pl.ds(r * SUBLANE, SUBLANE)], sem.at[s])

    def wait(s):
        pltpu.make_async_copy(ys_hbm.at[pl.ds(0, n_rows * SUBLANE)], ybuf.at[s], sem.at[s]).wait()

    @pl.when(i == 0)
    def _():
        second = jnp.minimum(1, nsteps - 1) * tm

        def body(r, carry):
            k = r // tm
            t = r - k * tm
            row_copy(dest_ref[k * m_total + t], 0, r).start()
            row_copy(dest_ref[k * m_total + second + t], 1, r).start()
            return carry
        lax.fori_loop(0, n_rows, body, 0)

    wait(slot)
    g = gate_ref[...]
    parts = []
    for j in range(SUBLANE):
        acc = g[:, 0:1] * _load_row_tiles(ybuf.at[slot], 0, tm, j)
        for k in range(1, top_k):
            acc = acc + g[:, k:k + 1] * _load_row_tiles(ybuf.at[slot], k * tm, tm, j)
        parts.append(acc)
    y = jnp.concatenate(parts, axis=1)
    nxt = jnp.minimum(i + 2, nsteps - 1) * tm
    for r in range(n_rows):
        k, t = divmod(r, tm)
        row_copy(dest_ref[k * m_total + nxt + t], slot, r).start(priority=r % 2)
    res = _layer_norm(alpha * x1_ref[...] + y, g2_ref[...], b2_ref[...])

    @pl.when(i < n_first)
    def _():
        op_ref[...] = res

    @pl.when(i >= n_first)
    def _():
        os_ref[...] = res

    @pl.when(i == nsteps - 1)
    def _():
        wait(0)
        wait(1)


def _combine(dest_km, x1, gates_mk, ys, g2, b2, *, tm, alpha, m_first):
    m, d = x1.shape
    n_first = m_first // tm
    assert m_first % tm == 0 and 0 < m_first < m
    kern = functools.partial(_combine_kernel, tm=tm, top_k=TOP_K, alpha=alpha, m_total=m, n_first=n_first)
    grid_spec = pltpu.PrefetchScalarGridSpec(
        num_scalar_prefetch=1,
        grid=(m // tm,),
        in_specs=[pl.BlockSpec((tm, d), lambda i, ds: (i, 0)),
                  pl.BlockSpec((tm, SUBLANE), lambda i, ds: (i, 0)),
                  pl.BlockSpec(memory_space=pl.ANY),
                  pl.BlockSpec((1, d), lambda i, ds: (0, 0)),
                  pl.BlockSpec((1, d), lambda i, ds: (0, 0))],
        out_specs=[pl.BlockSpec((tm, d), lambda i, ds: (jnp.minimum(i, n_first - 1), 0)),
                   pl.BlockSpec((tm, d), lambda i, ds: (jnp.maximum(i - n_first, 0), 0))],
        scratch_shapes=[pltpu.VMEM((2, TOP_K * tm * SUBLANE, LANE), F32), pltpu.SemaphoreType.DMA((2,))])
    return pl.pallas_call(
        kern,
        grid_spec=grid_spec,
        out_shape=[jax.ShapeDtypeStruct((m_first, d), F32), jax.ShapeDtypeStruct((m - m_first, d), F32)],
        compiler_params=_cparams(("arbitrary",), VMEM_LIMIT),
        name="moe_combine",
    )(dest_km, x1, gates_mk, ys, g2, b2)


def _pack_w_in(w_in, splits):
    q_lora, kv_lora, rope, conv_ch, gv, nh, _, d, _ = splits
    offs = [0]
    for s in splits:
        offs.append(offs[-1] + s)
    part = [w_in[:, offs[i]:offs[i + 1]] for i in range(len(splits))]
    q_lat, kv_lat, k_r, qkv, z, a, b, g_a, g_b = part
    dm = w_in.shape[0]
    half = rope // 2
    zpad = lambda n: jnp.zeros((dm, n), w_in.dtype)
    k_sw = jnp.concatenate([k_r[:, half:], k_r[:, :half]], axis=1)
    small = jnp.concatenate([q_lat, kv_lat, k_r, zpad(LANE - rope), k_sw, zpad(LANE - rope),
                             a, b, zpad(LANE - 2 * nh)], axis=1)
    return jnp.concatenate([qkv, small, z, g_a, g_b], axis=1).astype(BF16)


def _pack_w_uq(w_uq, nope, rope):
    w = jnp.transpose(w_uq, (1, 0, 2))
    half = rope // 2
    r = w[..., nope:]
    zp = jnp.zeros(r.shape[:-1] + (LANE - rope,), w.dtype)
    r_sw = jnp.concatenate([r[..., half:], r[..., :half]], axis=-1)
    return jnp.concatenate([w[..., :nope], r, zp, r_sw, zp], axis=-1).astype(BF16)


def _rope_tables(pos, rope):
    half = rope // 2
    inv = ROPE_THETA ** (-jnp.arange(half, dtype=F32) / half)
    ang = pos.astype(F32)[:, None] * inv[None, :]
    cos, sin = jnp.cos(ang), jnp.sin(ang)
    zp = jnp.zeros((pos.shape[0], LANE - rope), F32)
    return (jnp.concatenate([cos, cos, zp], axis=1), jnp.concatenate([-sin, sin, zp], axis=1))


def _route_meta(idx_t, rank_t, counts, m, n_exp, blk):
    a = m * TOP_K
    counts = counts.astype(jnp.int32)
    padded = (counts + blk - 1) // blk * blk
    pad_end = jnp.cumsum(padded)
    pad_start = pad_end - padded
    experts = jnp.arange(n_exp, dtype=jnp.int32)
    e_km = idx_t[:TOP_K]
    start_km = jnp.sum(jnp.where(e_km[:, :, None] == experts, pad_start, 0), axis=-1)
    dest_km = (start_km + rank_t[:TOP_K]).astype(jnp.int32).reshape(a)
    nb = a // blk + n_exp
    tok_km = jnp.tile(jnp.arange(m, dtype=jnp.int32), TOP_K)
    filler = jnp.arange(nb * blk, dtype=jnp.int32) % m
    rows = filler.at[dest_km].set(tok_km, unique_indices=True, mode='promise_in_bounds')
    first_row = jnp.arange(nb, dtype=jnp.int32) * blk
    block_e = jnp.minimum(jnp.sum((pad_end[None, :] <= first_row[:, None]).astype(jnp.int32), axis=1),
                          n_exp - 1).astype(jnp.int32)
    n_used = (pad_end[-1] // blk).astype(jnp.int32).reshape(1)
    return block_e, rows, n_used, dest_km


def kernel(x_prompt, x_sample, cache_ckv, cache_krope, page_table, state_conv, state_ssm, w_in, q_norm_w, kv_norm_w, w_uq, w_uk, w_uv, conv_w, a_log, dt_bias, gdn_norm_w, w_o, ln1_g, ln1_b, w_router, b_router, w_gu, b_gu, w_down, b_down, ln2_g, ln2_b):
    bp, tp, d = x_prompt.shape
    bs, ts, _ = x_sample.shape
    depth = w_in.shape[0]
    q_lora, n_heads, qk = w_uq.shape[1:]
    kv_lora, _, nope = w_uk.shape[1:]
    rope = qk - nope
    vh = w_uv.shape[3]
    cw_taps, conv_ch = conv_w.shape[1:]
    g_heads = a_log.shape[1]
    dk, dv = state_ssm.shape[3:]
    n_exp = w_router.shape[2]
    page = cache_ckv.shape[2]
    past = page_table.shape[1] * page
    splits = (q_lora, kv_lora, rope, conv_ch, g_heads * dv, g_heads, g_heads, d, d)
    assert sum(splits) == w_in.shape[2]
    alpha = (2 * depth) ** 0.25
    scale = (nope + rope) ** -0.5 * LOG2E
    mp, ms = bp * tp, bs * ts
    m = mp + ms
    small_w = q_lora + kv_lora + 3 * LANE
    assert small_w == d and conv_ch % d == 0
    qkv_blk, small_blk = 0, conv_ch // d
    z_blk, ga_blk, gb_blk = small_blk + 1, small_blk + 2, small_blk + 3
    ab_blk = (conv_ch + q_lora + kv_lora + 2 * LANE) // LANE

    cos_p, sin_p = _rope_tables(jnp.arange(tp, dtype=jnp.int32), rope)
    cos_s, sin_s = _rope_tables(past + jnp.arange(ts, dtype=jnp.int32), rope)
    cos_t = jnp.concatenate([jnp.tile(cos_p, (bp, 1)), jnp.tile(cos_s, (bs, 1))], axis=0)
    sin_t = jnp.concatenate([jnp.tile(sin_p, (bp, 1)), jnp.tile(sin_s, (bs, 1))], axis=0)

    x_p, x_s = x_prompt.reshape(mp, d), x_sample.reshape(ms, d)
    outs = {k: [] for k in ("ckv_p", "kr_p", "conv_p", "ssm_p", "ckv_s", "kr_s", "conv_s", "ssm_s")}
    pad_lanes = lambda v: jnp.pad(v, (0, LANE - v.shape[0])).reshape(1, LANE)
    for l in range(depth):
        x = jnp.concatenate([x_p, x_s], axis=0)
        w_pack = _pack_w_in(w_in[l], splits)
        wq = _pack_w_uq(w_uq[l], nope, rope)
        wuk = jnp.transpose(w_uk[l], (1, 2, 0)).astype(BF16)
        wuv = jnp.transpose(w_uv[l], (1, 0, 2)).astype(BF16)
        h = _in_proj(x, w_pack, tm=_row_tile(m, 2304), tn=1024)
        qf, c_all, kr_all, kt, cb = _mla_proj(
            h, small_blk, cos_t, sin_t, q_norm_w[l].reshape(1, q_lora), kv_norm_w[l].reshape(1, kv_lora),
            wq, wuk, tm=256, q_lora=q_lora, kv_lora=kv_lora, nope=nope, rope=rope, scale=scale)
        om_p = _attn_prompt(qf, kt, cb, wuv, batch=bp, seq=tp, tq=128, tk=min(1024, tp))
        qs = qf[:, mp:, :].reshape(n_heads, bs, ts, qf.shape[-1])
        qs = jnp.transpose(qs, (1, 0, 2, 3)).reshape(bs, n_heads * ts, qf.shape[-1])
        om_s = _attn_sample(page_table, qs, c_all, kr_all, mp // ts, cache_ckv[l],
                            jnp.swapaxes(cache_krope[l], 1, 2), wuv)

        alog_p = pad_lanes(a_log[l])
        dtb_p = pad_lanes(dt_bias[l])
        nw = gdn_norm_w[l].reshape(1, dv)
        gdn_kw = dict(qkv_blk=qkv_blk, ab_blk=ab_blk, z_blk=z_blk, conv_w=cw_taps)
        zeros_conv = jnp.zeros((bp, SUBLANE, conv_ch), F32)
        zeros_ssm = jnp.zeros((bp, g_heads, dk, dv), F32)
        og_p, ssm_p, ncv_p = _gdn(h, 0, zeros_conv, zeros_ssm, conv_w[l], alog_p, dtb_p, nw,
                                  batch=bp, seq=tp, chunk=min(GDN_CHUNK, tp), group=bp, **gdn_kw)
        conv8_s = jnp.pad(state_conv[l], ((0, 0), (SUBLANE - (cw_taps - 1), 0), (0, 0)))
        og_s, ssm_s, ncv_s = _gdn(h, mp, conv8_s, state_ssm[l], conv_w[l], alog_p, dtb_p, nw,
                                  batch=bs, seq=ts, chunk=ts, group=math.gcd(bs, GDN_SAMPLE_GROUP), **gdn_kw)

        x1, x1t, idx_t, gate_t, rank_t, counts = _post(
            x, om_p, om_s, og_p, og_s, h, ga_blk, gb_blk, w_o[l].astype(BF16), ln1_g[l].reshape(1, d), ln1_b[l].reshape(1, d),
            w_router[l].T, b_router[l].reshape(n_exp, 1), tm=256, alpha=alpha)
        block_e, rows, n_used, dest_km = _route_meta(idx_t, rank_t, counts[:, 0], m, n_exp, MOE_ROWS)
        ys = _moe_experts(block_e, rows, n_used, x1t, w_gu[l], b_gu[l], w_down[l], b_down[l], blk=MOE_ROWS)
        x_p, x_s = _combine(dest_km, x1, gate_t.T, ys, ln2_g[l].reshape(1, d), ln2_b[l].reshape(1, d),
                            tm=128, alpha=alpha, m_first=mp)

        outs["ckv_p"].append(c_all[:mp].reshape(bp, tp, kv_lora))
        outs["kr_p"].append(kr_all[:mp].reshape(bp, tp, rope))
        outs["conv_p"].append(ncv_p[:, SUBLANE - (cw_taps - 1):, :])
        outs["ssm_p"].append(ssm_p)
        outs["ckv_s"].append(c_all[mp:].reshape(bs, ts, kv_lora))
        outs["kr_s"].append(kr_all[mp:].reshape(bs, ts, rope))
        outs["conv_s"].append(ncv_s[:, SUBLANE - (cw_taps - 1):, :])
        outs["ssm_s"].append(ssm_s)

    return (x_p.reshape(bp, tp, d), x_s.reshape(bs, ts, d),
            jnp.stack(outs["ckv_p"]), jnp.stack(outs["kr_p"]), jnp.stack(outs["conv_p"]), jnp.stack(outs["ssm_p"]),
            jnp.stack(outs["ckv_s"]), jnp.stack(outs["kr_s"]), jnp.stack(outs["conv_s"]), jnp.stack(outs["ssm_s"]))
```

```python
import functools
import math

import jax
import jax.numpy as jnp
from jax import lax
from jax.experimental import pallas as pl
from jax.experimental.pallas import tpu as pltpu

F32 = jnp.float32
BF16 = jnp.bfloat16

ROPE_THETA = 10000.0
NORM_EPS = 1e-6
TOP_K = 4
SWIGLU_LIMIT = 7.0
SWIGLU_ALPHA = 1.702
GDN_CHUNK = 128
GDN_DIAG = 16
GDN_SAMPLE_GROUP = 8
MOE_ROWS = 512
PAGE_CHUNK = 8
DIAG_VARIANTS = 4
LANE = 128
SUBLANE = 8
VMEM_LIMIT = 56 * 1024 * 1024
NEG = -1e30
LOG2E = 1.4426950408889634


def _cparams(sem, vmem=None):
    return pltpu.CompilerParams(dimension_semantics=sem, vmem_limit_bytes=vmem)


def _dot(a, b, **kw):
    return jnp.dot(a, b, preferred_element_type=F32, **kw)


def _dot_nt(a, b, **kw):
    return lax.dot_general(a, b, (((1,), (1,)), ((), ())), preferred_element_type=F32, **kw)


def _bdot(a, b):
    return lax.dot_general(a, b, (((2,), (1,)), ((0,), (0,))), preferred_element_type=F32)


def _bdot_nt(a, b):
    return lax.dot_general(a, b, (((2,), (2,)), ((0,), (0,))), preferred_element_type=F32)


def _bdot_tn(a, b):
    return lax.dot_general(a, b, (((1,), (1,)), ((0,), (0,))), preferred_element_type=F32)


def _sigmoid(x):
    return 1.0 / (1.0 + jnp.exp(-x))


def _layer_norm(v, g, b):
    mu = jnp.mean(v, -1, keepdims=True)
    vc = v - mu
    var = jnp.mean(vc * vc, -1, keepdims=True)
    return vc * lax.rsqrt(var + NORM_EPS) * g + b


def _inproj_kernel(x_ref, w_ref, o_ref, xb_ref):
    @pl.when(pl.program_id(1) == 0)
    def _():
        xb_ref[...] = x_ref[...].astype(BF16)

    o_ref[...] = _dot(xb_ref[...], w_ref[...])


def _row_tile(m, target):
    return max(t for t in range(SUBLANE, target + 1, SUBLANE) if m % t == 0)


def _in_proj(x, w, tm, tn):
    m, k = x.shape
    n = w.shape[1]
    return pl.pallas_call(
        _inproj_kernel,
        grid=(m // tm, n // tn),
        in_specs=[pl.BlockSpec((tm, k), lambda i, j: (i, 0)),
                  pl.BlockSpec((k, tn), lambda i, j: (0, j))],
        out_specs=pl.BlockSpec((tm, tn), lambda i, j: (i, j)),
        out_shape=jax.ShapeDtypeStruct((m, n), F32),
        scratch_shapes=[pltpu.VMEM((tm, k), BF16)],
        compiler_params=_cparams(("parallel", "arbitrary"), VMEM_LIMIT),
        name="in_proj",
    )(x, w)


def _mla_proj_kernel(h_ref, cos_ref, sin_ref, qnw_ref, kvnw_ref, wq_ref, wuk_ref,
                     qf_ref, c_ref, kr_ref, kt_ref, cb_ref, *, n_heads, q_lora, kv_lora, nope, rope, scale):
    hs = h_ref[...]
    cos = cos_ref[...]
    sin = sin_ref[...]
    q_lat = hs[:, :q_lora]
    qn = q_lat * lax.rsqrt(jnp.mean(q_lat * q_lat, -1, keepdims=True) + NORM_EPS) * qnw_ref[...]
    qn = qn.astype(BF16)
    kv = hs[:, q_lora:q_lora + kv_lora]
    c = kv * lax.rsqrt(jnp.mean(kv * kv, -1, keepdims=True) + NORM_EPS) * kvnw_ref[...]
    o = q_lora + kv_lora
    kr = hs[:, o:o + LANE] * cos + hs[:, o + LANE:o + 2 * LANE] * sin
    c_ref[...] = c
    kr_ref[...] = kr[:, :rope]
    cb = c.astype(BF16)
    cb_ref[...] = cb
    kfull = jnp.concatenate([cb, kr.astype(BF16)], axis=1)
    kw = kfull.shape[1]
    eye = (lax.broadcasted_iota(jnp.int32, (kw, kw), 0) == lax.broadcasted_iota(jnp.int32, (kw, kw), 1))
    kt_ref[...] = _dot_nt(eye.astype(BF16), kfull).astype(BF16)
    for h in range(n_heads):
        qh = _dot(qn, wq_ref[h])
        qa = _dot(qh[:, :nope].astype(BF16), wuk_ref[h])
        qr = qh[:, nope:nope + LANE] * cos + qh[:, nope + LANE:nope + 2 * LANE] * sin
        qf_ref[h, :, :kv_lora] = (qa * scale).astype(BF16)
        qf_ref[h, :, kv_lora:] = (qr * scale).astype(BF16)


def _mla_proj(h, col_blk, cos_t, sin_t, qnw, kvnw, wq, wuk, *, tm, q_lora, kv_lora, nope, rope, scale):
    m = h.shape[0]
    n_heads = wq.shape[0]
    wcol = q_lora + kv_lora + 3 * LANE
    kw = kv_lora + LANE
    kern = functools.partial(_mla_proj_kernel, n_heads=n_heads, q_lora=q_lora, kv_lora=kv_lora,
                             nope=nope, rope=rope, scale=scale)
    return pl.pallas_call(
        kern,
        grid=(m // tm,),
        in_specs=[pl.BlockSpec((tm, wcol), lambda i: (i, col_blk)),
                  pl.BlockSpec((tm, LANE), lambda i: (i, 0)),
                  pl.BlockSpec((tm, LANE), lambda i: (i, 0)),
                  pl.BlockSpec((1, q_lora), lambda i: (0, 0)),
                  pl.BlockSpec((1, kv_lora), lambda i: (0, 0)),
                  pl.BlockSpec(wq.shape, lambda i: (0, 0, 0)),
                  pl.BlockSpec(wuk.shape, lambda i: (0, 0, 0))],
        out_specs=[pl.BlockSpec((n_heads, tm, kw), lambda i: (0, i, 0)),
                   pl.BlockSpec((tm, kv_lora), lambda i: (i, 0)),
                   pl.BlockSpec((tm, rope), lambda i: (i, 0)),
                   pl.BlockSpec((kw, tm), lambda i: (0, i)),
                   pl.BlockSpec((tm, kv_lora), lambda i: (i, 0))],
        out_shape=[jax.ShapeDtypeStruct((n_heads, m, kw), BF16),
                   jax.ShapeDtypeStruct((m, kv_lora), F32),
                   jax.ShapeDtypeStruct((m, rope), F32),
                   jax.ShapeDtypeStruct((kw, m), BF16),
                   jax.ShapeDtypeStruct((m, kv_lora), BF16)],
        compiler_params=_cparams(("parallel",), VMEM_LIMIT),
        name="mla_proj",
    )(h, cos_t, sin_t, qnw, kvnw, wq, wuk)


def _softmax_step(s, v, m_ref, l_ref, acc_ref):
    m_prev = m_ref[...]
    m_new = jnp.maximum(m_prev, jnp.max(s, -1, keepdims=True))
    alpha = jnp.exp2(m_prev - m_new)
    p = jnp.exp2(s - m_new)
    l_ref[...] = alpha * l_ref[...] + jnp.sum(p, -1, keepdims=True)
    acc_ref[...] = alpha * acc_ref[...] + _dot(p.astype(BF16), v)
    m_ref[...] = m_new


def _attn_prompt_kernel(qf_ref, kt_ref, v_ref, wuv_ref, o_ref, m_ref, l_ref, acc_ref, *, n_heads, tq, tk, dv, vh, n_split):
    i = pl.program_id(1)
    m_ref[...] = jnp.full(m_ref.shape, NEG, F32)
    l_ref[...] = jnp.zeros(l_ref.shape, F32)
    acc_ref[...] = jnp.zeros(acc_ref.shape, F32)
    n_full = (i * tq) // tk
    off = i * tq - n_full * tk
    hs = n_heads // n_split
    rs = hs * tq

    def step(j, width, masked):
        start = pl.multiple_of(j * tk, tk)
        kt = kt_ref[:, pl.ds(start, width)]
        v = v_ref[pl.ds(start, width), :]
        for g in range(n_split):
            q = qf_ref[g * hs:(g + 1) * hs].reshape(rs, qf_ref.shape[-1])
            s = _dot(q, kt)
            if masked:
                row = lax.broadcasted_iota(jnp.int32, (tq, width), 0)
                col = lax.broadcasted_iota(jnp.int32, (tq, width), 1)
                s = jnp.where((col <= row + off)[None], s.reshape(hs, tq, width), NEG).reshape(rs, width)
            sl = slice(g * rs, (g + 1) * rs)
            _softmax_step(s, v, m_ref.at[sl], l_ref.at[sl], acc_ref.at[sl])

    def body(j, carry):
        step(j, tk, False)
        return carry

    lax.fori_loop(0, n_full, body, 0)
    need = off + tq
    widths = [tk * (v + 1) // DIAG_VARIANTS for v in range(DIAG_VARIANTS)]
    for v, width in enumerate(widths):
        lo = widths[v - 1] if v else 0

        @pl.when(jnp.logical_and(need > lo, need <= width))
        def _(width=width):
            step(n_full, width, True)

    o = acc_ref[...] / l_ref[...]
    for h in range(n_heads):
        oh = o[h * tq:(h + 1) * tq].astype(BF16)
        o_ref[:, h * vh:(h + 1) * vh] = _dot(oh, wuv_ref[h])


def _attn_prompt(qf, kt, cb, wuv, *, batch, seq, tq, tk):
    n_heads, _, kw = qf.shape
    dv, vh = wuv.shape[1], wuv.shape[2]
    nq = seq // tq
    rows = n_heads * tq
    assert seq % tk == 0 and tk % tq == 0 and tk % (DIAG_VARIANTS * LANE) == 0
    kern = functools.partial(_attn_prompt_kernel, n_heads=n_heads, tq=tq, tk=tk, dv=dv, vh=vh, n_split=2)
    return pl.pallas_call(
        kern,
        grid=(batch, nq),
        in_specs=[pl.BlockSpec((n_heads, tq, kw), lambda b, i: (0, b * nq + i, 0)),
                  pl.BlockSpec((kw, seq), lambda b, i: (0, b)),
                  pl.BlockSpec((seq, dv), lambda b, i: (b, 0)),
                  pl.BlockSpec(wuv.shape, lambda b, i: (0, 0, 0))],
        out_specs=pl.BlockSpec((tq, n_heads * vh), lambda b, i: (b * nq + i, 0)),
        out_shape=jax.ShapeDtypeStruct((batch * seq, n_heads * vh), F32),
        scratch_shapes=[pltpu.VMEM((rows, 1), F32), pltpu.VMEM((rows, 1), F32),
                        pltpu.VMEM((rows, dv), F32)],
        compiler_params=_cparams(("parallel", "arbitrary"), VMEM_LIMIT),
        name="attn_prompt",
    )(qf, kt, cb, wuv)


def _attn_sample_kernel(pt_ref, q_ref, cn_ref, krn_ref, cc_hbm, cr_hbm, wuv_ref, o_ref,
                        cbuf, rbuf, sem, *, n_pages, page, n_heads, ts, dv, dr, vh):
    b = pl.program_id(0)
    nb = pl.num_programs(0)
    slot = b % 2
    rows = n_heads * ts

    def fetch(bb, s):
        for p in range(n_pages):
            pg = pt_ref[bb, p]
            pltpu.make_async_copy(cc_hbm.at[pg], cbuf.at[s, p], sem.at[0, s]).start()
            pltpu.make_async_copy(cr_hbm.at[pg], rbuf.at[s, p], sem.at[1, s]).start(priority=1)

    def wait(s):
        pltpu.make_async_copy(cc_hbm.at[pl.ds(0, n_pages)], cbuf.at[s], sem.at[0, s]).wait()
        pltpu.make_async_copy(cr_hbm.at[pl.ds(0, n_pages)], rbuf.at[s], sem.at[1, s]).wait()

    @pl.when(b == 0)
    def _():
        fetch(0, 0)

    wait(slot)
    fetch(jnp.minimum(b + 1, nb - 1), 1 - slot)

    q = q_ref[0].astype(F32)
    qc = q[:, :dv]
    qr = q[:, dv:dv + dr]
    n_chunks = n_pages // PAGE_CHUNK
    ck = PAGE_CHUNK * page
    parts = []
    for ch in range(n_chunks):
        c_ch = cbuf[slot, ch * PAGE_CHUNK:(ch + 1) * PAGE_CHUNK].reshape(ck, dv)
        r_ch = jnp.concatenate([rbuf[slot, ch * PAGE_CHUNK + u] for u in range(PAGE_CHUNK)], axis=1)
        parts.append(_dot_nt(qc, c_ch) + _dot(qr, r_ch))
    cn = jnp.concatenate([cn_ref[...], jnp.zeros((LANE - ts, dv), F32)], axis=0)
    krn = jnp.concatenate([krn_ref[...], jnp.zeros((LANE - ts, dr), F32)], axis=0)
    row = lax.broadcasted_iota(jnp.int32, (ts, LANE), 0)
    col = lax.broadcasted_iota(jnp.int32, (ts, LANE), 1)
    s_new = (_dot_nt(qc, cn) + _dot_nt(qr, krn)).reshape(n_heads, ts, LANE)
    s_new = jnp.where((col <= row)[None], s_new, NEG).reshape(rows, LANE)

    m = jnp.max(s_new, -1, keepdims=True)
    for s in parts:
        m = jnp.maximum(m, jnp.max(s, -1, keepdims=True))
    p_new = jnp.exp2(s_new - m)
    l = jnp.sum(p_new, -1, keepdims=True)
    acc = _dot(p_new, cn)
    for ch in range(n_chunks):
        p = jnp.exp2(parts[ch] - m)
        l = l + jnp.sum(p, -1, keepdims=True)
        acc = acc + _dot(p, cbuf[slot, ch * PAGE_CHUNK:(ch + 1) * PAGE_CHUNK].reshape(ck, dv))
    o = acc / l
    for h in range(n_heads):
        oh = o[h * ts:(h + 1) * ts].astype(BF16)
        o_ref[:, h * vh:(h + 1) * vh] = _dot(oh, wuv_ref[h])

    @pl.when(b == nb - 1)
    def _():
        wait(1 - slot)


def _attn_sample(page_table, qs, c_all, kr_all, row_blk_off, cache_c, cache_rt, wuv):
    bs, rows, kw = qs.shape
    n_heads, dv, vh = wuv.shape
    ts = rows // n_heads
    n_pages = page_table.shape[1]
    page = cache_c.shape[1]
    dr = cache_rt.shape[1]
    assert n_pages % PAGE_CHUNK == 0
    kern = functools.partial(_attn_sample_kernel, n_pages=n_pages, page=page, n_heads=n_heads, ts=ts,
                             dv=dv, dr=dr, vh=vh)
    grid_spec = pltpu.PrefetchScalarGridSpec(
        num_scalar_prefetch=1,
        grid=(bs,),
        in_specs=[pl.BlockSpec((1, rows, kw), lambda b, pt: (b, 0, 0)),
                  pl.BlockSpec((ts, dv), lambda b, pt: (row_blk_off + b, 0)),
                  pl.BlockSpec((ts, dr), lambda b, pt: (row_blk_off + b, 0)),
                  pl.BlockSpec(memory_space=pl.ANY),
                  pl.BlockSpec(memory_space=pl.ANY),
                  pl.BlockSpec(wuv.shape, lambda b, pt: (0, 0, 0))],
        out_specs=pl.BlockSpec((ts, n_heads * vh), lambda b, pt: (b, 0)),
        scratch_shapes=[pltpu.VMEM((2, n_pages, page, dv), F32), pltpu.VMEM((2, n_pages, dr, page), F32),
                        pltpu.SemaphoreType.DMA((2, 2))])
    return pl.pallas_call(
        kern,
        grid_spec=grid_spec,
        out_shape=jax.ShapeDtypeStruct((bs * ts, n_heads * vh), F32),
        compiler_params=_cparams(("arbitrary",), VMEM_LIMIT),
        name="attn_sample",
    )(page_table, qs, c_all, kr_all, cache_c, cache_rt, wuv)


def _neumann_inv(low, eye, steps):
    p = eye - low
    x = low
    for _ in range(steps):
        x = _bdot(x, x)
        p = p + _bdot(p, x)
    return p


def _gdn_kernel(*refs, n_in, n_seq, chunk, n_heads, dk, dv, conv_w):
    qkv_refs, ab_refs, z_refs = refs[:n_in], refs[n_in:2 * n_in], refs[2 * n_in:3 * n_in]
    (cs_ref, s0_ref, cw_ref, alog_ref, dtb_ref, nw_ref,
     o_ref, sfin_ref, ncv_ref, xp_ref, s_ref) = refs[3 * n_in:]
    n = pl.program_id(1)
    c = chunk
    per = n_seq // n_in
    n_prob = n_seq * n_heads
    hi = lax.Precision.HIGHEST

    def seq_rows(group, g):
        k = g % per
        return group[g // per][k * c:(k + 1) * c, :]

    @pl.when(n == 0)
    def _():
        s_ref[...] = s0_ref[...].reshape(n_prob, dk, dv)
        xp_ref[:, 0:SUBLANE, :] = cs_ref[...]

    r_i = lax.broadcasted_iota(jnp.int32, (c, c), 0)
    c_i = lax.broadcasted_iota(jnp.int32, (c, c), 1)
    causal = (c_i <= r_i)[None]
    strict = (c_i < r_i)[None]
    eye_c = (r_i == c_i).astype(F32)[None]
    db = min(GDN_DIAG, c)
    n_blk = c // db
    sh = int(math.log2(db))
    same_blk = (jnp.right_shift(r_i, sh) == jnp.right_shift(c_i, sh))[None]
    r_l = lax.broadcasted_iota(jnp.int32, (LANE, LANE), 0)
    c_l = lax.broadcasted_iota(jnp.int32, (LANE, LANE), 1)
    eye_l = (r_l == c_l).astype(F32)
    tril = (c_i <= r_i).astype(F32)
    cw = cw_ref[...]
    hk = n_heads * dk

    q_l, k_l, v_l, z_l, beta_l, gcol_l, grow_l, tails = [], [], [], [], [], [], [], []
    for g in range(n_seq):
        xp_ref[g, SUBLANE:SUBLANE + c, :] = seq_rows(qkv_refs, g)
        conv = xp_ref[g, pl.ds(SUBLANE - (conv_w - 1), c), :] * cw[0:1, :]
        for j in range(1, conv_w):
            conv = conv + xp_ref[g, pl.ds(SUBLANE - (conv_w - 1) + j, c), :] * cw[j:j + 1, :]
        tail = xp_ref[g, c:c + SUBLANE, :]
        xp_ref[g, 0:SUBLANE, :] = tail
        tails.append(tail)
        act = conv * _sigmoid(conv)
        ab = seq_rows(ab_refs, g)
        apb = ab + dtb_ref[...]
        softplus = jnp.maximum(apb, 0.0) + jnp.log(1.0 + jnp.exp(-jnp.abs(apb)))
        gfull = -jnp.exp(alog_ref[...]) * softplus
        betaf = _sigmoid(ab)
        gc = _dot(tril, gfull, precision=hi)
        gc_t = _dot_nt(eye_l, gc, precision=hi)
        zg = seq_rows(z_refs, g)
        for h in range(n_heads):
            q_l.append(act[:, h * dk:(h + 1) * dk])
            k_l.append(act[:, hk + h * dk:hk + (h + 1) * dk])
            v_l.append(act[:, 2 * hk + h * dv:2 * hk + (h + 1) * dv])
            z_l.append(zg[:, h * dv:(h + 1) * dv])
            beta_l.append(betaf[:, n_heads + h:n_heads + h + 1])
            gcol_l.append(gc[:, h:h + 1])
            grow_l.append(gc_t[h:h + 1, :])

    q = jnp.stack(q_l)
    k = jnp.stack(k_l)
    v = jnp.stack(v_l)
    z = jnp.stack(z_l)
    beta = jnp.stack(beta_l)
    gcol = jnp.stack(gcol_l)
    grow = jnp.stack(grow_l)
    q = q * lax.rsqrt(jnp.sum(q * q, -1, keepdims=True) + NORM_EPS) * (dk ** -0.5)
    k = k * lax.rsqrt(jnp.sum(k * k, -1, keepdims=True) + NORM_EPS)
    decay = jnp.where(causal, jnp.exp(jnp.where(causal, gcol - grow, 0.0)), 0.0)
    kb = k * beta
    lower = jnp.where(strict, _bdot_nt(kb, k) * decay, 0.0)
    attn = _bdot_nt(q, k) * decay
    l_d = jnp.where(same_blk, lower, 0.0)
    t_inv = _neumann_inv(l_d, eye_c, int(math.log2(db)) - 1)
    if n_blk > 1:
        m_inv = _neumann_inv(_bdot(t_inv, lower - l_d), eye_c, int(math.log2(n_blk)) - 1)
        t_inv = _bdot(m_inv, t_inv)
    egc = jnp.exp(gcol)
    uw = _bdot(t_inv, jnp.concatenate([v * beta, kb * egc], axis=2))
    s = s_ref[...]
    v_new = uw[:, :, :dv] - _bdot(uw[:, :, dv:], s)
    o = _bdot(q * egc, s) + _bdot(attn, v_new)
    glast = gcol[:, c - 1:c, :]
    kdec = k * jnp.exp(glast - gcol)
    s_new = s * jnp.exp(glast) + _bdot_tn(kdec, v_new)
    s_ref[...] = s_new
    on = o * lax.rsqrt(jnp.mean(o * o, -1, keepdims=True) + NORM_EPS) * nw_ref[...] * (z * _sigmoid(z))
    for g in range(n_seq):
        for h in range(n_heads):
            o_ref[g, :, h * dv:(h + 1) * dv] = on[g * n_heads + h]

    @pl.when(n == pl.num_programs(1) - 1)
    def _():
        sfin_ref[...] = s_new.reshape(n_seq, n_heads, dk, dv)
        for g in range(n_seq):
            ncv_ref[g] = tails[g]


def _gdn(h, row_off, conv_state8, ssm_state, cw, alog_p, dtb_p, nw, *, batch, seq, chunk, group,
         qkv_blk, ab_blk, z_blk, conv_w):
    _, n_heads, dk, dv = ssm_state.shape
    nc = seq // chunk
    ch = cw.shape[1]
    hd = n_heads * dv
    contiguous = nc == 1
    n_in = 1 if contiguous else group
    assert batch % group == 0 and row_off % (group * chunk) == 0
    kern = functools.partial(_gdn_kernel, n_in=n_in, n_seq=group, chunk=chunk, n_heads=n_heads,
                             dk=dk, dv=dv, conv_w=conv_w)
    if contiguous:
        rb = group * chunk
        row_maps = [lambda i, n: row_off // rb + i]
    else:
        rb = chunk
        row_maps = [(lambda i, n, g=g: row_off // rb + (i * group + g) * nc + n) for g in range(group)]

    def specs(width, col_blk):
        return [pl.BlockSpec((rb, width), lambda i, n, r=r: (r(i, n), col_blk)) for r in row_maps]

    in_specs = specs(ch, qkv_blk) + specs(LANE, ab_blk) + specs(hd, z_blk)
    in_specs += [pl.BlockSpec((group, SUBLANE, ch), lambda i, n: (i, 0, 0)),
                 pl.BlockSpec((group, n_heads, dk, dv), lambda i, n: (i, 0, 0, 0)),
                 pl.BlockSpec(cw.shape, lambda i, n: (0, 0)),
                 pl.BlockSpec((1, LANE), lambda i, n: (0, 0)),
                 pl.BlockSpec((1, LANE), lambda i, n: (0, 0)),
                 pl.BlockSpec((1, dv), lambda i, n: (0, 0))]
    o3, sfin, ncv = pl.pallas_call(
        kern,
        grid=(batch // group, nc),
        in_specs=in_specs,
        out_specs=[pl.BlockSpec((group, chunk, hd), lambda i, n: (i, n, 0)),
                   pl.BlockSpec((group, n_heads, dk, dv), lambda i, n: (i, 0, 0, 0)),
                   pl.BlockSpec((group, SUBLANE, ch), lambda i, n: (i, 0, 0))],
        out_shape=[jax.ShapeDtypeStruct((batch, seq, hd), F32),
                   jax.ShapeDtypeStruct(ssm_state.shape, F32),
                   jax.ShapeDtypeStruct((batch, SUBLANE, ch), F32)],
        scratch_shapes=[pltpu.VMEM((group, chunk + SUBLANE, ch), F32),
                        pltpu.VMEM((group * n_heads, dk, dv), F32)],
        compiler_params=_cparams(("parallel", "arbitrary"), VMEM_LIMIT),
        name="gdn",
    )(*([h] * (3 * n_in)), conv_state8, ssm_state, cw, alog_p, dtb_p, nw)
    return o3.reshape(batch * seq, hd), sfin, ncv


def _store_row_tiles(ref, val):
    rows = val.shape[0]
    for j in range(SUBLANE):
        ref[pl.ds(j, rows, stride=SUBLANE), :] = val[:, j * LANE:(j + 1) * LANE]


def _load_row_tiles(ref, start, rows, j):
    return ref[pl.ds(start * SUBLANE + j, rows, stride=SUBLANE), :]


def _post_kernel(x_ref, omp_ref, oms_ref, ogp_ref, ogs_ref, ga_ref, gb_ref, wo_ref, g1_ref, b1_ref, wr_ref, br_ref,
                 x1_ref, x1t_ref, idx_ref, gate_ref, rank_ref, cnt_ref, carry_ref, *, alpha, top_k, n_first):
    @pl.when(pl.program_id(0) == 0)
    def _():
        carry_ref[...] = jnp.zeros(carry_ref.shape, F32)

    first = pl.program_id(0) < n_first
    om = jnp.where(first, omp_ref[...], oms_ref[...])
    og = jnp.where(first, ogp_ref[...], ogs_ref[...])
    mix = _sigmoid(ga_ref[...]) * om + _sigmoid(gb_ref[...]) * og
    y = _dot(mix.astype(BF16), wo_ref[...])
    x1 = _layer_norm(alpha * x_ref[...] + y, g1_ref[...], b1_ref[...])
    x1_ref[...] = x1
    _store_row_tiles(x1t_ref, x1)
    logits = _dot_nt(wr_ref[...], x1, precision=lax.Precision.HIGHEST) + br_ref[...]
    n_exp, tm = logits.shape
    e_i = lax.broadcasted_iota(jnp.int32, (n_exp, tm), 0)
    vals, idxs = [], []
    for _ in range(top_k):
        mx = jnp.max(logits, axis=0, keepdims=True)
        ix = jnp.min(jnp.where(logits == mx, e_i, n_exp), axis=0, keepdims=True)
        vals.append(mx)
        idxs.append(ix)
        logits = jnp.where(e_i == ix, -jnp.inf, logits)
    es = [jnp.exp(v - vals[0]) for v in vals]
    tot = es[0]
    for e in es[1:]:
        tot = tot + e
    onehots = [e_i == ix for ix in idxs]
    sel = onehots[0].astype(F32)
    for oh in onehots[1:]:
        sel = sel + oh.astype(F32)
    t_r = lax.broadcasted_iota(jnp.int32, (tm, tm), 0)
    t_c = lax.broadcasted_iota(jnp.int32, (tm, tm), 1)
    before = _dot(sel.astype(BF16), (t_r < t_c).astype(BF16)) + carry_ref[...]
    carry = carry_ref[...] + jnp.sum(sel, axis=1, keepdims=True)
    carry_ref[...] = carry
    cnt_ref[...] = carry
    r_i = lax.broadcasted_iota(jnp.int32, (SUBLANE, tm), 0)
    idx_o = jnp.zeros((SUBLANE, tm), jnp.int32)
    gate_o = jnp.zeros((SUBLANE, tm), F32)
    rank_o = jnp.zeros((SUBLANE, tm), F32)
    for k in range(top_k):
        idx_o = jnp.where(r_i == k, idxs[k], idx_o)
        gate_o = jnp.where(r_i == k, es[k] / tot, gate_o)
        rank_k = jnp.sum(jnp.where(onehots[k], before, 0.0), axis=0, keepdims=True)
        rank_o = jnp.where(r_i == k, rank_k, rank_o)
    idx_ref[...] = idx_o
    gate_ref[...] = gate_o
    rank_ref[...] = rank_o.astype(jnp.int32)


def _post(x, om_p, om_s, og_p, og_s, h, ga_blk, gb_blk, wo, g1, b1, wr_t, br, *, tm, alpha):
    m, d = x.shape
    n_exp = wr_t.shape[0]
    n_first = om_p.shape[0] // tm
    assert om_p.shape[0] % tm == 0 and om_s.shape[0] % tm == 0 and d == SUBLANE * LANE
    kern = functools.partial(_post_kernel, alpha=alpha, top_k=TOP_K, n_first=n_first)
    row = lambda i: (i, 0)
    const = lambda i: (0, 0)
    first = lambda i: (jnp.minimum(i, n_first - 1), 0)
    second = lambda i: (jnp.maximum(i - n_first, 0), 0)
    return pl.pallas_call(
        kern,
        grid=(m // tm,),
        in_specs=[pl.BlockSpec((tm, d), row),
                  pl.BlockSpec((tm, d), first), pl.BlockSpec((tm, d), second),
                  pl.BlockSpec((tm, d), first), pl.BlockSpec((tm, d), second),
                  pl.BlockSpec((tm, d), lambda i: (i, ga_blk)),
                  pl.BlockSpec((tm, d), lambda i: (i, gb_blk)),
                  pl.BlockSpec(wo.shape, const), pl.BlockSpec((1, d), const), pl.BlockSpec((1, d), const),
                  pl.BlockSpec(wr_t.shape, const), pl.BlockSpec((n_exp, 1), const)],
        out_specs=[pl.BlockSpec((tm, d), row),
                   pl.BlockSpec((tm * SUBLANE, LANE), row),
                   pl.BlockSpec((SUBLANE, tm), lambda i: (0, i)),
                   pl.BlockSpec((SUBLANE, tm), lambda i: (0, i)),
                   pl.BlockSpec((SUBLANE, tm), lambda i: (0, i)),
                   pl.BlockSpec((n_exp, 1), const)],
        out_shape=[jax.ShapeDtypeStruct((m, d), F32),
                   jax.ShapeDtypeStruct((m * SUBLANE, LANE), F32),
                   jax.ShapeDtypeStruct((SUBLANE, m), jnp.int32),
                   jax.ShapeDtypeStruct((SUBLANE, m), F32),
                   jax.ShapeDtypeStruct((SUBLANE, m), jnp.int32),
                   jax.ShapeDtypeStruct((n_exp, 1), F32)],
        scratch_shapes=[pltpu.VMEM((n_exp, 1), F32)],
        compiler_params=_cparams(("arbitrary",), VMEM_LIMIT),
        name="post_mix",
    )(x, om_p, om_s, og_p, og_s, h, h, wo, g1, b1, wr_t, br)


def _moe_kernel(be_ref, rows_ref, nused_ref, x_hbm, wgu_ref, bgu_ref, wd_ref, bd_ref, o_ref,
                xbuf, sem, wgu_bf, wd_bf, xb_ref, *, blk, d_exp):
    i = pl.program_id(0)
    n_used = nused_ref[0]
    slot = i % 2

    def row_copy(tok, s, r):
        return pltpu.make_async_copy(x_hbm.at[pl.ds(tok * SUBLANE, SUBLANE)],
                                     xbuf.at[s, pl.ds(r * SUBLANE, SUBLANE)], sem.at[s])

    def wait(s):
        pltpu.make_async_copy(x_hbm.at[pl.ds(0, blk * SUBLANE)], xbuf.at[s], sem.at[s]).wait()

    @pl.when(jnp.logical_and(i == 0, n_used > 0))
    def _():
        second = jnp.minimum(1, n_used - 1) * blk

        def body(r, carry):
            row_copy(rows_ref[r], 0, r).start()
            row_copy(rows_ref[second + r], 1, r).start()
            return carry
        lax.fori_loop(0, blk, body, 0)

    @pl.when(i < n_used)
    def _():
        e = be_ref[i]
        e_prev = be_ref[jnp.maximum(i - 1, 0)]

        @pl.when(jnp.logical_or(i == 0, e != e_prev))
        def _():
            wgu_bf[...] = wgu_ref[0].astype(BF16)
            wd_bf[...] = wd_ref[0].astype(BF16)

        wait(slot)
        for j in range(SUBLANE):
            xb_ref[:, j * LANE:(j + 1) * LANE] = _load_row_tiles(xbuf.at[slot], 0, blk, j).astype(BF16)
        nxt = jnp.minimum(i + 2, n_used - 1) * blk
        for r in range(blk):
            row_copy(rows_ref[nxt + r], slot, r).start(priority=r % 2)
        hh = _dot(xb_ref[...], wgu_bf[...]) + bgu_ref[0]
        gate = jnp.minimum(hh[:, :d_exp], SWIGLU_LIMIT)
        up = jnp.clip(hh[:, d_exp:], -SWIGLU_LIMIT, SWIGLU_LIMIT)
        act = (up + 1.0) * gate * _sigmoid(SWIGLU_ALPHA * gate)
        _store_row_tiles(o_ref, _dot(act.astype(BF16), wd_bf[...]) + bd_ref[0])

        @pl.when(i == n_used - 1)
        def _():
            wait(0)
            wait(1)

    @pl.when(i >= n_used)
    def _():
        o_ref[...] = jnp.zeros(o_ref.shape, F32)


def _moe_experts(block_e, rows, n_used, x1, w_gu, b_gu, w_down, b_down, *, blk):
    n_exp, d, d2 = w_gu.shape
    d_exp = d2 // 2
    nb = block_e.shape[0]
    kern = functools.partial(_moe_kernel, blk=blk, d_exp=d_exp)
    grid_spec = pltpu.PrefetchScalarGridSpec(
        num_scalar_prefetch=3,
        grid=(nb,),
        in_specs=[pl.BlockSpec(memory_space=pl.ANY),
                  pl.BlockSpec((1, d, d2), lambda i, be, rw, nu: (be[i], 0, 0)),
                  pl.BlockSpec((1, 1, d2), lambda i, be, rw, nu: (be[i], 0, 0)),
                  pl.BlockSpec((1, d_exp, d), lambda i, be, rw, nu: (be[i], 0, 0)),
                  pl.BlockSpec((1, 1, d), lambda i, be, rw, nu: (be[i], 0, 0))],
        out_specs=pl.BlockSpec((blk * SUBLANE, LANE), lambda i, be, rw, nu: (i, 0)),
        scratch_shapes=[pltpu.VMEM((2, blk * SUBLANE, LANE), F32), pltpu.SemaphoreType.DMA((2,)),
                        pltpu.VMEM((d, d2), BF16), pltpu.VMEM((d_exp, d), BF16),
                        pltpu.VMEM((blk, d), BF16)])
    return pl.pallas_call(
        kern,
        grid_spec=grid_spec,
        out_shape=jax.ShapeDtypeStruct((nb * blk * SUBLANE, LANE), F32),
        compiler_params=_cparams(("arbitrary",), VMEM_LIMIT),
        name="moe_experts",
    )(block_e, rows, n_used, x1, w_gu, b_gu.reshape(n_exp, 1, d2), w_down, b_down.reshape(n_exp, 1, d))


def _combine_kernel(dest_ref, x1_ref, gate_ref, ys_hbm, g2_ref, b2_ref, op_ref, os_ref, ybuf, sem,
                    *, tm, top_k, alpha, m_total, n_first):
    i = pl.program_id(0)
    nsteps = pl.num_programs(0)
    slot = i % 2
    n_rows = top_k * tm

    def row_copy(d, s, r):
        return pltpu.make_async_copy(ys_hbm.at[pl.ds(d * SUBLANE, SUBLANE)],
                                     ybuf.at[s, pl.ds(r * SUBLANE, SUBLANE)], sem.at[s])

    def wait(s):
        pltpu.make_async_copy(ys_hbm.at[pl.ds(0, n_rows * SUBLANE)], ybuf.at[s], sem.at[s]).wait()

    @pl.when(i == 0)
    def _():
        second = jnp.minimum(1, nsteps - 1) * tm

        def body(r, carry):
            k = r // tm
            t = r - k * tm
            row_copy(dest_ref[k * m_total + t], 0, r).start()
            row_copy(dest_ref[k * m_total + second + t], 1, r).start()
            return carry
        lax.fori_loop(0, n_rows, body, 0)

    wait(slot)
    g = gate_ref[...]
    parts = []
    for j in range(SUBLANE):
        acc = g[:, 0:1] * _load_row_tiles(ybuf.at[slot], 0, tm, j)
        for k in range(1, top_k):
            acc = acc + g[:, k:k + 1] * _load_row_tiles(ybuf.at[slot], k * tm, tm, j)
        parts.append(acc)
    y = jnp.concatenate(parts, axis=1)
    nxt = jnp.minimum(i + 2, nsteps - 1) * tm
    for r in range(n_rows):
        k, t = divmod(r, tm)
        row_copy(dest_ref[k * m_total + nxt + t], slot, r).start(priority=r % 2)
    res = _layer_norm(alpha * x1_ref[...] + y, g2_ref[...], b2_ref[...])

    @pl.when(i < n_first)
    def _():
        op_ref[...] = res

    @pl.when(i >= n_first)
    def _():
        os_ref[...] = res

    @pl.when(i == nsteps - 1)
    def _():
        wait(0)
        wait(1)


def _combine(dest_km, x1, gates_mk, ys, g2, b2, *, tm, alpha, m_first):
    m, d = x1.shape
    n_first = m_first // tm
    assert m_first % tm == 0 and 0 < m_first < m
    kern = functools.partial(_combine_kernel, tm=tm, top_k=TOP_K, alpha=alpha, m_total=m, n_first=n_first)
    grid_spec = pltpu.PrefetchScalarGridSpec(
        num_scalar_prefetch=1,
        grid=(m // tm,),
        in_specs=[pl.BlockSpec((tm, d), lambda i, ds: (i, 0)),
                  pl.BlockSpec((tm, SUBLANE), lambda i, ds: (i, 0)),
                  pl.BlockSpec(memory_space=pl.ANY),
                  pl.BlockSpec((1, d), lambda i, ds: (0, 0)),
                  pl.BlockSpec((1, d), lambda i, ds: (0, 0))],
        out_specs=[pl.BlockSpec((tm, d), lambda i, ds: (jnp.minimum(i, n_first - 1), 0)),
                   pl.BlockSpec((tm, d), lambda i, ds: (jnp.maximum(i - n_first, 0), 0))],
        scratch_shapes=[pltpu.VMEM((2, TOP_K * tm * SUBLANE, LANE), F32), pltpu.SemaphoreType.DMA((2,))])
    return pl.pallas_call(
        kern,
        grid_spec=grid_spec,
        out_shape=[jax.ShapeDtypeStruct((m_first, d), F32), jax.ShapeDtypeStruct((m - m_first, d), F32)],
        compiler_params=_cparams(("arbitrary",), VMEM_LIMIT),
        name="moe_combine",
    )(dest_km, x1, gates_mk, ys, g2, b2)


def _pack_w_in(w_in, splits):
    q_lora, kv_lora, rope, conv_ch, gv, nh, _, d, _ = splits
    offs = [0]
    for s in splits:
        offs.append(offs[-1] + s)
    part = [w_in[:, offs[i]:offs[i + 1]] for i in range(len(splits))]
    q_lat, kv_lat, k_r, qkv, z, a, b, g_a, g_b = part
    dm = w_in.shape[0]
    half = rope // 2
    zpad = lambda n: jnp.zeros((dm, n), w_in.dtype)
    k_sw = jnp.concatenate([k_r[:, half:], k_r[:, :half]], axis=1)
    small = jnp.concatenate([q_lat, kv_lat, k_r, zpad(LANE - rope), k_sw, zpad(LANE - rope),
                             a, b, zpad(LANE - 2 * nh)], axis=1)
    return jnp.concatenate([qkv, small, z, g_a, g_b], axis=1).astype(BF16)


def _pack_w_uq(w_uq, nope, rope):
    w = jnp.transpose(w_uq, (1, 0, 2))
    half = rope // 2
    r = w[..., nope:]
    zp = jnp.zeros(r.shape[:-1] + (LANE - rope,), w.dtype)
    r_sw = jnp.concatenate([r[..., half:], r[..., :half]], axis=-1)
    return jnp.concatenate([w[..., :nope], r, zp, r_sw, zp], axis=-1).astype(BF16)


def _rope_tables(pos, rope):
    half = rope // 2
    inv = ROPE_THETA ** (-jnp.arange(half, dtype=F32) / half)
    ang = pos.astype(F32)[:, None] * inv[None, :]
    cos, sin = jnp.cos(ang), jnp.sin(ang)
    zp = jnp.zeros((pos.shape[0], LANE - rope), F32)
    return (jnp.concatenate([cos, cos, zp], axis=1), jnp.concatenate([-sin, sin, zp], axis=1))


def _route_meta(idx_t, rank_t, counts, m, n_exp, blk):
    a = m * TOP_K
    counts = counts.astype(jnp.int32)
    padded = (counts + blk - 1) // blk * blk
    pad_end = jnp.cumsum(padded)
    pad_start = pad_end - padded
    experts = jnp.arange(n_exp, dtype=jnp.int32)
    e_km = idx_t[:TOP_K]
    start_km = jnp.sum(jnp.where(e_km[:, :, None] == experts, pad_start, 0), axis=-1)
    dest_km = (start_km + rank_t[:TOP_K]).astype(jnp.int32).reshape(a)
    nb = a // blk + n_exp
    tok_km = jnp.tile(jnp.arange(m, dtype=jnp.int32), TOP_K)
    filler = jnp.arange(nb * blk, dtype=jnp.int32) % m
    rows = filler.at[dest_km].set(tok_km, unique_indices=True, mode='promise_in_bounds')
    first_row = jnp.arange(nb, dtype=jnp.int32) * blk
    block_e = jnp.minimum(jnp.sum((pad_end[None, :] <= first_row[:, None]).astype(jnp.int32), axis=1),
                          n_exp - 1).astype(jnp.int32)
    n_used = (pad_end[-1] // blk).astype(jnp.int32).reshape(1)
    return block_e, rows, n_used, dest_km


def kernel(x_prompt, x_sample, cache_ckv, cache_krope, page_table, state_conv, state_ssm, w_in, q_norm_w, kv_norm_w, w_uq, w_uk, w_uv, conv_w, a_log, dt_bias, gdn_norm_w, w_o, ln1_g, ln1_b, w_router, b_router, w_gu, b_gu, w_down, b_down, ln2_g, ln2_b):
    bp, tp, d = x_prompt.shape
    bs, ts, _ = x_sample.shape
    depth = w_in.shape[0]
    q_lora, n_heads, qk = w_uq.shape[1:]
    kv_lora, _, nope = w_uk.shape[1:]
    rope = qk - nope
    vh = w_uv.shape[3]
    cw_taps, conv_ch = conv_w.shape[1:]
    g_heads = a_log.shape[1]
    dk, dv = state_ssm.shape[3:]
    n_exp = w_router.shape[2]
    page = cache_ckv.shape[2]
    past = page_table.shape[1] * page
    splits = (q_lora, kv_lora, rope, conv_ch, g_heads * dv, g_heads, g_heads, d, d)
    assert sum(splits) == w_in.shape[2]
    alpha = (2 * depth) ** 0.25
    scale = (nope + rope) ** -0.5 * LOG2E
    mp, ms = bp * tp, bs * ts
    m = mp + ms
    small_w = q_lora + kv_lora + 3 * LANE
    assert small_w == d and conv_ch % d == 0
    qkv_blk, small_blk = 0, conv_ch // d
    z_blk, ga_blk, gb_blk = small_blk + 1, small_blk + 2, small_blk + 3
    ab_blk = (conv_ch + q_lora + kv_lora + 2 * LANE) // LANE

    cos_p, sin_p = _rope_tables(jnp.arange(tp, dtype=jnp.int32), rope)
    cos_s, sin_s = _rope_tables(past + jnp.arange(ts, dtype=jnp.int32), rope)
    cos_t = jnp.concatenate([jnp.tile(cos_p, (bp, 1)), jnp.tile(cos_s, (bs, 1))], axis=0)
    sin_t = jnp.concatenate([jnp.tile(sin_p, (bp, 1)), jnp.tile(sin_s, (bs, 1))], axis=0)

    x_p, x_s = x_prompt.reshape(mp, d), x_sample.reshape(ms, d)
    outs = {k: [] for k in ("ckv_p", "kr_p", "conv_p", "ssm_p", "ckv_s", "kr_s", "conv_s", "ssm_s")}
    pad_lanes = lambda v: jnp.pad(v, (0, LANE - v.shape[0])).reshape(1, LANE)
    for l in range(depth):
        x = jnp.concatenate([x_p, x_s], axis=0)
        w_pack = _pack_w_in(w_in[l], splits)
        wq = _pack_w_uq(w_uq[l], nope, rope)
        wuk = jnp.transpose(w_uk[l], (1, 2, 0)).astype(BF16)
        wuv = jnp.transpose(w_uv[l], (1, 0, 2)).astype(BF16)
        h = _in_proj(x, w_pack, tm=_row_tile(m, 2304), tn=1024)
        qf, c_all, kr_all, kt, cb = _mla_proj(
            h, small_blk, cos_t, sin_t, q_norm_w[l].reshape(1, q_lora), kv_norm_w[l].reshape(1, kv_lora),
            wq, wuk, tm=512, q_lora=q_lora, kv_lora=kv_lora, nope=nope, rope=rope, scale=scale)
        om_p = _attn_prompt(qf, kt, cb, wuv, batch=bp, seq=tp, tq=128, tk=min(1024, tp))
        qs = qf[:, mp:, :].reshape(n_heads, bs, ts, qf.shape[-1])
        qs = jnp.transpose(qs, (1, 0, 2, 3)).reshape(bs, n_heads * ts, qf.shape[-1])
        om_s = _attn_sample(page_table, qs, c_all, kr_all, mp // ts, cache_ckv[l],
                            jnp.swapaxes(cache_krope[l], 1, 2), wuv)

        alog_p = pad_lanes(a_log[l])
        dtb_p = pad_lanes(dt_bias[l])
        nw = gdn_norm_w[l].reshape(1, dv)
        gdn_kw = dict(qkv_blk=qkv_blk, ab_blk=ab_blk, z_blk=z_blk, conv_w=cw_taps)
        zeros_conv = jnp.zeros((bp, SUBLANE, conv_ch), F32)
        zeros_ssm = jnp.zeros((bp, g_heads, dk, dv), F32)
        og_p, ssm_p, ncv_p = _gdn(h, 0, zeros_conv, zeros_ssm, conv_w[l], alog_p, dtb_p, nw,
                                  batch=bp, seq=tp, chunk=min(GDN_CHUNK, tp), group=bp, **gdn_kw)
        conv8_s = jnp.pad(state_conv[l], ((0, 0), (SUBLANE - (cw_taps - 1), 0), (0, 0)))
        og_s, ssm_s, ncv_s = _gdn(h, mp, conv8_s, state_ssm[l], conv_w[l], alog_p, dtb_p, nw,
                                  batch=bs, seq=ts, chunk=ts, group=math.gcd(bs, GDN_SAMPLE_GROUP), **gdn_kw)

        x1, x1t, idx_t, gate_t, rank_t, counts = _post(
            x, om_p, om_s, og_p, og_s, h, ga_blk, gb_blk, w_o[l].astype(BF16), ln1_g[l].reshape(1, d), ln1_b[l].reshape(1, d),
            w_router[l].T, b_router[l].reshape(n_exp, 1), tm=512, alpha=alpha)
        block_e, rows, n_used, dest_km = _route_meta(idx_t, rank_t, counts[:, 0], m, n_exp, MOE_ROWS)
        ys = _moe_experts(block_e, rows, n_used, x1t, w_gu[l], b_gu[l], w_down[l], b_down[l], blk=MOE_ROWS)
        x_p, x_s = _combine(dest_km, x1, gate_t.T, ys, ln2_g[l].reshape(1, d), ln2_b[l].reshape(1, d),
                            tm=128, alpha=alpha, m_first=mp)

        outs["ckv_p"].append(c_all[:mp].reshape(bp, tp, kv_lora))
        outs["kr_p"].append(kr_all[:mp].reshape(bp, tp, rope))
        outs["conv_p"].append(ncv_p[:, SUBLANE - (cw_taps - 1):, :])
        outs["ssm_p"].append(ssm_p)
        outs["ckv_s"].append(c_all[mp:].reshape(bs, ts, kv_lora))
        outs["kr_s"].append(kr_all[mp:].reshape(bs, ts, rope))
        outs["conv_s"].append(ncv_s[:, SUBLANE - (cw_taps - 1):, :])
        outs["ssm_s"].append(ssm_s)

    return (x_p.reshape(bp, tp, d), x_s.reshape(bs, ts, d),
            jnp.stack(outs["ckv_p"]), jnp.stack(outs["kr_p"]), jnp.stack(outs["conv_p"]), jnp.stack(outs["ssm_p"]),
            jnp.stack(outs["ckv_s"]), jnp.stack(outs["kr_s"]), jnp.stack(outs["conv_s"]), jnp.stack(outs["ssm_s"]))
```

```python
import functools
import math

import jax
import jax.numpy as jnp
from jax import lax
from jax.experimental import pallas as pl
from jax.experimental.pallas import tpu as pltpu

F32 = jnp.float32
BF16 = jnp.bfloat16

ROPE_THETA = 10000.0
NORM_EPS = 1e-6
TOP_K = 4
SWIGLU_LIMIT = 7.0
SWIGLU_ALPHA = 1.702
GDN_CHUNK = 128
GDN_DIAG = 16
GDN_SAMPLE_GROUP = 8
MOE_ROWS = 512
PAGE_CHUNK = 8
DIAG_VARIANTS = 8
LANE = 128
SUBLANE = 8
VMEM_LIMIT = 56 * 1024 * 1024
NEG = -1e30
LOG2E = 1.4426950408889634


def _cparams(sem, vmem=None):
    return pltpu.CompilerParams(dimension_semantics=sem, vmem_limit_bytes=vmem)


def _dot(a, b, **kw):
    return jnp.dot(a, b, preferred_element_type=F32, **kw)


def _dot_nt(a, b, **kw):
    return lax.dot_general(a, b, (((1,), (1,)), ((), ())), preferred_element_type=F32, **kw)


def _bdot(a, b):
    return lax.dot_general(a, b, (((2,), (1,)), ((0,), (0,))), preferred_element_type=F32)


def _bdot_nt(a, b):
    return lax.dot_general(a, b, (((2,), (2,)), ((0,), (0,))), preferred_element_type=F32)


def _bdot_tn(a, b):
    return lax.dot_general(a, b, (((1,), (1,)), ((0,), (0,))), preferred_element_type=F32)


def _sigmoid(x):
    return 1.0 / (1.0 + jnp.exp(-x))


def _layer_norm(v, g, b):
    mu = jnp.mean(v, -1, keepdims=True)
    vc = v - mu
    var = jnp.mean(vc * vc, -1, keepdims=True)
    return vc * lax.rsqrt(var + NORM_EPS) * g + b


def _inproj_kernel(x_ref, w_ref, o_ref, xb_ref):
    @pl.when(pl.program_id(1) == 0)
    def _():
        xb_ref[...] = x_ref[...].astype(BF16)

    o_ref[...] = _dot(xb_ref[...], w_ref[...])


def _row_tile(m, target):
    return max(t for t in range(SUBLANE, target + 1, SUBLANE) if m % t == 0)


def _in_proj(x, w, tm, tn):
    m, k = x.shape
    n = w.shape[1]
    return pl.pallas_call(
        _inproj_kernel,
        grid=(m // tm, n // tn),
        in_specs=[pl.BlockSpec((tm, k), lambda i, j: (i, 0)),
                  pl.BlockSpec((k, tn), lambda i, j: (0, j))],
        out_specs=pl.BlockSpec((tm, tn), lambda i, j: (i, j)),
        out_shape=jax.ShapeDtypeStruct((m, n), F32),
        scratch_shapes=[pltpu.VMEM((tm, k), BF16)],
        compiler_params=_cparams(("parallel", "arbitrary"), VMEM_LIMIT),
        name="in_proj",
    )(x, w)


def _mla_proj_kernel(h_ref, cos_ref, sin_ref, qnw_ref, kvnw_ref, wq_ref, wuk_ref,
                     qf_ref, c_ref, kr_ref, kt_ref, cb_ref, *, n_heads, q_lora, kv_lora, nope, rope, scale):
    hs = h_ref[...]
    cos = cos_ref[...]
    sin = sin_ref[...]
    q_lat = hs[:, :q_lora]
    qn = q_lat * lax.rsqrt(jnp.mean(q_lat * q_lat, -1, keepdims=True) + NORM_EPS) * qnw_ref[...]
    qn = qn.astype(BF16)
    kv = hs[:, q_lora:q_lora + kv_lora]
    c = kv * lax.rsqrt(jnp.mean(kv * kv, -1, keepdims=True) + NORM_EPS) * kvnw_ref[...]
    o = q_lora + kv_lora
    kr = hs[:, o:o + LANE] * cos + hs[:, o + LANE:o + 2 * LANE] * sin
    c_ref[...] = c
    kr_ref[...] = kr[:, :rope]
    cb = c.astype(BF16)
    cb_ref[...] = cb
    kfull = jnp.concatenate([cb, kr.astype(BF16)], axis=1)
    kw = kfull.shape[1]
    eye = (lax.broadcasted_iota(jnp.int32, (kw, kw), 0) == lax.broadcasted_iota(jnp.int32, (kw, kw), 1))
    kt_ref[...] = _dot_nt(eye.astype(BF16), kfull).astype(BF16)
    for h in range(n_heads):
        qh = _dot(qn, wq_ref[h])
        qa = _dot(qh[:, :nope].astype(BF16), wuk_ref[h])
        qr = qh[:, nope:nope + LANE] * cos + qh[:, nope + LANE:nope + 2 * LANE] * sin
        qf_ref[h, :, :kv_lora] = (qa * scale).astype(BF16)
        qf_ref[h, :, kv_lora:] = (qr * scale).astype(BF16)


def _mla_proj(h, col_blk, cos_t, sin_t, qnw, kvnw, wq, wuk, *, tm, q_lora, kv_lora, nope, rope, scale):
    m = h.shape[0]
    n_heads = wq.shape[0]
    wcol = q_lora + kv_lora + 3 * LANE
    kw = kv_lora + LANE
    kern = functools.partial(_mla_proj_kernel, n_heads=n_heads, q_lora=q_lora, kv_lora=kv_lora,
                             nope=nope, rope=rope, scale=scale)
    return pl.pallas_call(
        kern,
        grid=(m // tm,),
        in_specs=[pl.BlockSpec((tm, wcol), lambda i: (i, col_blk)),
                  pl.BlockSpec((tm, LANE), lambda i: (i, 0)),
                  pl.BlockSpec((tm, LANE), lambda i: (i, 0)),
                  pl.BlockSpec((1, q_lora), lambda i: (0, 0)),
                  pl.BlockSpec((1, kv_lora), lambda i: (0, 0)),
                  pl.BlockSpec(wq.shape, lambda i: (0, 0, 0)),
                  pl.BlockSpec(wuk.shape, lambda i: (0, 0, 0))],
        out_specs=[pl.BlockSpec((n_heads, tm, kw), lambda i: (0, i, 0)),
                   pl.BlockSpec((tm, kv_lora), lambda i: (i, 0)),
                   pl.BlockSpec((tm, rope), lambda i: (i, 0)),
                   pl.BlockSpec((kw, tm), lambda i: (0, i)),
                   pl.BlockSpec((tm, kv_lora), lambda i: (i, 0))],
        out_shape=[jax.ShapeDtypeStruct((n_heads, m, kw), BF16),
                   jax.ShapeDtypeStruct((m, kv_lora), F32),
                   jax.ShapeDtypeStruct((m, rope), F32),
                   jax.ShapeDtypeStruct((kw, m), BF16),
                   jax.ShapeDtypeStruct((m, kv_lora), BF16)],
        compiler_params=_cparams(("parallel",), VMEM_LIMIT),
        name="mla_proj",
    )(h, cos_t, sin_t, qnw, kvnw, wq, wuk)


def _softmax_step(s, v, m_ref, l_ref, acc_ref):
    m_prev = m_ref[...]
    m_new = jnp.maximum(m_prev, jnp.max(s, -1, keepdims=True))
    alpha = jnp.exp2(m_prev - m_new)
    p = jnp.exp2(s - m_new)
    l_ref[...] = alpha * l_ref[...] + jnp.sum(p, -1, keepdims=True)
    acc_ref[...] = alpha * acc_ref[...] + _dot(p.astype(BF16), v)
    m_ref[...] = m_new


def _attn_prompt_kernel(qf_ref, kt_ref, v_ref, wuv_ref, o_ref, m_ref, l_ref, acc_ref, *, n_heads, tq, tk, dv, vh, n_split):
    i = pl.program_id(1)
    m_ref[...] = jnp.full(m_ref.shape, NEG, F32)
    l_ref[...] = jnp.zeros(l_ref.shape, F32)
    acc_ref[...] = jnp.zeros(acc_ref.shape, F32)
    n_full = (i * tq) // tk
    off = i * tq - n_full * tk
    hs = n_heads // n_split
    rs = hs * tq

    def step(j, width, masked):
        start = pl.multiple_of(j * tk, tk)
        kt = kt_ref[:, pl.ds(start, width)]
        v = v_ref[pl.ds(start, width), :]
        for g in range(n_split):
            q = qf_ref[g * hs:(g + 1) * hs].reshape(rs, qf_ref.shape[-1])
            s = _dot(q, kt)
            if masked:
                row = lax.broadcasted_iota(jnp.int32, (tq, width), 0)
                col = lax.broadcasted_iota(jnp.int32, (tq, width), 1)
                s = jnp.where((col <= row + off)[None], s.reshape(hs, tq, width), NEG).reshape(rs, width)
            sl = slice(g * rs, (g + 1) * rs)
            _softmax_step(s, v, m_ref.at[sl], l_ref.at[sl], acc_ref.at[sl])

    def body(j, carry):
        step(j, tk, False)
        return carry

    lax.fori_loop(0, n_full, body, 0)
    need = off + tq
    widths = [tk * (v + 1) // DIAG_VARIANTS for v in range(DIAG_VARIANTS)]
    for v, width in enumerate(widths):
        lo = widths[v - 1] if v else 0

        @pl.when(jnp.logical_and(need > lo, need <= width))
        def _(width=width):
            step(n_full, width, True)

    o = acc_ref[...] / l_ref[...]
    for h in range(n_heads):
        oh = o[h * tq:(h + 1) * tq].astype(BF16)
        o_ref[:, h * vh:(h + 1) * vh] = _dot(oh, wuv_ref[h])


def _attn_prompt(qf, kt, cb, wuv, *, batch, seq, tq, tk):
    n_heads, _, kw = qf.shape
    dv, vh = wuv.shape[1], wuv.shape[2]
    nq = seq // tq
    rows = n_heads * tq
    assert seq % tk == 0 and tk % tq == 0 and tk % (DIAG_VARIANTS * LANE) == 0
    kern = functools.partial(_attn_prompt_kernel, n_heads=n_heads, tq=tq, tk=tk, dv=dv, vh=vh, n_split=2)
    return pl.pallas_call(
        kern,
        grid=(batch, nq),
        in_specs=[pl.BlockSpec((n_heads, tq, kw), lambda b, i: (0, b * nq + i, 0)),
                  pl.BlockSpec((kw, seq), lambda b, i: (0, b)),
                  pl.BlockSpec((seq, dv), lambda b, i: (b, 0)),
                  pl.BlockSpec(wuv.shape, lambda b, i: (0, 0, 0))],
        out_specs=pl.BlockSpec((tq, n_heads * vh), lambda b, i: (b * nq + i, 0)),
        out_shape=jax.ShapeDtypeStruct((batch * seq, n_heads * vh), F32),
        scratch_shapes=[pltpu.VMEM((rows, 1), F32), pltpu.VMEM((rows, 1), F32),
                        pltpu.VMEM((rows, dv), F32)],
        compiler_params=_cparams(("parallel", "arbitrary"), VMEM_LIMIT),
        name="attn_prompt",
    )(qf, kt, cb, wuv)


def _attn_sample_kernel(pt_ref, q_ref, cn_ref, krn_ref, cc_hbm, cr_hbm, wuv_ref, o_ref,
                        cbuf, rbuf, sem, *, n_pages, page, n_heads, ts, dv, dr, vh):
    b = pl.program_id(0)
    nb = pl.num_programs(0)
    slot = b % 2
    rows = n_heads * ts

    def fetch(bb, s):
        for p in range(n_pages):
            pg = pt_ref[bb, p]
            pltpu.make_async_copy(cc_hbm.at[pg], cbuf.at[s, p], sem.at[0, s]).start()
            pltpu.make_async_copy(cr_hbm.at[pg], rbuf.at[s, p], sem.at[1, s]).start(priority=1)

    def wait(s):
        pltpu.make_async_copy(cc_hbm.at[pl.ds(0, n_pages)], cbuf.at[s], sem.at[0, s]).wait()
        pltpu.make_async_copy(cr_hbm.at[pl.ds(0, n_pages)], rbuf.at[s], sem.at[1, s]).wait()

    @pl.when(b == 0)
    def _():
        fetch(0, 0)

    wait(slot)
    fetch(jnp.minimum(b + 1, nb - 1), 1 - slot)

    q = q_ref[0].astype(F32)
    qc = q[:, :dv]
    qr = q[:, dv:dv + dr]
    n_chunks = n_pages // PAGE_CHUNK
    ck = PAGE_CHUNK * page
    parts = []
    for ch in range(n_chunks):
        c_ch = cbuf[slot, ch * PAGE_CHUNK:(ch + 1) * PAGE_CHUNK].reshape(ck, dv)
        r_ch = jnp.concatenate([rbuf[slot, ch * PAGE_CHUNK + u] for u in range(PAGE_CHUNK)], axis=1)
        parts.append(_dot_nt(qc, c_ch) + _dot(qr, r_ch))
    cn = jnp.concatenate([cn_ref[...], jnp.zeros((LANE - ts, dv), F32)], axis=0)
    krn = jnp.concatenate([krn_ref[...], jnp.zeros((LANE - ts, dr), F32)], axis=0)
    row = lax.broadcasted_iota(jnp.int32, (ts, LANE), 0)
    col = lax.broadcasted_iota(jnp.int32, (ts, LANE), 1)
    s_new = (_dot_nt(qc, cn) + _dot_nt(qr, krn)).reshape(n_heads, ts, LANE)
    s_new = jnp.where((col <= row)[None], s_new, NEG).reshape(rows, LANE)

    m = jnp.max(s_new, -1, keepdims=True)
    for s in parts:
        m = jnp.maximum(m, jnp.max(s, -1, keepdims=True))
    p_new = jnp.exp2(s_new - m)
    l = jnp.sum(p_new, -1, keepdims=True)
    acc = _dot(p_new, cn)
    for ch in range(n_chunks):
        p = jnp.exp2(parts[ch] - m)
        l = l + jnp.sum(p, -1, keepdims=True)
        acc = acc + _dot(p, cbuf[slot, ch * PAGE_CHUNK:(ch + 1) * PAGE_CHUNK].reshape(ck, dv))
    o = acc / l
    for h in range(n_heads):
        oh = o[h * ts:(h + 1) * ts].astype(BF16)
        o_ref[:, h * vh:(h + 1) * vh] = _dot(oh, wuv_ref[h])

    @pl.when(b == nb - 1)
    def _():
        wait(1 - slot)


def _attn_sample(page_table, qs, c_all, kr_all, row_blk_off, cache_c, cache_rt, wuv):
    bs, rows, kw = qs.shape
    n_heads, dv, vh = wuv.shape
    ts = rows // n_heads
    n_pages = page_table.shape[1]
    page = cache_c.shape[1]
    dr = cache_rt.shape[1]
    assert n_pages % PAGE_CHUNK == 0
    kern = functools.partial(_attn_sample_kernel, n_pages=n_pages, page=page, n_heads=n_heads, ts=ts,
                             dv=dv, dr=dr, vh=vh)
    grid_spec = pltpu.PrefetchScalarGridSpec(
        num_scalar_prefetch=1,
        grid=(bs,),
        in_specs=[pl.BlockSpec((1, rows, kw), lambda b, pt: (b, 0, 0)),
                  pl.BlockSpec((ts, dv), lambda b, pt: (row_blk_off + b, 0)),
                  pl.BlockSpec((ts, dr), lambda b, pt: (row_blk_off + b, 0)),
                  pl.BlockSpec(memory_space=pl.ANY),
                  pl.BlockSpec(memory_space=pl.ANY),
                  pl.BlockSpec(wuv.shape, lambda b, pt: (0, 0, 0))],
        out_specs=pl.BlockSpec((ts, n_heads * vh), lambda b, pt: (b, 0)),
        scratch_shapes=[pltpu.VMEM((2, n_pages, page, dv), F32), pltpu.VMEM((2, n_pages, dr, page), F32),
                        pltpu.SemaphoreType.DMA((2, 2))])
    return pl.pallas_call(
        kern,
        grid_spec=grid_spec,
        out_shape=jax.ShapeDtypeStruct((bs * ts, n_heads * vh), F32),
        compiler_params=_cparams(("arbitrary",), VMEM_LIMIT),
        name="attn_sample",
    )(page_table, qs, c_all, kr_all, cache_c, cache_rt, wuv)


def _neumann_inv(low, eye, steps):
    p = eye - low
    x = low
    for _ in range(steps):
        x = _bdot(x, x)
        p = p + _bdot(p, x)
    return p


def _gdn_kernel(*refs, n_in, n_seq, chunk, n_heads, dk, dv, conv_w):
    qkv_refs, ab_refs, z_refs = refs[:n_in], refs[n_in:2 * n_in], refs[2 * n_in:3 * n_in]
    (cs_ref, s0_ref, cw_ref, alog_ref, dtb_ref, nw_ref,
     o_ref, sfin_ref, ncv_ref, xp_ref, s_ref) = refs[3 * n_in:]
    n = pl.program_id(1)
    c = chunk
    per = n_seq // n_in
    n_prob = n_seq * n_heads
    hi = lax.Precision.HIGHEST

    def seq_rows(group, g):
        k = g % per
        return group[g // per][k * c:(k + 1) * c, :]

    @pl.when(n == 0)
    def _():
        s_ref[...] = s0_ref[...].reshape(n_prob, dk, dv)
        xp_ref[:, 0:SUBLANE, :] = cs_ref[...]

    r_i = lax.broadcasted_iota(jnp.int32, (c, c), 0)
    c_i = lax.broadcasted_iota(jnp.int32, (c, c), 1)
    causal = (c_i <= r_i)[None]
    strict = (c_i < r_i)[None]
    eye_c = (r_i == c_i).astype(F32)[None]
    db = min(GDN_DIAG, c)
    n_blk = c // db
    sh = int(math.log2(db))
    same_blk = (jnp.right_shift(r_i, sh) == jnp.right_shift(c_i, sh))[None]
    r_l = lax.broadcasted_iota(jnp.int32, (LANE, LANE), 0)
    c_l = lax.broadcasted_iota(jnp.int32, (LANE, LANE), 1)
    eye_l = (r_l == c_l).astype(F32)
    tril = (c_i <= r_i).astype(F32)
    cw = cw_ref[...]
    hk = n_heads * dk

    q_l, k_l, v_l, z_l, beta_l, gcol_l, grow_l, tails = [], [], [], [], [], [], [], []
    for g in range(n_seq):
        xp_ref[g, SUBLANE:SUBLANE + c, :] = seq_rows(qkv_refs, g)
        conv = xp_ref[g, pl.ds(SUBLANE - (conv_w - 1), c), :] * cw[0:1, :]
        for j in range(1, conv_w):
            conv = conv + xp_ref[g, pl.ds(SUBLANE - (conv_w - 1) + j, c), :] * cw[j:j + 1, :]
        tail = xp_ref[g, c:c + SUBLANE, :]
        xp_ref[g, 0:SUBLANE, :] = tail
        tails.append(tail)
        act = conv * _sigmoid(conv)
        ab = seq_rows(ab_refs, g)
        apb = ab + dtb_ref[...]
        softplus = jnp.maximum(apb, 0.0) + jnp.log(1.0 + jnp.exp(-jnp.abs(apb)))
        gfull = -jnp.exp(alog_ref[...]) * softplus
        betaf = _sigmoid(ab)
        gc = _dot(tril, gfull, precision=hi)
        gc_t = _dot_nt(eye_l, gc, precision=hi)
        zg = seq_rows(z_refs, g)
        for h in range(n_heads):
            q_l.append(act[:, h * dk:(h + 1) * dk])
            k_l.append(act[:, hk + h * dk:hk + (h + 1) * dk])
            v_l.append(act[:, 2 * hk + h * dv:2 * hk + (h + 1) * dv])
            z_l.append(zg[:, h * dv:(h + 1) * dv])
            beta_l.append(betaf[:, n_heads + h:n_heads + h + 1])
            gcol_l.append(gc[:, h:h + 1])
            grow_l.append(gc_t[h:h + 1, :])

    q = jnp.stack(q_l)
    k = jnp.stack(k_l)
    v = jnp.stack(v_l)
    z = jnp.stack(z_l)
    beta = jnp.stack(beta_l)
    gcol = jnp.stack(gcol_l)
    grow = jnp.stack(grow_l)
    q = q * lax.rsqrt(jnp.sum(q * q, -1, keepdims=True) + NORM_EPS) * (dk ** -0.5)
    k = k * lax.rsqrt(jnp.sum(k * k, -1, keepdims=True) + NORM_EPS)
    decay = jnp.where(causal, jnp.exp(jnp.where(causal, gcol - grow, 0.0)), 0.0)
    kb = k * beta
    lower = jnp.where(strict, _bdot_nt(kb, k) * decay, 0.0)
    attn = _bdot_nt(q, k) * decay
    l_d = jnp.where(same_blk, lower, 0.0)
    t_inv = _neumann_inv(l_d, eye_c, int(math.log2(db)) - 1)
    if n_blk > 1:
        m_inv = _neumann_inv(_bdot(t_inv, lower - l_d), eye_c, int(math.log2(n_blk)) - 1)
        t_inv = _bdot(m_inv, t_inv)
    egc = jnp.exp(gcol)
    uw = _bdot(t_inv, jnp.concatenate([v * beta, kb * egc], axis=2))
    s = s_ref[...]
    v_new = uw[:, :, :dv] - _bdot(uw[:, :, dv:], s)
    o = _bdot(q * egc, s) + _bdot(attn, v_new)
    glast = gcol[:, c - 1:c, :]
    kdec = k * jnp.exp(glast - gcol)
    s_new = s * jnp.exp(glast) + _bdot_tn(kdec, v_new)
    s_ref[...] = s_new
    on = o * lax.rsqrt(jnp.mean(o * o, -1, keepdims=True) + NORM_EPS) * nw_ref[...] * (z * _sigmoid(z))
    for g in range(n_seq):
        for h in range(n_heads):
            o_ref[g, :, h * dv:(h + 1) * dv] = on[g * n_heads + h]

    @pl.when(n == pl.num_programs(1) - 1)
    def _():
        sfin_ref[...] = s_new.reshape(n_seq, n_heads, dk, dv)
        for g in range(n_seq):
            ncv_ref[g] = tails[g]


def _gdn(h, row_off, conv_state8, ssm_state, cw, alog_p, dtb_p, nw, *, batch, seq, chunk, group,
         qkv_blk, ab_blk, z_blk, conv_w):
    _, n_heads, dk, dv = ssm_state.shape
    nc = seq // chunk
    ch = cw.shape[1]
    hd = n_heads * dv
    contiguous = nc == 1
    n_in = 1 if contiguous else group
    assert batch % group == 0 and row_off % (group * chunk) == 0
    kern = functools.partial(_gdn_kernel, n_in=n_in, n_seq=group, chunk=chunk, n_heads=n_heads,
                             dk=dk, dv=dv, conv_w=conv_w)
    if contiguous:
        rb = group * chunk
        row_maps = [lambda i, n: row_off // rb + i]
    else:
        rb = chunk
        row_maps = [(lambda i, n, g=g: row_off // rb + (i * group + g) * nc + n) for g in range(group)]

    def specs(width, col_blk):
        return [pl.BlockSpec((rb, width), lambda i, n, r=r: (r(i, n), col_blk)) for r in row_maps]

    in_specs = specs(ch, qkv_blk) + specs(LANE, ab_blk) + specs(hd, z_blk)
    in_specs += [pl.BlockSpec((group, SUBLANE, ch), lambda i, n: (i, 0, 0)),
                 pl.BlockSpec((group, n_heads, dk, dv), lambda i, n: (i, 0, 0, 0)),
                 pl.BlockSpec(cw.shape, lambda i, n: (0, 0)),
                 pl.BlockSpec((1, LANE), lambda i, n: (0, 0)),
                 pl.BlockSpec((1, LANE), lambda i, n: (0, 0)),
                 pl.BlockSpec((1, dv), lambda i, n: (0, 0))]
    o3, sfin, ncv = pl.pallas_call(
        kern,
        grid=(batch // group, nc),
        in_specs=in_specs,
        out_specs=[pl.BlockSpec((group, chunk, hd), lambda i, n: (i, n, 0)),
                   pl.BlockSpec((group, n_heads, dk, dv), lambda i, n: (i, 0, 0, 0)),
                   pl.BlockSpec((group, SUBLANE, ch), lambda i, n: (i, 0, 0))],
        out_shape=[jax.ShapeDtypeStruct((batch, seq, hd), F32),
                   jax.ShapeDtypeStruct(ssm_state.shape, F32),
                   jax.ShapeDtypeStruct((batch, SUBLANE, ch), F32)],
        scratch_shapes=[pltpu.VMEM((group, chunk + SUBLANE, ch), F32),
                        pltpu.VMEM((group * n_heads, dk, dv), F32)],
        compiler_params=_cparams(("parallel", "arbitrary"), VMEM_LIMIT),
        name="gdn",
    )(*([h] * (3 * n_in)), conv_state8, ssm_state, cw, alog_p, dtb_p, nw)
    return o3.reshape(batch * seq, hd), sfin, ncv


def _store_row_tiles(ref, val):
    rows = val.shape[0]
    for j in range(SUBLANE):
        ref[pl.ds(j, rows, stride=SUBLANE), :] = val[:, j * LANE:(j + 1) * LANE]


def _load_row_tiles(ref, start, rows, j):
    return ref[pl.ds(start * SUBLANE + j, rows, stride=SUBLANE), :]


def _post_kernel(x_ref, omp_ref, oms_ref, ogp_ref, ogs_ref, ga_ref, gb_ref, wo_ref, g1_ref, b1_ref, wr_ref, br_ref,
                 x1_ref, x1t_ref, idx_ref, gate_ref, rank_ref, cnt_ref, carry_ref, *, alpha, top_k, n_first):
    @pl.when(pl.program_id(0) == 0)
    def _():
        carry_ref[...] = jnp.zeros(carry_ref.shape, F32)

    first = pl.program_id(0) < n_first
    om = jnp.where(first, omp_ref[...], oms_ref[...])
    og = jnp.where(first, ogp_ref[...], ogs_ref[...])
    mix = _sigmoid(ga_ref[...]) * om + _sigmoid(gb_ref[...]) * og
    y = _dot(mix.astype(BF16), wo_ref[...])
    x1 = _layer_norm(alpha * x_ref[...] + y, g1_ref[...], b1_ref[...])
    x1_ref[...] = x1
    _store_row_tiles(x1t_ref, x1)
    logits = _dot_nt(wr_ref[...], x1, precision=lax.Precision.HIGHEST) + br_ref[...]
    n_exp, tm = logits.shape
    e_i = lax.broadcasted_iota(jnp.int32, (n_exp, tm), 0)
    vals, idxs = [], []
    for _ in range(top_k):
        mx = jnp.max(logits, axis=0, keepdims=True)
        ix = jnp.min(jnp.where(logits == mx, e_i, n_exp), axis=0, keepdims=True)
        vals.append(mx)
        idxs.append(ix)
        logits = jnp.where(e_i == ix, -jnp.inf, logits)
    es = [jnp.exp(v - vals[0]) for v in vals]
    tot = es[0]
    for e in es[1:]:
        tot = tot + e
    onehots = [e_i == ix for ix in idxs]
    sel = onehots[0].astype(F32)
    for oh in onehots[1:]:
        sel = sel + oh.astype(F32)
    t_r = lax.broadcasted_iota(jnp.int32, (tm, tm), 0)
    t_c = lax.broadcasted_iota(jnp.int32, (tm, tm), 1)
    before = _dot(sel.astype(BF16), (t_r < t_c).astype(BF16)) + carry_ref[...]
    carry = carry_ref[...] + jnp.sum(sel, axis=1, keepdims=True)
    carry_ref[...] = carry
    cnt_ref[...] = carry
    r_i = lax.broadcasted_iota(jnp.int32, (SUBLANE, tm), 0)
    idx_o = jnp.zeros((SUBLANE, tm), jnp.int32)
    gate_o = jnp.zeros((SUBLANE, tm), F32)
    rank_o = jnp.zeros((SUBLANE, tm), F32)
    for k in range(top_k):
        idx_o = jnp.where(r_i == k, idxs[k], idx_o)
        gate_o = jnp.where(r_i == k, es[k] / tot, gate_o)
        rank_k = jnp.sum(jnp.where(onehots[k], before, 0.0), axis=0, keepdims=True)
        rank_o = jnp.where(r_i == k, rank_k, rank_o)
    idx_ref[...] = idx_o
    gate_ref[...] = gate_o
    rank_ref[...] = rank_o.astype(jnp.int32)


def _post(x, om_p, om_s, og_p, og_s, h, ga_blk, gb_blk, wo, g1, b1, wr_t, br, *, tm, alpha):
    m, d = x.shape
    n_exp = wr_t.shape[0]
    n_first = om_p.shape[0] // tm
    assert om_p.shape[0] % tm == 0 and om_s.shape[0] % tm == 0 and d == SUBLANE * LANE
    kern = functools.partial(_post_kernel, alpha=alpha, top_k=TOP_K, n_first=n_first)
    row = lambda i: (i, 0)
    const = lambda i: (0, 0)
    first = lambda i: (jnp.minimum(i, n_first - 1), 0)
    second = lambda i: (jnp.maximum(i - n_first, 0), 0)
    return pl.pallas_call(
        kern,
        grid=(m // tm,),
        in_specs=[pl.BlockSpec((tm, d), row),
                  pl.BlockSpec((tm, d), first), pl.BlockSpec((tm, d), second),
                  pl.BlockSpec((tm, d), first), pl.BlockSpec((tm, d), second),
                  pl.BlockSpec((tm, d), lambda i: (i, ga_blk)),
                  pl.BlockSpec((tm, d), lambda i: (i, gb_blk)),
                  pl.BlockSpec(wo.shape, const), pl.BlockSpec((1, d), const), pl.BlockSpec((1, d), const),
                  pl.BlockSpec(wr_t.shape, const), pl.BlockSpec((n_exp, 1), const)],
        out_specs=[pl.BlockSpec((tm, d), row),
                   pl.BlockSpec((tm * SUBLANE, LANE), row),
                   pl.BlockSpec((SUBLANE, tm), lambda i: (0, i)),
                   pl.BlockSpec((SUBLANE, tm), lambda i: (0, i)),
                   pl.BlockSpec((SUBLANE, tm), lambda i: (0, i)),
                   pl.BlockSpec((n_exp, 1), const)],
        out_shape=[jax.ShapeDtypeStruct((m, d), F32),
                   jax.ShapeDtypeStruct((m * SUBLANE, LANE), F32),
                   jax.ShapeDtypeStruct((SUBLANE, m), jnp.int32),
                   jax.ShapeDtypeStruct((SUBLANE, m), F32),
                   jax.ShapeDtypeStruct((SUBLANE, m), jnp.int32),
                   jax.ShapeDtypeStruct((n_exp, 1), F32)],
        scratch_shapes=[pltpu.VMEM((n_exp, 1), F32)],
        compiler_params=_cparams(("arbitrary",), VMEM_LIMIT),
        name="post_mix",
    )(x, om_p, om_s, og_p, og_s, h, h, wo, g1, b1, wr_t, br)


def _moe_kernel(be_ref, rows_ref, nused_ref, x_hbm, wgu_ref, bgu_ref, wd_ref, bd_ref, o_ref,
                xbuf, sem, wgu_bf, wd_bf, xb_ref, *, blk, d_exp):
    i = pl.program_id(0)
    n_used = nused_ref[0]
    slot = i % 2

    def row_copy(tok, s, r):
        return pltpu.make_async_copy(x_hbm.at[pl.ds(tok * SUBLANE, SUBLANE)],
                                     xbuf.at[s, pl.ds(r * SUBLANE, SUBLANE)], sem.at[s])

    def wait(s):
        pltpu.make_async_copy(x_hbm.at[pl.ds(0, blk * SUBLANE)], xbuf.at[s], sem.at[s]).wait()

    @pl.when(jnp.logical_and(i == 0, n_used > 0))
    def _():
        second = jnp.minimum(1, n_used - 1) * blk

        def body(r, carry):
            row_copy(rows_ref[r], 0, r).start()
            row_copy(rows_ref[second + r], 1, r).start()
            return carry
        lax.fori_loop(0, blk, body, 0)

    @pl.when(i < n_used)
    def _():
        e = be_ref[i]
        e_prev = be_ref[jnp.maximum(i - 1, 0)]

        @pl.when(jnp.logical_or(i == 0, e != e_prev))
        def _():
            wgu_bf[...] = wgu_ref[0].astype(BF16)
            wd_bf[...] = wd_ref[0].astype(BF16)

        wait(slot)
        for j in range(SUBLANE):
            xb_ref[:, j * LANE:(j + 1) * LANE] = _load_row_tiles(xbuf.at[slot], 0, blk, j).astype(BF16)
        nxt = jnp.minimum(i + 2, n_used - 1) * blk
        for r in range(blk):
            row_copy(rows_ref[nxt + r], slot, r).start(priority=r % 2)
        hh = _dot(xb_ref[...], wgu_bf[...]) + bgu_ref[0]
        gate = jnp.minimum(hh[:, :d_exp], SWIGLU_LIMIT)
        up = jnp.clip(hh[:, d_exp:], -SWIGLU_LIMIT, SWIGLU_LIMIT)
        act = (up + 1.0) * gate * _sigmoid(SWIGLU_ALPHA * gate)
        _store_row_tiles(o_ref, _dot(act.astype(BF16), wd_bf[...]) + bd_ref[0])

        @pl.when(i == n_used - 1)
        def _():
            wait(0)
            wait(1)

    @pl.when(i >= n_used)
    def _():
        o_ref[...] = jnp.zeros(o_ref.shape, F32)


def _moe_experts(block_e, rows, n_used, x1, w_gu, b_gu, w_down, b_down, *, blk):
    n_exp, d, d2 = w_gu.shape
    d_exp = d2 // 2
    nb = block_e.shape[0]
    kern = functools.partial(_moe_kernel, blk=blk, d_exp=d_exp)
    grid_spec = pltpu.PrefetchScalarGridSpec(
        num_scalar_prefetch=3,
        grid=(nb,),
        in_specs=[pl.BlockSpec(memory_space=pl.ANY),
                  pl.BlockSpec((1, d, d2), lambda i, be, rw, nu: (be[i], 0, 0)),
                  pl.BlockSpec((1, 1, d2), lambda i, be, rw, nu: (be[i], 0, 0)),
                  pl.BlockSpec((1, d_exp, d), lambda i, be, rw, nu: (be[i], 0, 0)),
                  pl.BlockSpec((1, 1, d), lambda i, be, rw, nu: (be[i], 0, 0))],
        out_specs=pl.BlockSpec((blk * SUBLANE, LANE), lambda i, be, rw, nu: (i, 0)),
        scratch_shapes=[pltpu.VMEM((2, blk * SUBLANE, LANE), F32), pltpu.SemaphoreType.DMA((2,)),
                        pltpu.VMEM((d, d2), BF16), pltpu.VMEM((d_exp, d), BF16),
                        pltpu.VMEM((blk, d), BF16)])
    return pl.pallas_call(
        kern,
        grid_spec=grid_spec,
        out_shape=jax.ShapeDtypeStruct((nb * blk * SUBLANE, LANE), F32),
        compiler_params=_cparams(("arbitrary",), VMEM_LIMIT),
        name="moe_experts",
    )(block_e, rows, n_used, x1, w_gu, b_gu.reshape(n_exp, 1, d2), w_down, b_down.reshape(n_exp, 1, d))


def _combine_kernel(dest_ref, x1_ref, gate_ref, ys_hbm, g2_ref, b2_ref, op_ref, os_ref, ybuf, sem,
                    *, tm, top_k, alpha, m_total, n_first):
    i = pl.program_id(0)
    nsteps = pl.num_programs(0)
    slot = i % 2
    n_rows = top_k * tm

    def row_copy(d, s, r):
        return pltpu.make_async_copy(ys_hbm.at[pl.ds(d * SUBLANE, SUBLANE)],
                                     ybuf.at[s, pl.ds(r * SUBLANE, SUBLANE)], sem.at[s])

    def wait(s):
        pltpu.make_async_copy(ys_hbm.at[pl.ds(0, n_rows * SUBLANE)], ybuf.at[s], sem.at[s]).wait()

    @pl.when(i == 0)
    def _():
        second = jnp.minimum(1, nsteps - 1) * tm

        def body(r, carry):
            k = r // tm
            t = r - k * tm
            row_copy(dest_ref[k * m_total + t], 0, r).start()
            row_copy(dest_ref[k * m_total + second + t], 1, r).start()
            return carry
        lax.fori_loop(0, n_rows, body, 0)

    wait(slot)
    g = gate_ref[...]
    parts = []
    for j in range(SUBLANE):
        acc = g[:, 0:1] * _load_row_tiles(ybuf.at[slot], 0, tm, j)
        for k in range(1, top_k):
            acc = acc + g[:, k:k + 1] * _load_row_tiles(ybuf.at[slot], k * tm, tm, j)
        parts.append(acc)
    y = jnp.concatenate(parts, axis=1)
    nxt = jnp.minimum(i + 2, nsteps - 1) * tm
    for r in range(n_rows):
        k, t = divmod(r, tm)
        row_copy(dest_ref[k * m_total + nxt + t], slot, r).start(priority=r % 2)
    res = _layer_norm(alpha * x1_ref[...] + y, g2_ref[...], b2_ref[...])

    @pl.when(i < n_first)
    def _():
        op_ref[...] = res

    @pl.when(i >= n_first)
    def _():
        os_ref[...] = res

    @pl.when(i == nsteps - 1)
    def _():
        wait(0)
        wait(1)


def _combine(dest_km, x1, gates_mk, ys, g2, b2, *, tm, alpha, m_first):
    m, d = x1.shape
    n_first = m_first // tm
    assert m_first % tm == 0 and 0 < m_first < m
    kern = functools.partial(_combine_kernel, tm=tm, top_k=TOP_K, alpha=alpha, m_total=m, n_first=n_first)
    grid_spec = pltpu.PrefetchScalarGridSpec(
        num_scalar_prefetch=1,
        grid=(m // tm,),
        in_specs=[pl.BlockSpec((tm, d), lambda i, ds: (i, 0)),
                  pl.BlockSpec((tm, SUBLANE), lambda i, ds: (i, 0)),
                  pl.BlockSpec(memory_space=pl.ANY),
                  pl.BlockSpec((1, d), lambda i, ds: (0, 0)),
                  pl.BlockSpec((1, d), lambda i, ds: (0, 0))],
        out_specs=[pl.BlockSpec((tm, d), lambda i, ds: (jnp.minimum(i, n_first - 1), 0)),
                   pl.BlockSpec((tm, d), lambda i, ds: (jnp.maximum(i - n_first, 0), 0))],
        scratch_shapes=[pltpu.VMEM((2, TOP_K * tm * SUBLANE, LANE), F32), pltpu.SemaphoreType.DMA((2,))])
    return pl.pallas_call(
        kern,
        grid_spec=grid_spec,
        out_shape=[jax.ShapeDtypeStruct((m_first, d), F32), jax.ShapeDtypeStruct((m - m_first, d), F32)],
        compiler_params=_cparams(("arbitrary",), VMEM_LIMIT),
        name="moe_combine",
    )(dest_km, x1, gates_mk, ys, g2, b2)


def _pack_w_in(w_in, splits):
    q_lora, kv_lora, rope, conv_ch, gv, nh, _, d, _ = splits
    offs = [0]
    for s in splits:
        offs.append(offs[-1] + s)
    part = [w_in[:, offs[i]:offs[i + 1]] for i in range(len(splits))]
    q_lat, kv_lat, k_r, qkv, z, a, b, g_a, g_b = part
    dm = w_in.shape[0]
    half = rope // 2
    zpad = lambda n: jnp.zeros((dm, n), w_in.dtype)
    k_sw = jnp.concatenate([k_r[:, half:], k_r[:, :half]], axis=1)
    small = jnp.concatenate([q_lat, kv_lat, k_r, zpad(LANE - rope), k_sw, zpad(LANE - rope),
                             a, b, zpad(LANE - 2 * nh)], axis=1)
    return jnp.concatenate([qkv, small, z, g_a, g_b], axis=1).astype(BF16)


def _pack_w_uq(w_uq, nope, rope):
    w = jnp.transpose(w_uq, (1, 0, 2))
    half = rope // 2
    r = w[..., nope:]
    zp = jnp.zeros(r.shape[:-1] + (LANE - rope,), w.dtype)
    r_sw = jnp.concatenate([r[..., half:], r[..., :half]], axis=-1)
    return jnp.concatenate([w[..., :nope], r, zp, r_sw, zp], axis=-1).astype(BF16)


def _rope_tables(pos, rope):
    half = rope // 2
    inv = ROPE_THETA ** (-jnp.arange(half, dtype=F32) / half)
    ang = pos.astype(F32)[:, None] * inv[None, :]
    cos, sin = jnp.cos(ang), jnp.sin(ang)
    zp = jnp.zeros((pos.shape[0], LANE - rope), F32)
    return (jnp.concatenate([cos, cos, zp], axis=1), jnp.concatenate([-sin, sin, zp], axis=1))


def _route_meta(idx_t, rank_t, counts, m, n_exp, blk):
    a = m * TOP_K
    counts = counts.astype(jnp.int32)
    padded = (counts + blk - 1) // blk * blk
    pad_end = jnp.cumsum(padded)
    pad_start = pad_end - padded
    experts = jnp.arange(n_exp, dtype=jnp.int32)
    e_km = idx_t[:TOP_K]
    start_km = jnp.sum(jnp.where(e_km[:, :, None] == experts, pad_start, 0), axis=-1)
    dest_km = (start_km + rank_t[:TOP_K]).astype(jnp.int32).reshape(a)
    nb = a // blk + n_exp
    tok_km = jnp.tile(jnp.arange(m, dtype=jnp.int32), TOP_K)
    filler = jnp.arange(nb * blk, dtype=jnp.int32) % m
    rows = filler.at[dest_km].set(tok_km, unique_indices=True, mode='promise_in_bounds')
    first_row = jnp.arange(nb, dtype=jnp.int32) * blk
    block_e = jnp.minimum(jnp.sum((pad_end[None, :] <= first_row[:, None]).astype(jnp.int32), axis=1),
                          n_exp - 1).astype(jnp.int32)
    n_used = (pad_end[-1] // blk).astype(jnp.int32).reshape(1)
    return block_e, rows, n_used, dest_km


def kernel(x_prompt, x_sample, cache_ckv, cache_krope, page_table, state_conv, state_ssm, w_in, q_norm_w, kv_norm_w, w_uq, w_uk, w_uv, conv_w, a_log, dt_bias, gdn_norm_w, w_o, ln1_g, ln1_b, w_router, b_router, w_gu, b_gu, w_down, b_down, ln2_g, ln2_b):
    bp, tp, d = x_prompt.shape
    bs, ts, _ = x_sample.shape
    depth = w_in.shape[0]
    q_lora, n_heads, qk = w_uq.shape[1:]
    kv_lora, _, nope = w_uk.shape[1:]
    rope = qk - nope
    vh = w_uv.shape[3]
    cw_taps, conv_ch = conv_w.shape[1:]
    g_heads = a_log.shape[1]
    dk, dv = state_ssm.shape[3:]
    n_exp = w_router.shape[2]
    page = cache_ckv.shape[2]
    past = page_table.shape[1] * page
    splits = (q_lora, kv_lora, rope, conv_ch, g_heads * dv, g_heads, g_heads, d, d)
    assert sum(splits) == w_in.shape[2]
    alpha = (2 * depth) ** 0.25
    scale = (nope + rope) ** -0.5 * LOG2E
    mp, ms = bp * tp, bs * ts
    m = mp + ms
    small_w = q_lora + kv_lora + 3 * LANE
    assert small_w == d and conv_ch % d == 0
    qkv_blk, small_blk = 0, conv_ch // d
    z_blk, ga_blk, gb_blk = small_blk + 1, small_blk + 2, small_blk + 3
    ab_blk = (conv_ch + q_lora + kv_lora + 2 * LANE) // LANE

    cos_p, sin_p = _rope_tables(jnp.arange(tp, dtype=jnp.int32), rope)
    cos_s, sin_s = _rope_tables(past + jnp.arange(ts, dtype=jnp.int32), rope)
    cos_t = jnp.concatenate([jnp.tile(cos_p, (bp, 1)), jnp.tile(cos_s, (bs, 1))], axis=0)
    sin_t = jnp.concatenate([jnp.tile(sin_p, (bp, 1)), jnp.tile(sin_s, (bs, 1))], axis=0)

    x_p, x_s = x_prompt.reshape(mp, d), x_sample.reshape(ms, d)
    outs = {k: [] for k in ("ckv_p", "kr_p", "conv_p", "ssm_p", "ckv_s", "kr_s", "conv_s", "ssm_s")}
    pad_lanes = lambda v: jnp.pad(v, (0, LANE - v.shape[0])).reshape(1, LANE)
    for l in range(depth):
        x = jnp.concatenate([x_p, x_s], axis=0)
        w_pack = _pack_w_in(w_in[l], splits)
        wq = _pack_w_uq(w_uq[l], nope, rope)
        wuk = jnp.transpose(w_uk[l], (1, 2, 0)).astype(BF16)
        wuv = jnp.transpose(w_uv[l], (1, 0, 2)).astype(BF16)
        h = _in_proj(x, w_pack, tm=_row_tile(m, 2304), tn=1024)
        qf, c_all, kr_all, kt, cb = _mla_proj(
            h, small_blk, cos_t, sin_t, q_norm_w[l].reshape(1, q_lora), kv_norm_w[l].reshape(1, kv_lora),
            wq, wuk, tm=512, q_lora=q_lora, kv_lora=kv_lora, nope=nope, rope=rope, scale=scale)
        om_p = _attn_prompt(qf, kt, cb, wuv, batch=bp, seq=tp, tq=128, tk=min(1024, tp))
        qs = qf[:, mp:, :].reshape(n_heads, bs, ts, qf.shape[-1])
        qs = jnp.transpose(qs, (1, 0, 2, 3)).reshape(bs, n_heads * ts, qf.shape[-1])
        om_s = _attn_sample(page_table, qs, c_all, kr_all, mp // ts, cache_ckv[l],
                            jnp.swapaxes(cache_krope[l], 1, 2), wuv)

        alog_p = pad_lanes(a_log[l])
        dtb_p = pad_lanes(dt_bias[l])
        nw = gdn_norm_w[l].reshape(1, dv)
        gdn_kw = dict(qkv_blk=qkv_blk, ab_blk=ab_blk, z_blk=z_blk, conv_w=cw_taps)
        zeros_conv = jnp.zeros((bp, SUBLANE, conv_ch), F32)
        zeros_ssm = jnp.zeros((bp, g_heads, dk, dv), F32)
        og_p, ssm_p, ncv_p = _gdn(h, 0, zeros_conv, zeros_ssm, conv_w[l], alog_p, dtb_p, nw,
                                  batch=bp, seq=tp, chunk=min(GDN_CHUNK, tp), group=bp, **gdn_kw)
        conv8_s = jnp.pad(state_conv[l], ((0, 0), (SUBLANE - (cw_taps - 1), 0), (0, 0)))
        og_s, ssm_s, ncv_s = _gdn(h, mp, conv8_s, state_ssm[l], conv_w[l], alog_p, dtb_p, nw,
                                  batch=bs, seq=ts, chunk=ts, group=math.gcd(bs, GDN_SAMPLE_GROUP), **gdn_kw)

        x1, x1t, idx_t, gate_t, rank_t, counts = _post(
            x, om_p, om_s, og_p, og_s, h, ga_blk, gb_blk, w_o[l].astype(BF16), ln1_g[l].reshape(1, d), ln1_b[l].reshape(1, d),
            w_router[l].T, b_router[l].reshape(n_exp, 1), tm=512, alpha=alpha)
        block_e, rows, n_used, dest_km = _route_meta(idx_t, rank_t, counts[:, 0], m, n_exp, MOE_ROWS)
        ys = _moe_experts(block_e, rows, n_used, x1t, w_gu[l], b_gu[l], w_down[l], b_down[l], blk=MOE_ROWS)
        x_p, x_s = _combine(dest_km, x1, gate_t.T, ys, ln2_g[l].reshape(1, d), ln2_b[l].reshape(1, d),
                            tm=128, alpha=alpha, m_first=mp)

        outs["ckv_p"].append(c_all[:mp].reshape(bp, tp, kv_lora))
        outs["kr_p"].append(kr_all[:mp].reshape(bp, tp, rope))
        outs["conv_p"].append(ncv_p[:, SUBLANE - (cw_taps - 1):, :])
        outs["ssm_p"].append(ssm_p)
        outs["ckv_s"].append(c_all[mp:].reshape(bs, ts, kv_lora))
        outs["kr_s"].append(kr_all[mp:].reshape(bs, ts, rope))
        outs["conv_s"].append(ncv_s[:, SUBLANE - (cw_taps - 1):, :])
        outs["ssm_s"].append(ssm_s)

    return (x_p.reshape(bp, tp, d), x_s.reshape(bs, ts, d),
            jnp.stack(outs["ckv_p"]), jnp.stack(outs["kr_p"]), jnp.stack(outs["conv_p"]), jnp.stack(outs["ssm_p"]),
            jnp.stack(outs["ckv_s"]), jnp.stack(outs["kr_s"]), jnp.stack(outs["conv_s"]), jnp.stack(outs["ssm_s"]))
```

```python
import functools
import math

import jax
import jax.numpy as jnp
from jax import lax
from jax.experimental import pallas as pl
from jax.experimental.pallas import tpu as pltpu

F32 = jnp.float32
BF16 = jnp.bfloat16

ROPE_THETA = 10000.0
NORM_EPS = 1e-6
TOP_K = 4
SWIGLU_LIMIT = 7.0
SWIGLU_ALPHA = 1.702
GDN_CHUNK = 128
GDN_DIAG = 16
GDN_SAMPLE_GROUP = 8
MOE_ROWS = 512
PAGE_CHUNK = 8
DIAG_VARIANTS = 8
LANE = 128
SUBLANE = 8
VMEM_LIMIT = 56 * 1024 * 1024
NEG = -1e30
LOG2E = 1.4426950408889634


def _cparams(sem, vmem=None):
    return pltpu.CompilerParams(dimension_semantics=sem, vmem_limit_bytes=vmem)


def _dot(a, b, **kw):
    return jnp.dot(a, b, preferred_element_type=F32, **kw)


def _dot_nt(a, b, **kw):
    return lax.dot_general(a, b, (((1,), (1,)), ((), ())), preferred_element_type=F32, **kw)


def _bdot(a, b):
    return lax.dot_general(a, b, (((2,), (1,)), ((0,), (0,))), preferred_element_type=F32)


def _bdot_nt(a, b):
    return lax.dot_general(a, b, (((2,), (2,)), ((0,), (0,))), preferred_element_type=F32)


def _bdot_tn(a, b):
    return lax.dot_general(a, b, (((1,), (1,)), ((0,), (0,))), preferred_element_type=F32)


def _sigmoid(x):
    return 1.0 / (1.0 + jnp.exp(-x))


def _layer_norm(v, g, b):
    mu = jnp.mean(v, -1, keepdims=True)
    vc = v - mu
    var = jnp.mean(vc * vc, -1, keepdims=True)
    return vc * lax.rsqrt(var + NORM_EPS) * g + b


def _inproj_kernel(x_ref, w_ref, o_ref, xb_ref):
    @pl.when(pl.program_id(1) == 0)
    def _():
        xb_ref[...] = x_ref[...].astype(BF16)

    o_ref[...] = _dot(xb_ref[...], w_ref[...])


def _row_tile(m, target):
    return max(t for t in range(SUBLANE, target + 1, SUBLANE) if m % t == 0)


def _in_proj(x, w, tm, tn):
    m, k = x.shape
    n = w.shape[1]
    return pl.pallas_call(
        _inproj_kernel,
        grid=(m // tm, n // tn),
        in_specs=[pl.BlockSpec((tm, k), lambda i, j: (i, 0)),
                  pl.BlockSpec((k, tn), lambda i, j: (0, j))],
        out_specs=pl.BlockSpec((tm, tn), lambda i, j: (i, j)),
        out_shape=jax.ShapeDtypeStruct((m, n), F32),
        scratch_shapes=[pltpu.VMEM((tm, k), BF16)],
        compiler_params=_cparams(("parallel", "arbitrary"), VMEM_LIMIT),
        name="in_proj",
    )(x, w)


def _mla_proj_kernel(h_ref, cos_ref, sin_ref, qnw_ref, kvnw_ref, wq_ref, wuk_ref,
                     qf_ref, c_ref, kr_ref, kt_ref, cb_ref, *, n_heads, q_lora, kv_lora, nope, rope, scale):
    hs = h_ref[...]
    cos = cos_ref[...]
    sin = sin_ref[...]
    q_lat = hs[:, :q_lora]
    qn = q_lat * lax.rsqrt(jnp.mean(q_lat * q_lat, -1, keepdims=True) + NORM_EPS) * qnw_ref[...]
    qn = qn.astype(BF16)
    kv = hs[:, q_lora:q_lora + kv_lora]
    c = kv * lax.rsqrt(jnp.mean(kv * kv, -1, keepdims=True) + NORM_EPS) * kvnw_ref[...]
    o = q_lora + kv_lora
    kr = hs[:, o:o + LANE] * cos + hs[:, o + LANE:o + 2 * LANE] * sin
    c_ref[...] = c
    kr_ref[...] = kr[:, :rope]
    cb = c.astype(BF16)
    cb_ref[...] = cb
    kfull = jnp.concatenate([cb, kr.astype(BF16)], axis=1)
    kw = kfull.shape[1]
    eye = (lax.broadcasted_iota(jnp.int32, (kw, kw), 0) == lax.broadcasted_iota(jnp.int32, (kw, kw), 1))
    kt_ref[...] = _dot_nt(eye.astype(BF16), kfull).astype(BF16)
    for h in range(n_heads):
        qh = _dot(qn, wq_ref[h])
        qa = _dot(qh[:, :nope].astype(BF16), wuk_ref[h])
        qr = qh[:, nope:nope + LANE] * cos + qh[:, nope + LANE:nope + 2 * LANE] * sin
        qf_ref[h, :, :kv_lora] = (qa * scale).astype(BF16)
        qf_ref[h, :, kv_lora:] = (qr * scale).astype(BF16)


def _mla_proj(h, col_blk, cos_t, sin_t, qnw, kvnw, wq, wuk, *, tm, q_lora, kv_lora, nope, rope, scale):
    m = h.shape[0]
    n_heads = wq.shape[0]
    wcol = q_lora + kv_lora + 3 * LANE
    kw = kv_lora + LANE
    kern = functools.partial(_mla_proj_kernel, n_heads=n_heads, q_lora=q_lora, kv_lora=kv_lora,
                             nope=nope, rope=rope, scale=scale)
    return pl.pallas_call(
        kern,
        grid=(m // tm,),
        in_specs=[pl.BlockSpec((tm, wcol), lambda i: (i, col_blk)),
                  pl.BlockSpec((tm, LANE), lambda i: (i, 0)),
                  pl.BlockSpec((tm, LANE), lambda i: (i, 0)),
                  pl.BlockSpec((1, q_lora), lambda i: (0, 0)),
                  pl.BlockSpec((1, kv_lora), lambda i: (0, 0)),
                  pl.BlockSpec(wq.shape, lambda i: (0, 0, 0)),
                  pl.BlockSpec(wuk.shape, lambda i: (0, 0, 0))],
        out_specs=[pl.BlockSpec((n_heads, tm, kw), lambda i: (0, i, 0)),
                   pl.BlockSpec((tm, kv_lora), lambda i: (i, 0)),
                   pl.BlockSpec((tm, rope), lambda i: (i, 0)),
                   pl.BlockSpec((kw, tm), lambda i: (0, i)),
                   pl.BlockSpec((tm, kv_lora), lambda i: (i, 0))],
        out_shape=[jax.ShapeDtypeStruct((n_heads, m, kw), BF16),
                   jax.ShapeDtypeStruct((m, kv_lora), F32),
                   jax.ShapeDtypeStruct((m, rope), F32),
                   jax.ShapeDtypeStruct((kw, m), BF16),
                   jax.ShapeDtypeStruct((m, kv_lora), BF16)],
        compiler_params=_cparams(("parallel",), VMEM_LIMIT),
        name="mla_proj",
    )(h, cos_t, sin_t, qnw, kvnw, wq, wuk)


def _softmax_step(s, v, m_ref, l_ref, acc_ref):
    m_prev = m_ref[...]
    m_new = jnp.maximum(m_prev, jnp.max(s, -1, keepdims=True))
    alpha = jnp.exp2(m_prev - m_new)
    p = jnp.exp2(s - m_new)
    l_ref[...] = alpha * l_ref[...] + jnp.sum(p, -1, keepdims=True)
    acc_ref[...] = alpha * acc_ref[...] + _dot(p.astype(BF16), v)
    m_ref[...] = m_new


def _attn_prompt_kernel(qf_ref, kt_ref, v_ref, wuv_ref, o_ref, m_ref, l_ref, acc_ref, *, n_heads, tq, tk, dv, vh, n_split):
    i = pl.program_id(1)
    m_ref[...] = jnp.full(m_ref.shape, NEG, F32)
    l_ref[...] = jnp.zeros(l_ref.shape, F32)
    acc_ref[...] = jnp.zeros(acc_ref.shape, F32)
    n_full = (i * tq) // tk
    off = i * tq - n_full * tk
    hs = n_heads // n_split
    rs = hs * tq

    def step(j, width, masked):
        start = pl.multiple_of(j * tk, tk)
        kt = kt_ref[:, pl.ds(start, width)]
        v = v_ref[pl.ds(start, width), :]
        ps = []
        for g in range(n_split):
            q = qf_ref[g * hs:(g + 1) * hs].reshape(rs, qf_ref.shape[-1])
            s = _dot(q, kt)
            if masked:
                row = lax.broadcasted_iota(jnp.int32, (tq, width), 0)
                col = lax.broadcasted_iota(jnp.int32, (tq, width), 1)
                s = jnp.where((col <= row + off)[None], s.reshape(hs, tq, width), NEG).reshape(rs, width)
            sl = slice(g * rs, (g + 1) * rs)
            m_prev = m_ref[sl]
            m_new = jnp.maximum(m_prev, jnp.max(s, -1, keepdims=True))
            alpha = jnp.exp2(m_prev - m_new)
            p = jnp.exp2(s - m_new)
            l_ref[sl] = alpha * l_ref[sl] + jnp.sum(p, -1, keepdims=True)
            acc_ref[sl] = alpha * acc_ref[sl]
            m_ref[sl] = m_new
            ps.append(p.astype(BF16))
        acc_ref[...] += _dot(jnp.concatenate(ps, axis=0), v)

    def body(j, carry):
        step(j, tk, False)
        return carry

    lax.fori_loop(0, n_full, body, 0)
    need = off + tq
    widths = [tk * (v + 1) // DIAG_VARIANTS for v in range(DIAG_VARIANTS)]
    for v, width in enumerate(widths):
        lo = widths[v - 1] if v else 0

        @pl.when(jnp.logical_and(need > lo, need <= width))
        def _(width=width):
            step(n_full, width, True)

    o = acc_ref[...] / l_ref[...]
    for h in range(n_heads):
        oh = o[h * tq:(h + 1) * tq].astype(BF16)
        o_ref[:, h * vh:(h + 1) * vh] = _dot(oh, wuv_ref[h])


def _attn_prompt(qf, kt, cb, wuv, *, batch, seq, tq, tk):
    n_heads, _, kw = qf.shape
    dv, vh = wuv.shape[1], wuv.shape[2]
    nq = seq // tq
    rows = n_heads * tq
    assert seq % tk == 0 and tk % tq == 0 and tk % (DIAG_VARIANTS * LANE) == 0
    kern = functools.partial(_attn_prompt_kernel, n_heads=n_heads, tq=tq, tk=tk, dv=dv, vh=vh, n_split=2)
    return pl.pallas_call(
        kern,
        grid=(batch, nq),
        in_specs=[pl.BlockSpec((n_heads, tq, kw), lambda b, i: (0, b * nq + i, 0)),
                  pl.BlockSpec((kw, seq), lambda b, i: (0, b)),
                  pl.BlockSpec((seq, dv), lambda b, i: (b, 0)),
                  pl.BlockSpec(wuv.shape, lambda b, i: (0, 0, 0))],
        out_specs=pl.BlockSpec((tq, n_heads * vh), lambda b, i: (b * nq + i, 0)),
        out_shape=jax.ShapeDtypeStruct((batch * seq, n_heads * vh), F32),
        scratch_shapes=[pltpu.VMEM((rows, 1), F32), pltpu.VMEM((rows, 1), F32),
                        pltpu.VMEM((rows, dv), F32)],
        compiler_params=_cparams(("parallel", "arbitrary"), VMEM_LIMIT),
        name="attn_prompt",
    )(qf, kt, cb, wuv)


def _attn_sample_kernel(pt_ref, q_ref, cn_ref, krn_ref, cc_hbm, cr_hbm, wuv_ref, o_ref,
                        cbuf, rbuf, sem, *, n_pages, page, n_heads, ts, dv, dr, vh):
    b = pl.program_id(0)
    nb = pl.num_programs(0)
    slot = b % 2
    rows = n_heads * ts

    def fetch(bb, s):
        for p in range(n_pages):
            pg = pt_ref[bb, p]
            pltpu.make_async_copy(cc_hbm.at[pg], cbuf.at[s, p], sem.at[0, s]).start()
            pltpu.make_async_copy(cr_hbm.at[pg], rbuf.at[s, p], sem.at[1, s]).start(priority=1)

    def wait(s):
        pltpu.make_async_copy(cc_hbm.at[pl.ds(0, n_pages)], cbuf.at[s], sem.at[0, s]).wait()
        pltpu.make_async_copy(cr_hbm.at[pl.ds(0, n_pages)], rbuf.at[s], sem.at[1, s]).wait()

    @pl.when(b == 0)
    def _():
        fetch(0, 0)

    wait(slot)
    fetch(jnp.minimum(b + 1, nb - 1), 1 - slot)

    q = q_ref[0].astype(F32)
    qc = q[:, :dv]
    qr = q[:, dv:dv + dr]
    n_chunks = n_pages // PAGE_CHUNK
    ck = PAGE_CHUNK * page
    parts = []
    for ch in range(n_chunks):
        c_ch = cbuf[slot, ch * PAGE_CHUNK:(ch + 1) * PAGE_CHUNK].reshape(ck, dv)
        r_ch = jnp.concatenate([rbuf[slot, ch * PAGE_CHUNK + u] for u in range(PAGE_CHUNK)], axis=1)
        parts.append(_dot_nt(qc, c_ch) + _dot(qr, r_ch))
    cn = jnp.concatenate([cn_ref[...], jnp.zeros((LANE - ts, dv), F32)], axis=0)
    krn = jnp.concatenate([krn_ref[...], jnp.zeros((LANE - ts, dr), F32)], axis=0)
    row = lax.broadcasted_iota(jnp.int32, (ts, LANE), 0)
    col = lax.broadcasted_iota(jnp.int32, (ts, LANE), 1)
    s_new = (_dot_nt(qc, cn) + _dot_nt(qr, krn)).reshape(n_heads, ts, LANE)
    s_new = jnp.where((col <= row)[None], s_new, NEG).reshape(rows, LANE)

    m = jnp.max(s_new, -1, keepdims=True)
    for s in parts:
        m = jnp.maximum(m, jnp.max(s, -1, keepdims=True))
    p_new = jnp.exp2(s_new - m)
    l = jnp.sum(p_new, -1, keepdims=True)
    acc = _dot(p_new, cn)
    for ch in range(n_chunks):
        p = jnp.exp2(parts[ch] - m)
        l = l + jnp.sum(p, -1, keepdims=True)
        acc = acc + _dot(p, cbuf[slot, ch * PAGE_CHUNK:(ch + 1) * PAGE_CHUNK].reshape(ck, dv))
    o = acc / l
    for h in range(n_heads):
        oh = o[h * ts:(h + 1) * ts].astype(BF16)
        o_ref[:, h * vh:(h + 1) * vh] = _dot(oh, wuv_ref[h])

    @pl.when(b == nb - 1)
    def _():
        wait(1 - slot)


def _attn_sample(page_table, qs, c_all, kr_all, row_blk_off, cache_c, cache_rt, wuv):
    bs, rows, kw = qs.shape
    n_heads, dv, vh = wuv.shape
    ts = rows // n_heads
    n_pages = page_table.shape[1]
    page = cache_c.shape[1]
    dr = cache_rt.shape[1]
    assert n_pages % PAGE_CHUNK == 0
    kern = functools.partial(_attn_sample_kernel, n_pages=n_pages, page=page, n_heads=n_heads, ts=ts,
                             dv=dv, dr=dr, vh=vh)
    grid_spec = pltpu.PrefetchScalarGridSpec(
        num_scalar_prefetch=1,
        grid=(bs,),
        in_specs=[pl.BlockSpec((1, rows, kw), lambda b, pt: (b, 0, 0)),
                  pl.BlockSpec((ts, dv), lambda b, pt: (row_blk_off + b, 0)),
                  pl.BlockSpec((ts, dr), lambda b, pt: (row_blk_off + b, 0)),
                  pl.BlockSpec(memory_space=pl.ANY),
                  pl.BlockSpec(memory_space=pl.ANY),
                  pl.BlockSpec(wuv.shape, lambda b, pt: (0, 0, 0))],
        out_specs=pl.BlockSpec((ts, n_heads * vh), lambda b, pt: (b, 0)),
        scratch_shapes=[pltpu.VMEM((2, n_pages, page, dv), F32), pltpu.VMEM((2, n_pages, dr, page), F32),
                        pltpu.SemaphoreType.DMA((2, 2))])
    return pl.pallas_call(
        kern,
        grid_spec=grid_spec,
        out_shape=jax.ShapeDtypeStruct((bs * ts, n_heads * vh), F32),
        compiler_params=_cparams(("arbitrary",), VMEM_LIMIT),
        name="attn_sample",
    )(page_table, qs, c_all, kr_all, cache_c, cache_rt, wuv)


def _neumann_inv(low, eye, steps):
    p = eye - low
    x = low
    for _ in range(steps):
        x = _bdot(x, x)
        p = p + _bdot(p, x)
    return p


def _gdn_kernel(*refs, n_in, n_seq, chunk, n_heads, dk, dv, conv_w):
    qkv_refs, ab_refs, z_refs = refs[:n_in], refs[n_in:2 * n_in], refs[2 * n_in:3 * n_in]
    (cs_ref, s0_ref, cw_ref, alog_ref, dtb_ref, nw_ref,
     o_ref, sfin_ref, ncv_ref, xp_ref, s_ref) = refs[3 * n_in:]
    n = pl.program_id(1)
    c = chunk
    per = n_seq // n_in
    n_prob = n_seq * n_heads
    hi = lax.Precision.HIGHEST

    def seq_rows(group, g):
        k = g % per
        return group[g // per][k * c:(k + 1) * c, :]

    @pl.when(n == 0)
    def _():
        s_ref[...] = s0_ref[...].reshape(n_prob, dk, dv)
        xp_ref[:, 0:SUBLANE, :] = cs_ref[...]

    r_i = lax.broadcasted_iota(jnp.int32, (c, c), 0)
    c_i = lax.broadcasted_iota(jnp.int32, (c, c), 1)
    causal = (c_i <= r_i)[None]
    strict = (c_i < r_i)[None]
    eye_c = (r_i == c_i).astype(F32)[None]
    db = min(GDN_DIAG, c)
    n_blk = c // db
    sh = int(math.log2(db))
    same_blk = (jnp.right_shift(r_i, sh) == jnp.right_shift(c_i, sh))[None]
    r_l = lax.broadcasted_iota(jnp.int32, (LANE, LANE), 0)
    c_l = lax.broadcasted_iota(jnp.int32, (LANE, LANE), 1)
    eye_l = (r_l == c_l).astype(F32)
    tril = (c_i <= r_i).astype(F32)
    cw = cw_ref[...]
    hk = n_heads * dk

    q_l, k_l, v_l, z_l, beta_l, gcol_l, grow_l, tails = [], [], [], [], [], [], [], []
    for g in range(n_seq):
        xp_ref[g, SUBLANE:SUBLANE + c, :] = seq_rows(qkv_refs, g)
        conv = xp_ref[g, pl.ds(SUBLANE - (conv_w - 1), c), :] * cw[0:1, :]
        for j in range(1, conv_w):
            conv = conv + xp_ref[g, pl.ds(SUBLANE - (conv_w - 1) + j, c), :] * cw[j:j + 1, :]
        tail = xp_ref[g, c:c + SUBLANE, :]
        xp_ref[g, 0:SUBLANE, :] = tail
        tails.append(tail)
        act = conv * _sigmoid(conv)
        ab = seq_rows(ab_refs, g)
        apb = ab + dtb_ref[...]
        softplus = jnp.maximum(apb, 0.0) + jnp.log(1.0 + jnp.exp(-jnp.abs(apb)))
        gfull = -jnp.exp(alog_ref[...]) * softplus
        betaf = _sigmoid(ab)
        gc = _dot(tril, gfull, precision=hi)
        gc_t = _dot_nt(eye_l, gc, precision=hi)
        zg = seq_rows(z_refs, g)
        for h in range(n_heads):
            q_l.append(act[:, h * dk:(h + 1) * dk])
            k_l.append(act[:, hk + h * dk:hk + (h + 1) * dk])
            v_l.append(act[:, 2 * hk + h * dv:2 * hk + (h + 1) * dv])
            z_l.append(zg[:, h * dv:(h + 1) * dv])
            beta_l.append(betaf[:, n_heads + h:n_heads + h + 1])
            gcol_l.append(gc[:, h:h + 1])
            grow_l.append(gc_t[h:h + 1, :])

    q = jnp.stack(q_l)
    k = jnp.stack(k_l)
    v = jnp.stack(v_l)
    z = jnp.stack(z_l)
    beta = jnp.stack(beta_l)
    gcol = jnp.stack(gcol_l)
    grow = jnp.stack(grow_l)
    q = q * lax.rsqrt(jnp.sum(q * q, -1, keepdims=True) + NORM_EPS) * (dk ** -0.5)
    k = k * lax.rsqrt(jnp.sum(k * k, -1, keepdims=True) + NORM_EPS)
    decay = jnp.where(causal, jnp.exp(jnp.where(causal, gcol - grow, 0.0)), 0.0)
    kb = k * beta
    lower = jnp.where(strict, _bdot_nt(kb, k) * decay, 0.0)
    attn = _bdot_nt(q, k) * decay
    l_d = jnp.where(same_blk, lower, 0.0)
    t_inv = _neumann_inv(l_d, eye_c, int(math.log2(db)) - 1)
    if n_blk > 1:
        m_inv = _neumann_inv(_bdot(t_inv, lower - l_d), eye_c, int(math.log2(n_blk)) - 1)
        t_inv = _bdot(m_inv, t_inv)
    egc = jnp.exp(gcol)
    uw = _bdot(t_inv, jnp.concatenate([v * beta, kb * egc], axis=2))
    s = s_ref[...]
    v_new = uw[:, :, :dv] - _bdot(uw[:, :, dv:], s)
    o = _bdot(q * egc, s) + _bdot(attn, v_new)
    glast = gcol[:, c - 1:c, :]
    kdec = k * jnp.exp(glast - gcol)
    s_new = s * jnp.exp(glast) + _bdot_tn(kdec, v_new)
    s_ref[...] = s_new
    on = o * lax.rsqrt(jnp.mean(o * o, -1, keepdims=True) + NORM_EPS) * nw_ref[...] * (z * _sigmoid(z))
    for g in range(n_seq):
        for h in range(n_heads):
            o_ref[g, :, h * dv:(h + 1) * dv] = on[g * n_heads + h]

    @pl.when(n == pl.num_programs(1) - 1)
    def _():
        sfin_ref[...] = s_new.reshape(n_seq, n_heads, dk, dv)
        for g in range(n_seq):
            ncv_ref[g] = tails[g]


def _gdn(h, row_off, conv_state8, ssm_state, cw, alog_p, dtb_p, nw, *, batch, seq, chunk, group,
         qkv_blk, ab_blk, z_blk, conv_w):
    _, n_heads, dk, dv = ssm_state.shape
    nc = seq // chunk
    ch = cw.shape[1]
    hd = n_heads * dv
    contiguous = nc == 1
    n_in = 1 if contiguous else group
    assert batch % group == 0 and row_off % (group * chunk) == 0
    kern = functools.partial(_gdn_kernel, n_in=n_in, n_seq=group, chunk=chunk, n_heads=n_heads,
                             dk=dk, dv=dv, conv_w=conv_w)
    if contiguous:
        rb = group * chunk
        row_maps = [lambda i, n: row_off // rb + i]
    else:
        rb = chunk
        row_maps = [(lambda i, n, g=g: row_off // rb + (i * group + g) * nc + n) for g in range(group)]

    def specs(width, col_blk):
        return [pl.BlockSpec((rb, width), lambda i, n, r=r: (r(i, n), col_blk)) for r in row_maps]

    in_specs = specs(ch, qkv_blk) + specs(LANE, ab_blk) + specs(hd, z_blk)
    in_specs += [pl.BlockSpec((group, SUBLANE, ch), lambda i, n: (i, 0, 0)),
                 pl.BlockSpec((group, n_heads, dk, dv), lambda i, n: (i, 0, 0, 0)),
                 pl.BlockSpec(cw.shape, lambda i, n: (0, 0)),
                 pl.BlockSpec((1, LANE), lambda i, n: (0, 0)),
                 pl.BlockSpec((1, LANE), lambda i, n: (0, 0)),
                 pl.BlockSpec((1, dv), lambda i, n: (0, 0))]
    o3, sfin, ncv = pl.pallas_call(
        kern,
        grid=(batch // group, nc),
        in_specs=in_specs,
        out_specs=[pl.BlockSpec((group, chunk, hd), lambda i, n: (i, n, 0)),
                   pl.BlockSpec((group, n_heads, dk, dv), lambda i, n: (i, 0, 0, 0)),
                   pl.BlockSpec((group, SUBLANE, ch), lambda i, n: (i, 0, 0))],
        out_shape=[jax.ShapeDtypeStruct((batch, seq, hd), F32),
                   jax.ShapeDtypeStruct(ssm_state.shape, F32),
                   jax.ShapeDtypeStruct((batch, SUBLANE, ch), F32)],
        scratch_shapes=[pltpu.VMEM((group, chunk + SUBLANE, ch), F32),
                        pltpu.VMEM((group * n_heads, dk, dv), F32)],
        compiler_params=_cparams(("parallel", "arbitrary"), VMEM_LIMIT),
        name="gdn",
    )(*([h] * (3 * n_in)), conv_state8, ssm_state, cw, alog_p, dtb_p, nw)
    return o3.reshape(batch * seq, hd), sfin, ncv


def _store_row_tiles(ref, val):
    rows = val.shape[0]
    for j in range(SUBLANE):
        ref[pl.ds(j, rows, stride=SUBLANE), :] = val[:, j * LANE:(j + 1) * LANE]


def _load_row_tiles(ref, start, rows, j):
    return ref[pl.ds(start * SUBLANE + j, rows, stride=SUBLANE), :]


def _post_kernel(x_ref, omp_ref, oms_ref, ogp_ref, ogs_ref, ga_ref, gb_ref, wo_ref, g1_ref, b1_ref, wr_ref, br_ref,
                 x1_ref, x1t_ref, idx_ref, gate_ref, rank_ref, cnt_ref, carry_ref, *, alpha, top_k, n_first):
    @pl.when(pl.program_id(0) == 0)
    def _():
        carry_ref[...] = jnp.zeros(carry_ref.shape, F32)

    first = pl.program_id(0) < n_first
    om = jnp.where(first, omp_ref[...], oms_ref[...])
    og = jnp.where(first, ogp_ref[...], ogs_ref[...])
    mix = _sigmoid(ga_ref[...]) * om + _sigmoid(gb_ref[...]) * og
    y = _dot(mix.astype(BF16), wo_ref[...])
    x1 = _layer_norm(alpha * x_ref[...] + y, g1_ref[...], b1_ref[...])
    x1_ref[...] = x1
    _store_row_tiles(x1t_ref, x1)
    logits = _dot_nt(wr_ref[...], x1, precision=lax.Precision.HIGHEST) + br_ref[...]
    n_exp, tm = logits.shape
    e_i = lax.broadcasted_iota(jnp.int32, (n_exp, tm), 0)
    vals, idxs = [], []
    for _ in range(top_k):
        mx = jnp.max(logits, axis=0, keepdims=True)
        ix = jnp.min(jnp.where(logits == mx, e_i, n_exp), axis=0, keepdims=True)
        vals.append(mx)
        idxs.append(ix)
        logits = jnp.where(e_i == ix, -jnp.inf, logits)
    es = [jnp.exp(v - vals[0]) for v in vals]
    tot = es[0]
    for e in es[1:]:
        tot = tot + e
    onehots = [e_i == ix for ix in idxs]
    sel = onehots[0].astype(F32)
    for oh in onehots[1:]:
        sel = sel + oh.astype(F32)
    t_r = lax.broadcasted_iota(jnp.int32, (tm, tm), 0)
    t_c = lax.broadcasted_iota(jnp.int32, (tm, tm), 1)
    before = _dot(sel.astype(BF16), (t_r < t_c).astype(BF16)) + carry_ref[...]
    carry = carry_ref[...] + jnp.sum(sel, axis=1, keepdims=True)
    carry_ref[...] = carry
    cnt_ref[...] = carry
    r_i = lax.broadcasted_iota(jnp.int32, (SUBLANE, tm), 0)
    idx_o = jnp.zeros((SUBLANE, tm), jnp.int32)
    gate_o = jnp.zeros((SUBLANE, tm), F32)
    rank_o = jnp.zeros((SUBLANE, tm), F32)
    for k in range(top_k):
        idx_o = jnp.where(r_i == k, idxs[k], idx_o)
        gate_o = jnp.where(r_i == k, es[k] / tot, gate_o)
        rank_k = jnp.sum(jnp.where(onehots[k], before, 0.0), axis=0, keepdims=True)
        rank_o = jnp.where(r_i == k, rank_k, rank_o)
    idx_ref[...] = idx_o
    gate_ref[...] = gate_o
    rank_ref[...] = rank_o.astype(jnp.int32)


def _post(x, om_p, om_s, og_p, og_s, h, ga_blk, gb_blk, wo, g1, b1, wr_t, br, *, tm, alpha):
    m, d = x.shape
    n_exp = wr_t.shape[0]
    n_first = om_p.shape[0] // tm
    assert om_p.shape[0] % tm == 0 and om_s.shape[0] % tm == 0 and d == SUBLANE * LANE
    kern = functools.partial(_post_kernel, alpha=alpha, top_k=TOP_K, n_first=n_first)
    row = lambda i: (i, 0)
    const = lambda i: (0, 0)
    first = lambda i: (jnp.minimum(i, n_first - 1), 0)
    second = lambda i: (jnp.maximum(i - n_first, 0), 0)
    return pl.pallas_call(
        kern,
        grid=(m // tm,),
        in_specs=[pl.BlockSpec((tm, d), row),
                  pl.BlockSpec((tm, d), first), pl.BlockSpec((tm, d), second),
                  pl.BlockSpec((tm, d), first), pl.BlockSpec((tm, d), second),
                  pl.BlockSpec((tm, d), lambda i: (i, ga_blk)),
                  pl.BlockSpec((tm, d), lambda i: (i, gb_blk)),
                  pl.BlockSpec(wo.shape, const), pl.BlockSpec((1, d), const), pl.BlockSpec((1, d), const),
                  pl.BlockSpec(wr_t.shape, const), pl.BlockSpec((n_exp, 1), const)],
        out_specs=[pl.BlockSpec((tm, d), row),
                   pl.BlockSpec((tm * SUBLANE, LANE), row),
                   pl.BlockSpec((SUBLANE, tm), lambda i: (0, i)),
                   pl.BlockSpec((SUBLANE, tm), lambda i: (0, i)),
                   pl.BlockSpec((SUBLANE, tm), lambda i: (0, i)),
                   pl.BlockSpec((n_exp, 1), const)],
        out_shape=[jax.ShapeDtypeStruct((m, d), F32),
                   jax.ShapeDtypeStruct((m * SUBLANE, LANE), F32),
                   jax.ShapeDtypeStruct((SUBLANE, m), jnp.int32),
                   jax.ShapeDtypeStruct((SUBLANE, m), F32),
                   jax.ShapeDtypeStruct((SUBLANE, m), jnp.int32),
                   jax.ShapeDtypeStruct((n_exp, 1), F32)],
        scratch_shapes=[pltpu.VMEM((n_exp, 1), F32)],
        compiler_params=_cparams(("arbitrary",), VMEM_LIMIT),
        name="post_mix",
    )(x, om_p, om_s, og_p, og_s, h, h, wo, g1, b1, wr_t, br)


def _moe_kernel(be_ref, rows_ref, nused_ref, x_hbm, wgu_ref, bgu_ref, wd_ref, bd_ref, o_ref,
                xbuf, sem, wgu_bf, wd_bf, xb_ref, *, blk, d_exp):
    i = pl.program_id(0)
    n_used = nused_ref[0]
    slot = i % 2

    def row_copy(tok, s, r):
        return pltpu.make_async_copy(x_hbm.at[pl.ds(tok * SUBLANE, SUBLANE)],
                                     xbuf.at[s, pl.ds(r * SUBLANE, SUBLANE)], sem.at[s])

    def wait(s):
        pltpu.make_async_copy(x_hbm.at[pl.ds(0, blk * SUBLANE)], xbuf.at[s], sem.at[s]).wait()

    @pl.when(jnp.logical_and(i == 0, n_used > 0))
    def _():
        second = jnp.minimum(1, n_used - 1) * blk

        def body(r, carry):
            row_copy(rows_ref[r], 0, r).start()
            row_copy(rows_ref[second + r], 1, r).start()
            return carry
        lax.fori_loop(0, blk, body, 0)

    @pl.when(i < n_used)
    def _():
        e = be_ref[i]
        e_prev = be_ref[jnp.maximum(i - 1, 0)]

        @pl.when(jnp.logical_or(i == 0, e != e_prev))
        def _():
            wgu_bf[...] = wgu_ref[0].astype(BF16)
            wd_bf[...] = wd_ref[0].astype(BF16)

        wait(slot)
        for j in range(SUBLANE):
            xb_ref[:, j * LANE:(j + 1) * LANE] = _load_row_tiles(xbuf.at[slot], 0, blk, j).astype(BF16)
        nxt = jnp.minimum(i + 2, n_used - 1) * blk
        for r in range(blk):
            row_copy(rows_ref[nxt + r], slot, r).start(priority=r % 2)
        hh = _dot(xb_ref[...], wgu_bf[...]) + bgu_ref[0]
        gate = jnp.minimum(hh[:, :d_exp], SWIGLU_LIMIT)
        up = jnp.clip(hh[:, d_exp:], -SWIGLU_LIMIT, SWIGLU_LIMIT)
        act = (up + 1.0) * gate * _sigmoid(SWIGLU_ALPHA * gate)
        _store_row_tiles(o_ref, _dot(act.astype(BF16), wd_bf[...]) + bd_ref[0])

        @pl.when(i == n_used - 1)
        def _():
            wait(0)
            wait(1)

    @pl.when(i >= n_used)
    def _():
        o_ref[...] = jnp.zeros(o_ref.shape, F32)


def _moe_experts(block_e, rows, n_used, x1, w_gu, b_gu, w_down, b_down, *, blk):
    n_exp, d, d2 = w_gu.shape
    d_exp = d2 // 2
    nb = block_e.shape[0]
    kern = functools.partial(_moe_kernel, blk=blk, d_exp=d_exp)
    grid_spec = pltpu.PrefetchScalarGridSpec(
        num_scalar_prefetch=3,
        grid=(nb,),
        in_specs=[pl.BlockSpec(memory_space=pl.ANY),
                  pl.BlockSpec((1, d, d2), lambda i, be, rw, nu: (be[i], 0, 0)),
                  pl.BlockSpec((1, 1, d2), lambda i, be, rw, nu: (be[i], 0, 0)),
                  pl.BlockSpec((1, d_exp, d), lambda i, be, rw, nu: (be[i], 0, 0)),
                  pl.BlockSpec((1, 1, d), lambda i, be, rw, nu: (be[i], 0, 0))],
        out_specs=pl.BlockSpec((blk * SUBLANE, LANE), lambda i, be, rw, nu: (i, 0)),
        scratch_shapes=[pltpu.VMEM((2, blk * SUBLANE, LANE), F32), pltpu.SemaphoreType.DMA((2,)),
                        pltpu.VMEM((d, d2), BF16), pltpu.VMEM((d_exp, d), BF16),
                        pltpu.VMEM((blk, d), BF16)])
    return pl.pallas_call(
        kern,
        grid_spec=grid_spec,
        out_shape=jax.ShapeDtypeStruct((nb * blk * SUBLANE, LANE), F32),
        compiler_params=_cparams(("arbitrary",), VMEM_LIMIT),
        name="moe_experts",
    )(block_e, rows, n_used, x1, w_gu, b_gu.reshape(n_exp, 1, d2), w_down, b_down.reshape(n_exp, 1, d))


def _combine_kernel(dest_ref, x1_ref, gate_ref, ys_hbm, g2_ref, b2_ref, op_ref, os_ref, ybuf, sem,
                    *, tm, top_k, alpha, m_total, n_first):
    i = pl.program_id(0)
    nsteps = pl.num_programs(0)
    slot = i % 2
    n_rows = top_k * tm

    def row_copy(d, s, r):
        return pltpu.make_async_copy(ys_hbm.at[pl.ds(d * SUBLANE, SUBLANE)],
                                     ybuf.at[s, pl.ds(r * SUBLANE, SUBLANE)], sem.at[s])

    def wait(s):
        pltpu.make_async_copy(ys_hbm.at[pl.ds(0, n_rows * SUBLANE)], ybuf.at[s], sem.at[s]).wait()

    @pl.when(i == 0)
    def _():
        second = jnp.minimum(1, nsteps - 1) * tm

        def body(r, carry):
            k = r // tm
            t = r - k * tm
            row_copy(dest_ref[k * m_total + t], 0, r).start()
            row_copy(dest_ref[k * m_total + second + t], 1, r).start()
            return carry
        lax.fori_loop(0, n_rows, body, 0)

    wait(slot)
    g = gate_ref[...]
    parts = []
    for j in range(SUBLANE):
        acc = g[:, 0:1] * _load_row_tiles(ybuf.at[slot], 0, tm, j)
        for k in range(1, top_k):
            acc = acc + g[:, k:k + 1] * _load_row_tiles(ybuf.at[slot], k * tm, tm, j)
        parts.append(acc)
    y = jnp.concatenate(parts, axis=1)
    nxt = jnp.minimum(i + 2, nsteps - 1) * tm
    for r in range(n_rows):
        k, t = divmod(r, tm)
        row_copy(dest_ref[k * m_total + nxt + t], slot, r).start(priority=r % 2)
    res = _layer_norm(alpha * x1_ref[...] + y, g2_ref[...], b2_ref[...])

    @pl.when(i < n_first)
    def _():
        op_ref[...] = res

    @pl.when(i >= n_first)
    def _():
        os_ref[...] = res

    @pl.when(i == nsteps - 1)
    def _():
        wait(0)
        wait(1)


def _combine(dest_km, x1, gates_mk, ys, g2, b2, *, tm, alpha, m_first):
    m, d = x1.shape
    n_first = m_first // tm
    assert m_first % tm == 0 and 0 < m_first < m
    kern = functools.partial(_combine_kernel, tm=tm, top_k=TOP_K, alpha=alpha, m_total=m, n_first=n_first)
    grid_spec = pltpu.PrefetchScalarGridSpec(
        num_scalar_prefetch=1,
        grid=(m // tm,),
        in_specs=[pl.BlockSpec((tm, d), lambda i, ds: (i, 0)),
                  pl.BlockSpec((tm, SUBLANE), lambda i, ds: (i, 0)),
                  pl.BlockSpec(memory_space=pl.ANY),
                  pl.BlockSpec((1, d), lambda i, ds: (0, 0)),
                  pl.BlockSpec((1, d), lambda i, ds: (0, 0))],
        out_specs=[pl.BlockSpec((tm, d), lambda i, ds: (jnp.minimum(i, n_first - 1), 0)),
                   pl.BlockSpec((tm, d), lambda i, ds: (jnp.maximum(i - n_first, 0), 0))],
        scratch_shapes=[pltpu.VMEM((2, TOP_K * tm * SUBLANE, LANE), F32), pltpu.SemaphoreType.DMA((2,))])
    return pl.pallas_call(
        kern,
        grid_spec=grid_spec,
        out_shape=[jax.ShapeDtypeStruct((m_first, d), F32), jax.ShapeDtypeStruct((m - m_first, d), F32)],
        compiler_params=_cparams(("arbitrary",), VMEM_LIMIT),
        name="moe_combine",
    )(dest_km, x1, gates_mk, ys, g2, b2)


def _pack_w_in(w_in, splits):
    q_lora, kv_lora, rope, conv_ch, gv, nh, _, d, _ = splits
    offs = [0]
    for s in splits:
        offs.append(offs[-1] + s)
    part = [w_in[:, offs[i]:offs[i + 1]] for i in range(len(splits))]
    q_lat, kv_lat, k_r, qkv, z, a, b, g_a, g_b = part
    dm = w_in.shape[0]
    half = rope // 2
    zpad = lambda n: jnp.zeros((dm, n), w_in.dtype)
    k_sw = jnp.concatenate([k_r[:, half:], k_r[:, :half]], axis=1)
    small = jnp.concatenate([q_lat, kv_lat, k_r, zpad(LANE - rope), k_sw, zpad(LANE - rope),
                             a, b, zpad(LANE - 2 * nh)], axis=1)
    return jnp.concatenate([qkv, small, z, g_a, g_b], axis=1).astype(BF16)


def _pack_w_uq(w_uq, nope, rope):
    w = jnp.transpose(w_uq, (1, 0, 2))
    half = rope // 2
    r = w[..., nope:]
    zp = jnp.zeros(r.shape[:-1] + (LANE - rope,), w.dtype)
    r_sw = jnp.concatenate([r[..., half:], r[..., :half]], axis=-1)
    return jnp.concatenate([w[..., :nope], r, zp, r_sw, zp], axis=-1).astype(BF16)


def _rope_tables(pos, rope):
    half = rope // 2
    inv = ROPE_THETA ** (-jnp.arange(half, dtype=F32) / half)
    ang = pos.astype(F32)[:, None] * inv[None, :]
    cos, sin = jnp.cos(ang), jnp.sin(ang)
    zp = jnp.zeros((pos.shape[0], LANE - rope), F32)
    return (jnp.concatenate([cos, cos, zp], axis=1), jnp.concatenate([-sin, sin, zp], axis=1))


def _route_meta(idx_t, rank_t, counts, m, n_exp, blk):
    a = m * TOP_K
    counts = counts.astype(jnp.int32)
    padded = (counts + blk - 1) // blk * blk
    pad_end = jnp.cumsum(padded)
    pad_start = pad_end - padded
    experts = jnp.arange(n_exp, dtype=jnp.int32)
    e_km = idx_t[:TOP_K]
    start_km = jnp.sum(jnp.where(e_km[:, :, None] == experts, pad_start, 0), axis=-1)
    dest_km = (start_km + rank_t[:TOP_K]).astype(jnp.int32).reshape(a)
    nb = a // blk + n_exp
    tok_km = jnp.tile(jnp.arange(m, dtype=jnp.int32), TOP_K)
    filler = jnp.arange(nb * blk, dtype=jnp.int32) % m
    rows = filler.at[dest_km].set(tok_km, unique_indices=True, mode='promise_in_bounds')
    first_row = jnp.arange(nb, dtype=jnp.int32) * blk
    block_e = jnp.minimum(jnp.sum((pad_end[None, :] <= first_row[:, None]).astype(jnp.int32), axis=1),
                          n_exp - 1).astype(jnp.int32)
    n_used = (pad_end[-1] // blk).astype(jnp.int32).reshape(1)
    return block_e, rows, n_used, dest_km


def kernel(x_prompt, x_sample, cache_ckv, cache_krope, page_table, state_conv, state_ssm, w_in, q_norm_w, kv_norm_w, w_uq, w_uk, w_uv, conv_w, a_log, dt_bias, gdn_norm_w, w_o, ln1_g, ln1_b, w_router, b_router, w_gu, b_gu, w_down, b_down, ln2_g, ln2_b):
    bp, tp, d = x_prompt.shape
    bs, ts, _ = x_sample.shape
    depth = w_in.shape[0]
    q_lora, n_heads, qk = w_uq.shape[1:]
    kv_lora, _, nope = w_uk.shape[1:]
    rope = qk - nope
    vh = w_uv.shape[3]
    cw_taps, conv_ch = conv_w.shape[1:]
    g_heads = a_log.shape[1]
    dk, dv = state_ssm.shape[3:]
    n_exp = w_router.shape[2]
    page = cache_ckv.shape[2]
    past = page_table.shape[1] * page
    splits = (q_lora, kv_lora, rope, conv_ch, g_heads * dv, g_heads, g_heads, d, d)
    assert sum(splits) == w_in.shape[2]
    alpha = (2 * depth) ** 0.25
    scale = (nope + rope) ** -0.5 * LOG2E
    mp, ms = bp * tp, bs * ts
    m = mp + ms
    small_w = q_lora + kv_lora + 3 * LANE
    assert small_w == d and conv_ch % d == 0
    qkv_blk, small_blk = 0, conv_ch // d
    z_blk, ga_blk, gb_blk = small_blk + 1, small_blk + 2, small_blk + 3
    ab_blk = (conv_ch + q_lora + kv_lora + 2 * LANE) // LANE

    cos_p, sin_p = _rope_tables(jnp.arange(tp, dtype=jnp.int32), rope)
    cos_s, sin_s = _rope_tables(past + jnp.arange(ts, dtype=jnp.int32), rope)
    cos_t = jnp.concatenate([jnp.tile(cos_p, (bp, 1)), jnp.tile(cos_s, (bs, 1))], axis=0)
    sin_t = jnp.concatenate([jnp.tile(sin_p, (bp, 1)), jnp.tile(sin_s, (bs, 1))], axis=0)

    x_p, x_s = x_prompt.reshape(mp, d), x_sample.reshape(ms, d)
    outs = {k: [] for k in ("ckv_p", "kr_p", "conv_p", "ssm_p", "ckv_s", "kr_s", "conv_s", "ssm_s")}
    pad_lanes = lambda v: jnp.pad(v, (0, LANE - v.shape[0])).reshape(1, LANE)
    for l in range(depth):
        x = jnp.concatenate([x_p, x_s], axis=0)
        w_pack = _pack_w_in(w_in[l], splits)
        wq = _pack_w_uq(w_uq[l], nope, rope)
        wuk = jnp.transpose(w_uk[l], (1, 2, 0)).astype(BF16)
        wuv = jnp.transpose(w_uv[l], (1, 0, 2)).astype(BF16)
        h = _in_proj(x, w_pack, tm=_row_tile(m, 2304), tn=1024)
        qf, c_all, kr_all, kt, cb = _mla_proj(
            h, small_blk, cos_t, sin_t, q_norm_w[l].reshape(1, q_lora), kv_norm_w[l].reshape(1, kv_lora),
            wq, wuk, tm=512, q_lora=q_lora, kv_lora=kv_lora, nope=nope, rope=rope, scale=scale)
        om_p = _attn_prompt(qf, kt, cb, wuv, batch=bp, seq=tp, tq=128, tk=min(1024, tp))
        qs = qf[:, mp:, :].reshape(n_heads, bs, ts, qf.shape[-1])
        qs = jnp.transpose(qs, (1, 0, 2, 3)).reshape(bs, n_heads * ts, qf.shape[-1])
        om_s = _attn_sample(page_table, qs, c_all, kr_all, mp // ts, cache_ckv[l],
                            jnp.swapaxes(cache_krope[l], 1, 2), wuv)

        alog_p = pad_lanes(a_log[l])
        dtb_p = pad_lanes(dt_bias[l])
        nw = gdn_norm_w[l].reshape(1, dv)
        gdn_kw = dict(qkv_blk=qkv_blk, ab_blk=ab_blk, z_blk=z_blk, conv_w=cw_taps)
        zeros_conv = jnp.zeros((bp, SUBLANE, conv_ch), F32)
        zeros_ssm = jnp.zeros((bp, g_heads, dk, dv), F32)
        og_p, ssm_p, ncv_p = _gdn(h, 0, zeros_conv, zeros_ssm, conv_w[l], alog_p, dtb_p, nw,
                                  batch=bp, seq=tp, chunk=min(GDN_CHUNK, tp), group=bp, **gdn_kw)
        conv8_s = jnp.pad(state_conv[l], ((0, 0), (SUBLANE - (cw_taps - 1), 0), (0, 0)))
        og_s, ssm_s, ncv_s = _gdn(h, mp, conv8_s, state_ssm[l], conv_w[l], alog_p, dtb_p, nw,
                                  batch=bs, seq=ts, chunk=ts, group=math.gcd(bs, GDN_SAMPLE_GROUP), **gdn_kw)

        x1, x1t, idx_t, gate_t, rank_t, counts = _post(
            x, om_p, om_s, og_p, og_s, h, ga_blk, gb_blk, w_o[l].astype(BF16), ln1_g[l].reshape(1, d), ln1_b[l].reshape(1, d),
            w_router[l].T, b_router[l].reshape(n_exp, 1), tm=512, alpha=alpha)
        block_e, rows, n_used, dest_km = _route_meta(idx_t, rank_t, counts[:, 0], m, n_exp, MOE_ROWS)
        ys = _moe_experts(block_e, rows, n_used, x1t, w_gu[l], b_gu[l], w_down[l], b_down[l], blk=MOE_ROWS)
        x_p, x_s = _combine(dest_km, x1, gate_t.T, ys, ln2_g[l].reshape(1, d), ln2_b[l].reshape(1, d),
                            tm=128, alpha=alpha, m_first=mp)

        outs["ckv_p"].append(c_all[:mp].reshape(bp, tp, kv_lora))
        outs["kr_p"].append(kr_all[:mp].reshape(bp, tp, rope))
        outs["conv_p"].append(ncv_p[:, SUBLANE - (cw_taps - 1):, :])
        outs["ssm_p"].append(ssm_p)
        outs["ckv_s"].append(c_all[mp:].reshape(bs, ts, kv_lora))
        outs["kr_s"].append(kr_all[mp:].reshape(bs, ts, rope))
        outs["conv_s"].append(ncv_s[:, SUBLANE - (cw_taps - 1):, :])
        outs["ssm_s"].append(ssm_s)

    return (x_p.reshape(bp, tp, d), x_s.reshape(bs, ts, d),
            jnp.stack(outs["ckv_p"]), jnp.stack(outs["kr_p"]), jnp.stack(outs["conv_p"]), jnp.stack(outs["ssm_p"]),
            jnp.stack(outs["ckv_s"]), jnp.stack(outs["kr_s"]), jnp.stack(outs["conv_s"]), jnp.stack(outs["ssm_s"]))
```
